```python
import math
import jax
import jax.numpy as jnp
from jax import lax
import numpy as np

D_MODEL = 1024
BATCH = 1
SEQ = 16384
DEPTH = 2
DEC_BATCH = 32
DEC_SEQ = 1
PAST_LEN = 16384
PAGE_SIZE = 128

HD = D_MODEL // 16
RET_H = 6
MOBA_H = 6
MEM_H = 4
NSA_H = 8
NSA_KVH = 2
NSA_REP = NSA_H // NSA_KVH
HGRN_H = 4
RET_W = RET_H * HD
MOBA_W = MOBA_H * HD
MEM_W = MEM_H * HD
NSA_QW = NSA_H * HD
NSA_KVW = NSA_KVH * HD
HGRN_W = HGRN_H * HD
MIX_W = RET_W + MOBA_W + MEM_W
EVEN_IN = 3 * RET_W + 3 * MOBA_W + MEM_W + MIX_W
ODD_IN = NSA_QW + 6 * NSA_KVW + 3 * NSA_H + 3 * HGRN_W + MEM_W + MIX_W
N_EVEN = (DEPTH + 1) // 2
N_ODD = DEPTH // 2
N_MEM = 256
RET_CHUNK = 128
HGRN_CHUNK = 64
MOBA_BLOCK = 256
MOBA_TOPK = 3
NSA_BLOCK = 64
NSA_TOPN = 16
NSA_WINDOW = 512
Q_BLOCK = 128
ROPE_THETA = 10000.0
EPS = 1e-6
NEG = -1e30
FORCE = 1e30
TINY = 1e-30

kernel_name = 'hybrid_retnet_moba_nsa_hgrn2_decode_step'


def split_cols(a, sizes):
    offs = np.cumsum([0] + list(sizes))
    return [a[..., int(offs[i]):int(offs[i + 1])] for i in range(len(sizes))]


def rmsnorm(x, g):
    xf = x.astype(jnp.float32)
    y = xf * lax.rsqrt(jnp.mean(xf * xf, axis=-1, keepdims=True) + EPS)
    return (y * g.astype(jnp.float32)).astype(x.dtype)


def group_norm(x):
    xc = x - jnp.mean(x, axis=-1, keepdims=True)
    return xc * lax.rsqrt(jnp.mean(xc * xc, axis=-1, keepdims=True) + EPS)


def head_rms(x):
    return x * lax.rsqrt(jnp.mean(x * x, axis=-1, keepdims=True) + EPS)


def masked_softmax(s, mask):
    s = jnp.where(mask, s, NEG)
    m = jnp.max(s, axis=-1, keepdims=True)
    e = jnp.where(mask, jnp.exp(s - m), 0.0)
    return e / jnp.maximum(jnp.sum(e, axis=-1, keepdims=True), TINY)


def rope(x, pos):
    half = x.shape[-1] // 2
    inv = ROPE_THETA ** (-jnp.arange(half, dtype=jnp.float32) / half)
    ang = pos.astype(jnp.float32)[:, None] * inv[None, :]
    cos = jnp.cos(ang)[:, None, :]
    sin = jnp.sin(ang)[:, None, :]
    x1, x2 = x[..., :half], x[..., half:]
    return jnp.concatenate([x1 * cos - x2 * sin, x2 * cos + x1 * sin], axis=-1)


def sweep_queries(fn, xs):
    nq = xs[0].shape[0]
    qb = math.gcd(nq, Q_BLOCK)
    nc = nq // qb
    xs_c = tuple(a.reshape((nc, qb) + a.shape[1:]) for a in xs)
    out = lax.map(fn, xs_c)
    return jax.tree_util.tree_map(lambda o: o.reshape((nq,) + o.shape[2:]), out)


def retention(q, k, v, state):
    b, t, h, dh = q.shape
    c = math.gcd(t, RET_CHUNK)
    n = t // c
    lg = jnp.log(1.0 - jnp.power(2.0, -5.0 - jnp.arange(h, dtype=jnp.float32)))
    ti = jnp.arange(c, dtype=jnp.float32)
    causal = ti[:, None] >= ti[None, :]
    d_in = jnp.where(causal[None], jnp.exp(jnp.where(causal, ti[:, None] - ti[None, :], 0.0)[None] * lg[:, None, None]), 0.0)
    q_dec = jnp.exp((ti[:, None] + 1.0) * lg[None, :])
    k_dec = jnp.exp((c - 1.0 - ti)[:, None] * lg[None, :])
    c_dec = jnp.exp(c * lg)
    k = k * dh ** -0.5
    to_chunks = lambda a: a.reshape(b, n, c, h, a.shape[-1]).transpose(1, 0, 2, 3, 4)

    def step(s, inp):
        qc, kc, vc = inp
        att = jnp.einsum('bthd,bshd->bhts', qc, kc) * d_in[None]
        o = jnp.einsum('bhts,bshe->bthe', att, vc) + jnp.einsum('bthd,bhde->bthe', qc, s) * q_dec[None, :, :, None]
        s = s * c_dec[None, :, None, None] + jnp.einsum('bshd,bshe->bhde', kc * k_dec[None, :, :, None], vc)
        return s, o

    s, o = lax.scan(step, state, (to_chunks(q), to_chunks(k), to_chunks(v)))
    return o.transpose(1, 0, 2, 3, 4).reshape(b, t, h, -1), s


def hgrn2(q, f_logit, i, lb, state):
    b, t, h, dk = q.shape
    f = lb[None, None] + (1.0 - lb[None, None]) * jax.nn.sigmoid(f_logit)
    logf = jnp.log(f)
    k = 1.0 - f
    c = math.gcd(t, HGRN_CHUNK)
    n = t // c
    causal = jnp.tril(jnp.ones((c, c), dtype=bool))
    to_chunks = lambda a: a.reshape(b, n, c, h, a.shape[-1]).transpose(1, 0, 2, 3, 4)

    def step(s, inp):
        qc, lc, kc, vc = inp
        cum = jnp.cumsum(lc, axis=1)
        diff = cum[:, :, None] - cum[:, None, :]
        dec = jnp.exp(jnp.where(causal[None, :, :, None, None], diff, -jnp.inf))
        a = jnp.einsum('bthd,bshd,btshd->bhts', qc, kc, dec)
        o = jnp.einsum('bhts,bshe->bthe', a, vc) + jnp.einsum('bthd,bhde->bthe', qc * jnp.exp(cum), s)
        last = cum[:, -1]
        s = s * jnp.exp(last)[..., None] + jnp.einsum('bshd,bshe->bhde', kc * jnp.exp(last[:, None] - cum), vc)
        return s, o

    s, o = lax.scan(step, state, (to_chunks(q), to_chunks(logf), to_chunks(k), to_chunks(i)))
    return o.transpose(1, 0, 2, 3, 4).reshape(b, t, h, -1), s


def moba_seq(q, k, v, pos):
    tk, h, dh = k.shape
    nblk = -(-tk // MOBA_BLOCK)
    pad = nblk * MOBA_BLOCK - tk
    kb = jnp.pad(k, ((0, pad), (0, 0), (0, 0))).reshape(nblk, MOBA_BLOCK, h, dh)
    vb = jnp.pad(v, ((0, pad), (0, 0), (0, 0))).reshape(nblk, MOBA_BLOCK, h, dh)
    kmean = jnp.mean(kb, axis=1)
    kbh = kb.transpose(2, 0, 1, 3)
    vbh = vb.transpose(2, 0, 1, 3)
    topk = min(MOBA_TOPK, nblk)
    scale = dh ** -0.5
    hidx = jnp.arange(h)[None, :, None]
    blk_ids = jnp.arange(nblk)

    def chunk(a):
        qc, p = a
        qb = p.shape[0]
        qblk = p // MOBA_BLOCK
        gate = jnp.einsum('qhd,nhd->qhn', qc, kmean)
        past = blk_ids[None, None, :] < qblk[:, None, None]
        _, idx = lax.top_k(jnp.where(past, gate, NEG), topk)
        valid = idx < qblk[:, None, None]
        ks = kbh[hidx, idx]
        vs = vbh[hidx, idx]
        b0 = p[0] // MOBA_BLOCK
        ko = lax.dynamic_index_in_dim(kb, b0, 0, keepdims=False)
        vo = lax.dynamic_index_in_dim(vb, b0, 0, keepdims=False)
        own_ok = (b0 * MOBA_BLOCK + jnp.arange(MOBA_BLOCK))[None, :] <= p[:, None]
        nsel = topk * MOBA_BLOCK
        s_sel = jnp.einsum('qhd,qhjsd->qhjs', qc, ks).reshape(qb, h, nsel) * scale
        s_own = jnp.einsum('qhd,shd->qhs', qc, ko) * scale
        m_sel = jnp.broadcast_to(valid[..., None], (qb, h, topk, MOBA_BLOCK)).reshape(qb, h, nsel)
        m_own = jnp.broadcast_to(own_ok[:, None, :], (qb, h, MOBA_BLOCK))
        pr = masked_softmax(jnp.concatenate([s_sel, s_own], -1), jnp.concatenate([m_sel, m_own], -1))
        return (jnp.einsum('qhm,qhmd->qhd', pr[..., :nsel], vs.reshape(qb, h, nsel, dh))
                + jnp.einsum('qhs,shd->qhd', pr[..., nsel:], vo))

    return sweep_queries(chunk, (q, pos))


def nsa_compress(x, w1, w2, pe):
    g = x.shape[1]
    nb = x.shape[0] // NSA_BLOCK
    xb = x.reshape(nb, NSA_BLOCK, g, HD) + pe.astype(jnp.float32)[None, :, None, :]
    xb = xb.transpose(0, 2, 1, 3).reshape(nb, g, NSA_BLOCK * HD)
    return jax.nn.gelu(xb @ w1.astype(jnp.float32)) @ w2.astype(jnp.float32)


def nsa_seq(q, qr, kc, vc, ks, vs, pos, w1, w2, pe):
    tk, g, dh = kc.shape
    nb = -(-tk // NSA_BLOCK)
    padw = ((0, nb * NSA_BLOCK - tk), (0, 0), (0, 0))
    ck = nsa_compress(jnp.pad(kc, padw), w1[0], w2[0], pe[0])
    cv = nsa_compress(jnp.pad(vc, padw), w1[1], w2[1], pe[1])
    ksb = jnp.pad(ks, padw).reshape(nb, NSA_BLOCK, g, dh).transpose(2, 0, 1, 3)
    vsb = jnp.pad(vs, padw).reshape(nb, NSA_BLOCK, g, dh).transpose(2, 0, 1, 3)
    nsel = min(NSA_TOPN, nb)
    blk = jnp.arange(nb)
    gidx = jnp.arange(g)[None, :, None]
    scale = dh ** -0.5

    def chunk(a):
        qc, qrc, p = a
        qb = p.shape[0]
        cur = (p // NSA_BLOCK)[:, None]
        complete = (blk[None, :] * NSA_BLOCK + NSA_BLOCK - 1) <= p[:, None]
        s_c = jnp.einsum('qgrd,ngd->qgrn', qc, ck) * scale
        pc = masked_softmax(s_c, complete[:, None, None, :])
        o_c = jnp.einsum('qgrn,ngd->qgrd', pc, cv)
        imp = jnp.sum(pc, axis=2)
        forced = (blk[None, :] == 0) | (blk[None, :] == cur) | (blk[None, :] == cur - 1)
        score = jnp.where(forced[:, None, :], FORCE, jnp.where(complete[:, None, :], imp, NEG))
        vals, idx = lax.top_k(score, nsel)
        valid = vals > 0.5 * NEG
        kss = ksb[gidx, idx]
        vss = vsb[gidx, idx]
        kpos = idx[..., None] * NSA_BLOCK + jnp.arange(NSA_BLOCK)
        m = (valid[..., None] & (kpos <= p[:, None, None, None])).reshape(qb, g, 1, nsel * NSA_BLOCK)
        s_s = jnp.einsum('qgrd,qgjld->qgrjl', qrc, kss).reshape(qb, g, NSA_REP, nsel * NSA_BLOCK) * scale
        ps = masked_softmax(s_s, m)
        o_s = jnp.einsum('qgrm,qgmd->qgrd', ps, vss.reshape(qb, g, nsel * NSA_BLOCK, dh))
        return o_c, o_s

    return sweep_queries(chunk, (q, qr, pos))


def window_attend(qr, kw, vw, pos, kpos0):
    g, dh = kw.shape[1], kw.shape[2]
    kwp = jnp.pad(kw, ((NSA_WINDOW, 0), (0, 0), (0, 0)))
    vwp = jnp.pad(vw, ((NSA_WINDOW, 0), (0, 0), (0, 0)))
    scale = dh ** -0.5

    def chunk(a):
        qc, p = a
        qb = p.shape[0]
        span = NSA_WINDOW + qb - 1
        start = p[0] + 1 - kpos0
        kk = lax.dynamic_slice_in_dim(kwp, start, span, 0)
        vv = lax.dynamic_slice_in_dim(vwp, start, span, 0)
        kp = p[0] - NSA_WINDOW + 1 + jnp.arange(span)
        m = (kp[None, :] >= kpos0) & (kp[None, :] <= p[:, None]) & (kp[None, :] > p[:, None] - NSA_WINDOW)
        s = jnp.einsum('qgrd,lgd->qgrl', qc, kk) * scale
        pr = masked_softmax(s, m[:, None, None, :])
        return jnp.einsum('qgrl,lgd->qgrd', pr, vv)

    return sweep_queries(chunk, (qr, pos))


def mem_kv(mem, g, w):
    b, n, _ = mem.shape
    return (rmsnorm(mem, g) @ w).reshape(b, n, 2, MEM_H, HD)


def mem_attend(q, kv):
    kv = kv.astype(jnp.float32)
    s = jnp.einsum('bthd,bnhd->bhtn', q, kv[:, :, 0]) * HD ** -0.5
    p = jax.nn.softmax(s, axis=-1)
    return jnp.einsum('bhtn,bnhd->bthd', p, kv[:, :, 1])


def even_layer(x, pos, g, w_in, w_o, mkv, ret_state, moba_fn):
    b, t, _ = x.shape
    heads = lambda a, n: a.reshape(b, t, n, HD).astype(jnp.float32)
    h = rmsnorm(x, g)
    rq, rk, rv, mq, mk, mv, eq, gate = split_cols(h @ w_in, [RET_W] * 3 + [MOBA_W] * 3 + [MEM_W, MIX_W])
    o_ret, ret_state = retention(rope(heads(rq, RET_H), pos), rope(heads(rk, RET_H), pos), heads(rv, RET_H), ret_state)
    mq = rope(heads(mq, MOBA_H), pos)
    mk = rope(heads(mk, MOBA_H), pos)
    mv = heads(mv, MOBA_H)
    o_moba = moba_fn(mq, mk, mv)
    o_mem = mem_attend(heads(eq, MEM_H), mkv)
    mix = jnp.concatenate([group_norm(o_ret).reshape(b, t, RET_W), o_moba.reshape(b, t, MOBA_W),
                           o_mem.reshape(b, t, MEM_W)], axis=-1)
    mix = mix * jax.nn.silu(gate.astype(jnp.float32))
    y = x + (mix.astype(x.dtype) @ w_o).astype(x.dtype)
    return y, jnp.stack([mk, mv], axis=2), ret_state


def odd_layer(x, pos, lb, g, w_in, w_o, mkv, hg_state, nsa_fn):
    b, t, _ = x.shape
    heads = lambda a, n: a.reshape(b, t, n, HD).astype(jnp.float32)
    h = rmsnorm(x, g)
    nq, ck, cv, sk, sv, wk, wv, ng, hq, hf, hi, eq, gate = split_cols(
        h @ w_in, [NSA_QW] + [NSA_KVW] * 6 + [3 * NSA_H] + [HGRN_W] * 3 + [MEM_W, MIX_W])
    q = heads(nq, NSA_H)
    qr = rope(q, pos).reshape(b, t, NSA_KVH, NSA_REP, HD)
    q = q.reshape(b, t, NSA_KVH, NSA_REP, HD)
    ck = heads(ck, NSA_KVH)
    cv = heads(cv, NSA_KVH)
    sk = rope(heads(sk, NSA_KVH), pos)
    sv = heads(sv, NSA_KVH)
    wk = rope(heads(wk, NSA_KVH), pos)
    wv = heads(wv, NSA_KVH)
    o_c, o_s, o_w = nsa_fn(q, qr, ck, cv, sk, sv, wk, wv)
    gt = jax.nn.sigmoid(ng.astype(jnp.float32)).reshape(b, t, NSA_KVH, NSA_REP, 3)
    o_nsa = gt[..., 0:1] * o_c + gt[..., 1:2] * o_s + gt[..., 2:3] * o_w
    o_hg, hg_state = hgrn2(heads(hq, HGRN_H), heads(hf, HGRN_H), heads(hi, HGRN_H), lb, hg_state)
    o_mem = mem_attend(heads(eq, MEM_H), mkv)
    mix = jnp.concatenate([o_nsa.reshape(b, t, NSA_QW), head_rms(o_hg).reshape(b, t, HGRN_W),
                           o_mem.reshape(b, t, MEM_W)], axis=-1)
    mix = mix * jax.nn.silu(gate.astype(jnp.float32))
    y = x + (mix.astype(x.dtype) @ w_o).astype(x.dtype)
    return y, jnp.stack([ck, cv], axis=2), jnp.stack([sk, sv], axis=2), jnp.stack([wk, wv], axis=2), hg_state


def moba_prompt(q, k, v):
    pos = jnp.arange(q.shape[1], dtype=jnp.int32)
    return lax.map(lambda a: moba_seq(a[0], a[1], a[2], pos), (q, k, v))


def moba_sample_fn(pool, layer, page_table, past_len):
    def fn(q, k, v):
        pos = past_len + jnp.arange(q.shape[1], dtype=jnp.int32)

        def one(a):
            pt, q1, k1, v1 = a
            past = pool[pt, layer].reshape(past_len, 2, MOBA_H, HD).astype(jnp.float32)
            return moba_seq(q1, jnp.concatenate([past[:, 0], k1], 0), jnp.concatenate([past[:, 1], v1], 0), pos)

        return lax.map(one, (page_table, q, k, v))
    return fn


def nsa_prompt_fn(w1, w2, pe):
    def fn(q, qr, ck, cv, sk, sv, wk, wv):
        pos = jnp.arange(q.shape[1], dtype=jnp.int32)

        def one(a):
            q1, qr1, ck1, cv1, sk1, sv1, wk1, wv1 = a
            o_c, o_s = nsa_seq(q1, qr1, ck1, cv1, sk1, sv1, pos, w1, w2, pe)
            o_w = window_attend(qr1, wk1, wv1, pos, 0)
            return o_c, o_s, o_w

        return lax.map(one, (q, qr, ck, cv, sk, sv, wk, wv))
    return fn


def nsa_sample_fn(cmp_pool, slc_pool, wbuf, layer, page_table, past_len, w1, w2, pe):
    lbuf = wbuf.shape[1]

    def fn(q, qr, ck, cv, sk, sv, wk, wv):
        pos = past_len + jnp.arange(q.shape[1], dtype=jnp.int32)

        def one(a):
            pt, q1, qr1, ck1, cv1, sk1, sv1, wk1, wv1, wb1 = a
            cp = cmp_pool[pt, layer].reshape(past_len, 2, NSA_KVH, HD).astype(jnp.float32)
            sp = slc_pool[pt, layer].reshape(past_len, 2, NSA_KVH, HD).astype(jnp.float32)
            wb1 = wb1.astype(jnp.float32)
            cat = lambda u, w: jnp.concatenate([u, w], axis=0)
            o_c, o_s = nsa_seq(q1, qr1, cat(cp[:, 0], ck1), cat(cp[:, 1], cv1), cat(sp[:, 0], sk1),
                               cat(sp[:, 1], sv1), pos, w1, w2, pe)
            o_w = window_attend(qr1, cat(wb1[:, 0], wk1), cat(wb1[:, 1], wv1), pos, past_len - lbuf)
            return o_c, o_s, o_w

        return lax.map(one, (page_table, q, qr, ck, cv, sk, sv, wk, wv, wbuf))
    return fn


def setup_inputs(seed: int = 0) -> dict:
    key = jax.random.key(seed)
    ks = jax.random.split(key, 24)
    f32 = jnp.float32
    n_pages = PAST_LEN // PAGE_SIZE
    n_used = DEC_BATCH * n_pages
    n_pool = n_used + max(1, n_used // 4)
    wbuf = min(NSA_WINDOW, PAST_LEN)
    nrm = lambda k, shape, s=1.0: jax.random.normal(k, shape, f32) * s
    page_table = jax.random.permutation(ks[10], n_pool)[:n_used].reshape(DEC_BATCH, n_pages).astype(jnp.int32)
    return {
        'x_prompt': nrm(ks[0], (BATCH, SEQ, D_MODEL)),
        'x_sample': nrm(ks[1], (DEC_BATCH, DEC_SEQ, D_MODEL)),
        'mem_prompt': nrm(ks[2], (BATCH, N_MEM, D_MODEL)),
        'cache_moba_kv': nrm(ks[3], (n_pool, N_EVEN, PAGE_SIZE, 2, MOBA_H, HD)),
        'state_ret': nrm(ks[4], (N_EVEN, DEC_BATCH, RET_H, HD, HD), 0.3),
        'cache_nsa_cmp_kv': nrm(ks[5], (n_pool, N_ODD, PAGE_SIZE, 2, NSA_KVH, HD)),
        'cache_nsa_slc_kv': nrm(ks[6], (n_pool, N_ODD, PAGE_SIZE, 2, NSA_KVH, HD)),
        'cache_nsa_win_kv': nrm(ks[7], (N_ODD, DEC_BATCH, wbuf, 2, NSA_KVH, HD)),
        'state_hgrn': nrm(ks[8], (N_ODD, DEC_BATCH, HGRN_H, HD, HD), 0.3),
        'cache_mem_kv': nrm(ks[9], (DEPTH, DEC_BATCH, N_MEM, 2, MEM_H, HD)),
        'page_table': page_table,
        'norm_g': 1.0 + nrm(ks[11], (DEPTH, D_MODEL), 0.02),
        'mem_norm_g': 1.0 + nrm(ks[12], (DEPTH, D_MODEL), 0.02),
        'w_mem_kv': nrm(ks[13], (DEPTH, D_MODEL, 2 * MEM_W), D_MODEL ** -0.5),
        'w_in_even': nrm(ks[14], (N_EVEN, D_MODEL, EVEN_IN), D_MODEL ** -0.5),
        'w_in_odd': nrm(ks[15], (N_ODD, D_MODEL, ODD_IN), D_MODEL ** -0.5),
        'w_out': nrm(ks[16], (DEPTH, MIX_W, D_MODEL), MIX_W ** -0.5),
        'cmp_w1': nrm(ks[17], (N_ODD, 2, NSA_BLOCK * HD, HD), (NSA_BLOCK * HD) ** -0.5),
        'cmp_w2': nrm(ks[18], (N_ODD, 2, HD, HD), HD ** -0.5),
        'cmp_pe': nrm(ks[19], (N_ODD, 2, NSA_BLOCK, HD), 0.1),
        'hgrn_lb_logits': nrm(ks[20], (DEPTH, HGRN_W), 0.5),
        'final_g': 1.0 + nrm(ks[21], (D_MODEL,), 0.02),
    }


def reference(x_prompt, x_sample, mem_prompt, cache_moba_kv, state_ret, cache_nsa_cmp_kv, cache_nsa_slc_kv,
              cache_nsa_win_kv, state_hgrn, cache_mem_kv, page_table, norm_g, mem_norm_g, w_mem_kv, w_in_even,
              w_in_odd, w_out, cmp_w1, cmp_w2, cmp_pe, hgrn_lb_logits, final_g):
    f32 = jnp.float32
    dp, ds = x_prompt.dtype, x_sample.dtype
    bp, tp, _ = x_prompt.shape
    past_len = page_table.shape[1] * PAGE_SIZE
    pos_p = jnp.arange(tp, dtype=jnp.int32)
    pos_s = past_len + jnp.arange(x_sample.shape[1], dtype=jnp.int32)
    lb_prob = jax.nn.softmax(hgrn_lb_logits.astype(f32), axis=0)
    lb_all = jnp.cumsum(lb_prob, axis=0) - lb_prob[0]
    lbuf = cache_nsa_win_kv.shape[2]
    lw_p = min(NSA_WINDOW, tp)
    hp, hs = x_prompt, x_sample
    moba_p, moba_s, ret_p, ret_s = [], [], [], []
    cmp_p, cmp_s, slc_p, slc_s, win_p, win_s, hg_p, hg_s, mem_p = [], [], [], [], [], [], [], [], []
    for layer in range(DEPTH):
        mkv_p = mem_kv(mem_prompt, mem_norm_g[layer], w_mem_kv[layer])
        mem_p.append(mkv_p)
        mkv_s = cache_mem_kv[layer]
        if layer % 2 == 0:
            e = layer // 2
            hp, rows, st = even_layer(hp, pos_p, norm_g[layer], w_in_even[e], w_out[layer], mkv_p,
                                      jnp.zeros((bp, RET_H, HD, HD), f32), moba_prompt)
            moba_p.append(rows)
            ret_p.append(st)
            hs, rows, st = even_layer(hs, pos_s, norm_g[layer], w_in_even[e], w_out[layer], mkv_s,
                                      state_ret[e].astype(f32), moba_sample_fn(cache_moba_kv, e, page_table, past_len))
            moba_s.append(rows)
            ret_s.append(st)
        else:
            o = layer // 2
            lb = lb_all[layer].reshape(HGRN_H, HD)
            hp, ckv, skv, wkv, st = odd_layer(hp, pos_p, lb, norm_g[layer], w_in_odd[o], w_out[layer], mkv_p,
                                              jnp.zeros((bp, HGRN_H, HD, HD), f32),
                                              nsa_prompt_fn(cmp_w1[o], cmp_w2[o], cmp_pe[o]))
            cmp_p.append(ckv)
            slc_p.append(skv)
            win_p.append(wkv[:, tp - lw_p:])
            hg_p.append(st)
            wbuf = cache_nsa_win_kv[o]
            hs, ckv, skv, wkv, st = odd_layer(hs, pos_s, lb, norm_g[layer], w_in_odd[o], w_out[layer], mkv_s,
                                              state_hgrn[o].astype(f32),
                                              nsa_sample_fn(cache_nsa_cmp_kv, cache_nsa_slc_kv, wbuf, o, page_table,
                                                            past_len, cmp_w1[o], cmp_w2[o], cmp_pe[o]))
            cmp_s.append(ckv)
            slc_s.append(skv)
            win_s.append(jnp.concatenate([wbuf.astype(f32), wkv], axis=1)[:, -lbuf:])
            hg_s.append(st)
    y_prompt = rmsnorm(hp, final_g)
    y_sample = rmsnorm(hs, final_g)
    moba_kv_prompt = jnp.stack(moba_p, axis=1).astype(dp)
    moba_kv_sample = jnp.stack(moba_s, axis=1).astype(ds)
    ret_state_prompt = jnp.stack(ret_p, axis=0).astype(dp)
    ret_state_sample = jnp.stack(ret_s, axis=0).astype(ds)
    nsa_cmp_kv_prompt = jnp.stack(cmp_p, axis=1).astype(dp)
    nsa_cmp_kv_sample = jnp.stack(cmp_s, axis=1).astype(ds)
    nsa_slc_kv_prompt = jnp.stack(slc_p, axis=1).astype(dp)
    nsa_slc_kv_sample = jnp.stack(slc_s, axis=1).astype(ds)
    nsa_win_kv_prompt = jnp.stack(win_p, axis=0).astype(dp)
    nsa_win_kv_sample = jnp.stack(win_s, axis=0).astype(ds)
    hgrn_state_prompt = jnp.stack(hg_p, axis=0).astype(dp)
    hgrn_state_sample = jnp.stack(hg_s, axis=0).astype(ds)
    mem_kv_prompt = jnp.stack(mem_p, axis=0).astype(dp)
    return (y_prompt, y_sample, moba_kv_prompt, moba_kv_sample, ret_state_prompt, ret_state_sample,
            nsa_cmp_kv_prompt, nsa_cmp_kv_sample, nsa_slc_kv_prompt, nsa_slc_kv_sample, nsa_win_kv_prompt,
            nsa_win_kv_sample, hgrn_state_prompt, hgrn_state_sample, mem_kv_prompt)
```

```python
import functools
import math

import jax
import jax.numpy as jnp
import numpy as np
from jax import lax
from jax.experimental import pallas as pl
from jax.experimental.pallas import tpu as pltpu

F32 = jnp.float32
BF16 = jnp.bfloat16
HIGHEST = lax.Precision.HIGHEST

HD = 64
RET_H, MOBA_H, MEM_H, NSA_H, NSA_KVH, HGRN_H = 6, 6, 4, 8, 2, 4
NSA_REP = NSA_H // NSA_KVH
RET_W, MOBA_W, MEM_W = RET_H * HD, MOBA_H * HD, MEM_H * HD
NSA_QW, NSA_KVW, HGRN_W = NSA_H * HD, NSA_KVH * HD, HGRN_H * HD
MIX_W = RET_W + MOBA_W + MEM_W
PAGE = 128
RET_CHUNK, HGRN_CHUNK = 128, 64
MOBA_BLOCK, MOBA_TOPK = 256, 3
NSA_BLOCK, NSA_TOPN, NSA_WINDOW = 64, 16, 512
ROPE_THETA = 10000.0
EPS = 1e-6
NEG = -1e30
FORCE = 1e30
TINY = 1e-30
SCALE = HD ** -0.5
LANES = 128
VMEM_LIMIT = 56 * 1024 * 1024


def _params(*sem):
    return pltpu.CompilerParams(dimension_semantics=sem, vmem_limit_bytes=VMEM_LIMIT)


def _dot(a, b, precision=None):
    return jnp.dot(a, b, preferred_element_type=F32, precision=precision)


def _dot_nt(a, b, precision=None):
    return lax.dot_general(a, b, (((1,), (1,)), ((), ())), preferred_element_type=F32, precision=precision)


def _dot_tn(a, b, precision=None):
    return lax.dot_general(a, b, (((0,), (0,)), ((), ())), preferred_element_type=F32, precision=precision)


def _block_ones(width, value=1.0):
    r = lax.broadcasted_iota(jnp.int32, (width, width), 0) // HD
    c = lax.broadcasted_iota(jnp.int32, (width, width), 1) // HD
    return jnp.where(r == c, value, 0.0).astype(F32)


def _rope_tile(a, cos, sin, first_half):
    rot = jnp.where(first_half, pltpu.roll(a, LANES - HD // 2, 1), pltpu.roll(a, HD // 2, 1))
    return a * cos + rot * sin


def _norm_proj_body(x_ref, g_ref, w_ref, cos_ref, sin_ref, o_ref, *, rope_tiles, n_chunk):
    x = x_ref[...]
    ms = jnp.mean(x * x, axis=-1, keepdims=True)
    y = (x * lax.rsqrt(ms + EPS) * g_ref[...]).astype(BF16)
    n = o_ref.shape[1]
    lane = lax.broadcasted_iota(jnp.int32, (x.shape[0], LANES), 1)
    first_half = (lane % HD) < HD // 2
    for c0 in range(0, n, n_chunk):
        acc = _dot(y, w_ref[:, c0:c0 + n_chunk])
        for j in range(n_chunk // LANES):
            tile = (c0 // LANES) + j
            a = acc[:, j * LANES:(j + 1) * LANES]
            if tile in rope_tiles:
                a = _rope_tile(a, cos_ref[...], sin_ref[...], first_half)
            o_ref[:, tile * LANES:(tile + 1) * LANES] = a


def _norm_proj(x, g, w_bf16, cos, sin, rope_cols, tm):
    m, d = x.shape
    n = w_bf16.shape[1]
    assert m % tm == 0 and n % LANES == 0
    tiles = n // LANES
    k = next(c for c in (4, 3, 2, 1) if tiles % c == 0)
    rope_tiles = frozenset(t for a, b in rope_cols for t in range(a // LANES, b // LANES))
    body = functools.partial(_norm_proj_body, rope_tiles=rope_tiles, n_chunk=k * LANES)
    return pl.pallas_call(
        body,
        grid=(m // tm,),
        in_specs=[
            pl.BlockSpec((tm, d), lambda i: (i, 0)),
            pl.BlockSpec((1, d), lambda i: (0, 0)),
            pl.BlockSpec((d, n), lambda i: (0, 0)),
            pl.BlockSpec((tm, LANES), lambda i: (i, 0)),
            pl.BlockSpec((tm, LANES), lambda i: (i, 0)),
        ],
        out_specs=pl.BlockSpec((tm, n), lambda i: (i, 0)),
        out_shape=jax.ShapeDtypeStruct((m, n), F32),
        compiler_params=_params("parallel"),
        name="norm_proj",
    )(x, g.reshape(1, d), w_bf16, cos, sin)


def _rope_tables(pos):
    half = HD // 2
    inv = ROPE_THETA ** (-jnp.arange(half, dtype=F32) / half)
    ang = pos.astype(F32)[:, None] * inv[None, :]
    cos, sin = jnp.cos(ang), jnp.sin(ang)
    return jnp.concatenate([cos, cos, cos, cos], -1), jnp.concatenate([-sin, sin, -sin, sin], -1)


def _retention_consts(c):
    lg = np.log(1.0 - np.power(2.0, -5.0 - np.arange(RET_H, dtype=np.float64)))
    ti = np.arange(c, dtype=np.float64)
    causal = ti[:, None] >= ti[None, :]
    d_in = np.where(causal[None], np.exp(np.where(causal, ti[:, None] - ti[None, :], 0.0)[None] * lg[:, None, None]), 0.0)
    q_dec = np.repeat(np.exp((ti[:, None] + 1.0) * lg[None, :]), HD, axis=1)
    k_dec = np.repeat(np.exp((c - 1.0 - ti)[:, None] * lg[None, :]), HD, axis=1)
    c_dec = np.repeat(np.exp(c * lg), HD)
    head = np.arange(RET_W) // HD
    bd = (head[:, None] == head[None, :]).astype(np.float64)
    cmat = bd * c_dec[:, None]
    f = lambda a: jnp.asarray(a, dtype=F32)
    return f(d_in), f(q_dec), f(k_dec), f(cmat), f(bd)


def _retention_body(q_ref, k_ref, v_ref, din_ref, qdec_ref, kdec_ref, cmat_ref, bd_ref, o_ref, s_ref):
    @pl.when(pl.program_id(0) == 0)
    def _():
        s_ref[...] = jnp.zeros_like(s_ref)

    q = q_ref[...]
    k = k_ref[...] * SCALE
    v = v_ref[...]
    c, w = q.shape
    head = lax.broadcasted_iota(jnp.int32, (c, w), 1) // HD
    s = s_ref[...]
    o = _dot(q, s, HIGHEST) * qdec_ref[...]
    for h in range(RET_H):
        mh = head == h
        att = _dot_nt(jnp.where(mh, q, 0.0), k, HIGHEST) * din_ref[h]
        o = o + jnp.where(mh, _dot(att, v, HIGHEST), 0.0)
    s_ref[...] = s * cmat_ref[...] + _dot_tn(k * kdec_ref[...], v, HIGHEST) * bd_ref[...]
    seg = bd_ref[...] * (1.0 / HD)
    xc = o - _dot(o, seg, HIGHEST)
    var = _dot(xc * xc, seg, HIGHEST)
    o_ref[...] = xc * lax.rsqrt(var + EPS)


def _retention_prompt(q, k, v):
    t, w = q.shape
    c = math.gcd(t, RET_CHUNK)
    d_in, q_dec, k_dec, cmat, bd = _retention_consts(c)
    row = pl.BlockSpec((c, w), lambda i: (i, 0))
    const2 = lambda shape: pl.BlockSpec(shape, lambda i: (0,) * len(shape))
    return pl.pallas_call(
        _retention_body,
        grid=(t // c,),
        in_specs=[row, row, row, const2((RET_H, c, c)), const2((c, w)), const2((c, w)), const2((w, w)), const2((w, w))],
        out_specs=[row, const2((w, w))],
        out_shape=[jax.ShapeDtypeStruct((t, w), F32), jax.ShapeDtypeStruct((w, w), F32)],
        compiler_params=_params("arbitrary"),
        name="retention_prompt",
    )(q, k, v, d_in, q_dec, k_dec, cmat, bd)


def _diag_blocks(s, h):
    s4 = s.reshape(h, HD, h, HD)
    return jnp.stack([s4[i, :, i, :] for i in range(h)], axis=0)


def _block_mean_body(k_ref, o_ref):
    o_ref[0] = jnp.mean(k_ref[...], axis=0, keepdims=True)


def _block_mean(k, blk):
    t, w = k.shape
    assert t % blk == 0
    out = pl.pallas_call(
        _block_mean_body,
        grid=(t // blk,),
        in_specs=[pl.BlockSpec((blk, w), lambda i: (i, 0))],
        out_specs=pl.BlockSpec((1, 1, w), lambda i: (i, 0, 0)),
        out_shape=jax.ShapeDtypeStruct((t // blk, 1, w), F32),
        compiler_params=_params("parallel"),
        name="block_mean",
    )(k)
    return out.reshape(t // blk, w)


def _topk_mask(score, k):
    n = score.shape[-1]
    idx = lax.broadcasted_iota(jnp.int32, score.shape, score.ndim - 1)
    sel = jnp.zeros(score.shape, F32)
    work = score
    for _ in range(k):
        m = jnp.max(work, axis=-1, keepdims=True)
        first = jnp.min(jnp.where(work == m, idx, n), axis=-1, keepdims=True)
        pick = idx == first
        sel = jnp.where(pick, 1.0, sel)
        work = jnp.where(pick, -jnp.inf, work)
    return sel


def _moba_prompt_body(q_ref, kv_ref, kmean_ref, o_ref):
    qi = pl.program_id(1)
    q = q_ref[0]
    tq = q.shape[0]
    nblk = kmean_ref.shape[1]
    gate = _dot_nt(q, kmean_ref[0], HIGHEST)
    blk = lax.broadcasted_iota(jnp.int32, (tq, nblk), 1)
    past = blk < qi
    sel = jnp.where(past, _topk_mask(jnp.where(past, gate, NEG), min(MOBA_TOPK, nblk)), 0.0)
    qb = (q * SCALE).astype(BF16)

    def attend(j, mask, carry):
        m, l, acc = carry
        start = pl.multiple_of(j * tq, tq)
        kvj = kv_ref[0, pl.ds(start, tq), :]
        s = jnp.where(mask, _dot_nt(qb, kvj), NEG)
        m_new = jnp.maximum(m, jnp.max(s, axis=-1, keepdims=True))
        alpha = jnp.exp(m - m_new)
        p = jnp.where(mask, jnp.exp(s - m_new), 0.0)
        l = alpha * l + jnp.sum(p, axis=-1, keepdims=True)
        acc = alpha * acc + _dot(p.astype(BF16), kvj)
        return m_new, l, acc

    def past_block(j, carry):
        col = jnp.sum(jnp.where(blk == j, sel, 0.0), axis=-1, keepdims=True)
        return attend(j, col > 0.5, carry)

    init = (jnp.full((tq, 1), NEG, F32), jnp.zeros((tq, 1), F32), jnp.zeros((tq, LANES), F32))
    carry = lax.fori_loop(0, qi, past_block, init)
    r = lax.broadcasted_iota(jnp.int32, (tq, tq), 0)
    c = lax.broadcasted_iota(jnp.int32, (tq, tq), 1)
    m, l, acc = attend(qi, c <= r, carry)
    o_ref[0] = acc / jnp.maximum(l, TINY)


def _moba_prompt(q_pad, kv_bf16, kmean_pad):
    h, t, _ = q_pad.shape
    assert t % MOBA_BLOCK == 0
    nblk = t // MOBA_BLOCK
    return pl.pallas_call(
        _moba_prompt_body,
        grid=(h, nblk),
        in_specs=[
            pl.BlockSpec((1, MOBA_BLOCK, LANES), lambda a, i: (a, i, 0)),
            pl.BlockSpec((1, t, LANES), lambda a, i: (a, 0, 0)),
            pl.BlockSpec((1, nblk, LANES), lambda a, i: (a, 0, 0)),
        ],
        out_specs=pl.BlockSpec((1, MOBA_BLOCK, LANES), lambda a, i: (a, i, 0)),
        out_shape=jax.ShapeDtypeStruct((h, t, LANES), F32),
        compiler_params=_params("parallel", "arbitrary"),
        name="moba_prompt",
    )(q_pad, kv_bf16, kmean_pad)


def _mem_prompt_body(q_ref, k_ref, v_ref, o_ref):
    q = q_ref[...]
    kb = k_ref[...].astype(BF16)
    vb = v_ref[...].astype(BF16)
    head = lax.broadcasted_iota(jnp.int32, q.shape, 1) // HD
    o = jnp.zeros(q.shape, F32)
    for h in range(MEM_H):
        mh = head == h
        s = _dot_nt(jnp.where(mh, q, 0.0).astype(BF16), kb) * SCALE
        e = jnp.exp(s - jnp.max(s, axis=-1, keepdims=True))
        p = e / jnp.sum(e, axis=-1, keepdims=True)
        o = o + jnp.where(mh, _dot(p.astype(BF16), vb), 0.0)
    o_ref[...] = o


def _mem_prompt(q, k, v, tm):
    t, w = q.shape
    n = k.shape[0]
    return pl.pallas_call(
        _mem_prompt_body,
        grid=(t // tm,),
        in_specs=[pl.BlockSpec((tm, w), lambda i: (i, 0)), pl.BlockSpec((n, w), lambda i: (0, 0)),
                  pl.BlockSpec((n, w), lambda i: (0, 0))],
        out_specs=pl.BlockSpec((tm, w), lambda i: (i, 0)),
        out_shape=jax.ShapeDtypeStruct((t, w), F32),
        compiler_params=_params("parallel"),
        name="mem_prompt",
    )(q, k, v)


def _mix_out_body(x_ref, a_ref, b_ref, c_ref, gate_ref, wo_ref, fg_ref, o_ref, *, final_norm):
    gate = gate_ref[...]
    gate = gate * jax.nn.sigmoid(gate)
    y = x_ref[...]
    off = 0
    for ref in (a_ref, b_ref, c_ref):
        w = ref.shape[1]
        mix = (ref[...] * gate[:, off:off + w]).astype(BF16)
        y = y + _dot(mix, wo_ref[off:off + w, :])
        off += w
    if final_norm:
        ms = jnp.mean(y * y, axis=-1, keepdims=True)
        y = y * lax.rsqrt(ms + EPS) * fg_ref[...]
    o_ref[...] = y


def _mix_out(x, a, b, c, gate, wo_bf16, final_g, tm):
    m, d = x.shape
    row = lambda w: pl.BlockSpec((tm, w), lambda i: (i, 0))
    fg = jnp.ones((1, d), F32) if final_g is None else final_g.reshape(1, d).astype(F32)
    body = functools.partial(_mix_out_body, final_norm=final_g is not None)
    return pl.pallas_call(
        body,
        grid=(m // tm,),
        in_specs=[row(d), row(a.shape[1]), row(b.shape[1]), row(c.shape[1]), row(gate.shape[1]),
                  pl.BlockSpec(wo_bf16.shape, lambda i: (0, 0)), pl.BlockSpec((1, d), lambda i: (0, 0))],
        out_specs=row(d),
        out_shape=jax.ShapeDtypeStruct((m, d), F32),
        compiler_params=_params("parallel"),
        name="mix_out",
    )(x, a, b, c, gate, wo_bf16, fg)


def _compress_body(x_ref, pe_ref, w1_ref, w2_ref, o_ref):
    x = (x_ref[0] + pe_ref[0]).astype(BF16)
    hid = jax.nn.gelu(_dot(x, w1_ref[0]))
    o_ref[0] = _dot(hid.astype(BF16), w2_ref[0])


def _compress_weights(w1, w2, pe):
    eye = jnp.eye(NSA_KVH, dtype=F32)
    w1r = w1.astype(F32).reshape(2, NSA_BLOCK, HD, HD)
    w1b = jnp.einsum("klde,gh->klgdhe", w1r, eye).reshape(2, NSA_BLOCK * NSA_KVW, NSA_KVW)
    w2b = jnp.einsum("kde,gh->kgdhe", w2.astype(F32), eye).reshape(2, NSA_KVW, NSA_KVW)
    peb = jnp.broadcast_to(pe.astype(F32)[:, :, None, :], (2, NSA_BLOCK, NSA_KVH, HD)).reshape(2, 1, NSA_BLOCK * NSA_KVW)
    return w1b.astype(BF16), w2b.astype(BF16), peb


def _compress(x, w1b, w2b, peb):
    _, nb, kdim = x.shape
    return pl.pallas_call(
        _compress_body,
        grid=(2,),
        in_specs=[pl.BlockSpec((1, nb, kdim), lambda i: (i, 0, 0)), pl.BlockSpec((1, 1, kdim), lambda i: (i, 0, 0)),
                  pl.BlockSpec((1, kdim, NSA_KVW), lambda i: (i, 0, 0)),
                  pl.BlockSpec((1, NSA_KVW, NSA_KVW), lambda i: (i, 0, 0))],
        out_specs=pl.BlockSpec((1, nb, NSA_KVW), lambda i: (i, 0, 0)),
        out_shape=jax.ShapeDtypeStruct((2, nb, NSA_KVW), F32),
        compiler_params=_params("parallel"),
        name="nsa_compress",
    )(x, peb, w1b, w2b)


def _softmax_step(s, mask, kv, carry):
    m, l, acc = carry
    s = jnp.where(mask, s, NEG)
    m_new = jnp.maximum(m, jnp.max(s, axis=-1, keepdims=True))
    alpha = jnp.exp(m - m_new)
    p = jnp.where(mask, jnp.exp(s - m_new), 0.0)
    l = alpha * l + jnp.sum(p, axis=-1, keepdims=True)
    acc = alpha * acc + _dot(p.astype(BF16), kv)
    return m_new, l, acc


def _nsa_prompt_body(q_ref, qr_ref, gt_ref, ckv_ref, skv_ref, wkv_ref, o_ref, *, tq, kc):
    i = pl.program_id(1)
    t0 = i * tq
    rows = q_ref.shape[2]
    nb = ckv_ref.shape[1]
    q = q_ref[0, 0]
    qrb = (qr_ref[0, 0] * SCALE).astype(BF16)
    tpos = t0 + lax.broadcasted_iota(jnp.int32, (rows, 1), 0) % tq

    ckv = ckv_ref[0].astype(BF16)
    blk = lax.broadcasted_iota(jnp.int32, (rows, nb), 1)
    complete = blk * NSA_BLOCK + (NSA_BLOCK - 1) <= tpos
    s_c = jnp.where(complete, _dot_nt(q.astype(BF16), ckv) * SCALE, NEG)
    e = jnp.where(complete, jnp.exp(s_c - jnp.max(s_c, axis=-1, keepdims=True)), 0.0)
    pc = e / jnp.maximum(jnp.sum(e, axis=-1, keepdims=True), TINY)
    o_c = _dot(pc.astype(BF16), ckv)

    imp = pc[0:tq]
    for r in range(1, NSA_REP):
        imp = imp + pc[r * tq:(r + 1) * tq]
    blk_q = lax.broadcasted_iota(jnp.int32, (tq, nb), 1)
    tpos_q = t0 + lax.broadcasted_iota(jnp.int32, (tq, 1), 0)
    cur = tpos_q // NSA_BLOCK
    forced = (blk_q == 0) | (blk_q == cur) | (blk_q == cur - 1)
    complete_q = blk_q * NSA_BLOCK + (NSA_BLOCK - 1) <= tpos_q
    score = jnp.where(forced, FORCE, jnp.where(complete_q, imp, NEG))
    sel = jnp.where(score > 0.5 * NEG, _topk_mask(score, min(NSA_TOPN, nb)), 0.0).astype(BF16)

    init = (jnp.full((rows, 1), NEG, F32), jnp.zeros((rows, 1), F32), jnp.zeros((rows, LANES), F32))

    def sel_chunk(c, carry):
        start = pl.multiple_of(c * kc, kc)
        kv = skv_ref[0, pl.ds(start, kc), :]
        eb = lax.broadcasted_iota(jnp.int32, (nb, kc), 0)
        ek = lax.broadcasted_iota(jnp.int32, (nb, kc), 1)
        expand = jnp.where(eb == c * (kc // NSA_BLOCK) + ek // NSA_BLOCK, 1.0, 0.0).astype(BF16)
        selx = _dot(sel, expand)
        selx = jnp.concatenate([selx] * NSA_REP, axis=0)
        kpos = start + lax.broadcasted_iota(jnp.int32, (rows, kc), 1)
        mask = (selx > 0.5) & (kpos <= tpos)
        return _softmax_step(_dot_nt(qrb, kv), mask, kv, carry)

    n_chunks = (t0 + tq + kc - 1) // kc
    _, l_s, acc_s = lax.fori_loop(0, n_chunks, sel_chunk, init)
    o_s = acc_s / jnp.maximum(l_s, TINY)

    def win_chunk(c, carry):
        start = pl.multiple_of(c * tq, tq)
        kv = wkv_ref[0, pl.ds(start, tq), :]
        kpos = start + lax.broadcasted_iota(jnp.int32, (rows, tq), 1)
        mask = (kpos <= tpos) & (kpos > tpos - NSA_WINDOW)
        return _softmax_step(_dot_nt(qrb, kv), mask, kv, carry)

    first = jnp.maximum(i - (NSA_WINDOW + tq - 1) // tq, 0)
    _, l_w, acc_w = lax.fori_loop(first, i + 1, win_chunk, init)
    o_w = acc_w / jnp.maximum(l_w, TINY)

    gt = jax.nn.sigmoid(gt_ref[0, 0])
    o_ref[0, 0] = gt[:, 0:1] * o_c + gt[:, 1:2] * o_s + gt[:, 2:3] * o_w


def _nsa_prompt(q_st, qr_st, gt_st, ckv, skv, wkv, tq, kc):
    g, nt, rows, _ = q_st.shape
    t = skv.shape[1]
    nb = ckv.shape[1]
    assert t % kc == 0 and kc % NSA_BLOCK == 0 and t % tq == 0
    qspec = pl.BlockSpec((1, 1, rows, LANES), lambda a, i: (a, i, 0, 0))
    body = functools.partial(_nsa_prompt_body, tq=tq, kc=kc)
    return pl.pallas_call(
        body,
        grid=(g, nt),
        in_specs=[qspec, qspec, pl.BlockSpec((1, 1, rows, 3), lambda a, i: (a, i, 0, 0)),
                  pl.BlockSpec((1, nb, LANES), lambda a, i: (a, 0, 0)),
                  pl.BlockSpec((1, t, LANES), lambda a, i: (a, 0, 0)),
                  pl.BlockSpec((1, t, LANES), lambda a, i: (a, 0, 0))],
        out_specs=qspec,
        out_shape=jax.ShapeDtypeStruct((g, nt, rows, LANES), F32),
        compiler_params=_params("parallel", "parallel"),
        name="nsa_prompt",
    )(q_st, qr_st, gt_st, ckv, skv, wkv)


def _hgrn_body(q_ref, f_ref, i_ref, lb_ref, o_ref, s_ref, cum_ref, k_ref):
    @pl.when(pl.program_id(0) == 0)
    def _():
        s_ref[...] = jnp.zeros_like(s_ref)

    q = q_ref[...]
    v = i_ref[...]
    c, w = q.shape
    lb = lb_ref[...]
    f = lb + (1.0 - lb) * jax.nn.sigmoid(f_ref[...])
    kk = 1.0 - f
    tr = lax.broadcasted_iota(jnp.int32, (c, c), 0)
    tc = lax.broadcasted_iota(jnp.int32, (c, c), 1)
    cum = _dot(jnp.where(tr >= tc, 1.0, 0.0), jnp.log(f), HIGHEST)
    cum_ref[...] = cum
    k_ref[...] = kk
    bd = _block_ones(w)
    bd_b = bd.astype(BF16)
    s = s_ref[...]
    o = _dot(q * jnp.exp(cum), s, HIGHEST)
    t_idx = lax.broadcasted_iota(jnp.int32, (c, w), 0)

    def intra(j, o):
        cs = cum_ref[pl.ds(j, 1), :]
        e = jnp.where(t_idx >= j, jnp.exp(jnp.minimum(cum - cs, 0.0)) * q * k_ref[pl.ds(j, 1), :], 0.0)
        e_hi = e.astype(BF16)
        e_lo = (e - e_hi.astype(F32)).astype(BF16)
        a = _dot(e_hi, bd_b) + _dot(e_lo, bd_b)
        return o + a * i_ref[pl.ds(j, 1), :]

    o = lax.fori_loop(0, c, intra, o)
    last = cum[c - 1:c, :]
    row0 = t_idx == 0
    scale_mat = _dot_tn(jnp.where(row0, jnp.exp(last), 0.0), jnp.where(row0, 1.0, 0.0), HIGHEST)
    s_ref[...] = s * scale_mat + _dot_tn(kk * jnp.exp(last - cum), v, HIGHEST) * bd
    ms = _dot(o * o, bd * (1.0 / HD), HIGHEST)
    o_ref[...] = o * lax.rsqrt(ms + EPS)


def _hgrn_prompt(q, f_logit, i, lb):
    t, w = q.shape
    c = math.gcd(t, HGRN_CHUNK)
    row = pl.BlockSpec((c, w), lambda n: (n, 0))
    return pl.pallas_call(
        _hgrn_body,
        grid=(t // c,),
        in_specs=[row, row, row, pl.BlockSpec((1, w), lambda n: (0, 0))],
        out_specs=[row, pl.BlockSpec((w, w), lambda n: (0, 0))],
        out_shape=[jax.ShapeDtypeStruct((t, w), F32), jax.ShapeDtypeStruct((w, w), F32)],
        scratch_shapes=[pltpu.VMEM((c, w), F32), pltpu.VMEM((c, w), F32)],
        compiler_params=_params("arbitrary"),
        name="hgrn_prompt",
    )(q, f_logit, i, lb)


def _state_step_body(s_ref, q_ref, a_ref, b_ref, v_ref, o_ref, so_ref, *, mode):
    s = s_ref[...]
    if mode == "ret":
        k = a_ref[...] * SCALE
        dec = b_ref[...]
    else:
        lb = b_ref[...]
        dec = lb + (1.0 - lb) * jax.nn.sigmoid(a_ref[...])
        k = 1.0 - dec
    kv = k * v_ref[...]
    rows = s.shape[0]
    o = (q_ref[...] * (kv + dec * s)).reshape(rows // HD, HD, HD).sum(axis=1)
    so_ref[...] = dec * s + kv
    if mode == "ret":
        xc = o - jnp.mean(o, axis=-1, keepdims=True)
        o_ref[...] = xc * lax.rsqrt(jnp.mean(xc * xc, axis=-1, keepdims=True) + EPS)
    else:
        o_ref[...] = o * lax.rsqrt(jnp.mean(o * o, axis=-1, keepdims=True) + EPS)


def _state_step(state, q, a, b, v, mode):
    bsz, h = state.shape[0], state.shape[1]
    rows = bsz * h * HD
    col = lambda x: jnp.broadcast_to(x.reshape(bsz, h, HD, 1), (bsz, h, HD, HD)).reshape(rows, HD)
    vx = jnp.broadcast_to(v.reshape(bsz, h, 1, HD), (bsz, h, HD, HD)).reshape(rows, HD)
    bb = 8 if bsz % 8 == 0 else bsz
    br = bb * h * HD
    spec = pl.BlockSpec((br, HD), lambda i: (i, 0))
    ospec = pl.BlockSpec((br // HD, HD), lambda i: (i, 0))
    o, s_new = pl.pallas_call(
        functools.partial(_state_step_body, mode=mode),
        grid=(rows // br,),
        in_specs=[spec] * 5,
        out_specs=[ospec, spec],
        out_shape=[jax.ShapeDtypeStruct((rows // HD, HD), F32), jax.ShapeDtypeStruct((rows, HD), F32)],
        compiler_params=_params("parallel"),
        name="state_step_" + mode,
    )(state.reshape(rows, HD).astype(F32), col(q), col(a), col(b), vx)
    return o.reshape(bsz, h * HD), s_new.reshape(bsz, h, HD, HD)


def _expand_heads(p, width):
    g, n = p.shape
    return jnp.broadcast_to(p[:, None, :], (g, HD, n)).reshape(width, n)


def _head_sums(x, g):
    return x.reshape(g, HD, x.shape[-1]).sum(axis=1)


def _mem_sample_body(q_ref, kt_ref, vt_ref, o_ref):
    kt = kt_ref[0, 0]
    vt = vt_ref[0, 0]
    s = _head_sums(kt * q_ref[0], MEM_H) * SCALE
    e = jnp.exp(s - jnp.max(s, axis=-1, keepdims=True))
    p = e / jnp.sum(e, axis=-1, keepdims=True)
    o_ref[0] = jnp.sum(vt * _expand_heads(p, MEM_W), axis=-1, keepdims=True)


def _mem_sample(q, kvt):
    bsz, w = q.shape
    n = kvt.shape[-1]
    out = pl.pallas_call(
        _mem_sample_body,
        grid=(bsz,),
        in_specs=[pl.BlockSpec((1, w, 1), lambda i: (i, 0, 0)),
                  pl.BlockSpec((1, 1, w, n), lambda i: (i, 0, 0, 0)),
                  pl.BlockSpec((1, 1, w, n), lambda i: (i, 1, 0, 0))],
        out_specs=pl.BlockSpec((1, w, 1), lambda i: (i, 0, 0)),
        out_shape=jax.ShapeDtypeStruct((bsz, w, 1), F32),
        compiler_params=_params("parallel"),
        name="mem_sample",
    )(q.reshape(bsz, w, 1), kvt, kvt)
    return out.reshape(bsz, w)


def _topk_mask_axis0(score, k):
    n = score.shape[0]
    idx = lax.broadcasted_iota(jnp.int32, score.shape, 0)
    sel = jnp.zeros(score.shape, F32)
    work = score
    for _ in range(k):
        m = jnp.max(work, axis=0, keepdims=True)
        first = jnp.min(jnp.where(work == m, idx, n), axis=0, keepdims=True)
        pick = idx == first
        sel = jnp.where(pick, 1.0, sel)
        work = jnp.where(pick, -jnp.inf, work)
    return sel


def _paged_attn_body(pt_ref, pool_ref, qc_ref, kn_ref, vn_ref, mask_ref, o_ref, buf, sem, s_ref, acc_ref, st_ref,
                     *, layer, groups, reps, chunk, moba):
    b = pl.program_id(0)
    n_pages = pt_ref.shape[1]
    n = n_pages // chunk
    gw = groups * HD

    def copy(kv, c, i, slot):
        return pltpu.make_async_copy(pool_ref.at[pt_ref[b, c * chunk + i], layer, kv], buf.at[slot, i], sem.at[slot])

    def start(kv, c, slot):
        lax.fori_loop(0, chunk, lambda i, _: (copy(kv, c, i, slot).start(), 0)[1], 0)

    def wait(kv, c, slot):
        lax.fori_loop(0, chunk, lambda i, _: (copy(kv, c, i, slot).wait(), 0)[1], 0)

    acc_ref[...] = jnp.zeros_like(acc_ref)
    start(0, 0, 0)

    def k_compute(c, slot):
        def page(i, _):
            kt = buf[slot, i]
            for r in range(reps):
                s_ref[r, c * chunk + i] = _head_sums(kt * qc_ref[0, :, r:r + 1], groups)
            return 0
        lax.fori_loop(0, chunk, page, 0)

    def select():
        if moba:
            per = MOBA_BLOCK // PAGE
            nblk = n_pages // per
            sc = s_ref[0].reshape(nblk, per, groups, PAGE)
            gate = jnp.sum(jnp.sum(sc, axis=1, keepdims=True), axis=-1, keepdims=True) * (1.0 / MOBA_BLOCK)
            sel = _topk_mask_axis0(gate, min(MOBA_TOPK, nblk + 1))
            mask = jnp.broadcast_to(sel, (nblk, per, groups, PAGE)).reshape(n_pages, groups, PAGE) > 0.5
        else:
            mask = mask_ref[0] > 0.5
        for r in range(reps):
            qcol = qc_ref[0, :, r:r + 1]
            s_own = _head_sums(jnp.broadcast_to(kn_ref[0] * qcol, (gw, PAGE)), groups) * SCALE
            s = jnp.where(mask, s_ref[r] * SCALE, NEG)
            m = jnp.max(jnp.max(s, axis=0), axis=-1, keepdims=True)
            m = jnp.maximum(m, s_own)
            p = jnp.where(mask, jnp.exp(s - m), 0.0)
            p_own = jnp.exp(s_own - m)
            den = jnp.sum(jnp.sum(p, axis=0), axis=-1, keepdims=True) + p_own
            s_ref[r] = p
            st_ref[r, 0] = p_own
            st_ref[r, 1] = jnp.maximum(den, TINY)

    def v_compute(c, slot):
        def page(i, _):
            vt = buf[slot, i]
            for r in range(reps):
                acc_ref[r] += vt * _expand_heads(s_ref[r, c * chunk + i], gw)
            return 0
        lax.fori_loop(0, chunk, page, 0)

    def step(idx, _):
        slot = idx % 2
        nxt = idx + 1

        @pl.when(nxt < 2 * n)
        def _():
            start(nxt // n, nxt % n, 1 - slot)

        wait(idx // n, idx % n, slot)

        @pl.when(idx < n)
        def _():
            k_compute(idx, slot)

        @pl.when(idx == n - 1)
        def _():
            select()

        @pl.when(idx >= n)
        def _():
            v_compute(idx - n, slot)

        return 0

    lax.fori_loop(0, 2 * n, step, 0)
    for r in range(reps):
        tot = jnp.sum(acc_ref[r], axis=-1, keepdims=True)
        p_own = _expand_heads(st_ref[r, 0], gw)[:, 0:1]
        den = _expand_heads(st_ref[r, 1], gw)[:, 0:1]
        o_ref[0, :, r:r + 1] = (tot + p_own * vn_ref[0]) / den


def _paged_attn(page_table, pool_t, layer, q_cols, k_new, v_new, row_mask, groups, reps, moba):
    bsz, n_pages = page_table.shape
    gw = groups * HD
    chunk = math.gcd(n_pages, 16)
    if row_mask is None:
        row_mask = jnp.zeros((bsz, 1, groups, PAGE), F32)
    mshape = row_mask.shape[1:]
    body = functools.partial(_paged_attn_body, layer=layer, groups=groups, reps=reps, chunk=chunk, moba=moba)
    grid_spec = pltpu.PrefetchScalarGridSpec(
        num_scalar_prefetch=1,
        grid=(bsz,),
        in_specs=[
            pl.BlockSpec(memory_space=pl.ANY),
            pl.BlockSpec((1, gw, reps), lambda i, pt: (i, 0, 0)),
            pl.BlockSpec((1, gw, 1), lambda i, pt: (i, 0, 0)),
            pl.BlockSpec((1, gw, 1), lambda i, pt: (i, 0, 0)),
            pl.BlockSpec((1,) + mshape, lambda i, pt: (i, 0, 0, 0)),
        ],
        out_specs=pl.BlockSpec((1, gw, reps), lambda i, pt: (i, 0, 0)),
        scratch_shapes=[
            pltpu.VMEM((2, chunk, gw, PAGE), F32),
            pltpu.SemaphoreType.DMA((2,)),
            pltpu.VMEM((reps, n_pages, groups, PAGE), F32),
            pltpu.VMEM((reps, gw, PAGE), F32),
            pltpu.VMEM((reps, 2, groups, PAGE), F32),
        ],
    )
    return pl.pallas_call(
        body,
        grid_spec=grid_spec,
        out_shape=jax.ShapeDtypeStruct((bsz, gw, reps), F32),
        compiler_params=_params("arbitrary"),
        name="paged_attn_moba" if moba else "paged_attn_nsa",
    )(page_table, pool_t, q_cols, k_new.reshape(bsz, gw, 1), v_new.reshape(bsz, gw, 1), row_mask)


def _compress_weights_t(w1, pe):
    per = PAGE // NSA_BLOCK
    eye = jnp.eye(per, dtype=F32)
    w1r = w1.astype(F32).reshape(2, NSA_BLOCK, HD, HD)
    wd = jnp.einsum("klde,gh->kdglhe", w1r, eye).reshape(2, HD, PAGE, per * HD)
    ped = jnp.tile(pe.astype(F32).transpose(0, 2, 1), (1, 1, per)).reshape(2, HD, 1, PAGE)
    return wd.astype(BF16), ped


def _nsa_cmp_sample_body(pt_ref, pool_ref, q_ref, wd_ref, ped_ref, w2_ref, oc_ref, sel_ref, buf, sem, tok_ref,
                         *, layer):
    b = pl.program_id(0)
    j = pl.program_id(1)
    bsz, n_pages = pt_ref.shape
    t = b * 2 + j
    slot = t % 2

    def copy(bb, kv, p, g, sl):
        row = pl.multiple_of(((sl * NSA_KVH + g) * n_pages + p) * HD, HD)
        return pltpu.make_async_copy(pool_ref.at[pt_ref[bb, p], layer, kv, g], buf.at[pl.ds(row, HD)], sem.at[sl])

    def each(fn):
        def run(bb, kv, sl):
            def page(p, _):
                for g in range(NSA_KVH):
                    fn(copy(bb, kv, p, g, sl))
                return 0
            lax.fori_loop(0, n_pages, page, 0)
        return run

    start = each(lambda c: c.start())
    wait = each(lambda c: c.wait())

    @pl.when(t == 0)
    def _():
        start(0, 0, 0)

    @pl.when(t + 1 < 2 * bsz)
    def _():
        start((t + 1) // 2, (t + 1) % 2, 1 - slot)

    wait(b, j, slot)

    for g in range(NSA_KVH):
        base = (slot * NSA_KVH + g) * n_pages * HD

        def dstep(d, acc):
            a = buf[pl.ds(base + d, n_pages, stride=HD), :] + ped_ref[0, d]
            return acc + _dot(a.astype(BF16), wd_ref[0, d])

        acc = lax.fori_loop(0, HD, dstep, jnp.zeros((n_pages, PAGE), F32))
        tok_ref[j, g] = _dot(jax.nn.gelu(acc).astype(BF16), w2_ref[0])

    @pl.when(j == 1)
    def _():
        pos = n_pages * PAGE
        per = PAGE // NSA_BLOCK
        n_idx = (lax.broadcasted_iota(jnp.int32, (per, n_pages), 1) * per
                 + lax.broadcasted_iota(jnp.int32, (per, n_pages), 0))
        complete = n_idx * NSA_BLOCK + (NSA_BLOCK - 1) <= pos
        cur = pos // NSA_BLOCK
        forced = (n_idx == 0) | (n_idx == cur) | (n_idx == cur - 1)
        k_past = min(NSA_TOPN, per * n_pages + 1) - 1
        for g in range(NSA_KVH):
            qg = q_ref[0, g * NSA_REP:(g + 1) * NSA_REP, :].astype(BF16)
            ck = tok_ref[0, g].astype(BF16)
            cv = tok_ref[1, g].astype(BF16)
            s = [jnp.where(complete[h:h + 1], _dot_nt(qg, ck[:, h * HD:(h + 1) * HD]) * SCALE, NEG) for h in range(per)]
            m = functools.reduce(jnp.maximum, [jnp.max(x, axis=-1, keepdims=True) for x in s])
            e = [jnp.where(complete[h:h + 1], jnp.exp(s[h] - m), 0.0) for h in range(per)]
            den = jnp.maximum(sum(jnp.sum(x, axis=-1, keepdims=True) for x in e), TINY)
            pc = [x / den for x in e]
            oc_ref[0, g * NSA_REP:(g + 1) * NSA_REP, :] = sum(
                _dot(pc[h].astype(BF16), cv[:, h * HD:(h + 1) * HD]) for h in range(per))
            imp = jnp.concatenate([jnp.sum(x, axis=0, keepdims=True) for x in pc], axis=0)
            score = jnp.where(forced, FORCE, jnp.where(complete, imp, NEG))
            sel = jnp.zeros(score.shape, F32)
            work = score
            for _ in range(k_past):
                mx = jnp.max(jnp.max(work, axis=-1, keepdims=True), axis=0, keepdims=True)
                cand = jnp.where(work == mx, n_idx, per * n_pages)
                first = jnp.min(jnp.min(cand, axis=-1, keepdims=True), axis=0, keepdims=True)
                pick = n_idx == first
                sel = jnp.where(pick, 1.0, sel)
                work = jnp.where(pick, -jnp.inf, work)
            sel_ref[0, g] = jnp.where(score > 0.5 * NEG, sel, 0.0)


def _nsa_cmp_sample(page_table, pool_t, layer, q, wd, ped, w2b):
    bsz, n_pages = page_table.shape
    per = PAGE // NSA_BLOCK
    grid_spec = pltpu.PrefetchScalarGridSpec(
        num_scalar_prefetch=1,
        grid=(bsz, 2),
        in_specs=[
            pl.BlockSpec(memory_space=pl.ANY),
            pl.BlockSpec((1, NSA_H, HD), lambda i, j, pt: (i, 0, 0)),
            pl.BlockSpec((1, HD, PAGE, per * HD), lambda i, j, pt: (j, 0, 0, 0)),
            pl.BlockSpec((1, HD, 1, PAGE), lambda i, j, pt: (j, 0, 0, 0)),
            pl.BlockSpec((1, per * HD, per * HD), lambda i, j, pt: (j, 0, 0)),
        ],
        out_specs=[pl.BlockSpec((1, NSA_H, HD), lambda i, j, pt: (i, 0, 0)),
                   pl.BlockSpec((1, NSA_KVH, per, n_pages), lambda i, j, pt: (i, 0, 0, 0))],
        scratch_shapes=[
            pltpu.VMEM((2 * NSA_KVH * n_pages * HD, PAGE), F32),
            pltpu.SemaphoreType.DMA((2,)),
            pltpu.VMEM((2, NSA_KVH, n_pages, per * HD), F32),
        ],
    )
    return pl.pallas_call(
        functools.partial(_nsa_cmp_sample_body, layer=layer),
        grid_spec=grid_spec,
        out_shape=[jax.ShapeDtypeStruct((bsz, NSA_H, HD), F32),
                   jax.ShapeDtypeStruct((bsz, NSA_KVH, per, n_pages), F32)],
        compiler_params=_params("arbitrary", "arbitrary"),
        name="nsa_cmp_sample",
    )(page_table, pool_t, q, wd, ped, w2b)


def _win_sample_body(qc_ref, kt_ref, vt_ref, kn_ref, vn_ref, o_ref):
    kt = kt_ref[0, 0]
    vt = vt_ref[0, 0]
    gw, lbuf = kt.shape
    j = lax.broadcasted_iota(jnp.int32, (NSA_KVH, lbuf), 1)
    mask = j > lbuf - NSA_WINDOW
    for r in range(NSA_REP):
        qcol = qc_ref[0, :, r:r + 1]
        s = jnp.where(mask, _head_sums(kt * qcol, NSA_KVH) * SCALE, NEG)
        s_own = _head_sums(jnp.broadcast_to(kn_ref[0] * qcol, (gw, lbuf)), NSA_KVH) * SCALE
        m = jnp.maximum(jnp.max(s, axis=-1, keepdims=True), s_own)
        p = jnp.where(mask, jnp.exp(s - m), 0.0)
        p_own = jnp.exp(s_own - m)
        den = jnp.maximum(jnp.sum(p, axis=-1, keepdims=True) + p_own, TINY)
        tot = jnp.sum(vt * _expand_heads(p, gw), axis=-1, keepdims=True)
        o_ref[0, :, r:r + 1] = (tot + _expand_heads(p_own, gw)[:, 0:1] * vn_ref[0]) / _expand_heads(den, gw)[:, 0:1]


def _win_sample(q_cols, wbuf_t, k_new, v_new):
    bsz, gw, reps = q_cols.shape
    lbuf = wbuf_t.shape[-1]
    col = pl.BlockSpec((1, gw, 1), lambda i: (i, 0, 0))
    return pl.pallas_call(
        _win_sample_body,
        grid=(bsz,),
        in_specs=[pl.BlockSpec((1, gw, reps), lambda i: (i, 0, 0)),
                  pl.BlockSpec((1, 1, gw, lbuf), lambda i: (i, 0, 0, 0)),
                  pl.BlockSpec((1, 1, gw, lbuf), lambda i: (i, 1, 0, 0)), col, col],
        out_specs=pl.BlockSpec((1, gw, reps), lambda i: (i, 0, 0)),
        out_shape=jax.ShapeDtypeStruct((bsz, gw, reps), F32),
        compiler_params=_params("parallel"),
        name="nsa_win_sample",
    )(q_cols, wbuf_t, wbuf_t, k_new.reshape(bsz, gw, 1), v_new.reshape(bsz, gw, 1))


def _gate_combine_body(g_ref, c_ref, s_ref, w_ref, o_ref):
    gt = jax.nn.sigmoid(g_ref[...])
    o_ref[...] = gt[0] * c_ref[...] + gt[1] * s_ref[...] + gt[2] * w_ref[...]


def _gate_combine(gate_logits, o_c, o_s, o_w):
    return pl.pallas_call(
        _gate_combine_body,
        out_shape=jax.ShapeDtypeStruct(o_c.shape, F32),
        name="nsa_gate_combine",
    )(gate_logits, o_c, o_s, o_w)


ODD_NQ, ODD_NQR, ODD_KV = 0, NSA_QW, 2 * NSA_QW
ODD_HQ = ODD_KV + 6 * NSA_KVW
ODD_EQ = ODD_HQ + 3 * HGRN_W
ODD_GATE = ODD_EQ + MEM_W
ODD_NG = ODD_GATE + MIX_W
ODD_N = 4096


def _odd_weights(w_in):
    offs = np.cumsum([0, NSA_QW] + [NSA_KVW] * 6 + [3 * NSA_H] + [HGRN_W] * 3 + [MEM_W, MIX_W])
    nq = w_in[:, offs[0]:offs[1]]
    kv = w_in[:, offs[1]:offs[7]]
    ng = w_in[:, offs[7]:offs[8]]
    rest = w_in[:, offs[8]:]
    pad = jnp.zeros((w_in.shape[0], ODD_N - ODD_NG - 3 * NSA_H), w_in.dtype)
    return jnp.concatenate([nq, nq, kv, rest, ng, pad], axis=1)


def _stack_heads(a, tq, width):
    t = a.shape[0]
    a = a.reshape(t // tq, tq, NSA_KVH, NSA_REP, width).transpose(2, 0, 3, 1, 4)
    return a.reshape(NSA_KVH, t // tq, NSA_REP * tq, width)


def _unstack_heads(a, tq):
    g, nt, _, width = a.shape
    a = a.reshape(g, nt, NSA_REP, tq, width).transpose(1, 3, 0, 2, 4)
    return a.reshape(nt * tq, g * NSA_REP * width)


def _group_kv(k, v):
    t = k.shape[0]
    return jnp.concatenate([k.reshape(t, NSA_KVH, HD), v.reshape(t, NSA_KVH, HD)], axis=-1).transpose(1, 0, 2)


def _pad_lanes(a):
    return jnp.concatenate([a, jnp.zeros_like(a)], axis=-1)


def _odd_prompt(x, cos, sin, g, w_aug_bf16, w_o_bf16, mem_k, mem_v, lb, cmp_w, final_g, tm, tq, kc):
    t = x.shape[0]
    assert t % NSA_BLOCK == 0
    kvo = lambda j: ODD_KV + j * NSA_KVW
    proj = _norm_proj(x, g, w_aug_bf16, cos, sin, ((ODD_NQR, ODD_KV), (kvo(2), kvo(3)), (kvo(4), kvo(5))), tm)
    ck, cv, sk, sv, wk, wv = (proj[:, kvo(j):kvo(j + 1)] for j in range(6))
    w1b, w2b, peb = cmp_w
    nb = t // NSA_BLOCK
    cmp_tok = _compress(jnp.stack([ck, cv]).reshape(2, nb, NSA_BLOCK * NSA_KVW), w1b, w2b, peb)
    ckv = _group_kv(cmp_tok[0], cmp_tok[1])
    q_st = _pad_lanes(_stack_heads(proj[:, ODD_NQ:ODD_NQ + NSA_QW], tq, HD))
    qr_st = _pad_lanes(_stack_heads(proj[:, ODD_NQR:ODD_NQR + NSA_QW], tq, HD))
    gt_st = _stack_heads(proj[:, ODD_NG:ODD_NG + 3 * NSA_H], tq, 3)
    o_nsa = _nsa_prompt(q_st, qr_st, gt_st, ckv, _group_kv(sk, sv).astype(BF16), _group_kv(wk, wv).astype(BF16), tq, kc)
    o_nsa = _unstack_heads(o_nsa[..., HD:], tq)
    hq, hf, hi = (proj[:, ODD_HQ + j * HGRN_W:ODD_HQ + (j + 1) * HGRN_W] for j in range(3))
    o_hg, s_hg = _hgrn_prompt(hq, hf, hi, lb.reshape(1, HGRN_W))
    o_mem = _mem_prompt(proj[:, ODD_EQ:ODD_GATE], mem_k, mem_v, tm)
    y = _mix_out(x, o_nsa, o_hg, o_mem, proj[:, ODD_GATE:ODD_NG], w_o_bf16, final_g, tm)
    rows = lambda a, b: jnp.stack([a.reshape(t, NSA_KVH, HD), b.reshape(t, NSA_KVH, HD)], axis=1)
    return y, rows(ck, cv), rows(sk, sv), rows(wk, wv), _diag_blocks(s_hg, HGRN_H)

def _heads_major(a, h):
    t = a.shape[0]
    return a.reshape(t, h, HD).transpose(1, 0, 2)


def _even_prompt(x, cos, sin, g, w_in_bf16, w_o_bf16, mem_k, mem_v, tm):
    t = x.shape[0]
    o_mq = 3 * RET_W
    o_eq = 3 * RET_W + 3 * MOBA_W
    o_gate = o_eq + MEM_W
    proj = _norm_proj(x, g, w_in_bf16, cos, sin, ((0, 2 * RET_W), (o_mq, o_mq + 2 * MOBA_W)), tm)
    rq, rk, rv = proj[:, :RET_W], proj[:, RET_W:2 * RET_W], proj[:, 2 * RET_W:3 * RET_W]
    mq = proj[:, o_mq:o_mq + MOBA_W]
    mk = proj[:, o_mq + MOBA_W:o_mq + 2 * MOBA_W]
    mv = proj[:, o_mq + 2 * MOBA_W:o_eq]
    eq = proj[:, o_eq:o_gate]
    gate = proj[:, o_gate:]
    o_ret, s_ret = _retention_prompt(rq, rk, rv)
    kmean = _block_mean(mk, MOBA_BLOCK)
    zq = jnp.zeros((MOBA_H, t, HD), F32)
    q_pad = jnp.concatenate([_heads_major(mq, MOBA_H), zq], axis=-1)
    kv = jnp.concatenate([_heads_major(mk, MOBA_H), _heads_major(mv, MOBA_H)], axis=-1).astype(BF16)
    km = _heads_major(kmean, MOBA_H)
    o_moba = _moba_prompt(q_pad, kv, jnp.concatenate([km, jnp.zeros_like(km)], axis=-1))
    o_moba = o_moba[:, :, HD:].transpose(1, 0, 2).reshape(t, MOBA_W)
    o_mem = _mem_prompt(eq, mem_k, mem_v, tm)
    y = _mix_out(x, o_ret, o_moba, o_mem, gate, w_o_bf16, None, tm)
    rows = proj[:, o_mq + MOBA_W:o_eq].reshape(t, 2, MOBA_H, HD)
    return y, rows, _diag_blocks(s_ret, RET_H)


def _pages_t(pool, width):
    n_pool, n_layer = pool.shape[0], pool.shape[1]
    return pool.transpose(0, 1, 3, 4, 5, 2).reshape(n_pool, n_layer, 2, width, PAGE)


def _mem_t(cache):
    bsz, n = cache.shape[0], cache.shape[1]
    return cache.transpose(0, 2, 3, 4, 1).reshape(bsz, 2, MEM_W, n)


def _even_sample(x, cos, sin, g, w_in_bf16, w_o_bf16, mem_cache, state, page_table, pool, layer):
    bsz = x.shape[0]
    o_mq = 3 * RET_W
    o_eq = 3 * RET_W + 3 * MOBA_W
    o_gate = o_eq + MEM_W
    proj = _norm_proj(x, g, w_in_bf16, cos, sin, ((0, 2 * RET_W), (o_mq, o_mq + 2 * MOBA_W)), bsz)
    rq, rk, rv = proj[:, :RET_W], proj[:, RET_W:2 * RET_W], proj[:, 2 * RET_W:3 * RET_W]
    mq = proj[:, o_mq:o_mq + MOBA_W]
    mk = proj[:, o_mq + MOBA_W:o_mq + 2 * MOBA_W]
    mv = proj[:, o_mq + 2 * MOBA_W:o_eq]
    gamma = np.repeat(1.0 - np.power(2.0, -5.0 - np.arange(RET_H, dtype=np.float64)), HD)
    o_ret, s_ret = _state_step(state, rq, rk, jnp.broadcast_to(jnp.asarray(gamma, F32), (bsz, RET_W)), rv, "ret")
    o_moba = _paged_attn(page_table, _pages_t(pool, MOBA_W), layer, mq.reshape(bsz, MOBA_W, 1), mk, mv, None,
                         MOBA_H, 1, True).reshape(bsz, MOBA_W)
    o_mem = _mem_sample(proj[:, o_eq:o_gate], _mem_t(mem_cache))
    y = _mix_out(x, o_ret, o_moba, o_mem, proj[:, o_gate:], w_o_bf16, None, bsz)
    rows = proj[:, o_mq + MOBA_W:o_eq].reshape(bsz, 1, 2, MOBA_H, HD)
    return y, rows, s_ret


def _cols(a):
    bsz = a.shape[0]
    return a.reshape(bsz, NSA_KVH, NSA_REP, HD).transpose(0, 1, 3, 2).reshape(bsz, NSA_KVW, NSA_REP)


def _uncols(a):
    bsz = a.shape[0]
    return a.reshape(bsz, NSA_KVH, HD, NSA_REP).transpose(0, 1, 3, 2).reshape(bsz, NSA_QW)


def _odd_sample(x, cos, sin, g, w_aug_bf16, w_o_bf16, mem_cache, state, lb, page_table, cmp_pool, slc_pool, wbuf,
                layer, cmp_w_t, final_g):
    bsz = x.shape[0]
    lbuf = wbuf.shape[1]
    kvo = lambda j: ODD_KV + j * NSA_KVW
    proj = _norm_proj(x, g, w_aug_bf16, cos, sin, ((ODD_NQR, ODD_KV), (kvo(2), kvo(3)), (kvo(4), kvo(5))), bsz)
    ck, cv, sk, sv, wk, wv = (proj[:, kvo(j):kvo(j + 1)] for j in range(6))
    wd, ped, w2t = cmp_w_t
    n_pool, n_layer = cmp_pool.shape[0], cmp_pool.shape[1]
    cmp_t = cmp_pool.transpose(0, 1, 3, 4, 5, 2)
    o_c, sel = _nsa_cmp_sample(page_table, cmp_t, layer, proj[:, ODD_NQ:ODD_NQ + NSA_QW].reshape(bsz, NSA_H, HD),
                               wd, ped, w2t)
    row_mask = jnp.repeat(sel.transpose(0, 3, 1, 2), NSA_BLOCK, axis=-1)
    q_cols = _cols(proj[:, ODD_NQR:ODD_NQR + NSA_QW])
    o_s = _paged_attn(page_table, _pages_t(slc_pool, NSA_KVW), layer, q_cols, sk, sv, row_mask, NSA_KVH, NSA_REP, False)
    wbuf_t = wbuf.transpose(0, 2, 3, 4, 1).reshape(bsz, 2, NSA_KVW, lbuf)
    o_w = _win_sample(q_cols, wbuf_t, wk, wv)
    ng = proj[:, ODD_NG:ODD_NG + 3 * NSA_H].reshape(bsz, NSA_H, 3)
    gate_logits = jnp.repeat(ng.transpose(2, 0, 1), HD, axis=-1)
    o_nsa = _gate_combine(gate_logits, o_c.reshape(bsz, NSA_QW), _uncols(o_s), _uncols(o_w))
    hq, hf, hi = (proj[:, ODD_HQ + j * HGRN_W:ODD_HQ + (j + 1) * HGRN_W] for j in range(3))
    o_hg, s_hg = _state_step(state, hq, hf, jnp.broadcast_to(lb.reshape(1, HGRN_W), (bsz, HGRN_W)), hi, "hgrn")
    o_mem = _mem_sample(proj[:, ODD_EQ:ODD_GATE], _mem_t(mem_cache))
    y = _mix_out(x, o_nsa, o_hg, o_mem, proj[:, ODD_GATE:ODD_NG], w_o_bf16, final_g, bsz)
    rows = lambda a, b: jnp.stack([a.reshape(bsz, 1, NSA_KVH, HD), b.reshape(bsz, 1, NSA_KVH, HD)], axis=2)
    win = jnp.concatenate([wbuf.astype(F32), rows(wk, wv)], axis=1)[:, -lbuf:]
    return y, rows(ck, cv), rows(sk, sv), win, s_hg


def kernel(x_prompt, x_sample, mem_prompt, cache_moba_kv, state_ret, cache_nsa_cmp_kv, cache_nsa_slc_kv,
           cache_nsa_win_kv, state_hgrn, cache_mem_kv, page_table, norm_g, mem_norm_g, w_mem_kv, w_in_even,
           w_in_odd, w_out, cmp_w1, cmp_w2, cmp_pe, hgrn_lb_logits, final_g):
    bp, tp, d = x_prompt.shape
    bs, ts, _ = x_sample.shape
    depth = w_out.shape[0]
    assert bp == 1 and ts == 1 and depth == 2
    n_mem = mem_prompt.shape[1]
    past_len = page_table.shape[1] * PAGE
    assert past_len % MOBA_BLOCK == 0 and cache_moba_kv.shape[2] == PAGE
    tm, tq, kc = 256, 128, 256
    xp, xs, mem = x_prompt[0], x_sample[:, 0], mem_prompt[0]
    cos_p, sin_p = _rope_tables(jnp.arange(tp, dtype=jnp.int32))
    cos_s, sin_s = (jnp.broadcast_to(a, (bs, LANES)) for a in _rope_tables(jnp.full((1,), past_len, jnp.int32)))
    lb_prob = jax.nn.softmax(hgrn_lb_logits.astype(F32), axis=0)
    lb_all = jnp.cumsum(lb_prob, axis=0) - lb_prob[0]
    lw_p = min(NSA_WINDOW, tp)

    def mem_kv(layer):
        kv = _norm_proj(mem, mem_norm_g[layer], w_mem_kv[layer].astype(BF16), cos_p[:n_mem], sin_p[:n_mem], (), n_mem)
        return kv.reshape(n_mem, 2, MEM_W)

    w_in0 = w_in_even[0].astype(BF16)
    w_o0 = w_out[0].astype(BF16)
    mkv0 = mem_kv(0)
    hp, moba_p, ret_p = _even_prompt(xp, cos_p, sin_p, norm_g[0], w_in0, w_o0, mkv0[:, 0], mkv0[:, 1], tm)
    hs, moba_s, ret_s = _even_sample(xs, cos_s, sin_s, norm_g[0], w_in0, w_o0, cache_mem_kv[0], state_ret[0],
                                     page_table, cache_moba_kv, 0)

    w_in1 = _odd_weights(w_in_odd[0]).astype(BF16)
    w_o1 = w_out[1].astype(BF16)
    mkv1 = mem_kv(1)
    cmp_w = _compress_weights(cmp_w1[0], cmp_w2[0], cmp_pe[0])
    wd, ped = _compress_weights_t(cmp_w1[0], cmp_pe[0])
    yp, cmp_p, slc_p, win_p, hg_p = _odd_prompt(hp, cos_p, sin_p, norm_g[1], w_in1, w_o1, mkv1[:, 0], mkv1[:, 1],
                                                lb_all[1], cmp_w, final_g, tm, tq, kc)
    ys, cmp_s, slc_s, win_s, hg_s = _odd_sample(hs, cos_s, sin_s, norm_g[1], w_in1, w_o1, cache_mem_kv[1],
                                                state_hgrn[0], lb_all[1], page_table, cache_nsa_cmp_kv,
                                                cache_nsa_slc_kv, cache_nsa_win_kv[0], 0, (wd, ped, cmp_w[1]), final_g)

    return (yp[None], ys[:, None], moba_p[None, None], moba_s[:, None], ret_p[None, None], ret_s[None],
            cmp_p[None, None], cmp_s[:, None], slc_p[None, None], slc_s[:, None], win_p[None, tp - lw_p:][None],
            win_s[None], hg_p[None, None], hg_s[None], jnp.stack([mkv0, mkv1]).reshape(depth, 1, n_mem, 2, MEM_H, HD))
```

```python
import functools
import math

import jax
import jax.numpy as jnp
import numpy as np
from jax import lax
from jax.experimental import pallas as pl
from jax.experimental.pallas import tpu as pltpu

F32 = jnp.float32
BF16 = jnp.bfloat16
HIGHEST = lax.Precision.HIGHEST

HD = 64
RET_H, MOBA_H, MEM_H, NSA_H, NSA_KVH, HGRN_H = 6, 6, 4, 8, 2, 4
NSA_REP = NSA_H // NSA_KVH
RET_W, MOBA_W, MEM_W = RET_H * HD, MOBA_H * HD, MEM_H * HD
NSA_QW, NSA_KVW, HGRN_W = NSA_H * HD, NSA_KVH * HD, HGRN_H * HD
MIX_W = RET_W + MOBA_W + MEM_W
PAGE = 128
RET_CHUNK, HGRN_CHUNK = 128, 64
MOBA_BLOCK, MOBA_TOPK = 256, 3
NSA_BLOCK, NSA_TOPN, NSA_WINDOW = 64, 16, 512
ROPE_THETA = 10000.0
EPS = 1e-6
NEG = -1e30
FORCE = 1e30
TINY = 1e-30
SCALE = HD ** -0.5
LANES = 128
VMEM_LIMIT = 56 * 1024 * 1024


def _params(*sem):
    return pltpu.CompilerParams(dimension_semantics=sem, vmem_limit_bytes=VMEM_LIMIT)


def _dot(a, b, precision=None):
    return jnp.dot(a, b, preferred_element_type=F32, precision=precision)


def _dot_nt(a, b, precision=None):
    return lax.dot_general(a, b, (((1,), (1,)), ((), ())), preferred_element_type=F32, precision=precision)


def _dot_tn(a, b, precision=None):
    return lax.dot_general(a, b, (((0,), (0,)), ((), ())), preferred_element_type=F32, precision=precision)


def _block_ones(width, value=1.0):
    r = lax.broadcasted_iota(jnp.int32, (width, width), 0) // HD
    c = lax.broadcasted_iota(jnp.int32, (width, width), 1) // HD
    return jnp.where(r == c, value, 0.0).astype(F32)


def _rope_tile(a, cos, sin, first_half):
    rot = jnp.where(first_half, pltpu.roll(a, LANES - HD // 2, 1), pltpu.roll(a, HD // 2, 1))
    return a * cos + rot * sin


def _norm_proj_body(x_ref, g_ref, w_ref, cos_ref, sin_ref, o_ref, *, rope_tiles, n_chunk):
    x = x_ref[...]
    ms = jnp.mean(x * x, axis=-1, keepdims=True)
    y = (x * lax.rsqrt(ms + EPS) * g_ref[...]).astype(BF16)
    n = o_ref.shape[1]
    lane = lax.broadcasted_iota(jnp.int32, (x.shape[0], LANES), 1)
    first_half = (lane % HD) < HD // 2
    for c0 in range(0, n, n_chunk):
        acc = _dot(y, w_ref[:, c0:c0 + n_chunk])
        for j in range(n_chunk // LANES):
            tile = (c0 // LANES) + j
            a = acc[:, j * LANES:(j + 1) * LANES]
            if tile in rope_tiles:
                a = _rope_tile(a, cos_ref[...], sin_ref[...], first_half)
            o_ref[:, tile * LANES:(tile + 1) * LANES] = a


def _norm_proj(x, g, w_bf16, cos, sin, rope_cols, tm):
    m, d = x.shape
    n = w_bf16.shape[1]
    assert m % tm == 0 and n % LANES == 0
    tiles = n // LANES
    k = next(c for c in (4, 3, 2, 1) if tiles % c == 0)
    rope_tiles = frozenset(t for a, b in rope_cols for t in range(a // LANES, b // LANES))
    body = functools.partial(_norm_proj_body, rope_tiles=rope_tiles, n_chunk=k * LANES)
    return pl.pallas_call(
        body,
        grid=(m // tm,),
        in_specs=[
            pl.BlockSpec((tm, d), lambda i: (i, 0)),
            pl.BlockSpec((1, d), lambda i: (0, 0)),
            pl.BlockSpec((d, n), lambda i: (0, 0)),
            pl.BlockSpec((tm, LANES), lambda i: (i, 0)),
            pl.BlockSpec((tm, LANES), lambda i: (i, 0)),
        ],
        out_specs=pl.BlockSpec((tm, n), lambda i: (i, 0)),
        out_shape=jax.ShapeDtypeStruct((m, n), F32),
        compiler_params=_params("parallel"),
        name="norm_proj",
    )(x, g.reshape(1, d), w_bf16, cos, sin)


def _rope_tables(pos):
    half = HD // 2
    inv = ROPE_THETA ** (-jnp.arange(half, dtype=F32) / half)
    ang = pos.astype(F32)[:, None] * inv[None, :]
    cos, sin = jnp.cos(ang), jnp.sin(ang)
    return jnp.concatenate([cos, cos, cos, cos], -1), jnp.concatenate([-sin, sin, -sin, sin], -1)


def _retention_consts(c):
    lg = np.log(1.0 - np.power(2.0, -5.0 - np.arange(RET_H, dtype=np.float64)))
    ti = np.arange(c, dtype=np.float64)
    causal = ti[:, None] >= ti[None, :]
    d_in = np.where(causal[None], np.exp(np.where(causal, ti[:, None] - ti[None, :], 0.0)[None] * lg[:, None, None]), 0.0)
    q_dec = np.repeat(np.exp((ti[:, None] + 1.0) * lg[None, :]), HD, axis=1)
    k_dec = np.repeat(np.exp((c - 1.0 - ti)[:, None] * lg[None, :]), HD, axis=1)
    c_dec = np.repeat(np.exp(c * lg), HD)
    head = np.arange(RET_W) // HD
    bd = (head[:, None] == head[None, :]).astype(np.float64)
    cmat = bd * c_dec[:, None]
    f = lambda a: jnp.asarray(a, dtype=F32)
    return f(d_in), f(q_dec), f(k_dec), f(cmat), f(bd)


def _retention_body(q_ref, k_ref, v_ref, din_ref, qdec_ref, kdec_ref, cmat_ref, bd_ref, o_ref, s_ref):
    @pl.when(pl.program_id(0) == 0)
    def _():
        s_ref[...] = jnp.zeros_like(s_ref)

    q = q_ref[...]
    k = k_ref[...] * SCALE
    v = v_ref[...]
    c, w = q.shape
    head = lax.broadcasted_iota(jnp.int32, (c, w), 1) // HD
    s = s_ref[...]
    o = _dot(q, s, HIGHEST) * qdec_ref[...]
    for h in range(RET_H):
        mh = head == h
        att = _dot_nt(jnp.where(mh, q, 0.0), k, HIGHEST) * din_ref[h]
        o = o + jnp.where(mh, _dot(att, v, HIGHEST), 0.0)
    s_ref[...] = s * cmat_ref[...] + _dot_tn(k * kdec_ref[...], v, HIGHEST) * bd_ref[...]
    seg = bd_ref[...] * (1.0 / HD)
    xc = o - _dot(o, seg, HIGHEST)
    var = _dot(xc * xc, seg, HIGHEST)
    o_ref[...] = xc * lax.rsqrt(var + EPS)


def _retention_prompt(q, k, v):
    t, w = q.shape
    c = math.gcd(t, RET_CHUNK)
    d_in, q_dec, k_dec, cmat, bd = _retention_consts(c)
    row = pl.BlockSpec((c, w), lambda i: (i, 0))
    const2 = lambda shape: pl.BlockSpec(shape, lambda i: (0,) * len(shape))
    return pl.pallas_call(
        _retention_body,
        grid=(t // c,),
        in_specs=[row, row, row, const2((RET_H, c, c)), const2((c, w)), const2((c, w)), const2((w, w)), const2((w, w))],
        out_specs=[row, const2((w, w))],
        out_shape=[jax.ShapeDtypeStruct((t, w), F32), jax.ShapeDtypeStruct((w, w), F32)],
        compiler_params=_params("arbitrary"),
        name="retention_prompt",
    )(q, k, v, d_in, q_dec, k_dec, cmat, bd)


def _diag_blocks(s, h):
    s4 = s.reshape(h, HD, h, HD)
    return jnp.stack([s4[i, :, i, :] for i in range(h)], axis=0)


def _block_mean_body(k_ref, o_ref):
    o_ref[0] = jnp.mean(k_ref[...], axis=0, keepdims=True)


def _block_mean(k, blk):
    t, w = k.shape
    assert t % blk == 0
    out = pl.pallas_call(
        _block_mean_body,
        grid=(t // blk,),
        in_specs=[pl.BlockSpec((blk, w), lambda i: (i, 0))],
        out_specs=pl.BlockSpec((1, 1, w), lambda i: (i, 0, 0)),
        out_shape=jax.ShapeDtypeStruct((t // blk, 1, w), F32),
        compiler_params=_params("parallel"),
        name="block_mean",
    )(k)
    return out.reshape(t // blk, w)


def _topk_mask(score, k):
    n = score.shape[-1]
    idx = lax.broadcasted_iota(jnp.int32, score.shape, score.ndim - 1)
    sel = jnp.zeros(score.shape, F32)
    work = score
    for _ in range(k):
        m = jnp.max(work, axis=-1, keepdims=True)
        first = jnp.min(jnp.where(work == m, idx, n), axis=-1, keepdims=True)
        pick = idx == first
        sel = jnp.where(pick, 1.0, sel)
        work = jnp.where(pick, -jnp.inf, work)
    return sel


def _tile_loop(lo, hi, fn, unroll=4):
    def group(g, _):
        for u in range(unroll):
            fn(lo + unroll * g + u)
        return 0
    main = jnp.maximum(hi - lo, 0) // unroll
    lax.fori_loop(0, main, group, 0)
    lax.fori_loop(lo + main * unroll, hi, lambda j, _: (fn(j), 0)[1], 0)


def _lane_fold_max(s):
    out = s[:, 0:LANES]
    for c in range(1, s.shape[1] // LANES):
        out = jnp.maximum(out, s[:, c * LANES:(c + 1) * LANES])
    return out


def _moba_prompt_body(q_ref, ka_ref, vo_ref, kmean_ref, o_ref, s_ref, m_ref, acc_ref, *, topk):
    qi = pl.program_id(1)
    q = q_ref[0]
    tq = q.shape[0]
    gate = _dot_nt(q, kmean_ref[0], HIGHEST)
    blk = lax.broadcasted_iota(jnp.int32, (tq, LANES), 1) - HD
    past = (blk >= 0) & (blk < qi)
    sel = jnp.where(past, _topk_mask(jnp.where(past, gate, NEG), topk), 0.0)
    bias = jnp.where((blk < 0) | (blk == qi) | (sel > 0.5), 0.0, NEG)
    qa = (q * SCALE + bias).astype(BF16)
    m_ref[...] = jnp.full(m_ref.shape, NEG, F32)

    def score(j):
        start = pl.multiple_of(j * tq, tq)
        s = _dot_nt(qa, ka_ref[0, pl.ds(start, tq), :])
        s_ref[j] = s
        m_ref[...] = jnp.maximum(m_ref[...], _lane_fold_max(s))

    _tile_loop(0, qi, score)
    r = lax.broadcasted_iota(jnp.int32, (tq, tq), 0)
    c = lax.broadcasted_iota(jnp.int32, (tq, tq), 1)
    own = pl.multiple_of(qi * tq, tq)
    s = jnp.where(c <= r, _dot_nt(qa, ka_ref[0, pl.ds(own, tq), :]), NEG)
    s_ref[qi] = s
    m = jnp.max(jnp.maximum(m_ref[...], _lane_fold_max(s)), axis=-1, keepdims=True)

    acc_ref[...] = jnp.zeros(acc_ref.shape, F32)

    def accumulate(j):
        start = pl.multiple_of(j * tq, tq)
        p = jnp.exp(s_ref[j] - m).astype(BF16)
        acc_ref[...] += _dot(p, vo_ref[0, pl.ds(start, tq), :])

    _tile_loop(0, qi + 1, accumulate)
    acc = acc_ref[...]
    o_ref[0] = acc / jnp.maximum(acc[:, 0:1], TINY)


def _moba_prompt(q_pad, ka_bf16, vo_bf16, kmean_rows):
    h, t, _ = q_pad.shape
    nblk = t // MOBA_BLOCK
    assert t % MOBA_BLOCK == 0 and nblk <= LANES - HD
    return pl.pallas_call(
        functools.partial(_moba_prompt_body, topk=min(MOBA_TOPK, nblk)),
        grid=(h, nblk),
        in_specs=[
            pl.BlockSpec((1, MOBA_BLOCK, LANES), lambda a, i: (a, i, 0)),
            pl.BlockSpec((1, t, LANES), lambda a, i: (a, 0, 0)),
            pl.BlockSpec((1, t, LANES), lambda a, i: (a, 0, 0)),
            pl.BlockSpec((1, LANES, LANES), lambda a, i: (a, 0, 0)),
        ],
        out_specs=pl.BlockSpec((1, MOBA_BLOCK, LANES), lambda a, i: (a, i, 0)),
        out_shape=jax.ShapeDtypeStruct((h, t, LANES), F32),
        scratch_shapes=[pltpu.VMEM((nblk, MOBA_BLOCK, MOBA_BLOCK), F32), pltpu.VMEM((MOBA_BLOCK, LANES), F32),
                        pltpu.VMEM((MOBA_BLOCK, LANES), F32)],
        compiler_params=_params("parallel", "arbitrary"),
        name="moba_prompt",
    )(q_pad, ka_bf16, vo_bf16, kmean_rows)


def _mem_prompt_body(q_ref, k_ref, v_ref, o_ref):
    q = q_ref[...]
    kb = k_ref[...].astype(BF16)
    vb = v_ref[...].astype(BF16)
    head = lax.broadcasted_iota(jnp.int32, q.shape, 1) // HD
    o = jnp.zeros(q.shape, F32)
    for h in range(MEM_H):
        mh = head == h
        s = _dot_nt(jnp.where(mh, q, 0.0).astype(BF16), kb) * SCALE
        e = jnp.exp(s - jnp.max(s, axis=-1, keepdims=True))
        p = e / jnp.sum(e, axis=-1, keepdims=True)
        o = o + jnp.where(mh, _dot(p.astype(BF16), vb), 0.0)
    o_ref[...] = o


def _mem_prompt(q, k, v, tm):
    t, w = q.shape
    n = k.shape[0]
    return pl.pallas_call(
        _mem_prompt_body,
        grid=(t // tm,),
        in_specs=[pl.BlockSpec((tm, w), lambda i: (i, 0)), pl.BlockSpec((n, w), lambda i: (0, 0)),
                  pl.BlockSpec((n, w), lambda i: (0, 0))],
        out_specs=pl.BlockSpec((tm, w), lambda i: (i, 0)),
        out_shape=jax.ShapeDtypeStruct((t, w), F32),
        compiler_params=_params("parallel"),
        name="mem_prompt",
    )(q, k, v)


def _mix_out_body(x_ref, a_ref, b_ref, c_ref, gate_ref, wo_ref, fg_ref, o_ref, *, final_norm):
    gate = gate_ref[...]
    gate = gate * jax.nn.sigmoid(gate)
    y = x_ref[...]
    off = 0
    for ref in (a_ref, b_ref, c_ref):
        w = ref.shape[1]
        mix = (ref[...] * gate[:, off:off + w]).astype(BF16)
        y = y + _dot(mix, wo_ref[off:off + w, :])
        off += w
    if final_norm:
        ms = jnp.mean(y * y, axis=-1, keepdims=True)
        y = y * lax.rsqrt(ms + EPS) * fg_ref[...]
    o_ref[...] = y


def _mix_out(x, a, b, c, gate, wo_bf16, final_g, tm):
    m, d = x.shape
    row = lambda w: pl.BlockSpec((tm, w), lambda i: (i, 0))
    fg = jnp.ones((1, d), F32) if final_g is None else final_g.reshape(1, d).astype(F32)
    body = functools.partial(_mix_out_body, final_norm=final_g is not None)
    return pl.pallas_call(
        body,
        grid=(m // tm,),
        in_specs=[row(d), row(a.shape[1]), row(b.shape[1]), row(c.shape[1]), row(gate.shape[1]),
                  pl.BlockSpec(wo_bf16.shape, lambda i: (0, 0)), pl.BlockSpec((1, d), lambda i: (0, 0))],
        out_specs=row(d),
        out_shape=jax.ShapeDtypeStruct((m, d), F32),
        compiler_params=_params("parallel"),
        name="mix_out",
    )(x, a, b, c, gate, wo_bf16, fg)


def _compress_body(x_ref, pe_ref, w1_ref, w2_ref, o_ref):
    x = (x_ref[0] + pe_ref[0]).astype(BF16)
    hid = jax.nn.gelu(_dot(x, w1_ref[0]))
    o_ref[0] = _dot(hid.astype(BF16), w2_ref[0])


def _compress_weights(w1, w2, pe):
    eye = jnp.eye(NSA_KVH, dtype=F32)
    w1r = w1.astype(F32).reshape(2, NSA_BLOCK, HD, HD)
    w1b = jnp.einsum("klde,gh->klgdhe", w1r, eye).reshape(2, NSA_BLOCK * NSA_KVW, NSA_KVW)
    w2b = jnp.einsum("kde,gh->kgdhe", w2.astype(F32), eye).reshape(2, NSA_KVW, NSA_KVW)
    peb = jnp.broadcast_to(pe.astype(F32)[:, :, None, :], (2, NSA_BLOCK, NSA_KVH, HD)).reshape(2, 1, NSA_BLOCK * NSA_KVW)
    return w1b.astype(BF16), w2b.astype(BF16), peb


def _compress(x, w1b, w2b, peb):
    _, nb, kdim = x.shape
    return pl.pallas_call(
        _compress_body,
        grid=(2,),
        in_specs=[pl.BlockSpec((1, nb, kdim), lambda i: (i, 0, 0)), pl.BlockSpec((1, 1, kdim), lambda i: (i, 0, 0)),
                  pl.BlockSpec((1, kdim, NSA_KVW), lambda i: (i, 0, 0)),
                  pl.BlockSpec((1, NSA_KVW, NSA_KVW), lambda i: (i, 0, 0))],
        out_specs=pl.BlockSpec((1, nb, NSA_KVW), lambda i: (i, 0, 0)),
        out_shape=jax.ShapeDtypeStruct((2, nb, NSA_KVW), F32),
        compiler_params=_params("parallel"),
        name="nsa_compress",
    )(x, peb, w1b, w2b)


def _softmax_step(s, mask, kv, carry):
    m, l, acc = carry
    s = jnp.where(mask, s, NEG)
    m_new = jnp.maximum(m, jnp.max(s, axis=-1, keepdims=True))
    alpha = jnp.exp(m - m_new)
    p = jnp.where(mask, jnp.exp(s - m_new), 0.0)
    l = alpha * l + jnp.sum(p, axis=-1, keepdims=True)
    acc = alpha * acc + _dot(p.astype(BF16), kv)
    return m_new, l, acc


def _nsa_prompt_body(q_ref, qr_ref, gt_ref, ckv_ref, ska_ref, svo_ref, wkv_ref, o_ref, m_ref, acc_ref, *, tq, kc):
    i = pl.program_id(1)
    t0 = i * tq
    rows = q_ref.shape[2]
    nb = ckv_ref.shape[1]
    q = q_ref[0, 0]
    qrb = (qr_ref[0, 0] * SCALE).astype(BF16)
    tpos = t0 + lax.broadcasted_iota(jnp.int32, (rows, 1), 0) % tq

    ckv = ckv_ref[0].astype(BF16)
    blk = lax.broadcasted_iota(jnp.int32, (nb, rows), 0)
    tpos_l = t0 + lax.broadcasted_iota(jnp.int32, (nb, rows), 1) % tq
    complete = blk * NSA_BLOCK + (NSA_BLOCK - 1) <= tpos_l
    s_c = jnp.where(complete, _dot_nt(ckv, q.astype(BF16)) * SCALE, NEG)
    e = jnp.where(complete, jnp.exp(s_c - jnp.max(s_c, axis=0, keepdims=True)), 0.0)
    pc = e / jnp.maximum(jnp.sum(e, axis=0, keepdims=True), TINY)
    o_c = _dot_tn(pc.astype(BF16), ckv)

    imp = pc[:, 0:tq]
    for r in range(1, NSA_REP):
        imp = imp + pc[:, r * tq:(r + 1) * tq]
    blk_q = lax.broadcasted_iota(jnp.int32, (nb, tq), 0)
    tpos_q = t0 + lax.broadcasted_iota(jnp.int32, (nb, tq), 1)
    cur = tpos_q // NSA_BLOCK
    forced = (blk_q == 0) | (blk_q == cur) | (blk_q == cur - 1)
    complete_q = blk_q * NSA_BLOCK + (NSA_BLOCK - 1) <= tpos_q
    score = jnp.where(forced, FORCE, jnp.where(complete_q, imp, NEG))
    sel = jnp.where(score > 0.5 * NEG, _topk_mask_axis0(score, min(NSA_TOPN, nb)), 0.0)

    init = (jnp.full((rows, 1), NEG, F32), jnp.zeros((rows, 1), F32), jnp.zeros((rows, LANES), F32))

    bias = jnp.where(sel > 0.5, 0.0, NEG).T
    nbp = -(-nb // LANES) * LANES
    if nbp > nb:
        bias = jnp.concatenate([bias, jnp.full((tq, nbp - nb), NEG, F32)], axis=1)
    bias = jnp.concatenate([bias] * NSA_REP, axis=0)
    qrs = qr_ref[0, 0] * SCALE
    upper = lax.broadcasted_iota(jnp.int32, (rows, LANES), 1) >= HD
    n_span = -(-nb // HD)
    qa = []
    for sp in range(n_span):
        col = bias[:, (sp // 2) * LANES:(sp // 2 + 1) * LANES]
        if sp % 2 == 0:
            col = pltpu.roll(col, HD, 1)
        qa.append((qrs + jnp.where(upper, col, 0.0)).astype(BF16))
    cps = HD * NSA_BLOCK // kc
    n_chunks = (t0 + tq + kc - 1) // kc
    last = n_chunks - 1
    qa_last = qa[0]
    for sp in range(1, n_span):
        qa_last = jnp.where(last // cps == sp, qa[sp], qa_last)
    last_start = pl.multiple_of(last * kc, kc)
    kpos = last_start + lax.broadcasted_iota(jnp.int32, (rows, kc), 1)
    s_last = jnp.where(kpos <= tpos, _dot_nt(qa_last, ska_ref[0, pl.ds(last_start, kc), :]), NEG)

    def spans(fn):
        for sp in range(n_span):
            lo = sp * cps
            _tile_loop(jnp.minimum(lo, last), jnp.minimum(lo + cps, last), functools.partial(fn, qa[sp]))

    m_ref[...] = _lane_fold_max(s_last)

    def fold_max(qsp, c):
        start = pl.multiple_of(c * kc, kc)
        m_ref[...] = jnp.maximum(m_ref[...], _lane_fold_max(_dot_nt(qsp, ska_ref[0, pl.ds(start, kc), :])))

    spans(fold_max)
    m_s = jnp.max(m_ref[...], axis=-1, keepdims=True)
    acc_ref[...] = _dot(jnp.exp(s_last - m_s).astype(BF16), svo_ref[0, pl.ds(last_start, kc), :])

    def accumulate(qsp, c):
        start = pl.multiple_of(c * kc, kc)
        p = jnp.exp(_dot_nt(qsp, ska_ref[0, pl.ds(start, kc), :]) - m_s).astype(BF16)
        acc_ref[...] += _dot(p, svo_ref[0, pl.ds(start, kc), :])

    spans(accumulate)
    acc_s = acc_ref[...]
    o_s = acc_s / jnp.maximum(acc_s[:, 0:1], TINY)

    def win_chunk(c, carry):
        start = pl.multiple_of(c * tq, tq)
        kv = wkv_ref[0, pl.ds(start, tq), :]
        kpos = start + lax.broadcasted_iota(jnp.int32, (rows, tq), 1)
        mask = (kpos <= tpos) & (kpos > tpos - NSA_WINDOW)
        return _softmax_step(_dot_nt(qrb, kv), mask, kv, carry)

    first = jnp.maximum(i - (NSA_WINDOW + tq - 1) // tq, 0)
    _, l_w, acc_w = lax.fori_loop(first, i + 1, win_chunk, init)
    o_w = acc_w / jnp.maximum(l_w, TINY)

    gt = jax.nn.sigmoid(gt_ref[0, 0])
    o_ref[0, 0] = gt[:, 0:1] * o_c + gt[:, 1:2] * o_s + gt[:, 2:3] * o_w


def _nsa_prompt(q_st, qr_st, gt_st, ckv, ska, svo, wkv, tq, kc):
    g, nt, rows, _ = q_st.shape
    t = ska.shape[1]
    nb = ckv.shape[1]
    assert t % kc == 0 and kc % tq == 0 and (HD * NSA_BLOCK) % kc == 0 and t % tq == 0
    qspec = pl.BlockSpec((1, 1, rows, LANES), lambda a, i: (a, i, 0, 0))
    seq = pl.BlockSpec((1, t, LANES), lambda a, i: (a, 0, 0))
    body = functools.partial(_nsa_prompt_body, tq=tq, kc=kc)
    return pl.pallas_call(
        body,
        grid=(g, nt),
        in_specs=[qspec, qspec, pl.BlockSpec((1, 1, rows, 3), lambda a, i: (a, i, 0, 0)),
                  pl.BlockSpec((1, nb, LANES), lambda a, i: (a, 0, 0)), seq, seq, seq],
        out_specs=qspec,
        out_shape=jax.ShapeDtypeStruct((g, nt, rows, LANES), F32),
        scratch_shapes=[pltpu.VMEM((rows, LANES), F32), pltpu.VMEM((rows, LANES), F32)],
        compiler_params=_params("parallel", "arbitrary"),
        name="nsa_prompt",
    )(q_st, qr_st, gt_st, ckv, ska, svo, wkv)


def _hgrn_body(q_ref, f_ref, i_ref, lb_ref, o_ref, s_ref):
    @pl.when(pl.program_id(0) == 0)
    def _():
        s_ref[...] = jnp.zeros_like(s_ref)

    q = q_ref[...]
    v = i_ref[...]
    c, w = q.shape
    lb = lb_ref[...]
    f = lb + (1.0 - lb) * jax.nn.sigmoid(f_ref[...])
    kk = 1.0 - f
    tr = lax.broadcasted_iota(jnp.int32, (c, c), 0)
    tc = lax.broadcasted_iota(jnp.int32, (c, c), 1)
    cum = _dot(jnp.where(tr >= tc, 1.0, 0.0), jnp.log(f), HIGHEST)
    bd = _block_ones(w)
    bd_b = bd.astype(BF16)
    s = s_ref[...]
    o = _dot(q * jnp.exp(cum), s, HIGHEST)
    t_idx = lax.broadcasted_iota(jnp.int32, (c, w), 0)

    sub = 8
    groups = c // sub
    o_blk = [o[i * sub:(i + 1) * sub] for i in range(groups)]
    for g in range(groups):
        r0 = g * sub
        n = c - r0
        t_g = r0 + lax.broadcasted_iota(jnp.int32, (n, w), 0)
        es = [jnp.where(t_g >= j, jnp.exp(cum[r0:] - cum[j:j + 1]) * q[r0:] * kk[j:j + 1], 0.0)
              for j in range(r0, r0 + sub)]
        e = jnp.concatenate(es, axis=0)
        e_hi = e.astype(BF16)
        e_lo = (e - e_hi.astype(F32)).astype(BF16)
        a = _dot(e_hi, bd_b) + _dot(e_lo, bd_b)
        contrib = a[0:n] * v[r0:r0 + 1]
        for u in range(1, sub):
            contrib = contrib + a[u * n:(u + 1) * n] * v[r0 + u:r0 + u + 1]
        for i in range(g, groups):
            o_blk[i] = o_blk[i] + contrib[(i - g) * sub:(i - g + 1) * sub]
    o = jnp.concatenate(o_blk, axis=0)
    last = cum[c - 1:c, :]
    row0 = t_idx == 0
    scale_mat = _dot_tn(jnp.where(row0, jnp.exp(last), 0.0), jnp.where(row0, 1.0, 0.0), HIGHEST)
    s_ref[...] = s * scale_mat + _dot_tn(kk * jnp.exp(last - cum), v, HIGHEST) * bd
    ms = _dot(o * o, bd * (1.0 / HD), HIGHEST)
    o_ref[...] = o * lax.rsqrt(ms + EPS)


def _hgrn_prompt(q, f_logit, i, lb):
    t, w = q.shape
    c = math.gcd(t, HGRN_CHUNK)
    row = pl.BlockSpec((c, w), lambda n: (n, 0))
    return pl.pallas_call(
        _hgrn_body,
        grid=(t // c,),
        in_specs=[row, row, row, pl.BlockSpec((1, w), lambda n: (0, 0))],
        out_specs=[row, pl.BlockSpec((w, w), lambda n: (0, 0))],
        out_shape=[jax.ShapeDtypeStruct((t, w), F32), jax.ShapeDtypeStruct((w, w), F32)],
        compiler_params=_params("arbitrary"),
        name="hgrn_prompt",
    )(q, f_logit, i, lb)


def _state_step_body(s_ref, q_ref, a_ref, b_ref, v_ref, o_ref, so_ref, *, mode):
    s = s_ref[...]
    if mode == "ret":
        k = a_ref[...] * SCALE
        dec = b_ref[...]
    else:
        lb = b_ref[...]
        dec = lb + (1.0 - lb) * jax.nn.sigmoid(a_ref[...])
        k = 1.0 - dec
    kv = k * v_ref[...]
    rows = s.shape[0]
    o = (q_ref[...] * (kv + dec * s)).reshape(rows // HD, HD, HD).sum(axis=1)
    so_ref[...] = dec * s + kv
    if mode == "ret":
        xc = o - jnp.mean(o, axis=-1, keepdims=True)
        o_ref[...] = xc * lax.rsqrt(jnp.mean(xc * xc, axis=-1, keepdims=True) + EPS)
    else:
        o_ref[...] = o * lax.rsqrt(jnp.mean(o * o, axis=-1, keepdims=True) + EPS)


def _state_step(state, q, a, b, v, mode):
    bsz, h = state.shape[0], state.shape[1]
    rows = bsz * h * HD
    col = lambda x: jnp.broadcast_to(x.reshape(bsz, h, HD, 1), (bsz, h, HD, HD)).reshape(rows, HD)
    vx = jnp.broadcast_to(v.reshape(bsz, h, 1, HD), (bsz, h, HD, HD)).reshape(rows, HD)
    bb = 8 if bsz % 8 == 0 else bsz
    br = bb * h * HD
    spec = pl.BlockSpec((br, HD), lambda i: (i, 0))
    ospec = pl.BlockSpec((br // HD, HD), lambda i: (i, 0))
    o, s_new = pl.pallas_call(
        functools.partial(_state_step_body, mode=mode),
        grid=(rows // br,),
        in_specs=[spec] * 5,
        out_specs=[ospec, spec],
        out_shape=[jax.ShapeDtypeStruct((rows // HD, HD), F32), jax.ShapeDtypeStruct((rows, HD), F32)],
        compiler_params=_params("parallel"),
        name="state_step_" + mode,
    )(state.reshape(rows, HD).astype(F32), col(q), col(a), col(b), vx)
    return o.reshape(bsz, h * HD), s_new.reshape(bsz, h, HD, HD)


def _expand_heads(p, width):
    g, n = p.shape
    return jnp.broadcast_to(p[:, None, :], (g, HD, n)).reshape(width, n)


def _head_sums(x, g):
    return x.reshape(g, HD, x.shape[-1]).sum(axis=1)


def _mem_sample_body(q_ref, kt_ref, vt_ref, o_ref):
    kt = kt_ref[0, 0]
    vt = vt_ref[0, 0]
    s = _head_sums(kt * q_ref[0], MEM_H) * SCALE
    e = jnp.exp(s - jnp.max(s, axis=-1, keepdims=True))
    p = e / jnp.sum(e, axis=-1, keepdims=True)
    o_ref[0] = jnp.sum(vt * _expand_heads(p, MEM_W), axis=-1, keepdims=True)


def _mem_sample(q, kvt):
    bsz, w = q.shape
    n = kvt.shape[-1]
    out = pl.pallas_call(
        _mem_sample_body,
        grid=(bsz,),
        in_specs=[pl.BlockSpec((1, w, 1), lambda i: (i, 0, 0)),
                  pl.BlockSpec((1, 1, w, n), lambda i: (i, 0, 0, 0)),
                  pl.BlockSpec((1, 1, w, n), lambda i: (i, 1, 0, 0))],
        out_specs=pl.BlockSpec((1, w, 1), lambda i: (i, 0, 0)),
        out_shape=jax.ShapeDtypeStruct((bsz, w, 1), F32),
        compiler_params=_params("parallel"),
        name="mem_sample",
    )(q.reshape(bsz, w, 1), kvt, kvt)
    return out.reshape(bsz, w)


def _topk_mask_axis0(score, k):
    n = score.shape[0]
    idx = lax.broadcasted_iota(jnp.int32, score.shape, 0)
    sel = jnp.zeros(score.shape, F32)
    work = score
    for _ in range(k):
        m = jnp.max(work, axis=0, keepdims=True)
        first = jnp.min(jnp.where(work == m, idx, n), axis=0, keepdims=True)
        pick = idx == first
        sel = jnp.where(pick, 1.0, sel)
        work = jnp.where(pick, -jnp.inf, work)
    return sel


def _paged_attn_body(pt_ref, pool_ref, qc_ref, qbd_ref, kn_ref, vn_ref, mask_ref, o_ref, buf, sem, s_ref, acc_ref, st_ref,
                     *, layer, groups, reps, chunk, moba):
    b = pl.program_id(0)
    n_pages = pt_ref.shape[1]
    n = n_pages // chunk
    gw = groups * HD

    def copy(kv, c, i, slot):
        return pltpu.make_async_copy(pool_ref.at[pt_ref[b, c * chunk + i], layer, kv], buf.at[slot, i], sem.at[slot])

    def start(kv, c, slot):
        lax.fori_loop(0, chunk, lambda i, _: (copy(kv, c, i, slot).start(), 0)[1], 0)

    def wait(kv, c, slot):
        lax.fori_loop(0, chunk, lambda i, _: (copy(kv, c, i, slot).wait(), 0)[1], 0)

    acc_ref[...] = jnp.zeros_like(acc_ref)
    start(0, 0, 0)

    def k_compute(c, slot):
        def page(i, _):
            sc = _dot(qbd_ref[0], buf[slot, i].astype(BF16))
            for r in range(reps):
                s_ref[r, c * chunk + i] = sc[r * groups:(r + 1) * groups]
            return 0
        lax.fori_loop(0, chunk, page, 0)

    def select():
        if moba:
            per = MOBA_BLOCK // PAGE
            nblk = n_pages // per
            sc = s_ref[0].reshape(nblk, per, groups, PAGE)
            gate = jnp.sum(jnp.sum(sc, axis=1, keepdims=True), axis=-1, keepdims=True) * (1.0 / MOBA_BLOCK)
            sel = _topk_mask_axis0(gate, min(MOBA_TOPK, nblk + 1))
            mask = jnp.broadcast_to(sel, (nblk, per, groups, PAGE)).reshape(n_pages, groups, PAGE) > 0.5
        else:
            mask = mask_ref[0] > 0.5
        for r in range(reps):
            qcol = qc_ref[0, :, r:r + 1]
            s_own = _head_sums(jnp.broadcast_to(kn_ref[0] * qcol, (gw, PAGE)), groups) * SCALE
            s = jnp.where(mask, s_ref[r] * SCALE, NEG)
            m = jnp.max(jnp.max(s, axis=0), axis=-1, keepdims=True)
            m = jnp.maximum(m, s_own)
            p = jnp.where(mask, jnp.exp(s - m), 0.0)
            p_own = jnp.exp(s_own - m)
            den = jnp.sum(jnp.sum(p, axis=0), axis=-1, keepdims=True) + p_own
            s_ref[r] = p
            st_ref[r, 0] = p_own
            st_ref[r, 1] = jnp.maximum(den, TINY)

    def v_compute(c, slot):
        def page(i, _):
            vt = buf[slot, i]
            for r in range(reps):
                acc_ref[r] += vt * _expand_heads(s_ref[r, c * chunk + i], gw)
            return 0
        lax.fori_loop(0, chunk, page, 0)

    def step(idx, _):
        slot = idx % 2
        nxt = idx + 1

        @pl.when(nxt < 2 * n)
        def _():
            start(nxt // n, nxt % n, 1 - slot)

        wait(idx // n, idx % n, slot)

        @pl.when(idx < n)
        def _():
            k_compute(idx, slot)

        @pl.when(idx == n - 1)
        def _():
            select()

        @pl.when(idx >= n)
        def _():
            v_compute(idx - n, slot)

        return 0

    lax.fori_loop(0, 2 * n, step, 0)
    for r in range(reps):
        tot = jnp.sum(acc_ref[r], axis=-1, keepdims=True)
        p_own = _expand_heads(st_ref[r, 0], gw)[:, 0:1]
        den = _expand_heads(st_ref[r, 1], gw)[:, 0:1]
        o_ref[0, :, r:r + 1] = (tot + p_own * vn_ref[0]) / den


def _paged_attn(page_table, pool_t, layer, q_cols, k_new, v_new, row_mask, groups, reps, moba):
    bsz, n_pages = page_table.shape
    gw = groups * HD
    chunk = math.gcd(n_pages, 16)
    if row_mask is None:
        row_mask = jnp.zeros((bsz, 1, groups, PAGE), F32)
    mshape = row_mask.shape[1:]
    nq = -(-reps * groups // 8) * 8
    qh = q_cols.reshape(bsz, groups, HD, reps).transpose(0, 3, 1, 2)
    qbd = jnp.einsum("brgd,gh->brghd", qh, jnp.eye(groups, dtype=F32)).reshape(bsz, reps * groups, gw)
    qbd = jnp.pad(qbd, ((0, 0), (0, nq - reps * groups), (0, 0))).astype(BF16)
    body = functools.partial(_paged_attn_body, layer=layer, groups=groups, reps=reps, chunk=chunk, moba=moba)
    grid_spec = pltpu.PrefetchScalarGridSpec(
        num_scalar_prefetch=1,
        grid=(bsz,),
        in_specs=[
            pl.BlockSpec(memory_space=pl.ANY),
            pl.BlockSpec((1, gw, reps), lambda i, pt: (i, 0, 0)),
            pl.BlockSpec((1, nq, gw), lambda i, pt: (i, 0, 0)),
            pl.BlockSpec((1, gw, 1), lambda i, pt: (i, 0, 0)),
            pl.BlockSpec((1, gw, 1), lambda i, pt: (i, 0, 0)),
            pl.BlockSpec((1,) + mshape, lambda i, pt: (i, 0, 0, 0)),
        ],
        out_specs=pl.BlockSpec((1, gw, reps), lambda i, pt: (i, 0, 0)),
        scratch_shapes=[
            pltpu.VMEM((2, chunk, gw, PAGE), F32),
            pltpu.SemaphoreType.DMA((2,)),
            pltpu.VMEM((reps, n_pages, groups, PAGE), F32),
            pltpu.VMEM((reps, gw, PAGE), F32),
            pltpu.VMEM((reps, 2, groups, PAGE), F32),
        ],
    )
    return pl.pallas_call(
        body,
        grid_spec=grid_spec,
        out_shape=jax.ShapeDtypeStruct((bsz, gw, reps), F32),
        compiler_params=_params("arbitrary"),
        name="paged_attn_moba" if moba else "paged_attn_nsa",
    )(page_table, pool_t, q_cols, qbd, k_new.reshape(bsz, gw, 1), v_new.reshape(bsz, gw, 1), row_mask)


CMP_DB = 8


def _compress_weights_t(w1, pe):
    per = PAGE // NSA_BLOCK
    eye = jnp.eye(per, dtype=F32)
    w1r = w1.astype(F32).reshape(2, NSA_BLOCK, HD, HD)
    wd = jnp.einsum("klde,gh->kdglhe", w1r, eye).reshape(2, HD // CMP_DB, CMP_DB * PAGE, per * HD)
    ped = jnp.tile(pe.astype(F32).transpose(0, 2, 1), (1, 1, per)).reshape(2, HD // CMP_DB, 1, CMP_DB * PAGE)
    return wd.astype(BF16), ped


def _nsa_cmp_sample_body(pt_ref, pool_ref, q_ref, wd_ref, ped_ref, w2_ref, oc_ref, sel_ref, buf, sem, tok_ref,
                         *, layer):
    b = pl.program_id(0)
    j = pl.program_id(1)
    bsz, n_pages = pt_ref.shape
    t = b * 2 + j
    slot = t % 2

    def copy(bb, kv, p, g, sl):
        row = pl.multiple_of(((sl * NSA_KVH + g) * n_pages + p) * HD, HD)
        return pltpu.make_async_copy(pool_ref.at[pt_ref[bb, p], layer, kv, g], buf.at[pl.ds(row, HD)], sem.at[sl])

    def each(fn):
        def run(bb, kv, sl):
            def page(p, _):
                for g in range(NSA_KVH):
                    fn(copy(bb, kv, p, g, sl))
                return 0
            lax.fori_loop(0, n_pages, page, 0)
        return run

    start = each(lambda c: c.start())
    wait = each(lambda c: c.wait())

    @pl.when(t == 0)
    def _():
        start(0, 0, 0)

    @pl.when(t + 1 < 2 * bsz)
    def _():
        start((t + 1) // 2, (t + 1) % 2, 1 - slot)

    wait(b, j, slot)

    for g in range(NSA_KVH):
        base = (slot * NSA_KVH + g) * n_pages * HD

        def dstep(dd, acc):
            parts = [buf[pl.ds(base + dd * CMP_DB + u, n_pages, stride=HD), :] for u in range(CMP_DB)]
            a = jnp.concatenate(parts, axis=1) + ped_ref[0, dd]
            return acc + _dot(a.astype(BF16), wd_ref[0, dd])

        acc = lax.fori_loop(0, HD // CMP_DB, dstep, jnp.zeros((n_pages, PAGE), F32))
        tok_ref[j, g] = _dot(jax.nn.gelu(acc).astype(BF16), w2_ref[0])

    @pl.when(j == 1)
    def _():
        pos = n_pages * PAGE
        per = PAGE // NSA_BLOCK
        n_idx = (lax.broadcasted_iota(jnp.int32, (per, n_pages), 1) * per
                 + lax.broadcasted_iota(jnp.int32, (per, n_pages), 0))
        complete = n_idx * NSA_BLOCK + (NSA_BLOCK - 1) <= pos
        cur = pos // NSA_BLOCK
        forced = (n_idx == 0) | (n_idx == cur) | (n_idx == cur - 1)
        k_past = min(NSA_TOPN, per * n_pages + 1) - 1
        for g in range(NSA_KVH):
            qg = q_ref[0, g * NSA_REP:(g + 1) * NSA_REP, :].astype(BF16)
            ck = tok_ref[0, g].astype(BF16)
            cv = tok_ref[1, g].astype(BF16)
            s = [jnp.where(complete[h:h + 1], _dot_nt(qg, ck[:, h * HD:(h + 1) * HD]) * SCALE, NEG) for h in range(per)]
            m = functools.reduce(jnp.maximum, [jnp.max(x, axis=-1, keepdims=True) for x in s])
            e = [jnp.where(complete[h:h + 1], jnp.exp(s[h] - m), 0.0) for h in range(per)]
            den = jnp.maximum(sum(jnp.sum(x, axis=-1, keepdims=True) for x in e), TINY)
            pc = [x / den for x in e]
            oc_ref[0, g * NSA_REP:(g + 1) * NSA_REP, :] = sum(
                _dot(pc[h].astype(BF16), cv[:, h * HD:(h + 1) * HD]) for h in range(per))
            imp = jnp.concatenate([jnp.sum(x, axis=0, keepdims=True) for x in pc], axis=0)
            score = jnp.where(forced, FORCE, jnp.where(complete, imp, NEG))
            sel = jnp.zeros(score.shape, F32)
            work = score
            for _ in range(k_past):
                mx = jnp.max(jnp.max(work, axis=-1, keepdims=True), axis=0, keepdims=True)
                cand = jnp.where(work == mx, n_idx, per * n_pages)
                first = jnp.min(jnp.min(cand, axis=-1, keepdims=True), axis=0, keepdims=True)
                pick = n_idx == first
                sel = jnp.where(pick, 1.0, sel)
                work = jnp.where(pick, -jnp.inf, work)
            sel_ref[0, g] = jnp.where(score > 0.5 * NEG, sel, 0.0)


def _nsa_cmp_sample(page_table, pool_t, layer, q, wd, ped, w2b):
    bsz, n_pages = page_table.shape
    per = PAGE // NSA_BLOCK
    grid_spec = pltpu.PrefetchScalarGridSpec(
        num_scalar_prefetch=1,
        grid=(bsz, 2),
        in_specs=[
            pl.BlockSpec(memory_space=pl.ANY),
            pl.BlockSpec((1, NSA_H, HD), lambda i, j, pt: (i, 0, 0)),
            pl.BlockSpec((1, HD // CMP_DB, CMP_DB * PAGE, per * HD), lambda i, j, pt: (j, 0, 0, 0)),
            pl.BlockSpec((1, HD // CMP_DB, 1, CMP_DB * PAGE), lambda i, j, pt: (j, 0, 0, 0)),
            pl.BlockSpec((1, per * HD, per * HD), lambda i, j, pt: (j, 0, 0)),
        ],
        out_specs=[pl.BlockSpec((1, NSA_H, HD), lambda i, j, pt: (i, 0, 0)),
                   pl.BlockSpec((1, NSA_KVH, per, n_pages), lambda i, j, pt: (i, 0, 0, 0))],
        scratch_shapes=[
            pltpu.VMEM((2 * NSA_KVH * n_pages * HD, PAGE), F32),
            pltpu.SemaphoreType.DMA((2,)),
            pltpu.VMEM((2, NSA_KVH, n_pages, per * HD), F32),
        ],
    )
    return pl.pallas_call(
        functools.partial(_nsa_cmp_sample_body, layer=layer),
        grid_spec=grid_spec,
        out_shape=[jax.ShapeDtypeStruct((bsz, NSA_H, HD), F32),
                   jax.ShapeDtypeStruct((bsz, NSA_KVH, per, n_pages), F32)],
        compiler_params=_params("arbitrary", "arbitrary"),
        name="nsa_cmp_sample",
    )(page_table, pool_t, q, wd, ped, w2b)


def _win_sample_body(qc_ref, kt_ref, vt_ref, kn_ref, vn_ref, o_ref):
    kt = kt_ref[0, 0]
    vt = vt_ref[0, 0]
    gw, lbuf = kt.shape
    j = lax.broadcasted_iota(jnp.int32, (NSA_KVH, lbuf), 1)
    mask = j > lbuf - NSA_WINDOW
    for r in range(NSA_REP):
        qcol = qc_ref[0, :, r:r + 1]
        s = jnp.where(mask, _head_sums(kt * qcol, NSA_KVH) * SCALE, NEG)
        s_own = _head_sums(jnp.broadcast_to(kn_ref[0] * qcol, (gw, lbuf)), NSA_KVH) * SCALE
        m = jnp.maximum(jnp.max(s, axis=-1, keepdims=True), s_own)
        p = jnp.where(mask, jnp.exp(s - m), 0.0)
        p_own = jnp.exp(s_own - m)
        den = jnp.maximum(jnp.sum(p, axis=-1, keepdims=True) + p_own, TINY)
        tot = jnp.sum(vt * _expand_heads(p, gw), axis=-1, keepdims=True)
        o_ref[0, :, r:r + 1] = (tot + _expand_heads(p_own, gw)[:, 0:1] * vn_ref[0]) / _expand_heads(den, gw)[:, 0:1]


def _win_sample(q_cols, wbuf_t, k_new, v_new):
    bsz, gw, reps = q_cols.shape
    lbuf = wbuf_t.shape[-1]
    col = pl.BlockSpec((1, gw, 1), lambda i: (i, 0, 0))
    return pl.pallas_call(
        _win_sample_body,
        grid=(bsz,),
        in_specs=[pl.BlockSpec((1, gw, reps), lambda i: (i, 0, 0)),
                  pl.BlockSpec((1, 1, gw, lbuf), lambda i: (i, 0, 0, 0)),
                  pl.BlockSpec((1, 1, gw, lbuf), lambda i: (i, 1, 0, 0)), col, col],
        out_specs=pl.BlockSpec((1, gw, reps), lambda i: (i, 0, 0)),
        out_shape=jax.ShapeDtypeStruct((bsz, gw, reps), F32),
        compiler_params=_params("parallel"),
        name="nsa_win_sample",
    )(q_cols, wbuf_t, wbuf_t, k_new.reshape(bsz, gw, 1), v_new.reshape(bsz, gw, 1))


def _gate_combine_body(g_ref, c_ref, s_ref, w_ref, o_ref):
    gt = jax.nn.sigmoid(g_ref[...])
    o_ref[...] = gt[0] * c_ref[...] + gt[1] * s_ref[...] + gt[2] * w_ref[...]


def _gate_combine(gate_logits, o_c, o_s, o_w):
    return pl.pallas_call(
        _gate_combine_body,
        out_shape=jax.ShapeDtypeStruct(o_c.shape, F32),
        name="nsa_gate_combine",
    )(gate_logits, o_c, o_s, o_w)


ODD_NQ, ODD_NQR, ODD_KV = 0, NSA_QW, 2 * NSA_QW
ODD_HQ = ODD_KV + 6 * NSA_KVW
ODD_EQ = ODD_HQ + 3 * HGRN_W
ODD_GATE = ODD_EQ + MEM_W
ODD_NG = ODD_GATE + MIX_W
ODD_N = 4096


def _odd_weights(w_in):
    offs = np.cumsum([0, NSA_QW] + [NSA_KVW] * 6 + [3 * NSA_H] + [HGRN_W] * 3 + [MEM_W, MIX_W])
    nq = w_in[:, offs[0]:offs[1]]
    kv = w_in[:, offs[1]:offs[7]]
    ng = w_in[:, offs[7]:offs[8]]
    rest = w_in[:, offs[8]:]
    pad = jnp.zeros((w_in.shape[0], ODD_N - ODD_NG - 3 * NSA_H), w_in.dtype)
    return jnp.concatenate([nq, nq, kv, rest, ng, pad], axis=1)


def _stack_heads(a, tq, width):
    t = a.shape[0]
    a = a.reshape(t // tq, tq, NSA_KVH, NSA_REP, width).transpose(2, 0, 3, 1, 4)
    return a.reshape(NSA_KVH, t // tq, NSA_REP * tq, width)


def _unstack_heads(a, tq):
    g, nt, _, width = a.shape
    a = a.reshape(g, nt, NSA_REP, tq, width).transpose(1, 3, 0, 2, 4)
    return a.reshape(nt * tq, g * NSA_REP * width)


def _group_kv(k, v):
    t = k.shape[0]
    return jnp.concatenate([k.reshape(t, NSA_KVH, HD), v.reshape(t, NSA_KVH, HD)], axis=-1).transpose(1, 0, 2)


def _pad_lanes(a):
    return jnp.concatenate([a, jnp.zeros_like(a)], axis=-1)


def _odd_prompt(x, cos, sin, g, w_aug_bf16, w_o_bf16, mem_k, mem_v, lb, cmp_w, final_g, tm, tq, kc):
    t = x.shape[0]
    assert t % NSA_BLOCK == 0
    kvo = lambda j: ODD_KV + j * NSA_KVW
    proj = _norm_proj(x, g, w_aug_bf16, cos, sin, ((ODD_NQR, ODD_KV), (kvo(2), kvo(3)), (kvo(4), kvo(5))), tm)
    ck, cv, sk, sv, wk, wv = (proj[:, kvo(j):kvo(j + 1)] for j in range(6))
    w1b, w2b, peb = cmp_w
    nb = t // NSA_BLOCK
    cmp_tok = _compress(jnp.stack([ck, cv]).reshape(2, nb, NSA_BLOCK * NSA_KVW), w1b, w2b, peb)
    ckv = _group_kv(cmp_tok[0], cmp_tok[1])
    q_st = _pad_lanes(_stack_heads(proj[:, ODD_NQ:ODD_NQ + NSA_QW], tq, HD))
    qr_st = _pad_lanes(_stack_heads(proj[:, ODD_NQR:ODD_NQR + NSA_QW], tq, HD))
    gt_st = _stack_heads(proj[:, ODD_NG:ODD_NG + 3 * NSA_H], tq, 3)
    onehot = ((jnp.arange(t)[:, None] // NSA_BLOCK) % HD == jnp.arange(HD)[None, :]).astype(F32)
    ska = jnp.concatenate([sk.reshape(t, NSA_KVH, HD), jnp.broadcast_to(onehot[:, None], (t, NSA_KVH, HD))], axis=-1)
    svo = jnp.concatenate([jnp.ones((t, NSA_KVH, HD), F32), sv.reshape(t, NSA_KVH, HD)], axis=-1)
    o_nsa = _nsa_prompt(q_st, qr_st, gt_st, ckv, ska.transpose(1, 0, 2).astype(BF16),
                        svo.transpose(1, 0, 2).astype(BF16), _group_kv(wk, wv).astype(BF16), tq, kc)
    o_nsa = _unstack_heads(o_nsa[..., HD:], tq)
    hq, hf, hi = (proj[:, ODD_HQ + j * HGRN_W:ODD_HQ + (j + 1) * HGRN_W] for j in range(3))
    o_hg, s_hg = _hgrn_prompt(hq, hf, hi, lb.reshape(1, HGRN_W))
    o_mem = _mem_prompt(proj[:, ODD_EQ:ODD_GATE], mem_k, mem_v, tm)
    y = _mix_out(x, o_nsa, o_hg, o_mem, proj[:, ODD_GATE:ODD_NG], w_o_bf16, final_g, tm)
    rows = lambda a, b: jnp.stack([a.reshape(t, NSA_KVH, HD), b.reshape(t, NSA_KVH, HD)], axis=1)
    return y, rows(ck, cv), rows(sk, sv), rows(wk, wv), _diag_blocks(s_hg, HGRN_H)

def _heads_major(a, h):
    t = a.shape[0]
    return a.reshape(t, h, HD).transpose(1, 0, 2)


def _even_prompt(x, cos, sin, g, w_in_bf16, w_o_bf16, mem_k, mem_v, tm):
    t = x.shape[0]
    o_mq = 3 * RET_W
    o_eq = 3 * RET_W + 3 * MOBA_W
    o_gate = o_eq + MEM_W
    proj = _norm_proj(x, g, w_in_bf16, cos, sin, ((0, 2 * RET_W), (o_mq, o_mq + 2 * MOBA_W)), tm)
    rq, rk, rv = proj[:, :RET_W], proj[:, RET_W:2 * RET_W], proj[:, 2 * RET_W:3 * RET_W]
    mq = proj[:, o_mq:o_mq + MOBA_W]
    mk = proj[:, o_mq + MOBA_W:o_mq + 2 * MOBA_W]
    mv = proj[:, o_mq + 2 * MOBA_W:o_eq]
    eq = proj[:, o_eq:o_gate]
    gate = proj[:, o_gate:]
    o_ret, s_ret = _retention_prompt(rq, rk, rv)
    kmean = _block_mean(mk, MOBA_BLOCK)
    zq = jnp.zeros((MOBA_H, t, HD), F32)
    q_pad = jnp.concatenate([_heads_major(mq, MOBA_H), zq], axis=-1)
    vh = _heads_major(mv, MOBA_H)
    nblk = t // MOBA_BLOCK
    onehot = (jnp.arange(t)[:, None] // MOBA_BLOCK == jnp.arange(HD)[None, :]).astype(F32)
    ka = jnp.concatenate([_heads_major(mk, MOBA_H), jnp.broadcast_to(onehot, (MOBA_H, t, HD))], axis=-1).astype(BF16)
    vo = jnp.concatenate([jnp.ones_like(vh), vh], axis=-1).astype(BF16)
    km = jnp.pad(_heads_major(kmean, MOBA_H), ((0, 0), (HD, LANES - HD - nblk), (0, HD)))
    o_moba = _moba_prompt(q_pad, ka, vo, km)
    o_moba = o_moba[:, :, HD:].transpose(1, 0, 2).reshape(t, MOBA_W)
    o_mem = _mem_prompt(eq, mem_k, mem_v, tm)
    y = _mix_out(x, o_ret, o_moba, o_mem, gate, w_o_bf16, None, tm)
    rows = proj[:, o_mq + MOBA_W:o_eq].reshape(t, 2, MOBA_H, HD)
    return y, rows, _diag_blocks(s_ret, RET_H)


def _pages_t(pool, width):
    n_pool, n_layer = pool.shape[0], pool.shape[1]
    return pool.transpose(0, 1, 3, 4, 5, 2).reshape(n_pool, n_layer, 2, width, PAGE)


def _mem_t(cache):
    bsz, n = cache.shape[0], cache.shape[1]
    return cache.transpose(0, 2, 3, 4, 1).reshape(bsz, 2, MEM_W, n)


def _even_sample(x, cos, sin, g, w_in_bf16, w_o_bf16, mem_cache, state, page_table, pool, layer):
    bsz = x.shape[0]
    o_mq = 3 * RET_W
    o_eq = 3 * RET_W + 3 * MOBA_W
    o_gate = o_eq + MEM_W
    proj = _norm_proj(x, g, w_in_bf16, cos, sin, ((0, 2 * RET_W), (o_mq, o_mq + 2 * MOBA_W)), bsz)
    rq, rk, rv = proj[:, :RET_W], proj[:, RET_W:2 * RET_W], proj[:, 2 * RET_W:3 * RET_W]
    mq = proj[:, o_mq:o_mq + MOBA_W]
    mk = proj[:, o_mq + MOBA_W:o_mq + 2 * MOBA_W]
    mv = proj[:, o_mq + 2 * MOBA_W:o_eq]
    gamma = np.repeat(1.0 - np.power(2.0, -5.0 - np.arange(RET_H, dtype=np.float64)), HD)
    o_ret, s_ret = _state_step(state, rq, rk, jnp.broadcast_to(jnp.asarray(gamma, F32), (bsz, RET_W)), rv, "ret")
    o_moba = _paged_attn(page_table, _pages_t(pool, MOBA_W), layer, mq.reshape(bsz, MOBA_W, 1), mk, mv, None,
                         MOBA_H, 1, True).reshape(bsz, MOBA_W)
    o_mem = _mem_sample(proj[:, o_eq:o_gate], _mem_t(mem_cache))
    y = _mix_out(x, o_ret, o_moba, o_mem, proj[:, o_gate:], w_o_bf16, None, bsz)
    rows = proj[:, o_mq + MOBA_W:o_eq].reshape(bsz, 1, 2, MOBA_H, HD)
    return y, rows, s_ret


def _cols(a):
    bsz = a.shape[0]
    return a.reshape(bsz, NSA_KVH, NSA_REP, HD).transpose(0, 1, 3, 2).reshape(bsz, NSA_KVW, NSA_REP)


def _uncols(a):
    bsz = a.shape[0]
    return a.reshape(bsz, NSA_KVH, HD, NSA_REP).transpose(0, 1, 3, 2).reshape(bsz, NSA_QW)


def _odd_sample(x, cos, sin, g, w_aug_bf16, w_o_bf16, mem_cache, state, lb, page_table, cmp_pool, slc_pool, wbuf,
                layer, cmp_w_t, final_g):
    bsz = x.shape[0]
    lbuf = wbuf.shape[1]
    kvo = lambda j: ODD_KV + j * NSA_KVW
    proj = _norm_proj(x, g, w_aug_bf16, cos, sin, ((ODD_NQR, ODD_KV), (kvo(2), kvo(3)), (kvo(4), kvo(5))), bsz)
    ck, cv, sk, sv, wk, wv = (proj[:, kvo(j):kvo(j + 1)] for j in range(6))
    wd, ped, w2t = cmp_w_t
    n_pool, n_layer = cmp_pool.shape[0], cmp_pool.shape[1]
    cmp_t = cmp_pool.transpose(0, 1, 3, 4, 5, 2)
    o_c, sel = _nsa_cmp_sample(page_table, cmp_t, layer, proj[:, ODD_NQ:ODD_NQ + NSA_QW].reshape(bsz, NSA_H, HD),
                               wd, ped, w2t)
    row_mask = jnp.repeat(sel.transpose(0, 3, 1, 2), NSA_BLOCK, axis=-1)
    q_cols = _cols(proj[:, ODD_NQR:ODD_NQR + NSA_QW])
    o_s = _paged_attn(page_table, _pages_t(slc_pool, NSA_KVW), layer, q_cols, sk, sv, row_mask, NSA_KVH, NSA_REP, False)
    wbuf_t = wbuf.transpose(0, 2, 3, 4, 1).reshape(bsz, 2, NSA_KVW, lbuf)
    o_w = _win_sample(q_cols, wbuf_t, wk, wv)
    ng = proj[:, ODD_NG:ODD_NG + 3 * NSA_H].reshape(bsz, NSA_H, 3)
    gate_logits = jnp.repeat(ng.transpose(2, 0, 1), HD, axis=-1)
    o_nsa = _gate_combine(gate_logits, o_c.reshape(bsz, NSA_QW), _uncols(o_s), _uncols(o_w))
    hq, hf, hi = (proj[:, ODD_HQ + j * HGRN_W:ODD_HQ + (j + 1) * HGRN_W] for j in range(3))
    o_hg, s_hg = _state_step(state, hq, hf, jnp.broadcast_to(lb.reshape(1, HGRN_W), (bsz, HGRN_W)), hi, "hgrn")
    o_mem = _mem_sample(proj[:, ODD_EQ:ODD_GATE], _mem_t(mem_cache))
    y = _mix_out(x, o_nsa, o_hg, o_mem, proj[:, ODD_GATE:ODD_NG], w_o_bf16, final_g, bsz)
    rows = lambda a, b: jnp.stack([a.reshape(bsz, 1, NSA_KVH, HD), b.reshape(bsz, 1, NSA_KVH, HD)], axis=2)
    win = jnp.concatenate([wbuf.astype(F32), rows(wk, wv)], axis=1)[:, -lbuf:]
    return y, rows(ck, cv), rows(sk, sv), win, s_hg


def kernel(x_prompt, x_sample, mem_prompt, cache_moba_kv, state_ret, cache_nsa_cmp_kv, cache_nsa_slc_kv,
           cache_nsa_win_kv, state_hgrn, cache_mem_kv, page_table, norm_g, mem_norm_g, w_mem_kv, w_in_even,
           w_in_odd, w_out, cmp_w1, cmp_w2, cmp_pe, hgrn_lb_logits, final_g):
    bp, tp, d = x_prompt.shape
    bs, ts, _ = x_sample.shape
    depth = w_out.shape[0]
    assert bp == 1 and ts == 1 and depth == 2
    n_mem = mem_prompt.shape[1]
    past_len = page_table.shape[1] * PAGE
    assert past_len % MOBA_BLOCK == 0 and cache_moba_kv.shape[2] == PAGE
    tm, tq, kc = 256, 128, 256
    xp, xs, mem = x_prompt[0], x_sample[:, 0], mem_prompt[0]
    cos_p, sin_p = _rope_tables(jnp.arange(tp, dtype=jnp.int32))
    cos_s, sin_s = (jnp.broadcast_to(a, (bs, LANES)) for a in _rope_tables(jnp.full((1,), past_len, jnp.int32)))
    lb_prob = jax.nn.softmax(hgrn_lb_logits.astype(F32), axis=0)
    lb_all = jnp.cumsum(lb_prob, axis=0) - lb_prob[0]
    lw_p = min(NSA_WINDOW, tp)

    def mem_kv(layer):
        kv = _norm_proj(mem, mem_norm_g[layer], w_mem_kv[layer].astype(BF16), cos_p[:n_mem], sin_p[:n_mem], (), n_mem)
        return kv.reshape(n_mem, 2, MEM_W)

    w_in0 = w_in_even[0].astype(BF16)
    w_o0 = w_out[0].astype(BF16)
    mkv0 = mem_kv(0)
    hp, moba_p, ret_p = _even_prompt(xp, cos_p, sin_p, norm_g[0], w_in0, w_o0, mkv0[:, 0], mkv0[:, 1], tm)
    hs, moba_s, ret_s = _even_sample(xs, cos_s, sin_s, norm_g[0], w_in0, w_o0, cache_mem_kv[0], state_ret[0],
                                     page_table, cache_moba_kv, 0)

    w_in1 = _odd_weights(w_in_odd[0]).astype(BF16)
    w_o1 = w_out[1].astype(BF16)
    mkv1 = mem_kv(1)
    cmp_w = _compress_weights(cmp_w1[0], cmp_w2[0], cmp_pe[0])
    wd, ped = _compress_weights_t(cmp_w1[0], cmp_pe[0])
    yp, cmp_p, slc_p, win_p, hg_p = _odd_prompt(hp, cos_p, sin_p, norm_g[1], w_in1, w_o1, mkv1[:, 0], mkv1[:, 1],
                                                lb_all[1], cmp_w, final_g, tm, tq, kc)
    ys, cmp_s, slc_s, win_s, hg_s = _odd_sample(hs, cos_s, sin_s, norm_g[1], w_in1, w_o1, cache_mem_kv[1],
                                                state_hgrn[0], lb_all[1], page_table, cache_nsa_cmp_kv,
                                                cache_nsa_slc_kv, cache_nsa_win_kv[0], 0, (wd, ped, cmp_w[1]), final_g)

    return (yp[None], ys[:, None], moba_p[None, None], moba_s[:, None], ret_p[None, None], ret_s[None],
            cmp_p[None, None], cmp_s[:, None], slc_p[None, None], slc_s[:, None], win_p[None, tp - lw_p:][None],
            win_s[None], hg_p[None, None], hg_s[None], jnp.stack([mkv0, mkv1]).reshape(depth, 1, n_mem, 2, MEM_H, HD))
```

```python
import functools
import math

import jax
import jax.numpy as jnp
import numpy as np
from jax import lax
from jax.experimental import pallas as pl
from jax.experimental.pallas import tpu as pltpu

F32 = jnp.float32
BF16 = jnp.bfloat16
HIGHEST = lax.Precision.HIGHEST

HD = 64
RET_H, MOBA_H, MEM_H, NSA_H, NSA_KVH, HGRN_H = 6, 6, 4, 8, 2, 4
NSA_REP = NSA_H // NSA_KVH
RET_W, MOBA_W, MEM_W = RET_H * HD, MOBA_H * HD, MEM_H * HD
NSA_QW, NSA_KVW, HGRN_W = NSA_H * HD, NSA_KVH * HD, HGRN_H * HD
MIX_W = RET_W + MOBA_W + MEM_W
PAGE = 128
RET_CHUNK, HGRN_CHUNK = 128, 64
MOBA_BLOCK, MOBA_TOPK = 256, 3
NSA_BLOCK, NSA_TOPN, NSA_WINDOW = 64, 16, 512
ROPE_THETA = 10000.0
EPS = 1e-6
NEG = -1e30
FORCE = 1e30
TINY = 1e-30
SCALE = HD ** -0.5
LANES = 128
VMEM_LIMIT = 56 * 1024 * 1024


def _params(*sem):
    return pltpu.CompilerParams(dimension_semantics=sem, vmem_limit_bytes=VMEM_LIMIT)


def _dot(a, b, precision=None):
    return jnp.dot(a, b, preferred_element_type=F32, precision=precision)


def _dot_nt(a, b, precision=None):
    return lax.dot_general(a, b, (((1,), (1,)), ((), ())), preferred_element_type=F32, precision=precision)


def _dot_tn(a, b, precision=None):
    return lax.dot_general(a, b, (((0,), (0,)), ((), ())), preferred_element_type=F32, precision=precision)


def _block_ones(width, value=1.0):
    r = lax.broadcasted_iota(jnp.int32, (width, width), 0) // HD
    c = lax.broadcasted_iota(jnp.int32, (width, width), 1) // HD
    return jnp.where(r == c, value, 0.0).astype(F32)


def _rope_tile(a, cos, sin, first_half):
    rot = jnp.where(first_half, pltpu.roll(a, LANES - HD // 2, 1), pltpu.roll(a, HD // 2, 1))
    return a * cos + rot * sin


def _norm_proj_body(x_ref, g_ref, w_ref, cos_ref, sin_ref, o_ref, *, rope_tiles, n_chunk):
    x = x_ref[...]
    ms = jnp.mean(x * x, axis=-1, keepdims=True)
    y = (x * lax.rsqrt(ms + EPS) * g_ref[...]).astype(BF16)
    n = o_ref.shape[1]
    lane = lax.broadcasted_iota(jnp.int32, (x.shape[0], LANES), 1)
    first_half = (lane % HD) < HD // 2
    for c0 in range(0, n, n_chunk):
        acc = _dot(y, w_ref[:, c0:c0 + n_chunk])
        for j in range(n_chunk // LANES):
            tile = (c0 // LANES) + j
            a = acc[:, j * LANES:(j + 1) * LANES]
            if tile in rope_tiles:
                a = _rope_tile(a, cos_ref[...], sin_ref[...], first_half)
            o_ref[:, tile * LANES:(tile + 1) * LANES] = a


def _norm_proj(x, g, w_bf16, cos, sin, rope_cols, tm):
    m, d = x.shape
    n = w_bf16.shape[1]
    assert m % tm == 0 and n % LANES == 0
    tiles = n // LANES
    k = next(c for c in (4, 3, 2, 1) if tiles % c == 0)
    rope_tiles = frozenset(t for a, b in rope_cols for t in range(a // LANES, b // LANES))
    body = functools.partial(_norm_proj_body, rope_tiles=rope_tiles, n_chunk=k * LANES)
    return pl.pallas_call(
        body,
        grid=(m // tm,),
        in_specs=[
            pl.BlockSpec((tm, d), lambda i: (i, 0)),
            pl.BlockSpec((1, d), lambda i: (0, 0)),
            pl.BlockSpec((d, n), lambda i: (0, 0)),
            pl.BlockSpec((tm, LANES), lambda i: (i, 0)),
            pl.BlockSpec((tm, LANES), lambda i: (i, 0)),
        ],
        out_specs=pl.BlockSpec((tm, n), lambda i: (i, 0)),
        out_shape=jax.ShapeDtypeStruct((m, n), F32),
        compiler_params=_params("parallel"),
        name="norm_proj",
    )(x, g.reshape(1, d), w_bf16, cos, sin)


def _rope_tables(pos):
    half = HD // 2
    inv = ROPE_THETA ** (-jnp.arange(half, dtype=F32) / half)
    ang = pos.astype(F32)[:, None] * inv[None, :]
    cos, sin = jnp.cos(ang), jnp.sin(ang)
    return jnp.concatenate([cos, cos, cos, cos], -1), jnp.concatenate([-sin, sin, -sin, sin], -1)


def _retention_consts(c):
    lg = np.log(1.0 - np.power(2.0, -5.0 - np.arange(RET_H, dtype=np.float64)))
    ti = np.arange(c, dtype=np.float64)
    causal = ti[:, None] >= ti[None, :]
    d_in = np.where(causal[None], np.exp(np.where(causal, ti[:, None] - ti[None, :], 0.0)[None] * lg[:, None, None]), 0.0)
    q_dec = np.repeat(np.exp((ti[:, None] + 1.0) * lg[None, :]), HD, axis=1)
    k_dec = np.repeat(np.exp((c - 1.0 - ti)[:, None] * lg[None, :]), HD, axis=1)
    c_dec = np.repeat(np.exp(c * lg), HD)
    head = np.arange(RET_W) // HD
    bd = (head[:, None] == head[None, :]).astype(np.float64)
    cmat = bd * c_dec[:, None]
    f = lambda a: jnp.asarray(a, dtype=F32)
    return f(d_in), f(q_dec), f(k_dec), f(cmat), f(bd)


def _retention_body(q_ref, k_ref, v_ref, din_ref, qdec_ref, kdec_ref, cmat_ref, bd_ref, o_ref, s_ref):
    @pl.when(pl.program_id(0) == 0)
    def _():
        s_ref[...] = jnp.zeros_like(s_ref)

    q = q_ref[...]
    k = k_ref[...] * SCALE
    v = v_ref[...]
    c, w = q.shape
    head = lax.broadcasted_iota(jnp.int32, (c, w), 1) // HD
    s = s_ref[...]
    qb, kb, vb = q.astype(BF16), k.astype(BF16), v.astype(BF16)
    o = _dot(qb, s.astype(BF16)) * qdec_ref[...]
    for h in range(RET_H):
        mh = head == h
        att = _dot_nt(jnp.where(mh, qb, jnp.zeros_like(qb)), kb) * din_ref[h]
        o = o + jnp.where(mh, _dot(att.astype(BF16), vb), 0.0)
    s_ref[...] = s * cmat_ref[...] + _dot_tn((k * kdec_ref[...]).astype(BF16), vb) * bd_ref[...]
    seg = bd_ref[...] * (1.0 / HD)
    xc = o - _dot(o, seg, HIGHEST)
    var = _dot(xc * xc, seg, HIGHEST)
    o_ref[...] = xc * lax.rsqrt(var + EPS)


def _retention_prompt(q, k, v):
    t, w = q.shape
    c = math.gcd(t, RET_CHUNK)
    d_in, q_dec, k_dec, cmat, bd = _retention_consts(c)
    row = pl.BlockSpec((c, w), lambda i: (i, 0))
    const2 = lambda shape: pl.BlockSpec(shape, lambda i: (0,) * len(shape))
    return pl.pallas_call(
        _retention_body,
        grid=(t // c,),
        in_specs=[row, row, row, const2((RET_H, c, c)), const2((c, w)), const2((c, w)), const2((w, w)), const2((w, w))],
        out_specs=[row, const2((w, w))],
        out_shape=[jax.ShapeDtypeStruct((t, w), F32), jax.ShapeDtypeStruct((w, w), F32)],
        compiler_params=_params("arbitrary"),
        name="retention_prompt",
    )(q, k, v, d_in, q_dec, k_dec, cmat, bd)


def _diag_blocks(s, h):
    s4 = s.reshape(h, HD, h, HD)
    return jnp.stack([s4[i, :, i, :] for i in range(h)], axis=0)


def _block_mean_body(k_ref, o_ref):
    o_ref[0] = jnp.mean(k_ref[...], axis=0, keepdims=True)


def _block_mean(k, blk):
    t, w = k.shape
    assert t % blk == 0
    out = pl.pallas_call(
        _block_mean_body,
        grid=(t // blk,),
        in_specs=[pl.BlockSpec((blk, w), lambda i: (i, 0))],
        out_specs=pl.BlockSpec((1, 1, w), lambda i: (i, 0, 0)),
        out_shape=jax.ShapeDtypeStruct((t // blk, 1, w), F32),
        compiler_params=_params("parallel"),
        name="block_mean",
    )(k)
    return out.reshape(t // blk, w)


def _topk_mask(score, k):
    n = score.shape[-1]
    idx = lax.broadcasted_iota(jnp.int32, score.shape, score.ndim - 1)
    sel = jnp.zeros(score.shape, F32)
    work = score
    for _ in range(k):
        m = jnp.max(work, axis=-1, keepdims=True)
        first = jnp.min(jnp.where(work == m, idx, n), axis=-1, keepdims=True)
        pick = idx == first
        sel = jnp.where(pick, 1.0, sel)
        work = jnp.where(pick, -jnp.inf, work)
    return sel


def _tile_loop(lo, hi, fn, unroll=8):
    def group(g, _):
        for u in range(unroll):
            fn(lo + unroll * g + u)
        return 0
    n = jnp.maximum(hi - lo, 0)
    main = n // unroll
    lax.fori_loop(0, main, group, 0)
    done = lo + main * unroll
    rem = n - main * unroll
    size = unroll // 2
    while size >= 1:
        @pl.when((rem & size) != 0)
        def _(done=done, size=size):
            for u in range(size):
                fn(done + u)
        done = done + (rem & size)
        size //= 2


def _lane_fold_max(s):
    out = s[:, 0:LANES]
    for c in range(1, s.shape[1] // LANES):
        out = jnp.maximum(out, s[:, c * LANES:(c + 1) * LANES])
    return out


def _moba_prompt_body(q_ref, ka_ref, vo_ref, kmean_ref, o_ref, s_ref, m_ref, acc_ref, *, topk):
    qi = pl.program_id(1)
    q = q_ref[0]
    tq = q.shape[0]
    gate = _dot_nt(q, kmean_ref[0], HIGHEST)
    blk = lax.broadcasted_iota(jnp.int32, (tq, LANES), 1) - HD
    past = (blk >= 0) & (blk < qi)
    sel = jnp.where(past, _topk_mask(jnp.where(past, gate, NEG), topk), 0.0)
    bias = jnp.where((blk < 0) | (blk == qi) | (sel > 0.5), 0.0, NEG)
    qa = (q * SCALE + bias).astype(BF16)
    m_ref[...] = jnp.full(m_ref.shape, NEG, F32)

    def score(j):
        start = pl.multiple_of(j * tq, tq)
        s = _dot_nt(qa, ka_ref[0, pl.ds(start, tq), :])
        s_ref[j] = s
        m_ref[...] = jnp.maximum(m_ref[...], _lane_fold_max(s))

    _tile_loop(0, qi, score)
    r = lax.broadcasted_iota(jnp.int32, (tq, tq), 0)
    c = lax.broadcasted_iota(jnp.int32, (tq, tq), 1)
    own = pl.multiple_of(qi * tq, tq)
    s = jnp.where(c <= r, _dot_nt(qa, ka_ref[0, pl.ds(own, tq), :]), NEG)
    s_ref[qi] = s
    m = jnp.max(jnp.maximum(m_ref[...], _lane_fold_max(s)), axis=-1, keepdims=True)

    acc_ref[...] = jnp.zeros(acc_ref.shape, F32)

    def accumulate(j):
        start = pl.multiple_of(j * tq, tq)
        p = jnp.exp(s_ref[j] - m).astype(BF16)
        acc_ref[...] += _dot(p, vo_ref[0, pl.ds(start, tq), :])

    _tile_loop(0, qi + 1, accumulate)
    acc = acc_ref[...]
    o_ref[0] = acc / jnp.maximum(acc[:, 0:1], TINY)


def _moba_prompt(q_pad, ka_bf16, vo_bf16, kmean_rows):
    h, t, _ = q_pad.shape
    nblk = t // MOBA_BLOCK
    assert t % MOBA_BLOCK == 0 and nblk <= LANES - HD
    return pl.pallas_call(
        functools.partial(_moba_prompt_body, topk=min(MOBA_TOPK, nblk)),
        grid=(h, nblk),
        in_specs=[
            pl.BlockSpec((1, MOBA_BLOCK, LANES), lambda a, i: (a, i, 0)),
            pl.BlockSpec((1, t, LANES), lambda a, i: (a, 0, 0), pipeline_mode=pl.Buffered(1)),
            pl.BlockSpec((1, t, LANES), lambda a, i: (a, 0, 0), pipeline_mode=pl.Buffered(1)),
            pl.BlockSpec((1, LANES, LANES), lambda a, i: (a, 0, 0)),
        ],
        out_specs=pl.BlockSpec((1, MOBA_BLOCK, LANES), lambda a, i: (a, i, 0)),
        out_shape=jax.ShapeDtypeStruct((h, t, LANES), F32),
        scratch_shapes=[pltpu.VMEM((nblk, MOBA_BLOCK, MOBA_BLOCK), F32), pltpu.VMEM((MOBA_BLOCK, LANES), F32),
                        pltpu.VMEM((MOBA_BLOCK, LANES), F32)],
        compiler_params=_params("parallel", "arbitrary"),
        name="moba_prompt",
    )(q_pad, ka_bf16, vo_bf16, kmean_rows)


def _mem_prompt_body(q_ref, k_ref, v_ref, o_ref):
    q = q_ref[...]
    kb = k_ref[...].astype(BF16)
    vb = v_ref[...].astype(BF16)
    head = lax.broadcasted_iota(jnp.int32, q.shape, 1) // HD
    o = jnp.zeros(q.shape, F32)
    for h in range(MEM_H):
        mh = head == h
        s = _dot_nt(jnp.where(mh, q, 0.0).astype(BF16), kb) * SCALE
        e = jnp.exp(s - jnp.max(s, axis=-1, keepdims=True))
        p = e / jnp.sum(e, axis=-1, keepdims=True)
        o = o + jnp.where(mh, _dot(p.astype(BF16), vb), 0.0)
    o_ref[...] = o


def _mem_prompt(q, k, v, tm):
    t, w = q.shape
    n = k.shape[0]
    return pl.pallas_call(
        _mem_prompt_body,
        grid=(t // tm,),
        in_specs=[pl.BlockSpec((tm, w), lambda i: (i, 0)), pl.BlockSpec((n, w), lambda i: (0, 0)),
                  pl.BlockSpec((n, w), lambda i: (0, 0))],
        out_specs=pl.BlockSpec((tm, w), lambda i: (i, 0)),
        out_shape=jax.ShapeDtypeStruct((t, w), F32),
        compiler_params=_params("parallel"),
        name="mem_prompt",
    )(q, k, v)


def _mix_out_body(x_ref, a_ref, b_ref, c_ref, gate_ref, wo_ref, fg_ref, o_ref, *, final_norm):
    gate = gate_ref[...]
    gate = gate * jax.nn.sigmoid(gate)
    y = x_ref[...]
    off = 0
    for ref in (a_ref, b_ref, c_ref):
        w = ref.shape[1]
        mix = (ref[...] * gate[:, off:off + w]).astype(BF16)
        y = y + _dot(mix, wo_ref[off:off + w, :])
        off += w
    if final_norm:
        ms = jnp.mean(y * y, axis=-1, keepdims=True)
        y = y * lax.rsqrt(ms + EPS) * fg_ref[...]
    o_ref[...] = y


def _mix_out(x, a, b, c, gate, wo_bf16, final_g, tm):
    m, d = x.shape
    row = lambda w: pl.BlockSpec((tm, w), lambda i: (i, 0))
    fg = jnp.ones((1, d), F32) if final_g is None else final_g.reshape(1, d).astype(F32)
    body = functools.partial(_mix_out_body, final_norm=final_g is not None)
    return pl.pallas_call(
        body,
        grid=(m // tm,),
        in_specs=[row(d), row(a.shape[1]), row(b.shape[1]), row(c.shape[1]), row(gate.shape[1]),
                  pl.BlockSpec(wo_bf16.shape, lambda i: (0, 0)), pl.BlockSpec((1, d), lambda i: (0, 0))],
        out_specs=row(d),
        out_shape=jax.ShapeDtypeStruct((m, d), F32),
        compiler_params=_params("parallel"),
        name="mix_out",
    )(x, a, b, c, gate, wo_bf16, fg)


def _compress_body(x_ref, pe_ref, w1_ref, w2_ref, o_ref):
    x = (x_ref[0] + pe_ref[0]).astype(BF16)
    hid = jax.nn.gelu(_dot(x, w1_ref[0]))
    o_ref[0] = _dot(hid.astype(BF16), w2_ref[0])


def _compress_weights(w1, w2, pe):
    eye = jnp.eye(NSA_KVH, dtype=F32)
    w1r = w1.astype(F32).reshape(2, NSA_BLOCK, HD, HD)
    w1b = jnp.einsum("klde,gh->klgdhe", w1r, eye).reshape(2, NSA_BLOCK * NSA_KVW, NSA_KVW)
    w2b = jnp.einsum("kde,gh->kgdhe", w2.astype(F32), eye).reshape(2, NSA_KVW, NSA_KVW)
    peb = jnp.broadcast_to(pe.astype(F32)[:, :, None, :], (2, NSA_BLOCK, NSA_KVH, HD)).reshape(2, 1, NSA_BLOCK * NSA_KVW)
    return w1b.astype(BF16), w2b.astype(BF16), peb


def _compress(x, w1b, w2b, peb):
    _, nb, kdim = x.shape
    return pl.pallas_call(
        _compress_body,
        grid=(2,),
        in_specs=[pl.BlockSpec((1, nb, kdim), lambda i: (i, 0, 0)), pl.BlockSpec((1, 1, kdim), lambda i: (i, 0, 0)),
                  pl.BlockSpec((1, kdim, NSA_KVW), lambda i: (i, 0, 0)),
                  pl.BlockSpec((1, NSA_KVW, NSA_KVW), lambda i: (i, 0, 0))],
        out_specs=pl.BlockSpec((1, nb, NSA_KVW), lambda i: (i, 0, 0)),
        out_shape=jax.ShapeDtypeStruct((2, nb, NSA_KVW), F32),
        compiler_params=_params("parallel"),
        name="nsa_compress",
    )(x, peb, w1b, w2b)


def _softmax_step(s, mask, kv, carry):
    m, l, acc = carry
    s = jnp.where(mask, s, NEG)
    m_new = jnp.maximum(m, jnp.max(s, axis=-1, keepdims=True))
    alpha = jnp.exp(m - m_new)
    p = jnp.where(mask, jnp.exp(s - m_new), 0.0)
    l = alpha * l + jnp.sum(p, axis=-1, keepdims=True)
    acc = alpha * acc + _dot(p.astype(BF16), kv)
    return m_new, l, acc


def _nsa_prompt_body(q_ref, qr_ref, gt_ref, ckv_ref, ska_ref, svo_ref, wkv_ref, o_ref, s_ref, m_ref, acc_ref,
                     *, tq, kc):
    i = pl.program_id(1)
    t0 = i * tq
    rows = q_ref.shape[2]
    nb = ckv_ref.shape[1]
    q = q_ref[0, 0]
    qrb = (qr_ref[0, 0] * SCALE).astype(BF16)
    tpos = t0 + lax.broadcasted_iota(jnp.int32, (rows, 1), 0) % tq

    ckv = ckv_ref[0].astype(BF16)
    blk = lax.broadcasted_iota(jnp.int32, (nb, rows), 0)
    tpos_l = t0 + lax.broadcasted_iota(jnp.int32, (nb, rows), 1) % tq
    complete = blk * NSA_BLOCK + (NSA_BLOCK - 1) <= tpos_l
    s_c = jnp.where(complete, _dot_nt(ckv, q.astype(BF16)) * SCALE, NEG)
    e = jnp.where(complete, jnp.exp(s_c - jnp.max(s_c, axis=0, keepdims=True)), 0.0)
    pc = e / jnp.maximum(jnp.sum(e, axis=0, keepdims=True), TINY)
    o_c = _dot_tn(pc.astype(BF16), ckv)

    imp = pc[:, 0:tq]
    for r in range(1, NSA_REP):
        imp = imp + pc[:, r * tq:(r + 1) * tq]
    blk_q = lax.broadcasted_iota(jnp.int32, (nb, tq), 0)
    tpos_q = t0 + lax.broadcasted_iota(jnp.int32, (nb, tq), 1)
    cur = tpos_q // NSA_BLOCK
    forced = (blk_q == 0) | (blk_q == cur) | (blk_q == cur - 1)
    complete_q = blk_q * NSA_BLOCK + (NSA_BLOCK - 1) <= tpos_q
    score = jnp.where(forced, FORCE, jnp.where(complete_q, imp, NEG))
    sel = jnp.where(score > 0.5 * NEG, _topk_mask_axis0(score, min(NSA_TOPN, nb)), 0.0)

    init = (jnp.full((rows, 1), NEG, F32), jnp.zeros((rows, 1), F32), jnp.zeros((rows, LANES), F32))

    bias = jnp.where(sel > 0.5, 0.0, NEG).T
    nbp = -(-nb // LANES) * LANES
    if nbp > nb:
        bias = jnp.concatenate([bias, jnp.full((tq, nbp - nb), NEG, F32)], axis=1)
    bias = jnp.concatenate([bias] * NSA_REP, axis=0)
    qrs = qr_ref[0, 0] * SCALE
    upper = lax.broadcasted_iota(jnp.int32, (rows, LANES), 1) >= HD
    n_span = -(-nb // HD)
    qa = []
    for sp in range(n_span):
        col = bias[:, (sp // 2) * LANES:(sp // 2 + 1) * LANES]
        if sp % 2 == 0:
            col = pltpu.roll(col, HD, 1)
        qa.append((qrs + jnp.where(upper, col, 0.0)).astype(BF16))
    cps = HD * NSA_BLOCK // kc
    n_chunks = (t0 + tq + kc - 1) // kc
    last = n_chunks - 1
    qa_last = qa[0]
    for sp in range(1, n_span):
        qa_last = jnp.where(last // cps == sp, qa[sp], qa_last)
    last_start = pl.multiple_of(last * kc, kc)
    kpos = last_start + lax.broadcasted_iota(jnp.int32, (rows, kc), 1)
    s_last = jnp.where(kpos <= tpos, _dot_nt(qa_last, ska_ref[0, pl.ds(last_start, kc), :]), NEG)

    def spans(fn):
        for sp in range(n_span):
            lo = sp * cps
            _tile_loop(jnp.minimum(lo, last), jnp.minimum(lo + cps, last), functools.partial(fn, qa[sp]))

    m_ref[...] = _lane_fold_max(s_last)

    def fold_max(qsp, c):
        start = pl.multiple_of(c * kc, kc)
        s = _dot_nt(qsp, ska_ref[0, pl.ds(start, kc), :])
        s_ref[c] = s
        m_ref[...] = jnp.maximum(m_ref[...], _lane_fold_max(s))

    spans(fold_max)
    m_s = jnp.max(m_ref[...], axis=-1, keepdims=True)
    acc_ref[...] = _dot(jnp.exp(s_last - m_s).astype(BF16), svo_ref[0, pl.ds(last_start, kc), :])

    def accumulate(qsp, c):
        start = pl.multiple_of(c * kc, kc)
        p = jnp.exp(s_ref[c] - m_s).astype(BF16)
        acc_ref[...] += _dot(p, svo_ref[0, pl.ds(start, kc), :])

    spans(accumulate)
    acc_s = acc_ref[...]
    o_s = acc_s / jnp.maximum(acc_s[:, 0:1], TINY)

    def win_chunk(c, carry):
        start = pl.multiple_of(c * tq, tq)
        kv = wkv_ref[0, pl.ds(start, tq), :]
        kpos = start + lax.broadcasted_iota(jnp.int32, (rows, tq), 1)
        mask = (kpos <= tpos) & (kpos > tpos - NSA_WINDOW)
        return _softmax_step(_dot_nt(qrb, kv), mask, kv, carry)

    first = jnp.maximum(i - (NSA_WINDOW + tq - 1) // tq, 0)
    _, l_w, acc_w = lax.fori_loop(first, i + 1, win_chunk, init)
    o_w = acc_w / jnp.maximum(l_w, TINY)

    gt = jax.nn.sigmoid(gt_ref[0, 0])
    o_ref[0, 0] = gt[:, 0:1] * o_c + gt[:, 1:2] * o_s + gt[:, 2:3] * o_w


def _nsa_prompt(q_st, qr_st, gt_st, ckv, ska, svo, wkv, tq, kc):
    g, nt, rows, _ = q_st.shape
    t = ska.shape[1]
    nb = ckv.shape[1]
    assert t % kc == 0 and kc % tq == 0 and (HD * NSA_BLOCK) % kc == 0 and t % tq == 0
    qspec = pl.BlockSpec((1, 1, rows, LANES), lambda a, i: (a, i, 0, 0))
    seq = pl.BlockSpec((1, t, LANES), lambda a, i: (a, 0, 0), pipeline_mode=pl.Buffered(1))
    body = functools.partial(_nsa_prompt_body, tq=tq, kc=kc)
    return pl.pallas_call(
        body,
        grid=(g, nt),
        in_specs=[qspec, qspec, pl.BlockSpec((1, 1, rows, 3), lambda a, i: (a, i, 0, 0)),
                  pl.BlockSpec((1, nb, LANES), lambda a, i: (a, 0, 0)), seq, seq, seq],
        out_specs=qspec,
        out_shape=jax.ShapeDtypeStruct((g, nt, rows, LANES), F32),
        scratch_shapes=[pltpu.VMEM((t // kc, rows, kc), F32), pltpu.VMEM((rows, LANES), F32),
                        pltpu.VMEM((rows, LANES), F32)],
        compiler_params=_params("parallel", "arbitrary"),
        name="nsa_prompt",
    )(q_st, qr_st, gt_st, ckv, ska, svo, wkv)


def _hgrn_body(q_ref, f_ref, i_ref, lb_ref, o_ref, s_ref):
    @pl.when(pl.program_id(0) == 0)
    def _():
        s_ref[...] = jnp.zeros_like(s_ref)

    q = q_ref[...]
    v = i_ref[...]
    c, w = q.shape
    lb = lb_ref[...]
    f = lb + (1.0 - lb) * jax.nn.sigmoid(f_ref[...])
    kk = 1.0 - f
    tr = lax.broadcasted_iota(jnp.int32, (c, c), 0)
    tc = lax.broadcasted_iota(jnp.int32, (c, c), 1)
    cum = _dot(jnp.where(tr >= tc, 1.0, 0.0), jnp.log(f), HIGHEST)
    bd = _block_ones(w)
    bd_b = bd.astype(BF16)
    s = s_ref[...]
    o = _dot(q * jnp.exp(cum), s, HIGHEST)
    t_idx = lax.broadcasted_iota(jnp.int32, (c, w), 0)

    sub = 8
    groups = c // sub
    o_blk = [o[i * sub:(i + 1) * sub] for i in range(groups)]
    for g in range(groups):
        r0 = g * sub
        n = c - r0
        t_g = r0 + lax.broadcasted_iota(jnp.int32, (n, w), 0)
        es = [jnp.where(t_g >= j, jnp.exp(cum[r0:] - cum[j:j + 1]) * q[r0:] * kk[j:j + 1], 0.0)
              for j in range(r0, r0 + sub)]
        e = jnp.concatenate(es, axis=0)
        e_hi = e.astype(BF16)
        e_lo = (e - e_hi.astype(F32)).astype(BF16)
        a = _dot(e_hi, bd_b) + _dot(e_lo, bd_b)
        contrib = a[0:n] * v[r0:r0 + 1]
        for u in range(1, sub):
            contrib = contrib + a[u * n:(u + 1) * n] * v[r0 + u:r0 + u + 1]
        for i in range(g, groups):
            o_blk[i] = o_blk[i] + contrib[(i - g) * sub:(i - g + 1) * sub]
    o = jnp.concatenate(o_blk, axis=0)
    last = cum[c - 1:c, :]
    row0 = t_idx == 0
    scale_mat = _dot_tn(jnp.where(row0, jnp.exp(last), 0.0), jnp.where(row0, 1.0, 0.0), HIGHEST)
    s_ref[...] = s * scale_mat + _dot_tn(kk * jnp.exp(last - cum), v, HIGHEST) * bd
    ms = _dot(o * o, bd * (1.0 / HD), HIGHEST)
    o_ref[...] = o * lax.rsqrt(ms + EPS)


def _hgrn_prompt(q, f_logit, i, lb):
    t, w = q.shape
    c = math.gcd(t, HGRN_CHUNK)
    row = pl.BlockSpec((c, w), lambda n: (n, 0))
    return pl.pallas_call(
        _hgrn_body,
        grid=(t // c,),
        in_specs=[row, row, row, pl.BlockSpec((1, w), lambda n: (0, 0))],
        out_specs=[row, pl.BlockSpec((w, w), lambda n: (0, 0))],
        out_shape=[jax.ShapeDtypeStruct((t, w), F32), jax.ShapeDtypeStruct((w, w), F32)],
        compiler_params=_params("arbitrary"),
        name="hgrn_prompt",
    )(q, f_logit, i, lb)


def _state_step_body(s_ref, q_ref, a_ref, b_ref, v_ref, o_ref, so_ref, *, mode):
    s = s_ref[...]
    if mode == "ret":
        k = a_ref[...] * SCALE
        dec = b_ref[...]
    else:
        lb = b_ref[...]
        dec = lb + (1.0 - lb) * jax.nn.sigmoid(a_ref[...])
        k = 1.0 - dec
    kv = k * v_ref[...]
    rows = s.shape[0]
    o = (q_ref[...] * (kv + dec * s)).reshape(rows // HD, HD, HD).sum(axis=1)
    so_ref[...] = dec * s + kv
    if mode == "ret":
        xc = o - jnp.mean(o, axis=-1, keepdims=True)
        o_ref[...] = xc * lax.rsqrt(jnp.mean(xc * xc, axis=-1, keepdims=True) + EPS)
    else:
        o_ref[...] = o * lax.rsqrt(jnp.mean(o * o, axis=-1, keepdims=True) + EPS)


def _state_step(state, q, a, b, v, mode):
    bsz, h = state.shape[0], state.shape[1]
    rows = bsz * h * HD
    col = lambda x: jnp.broadcast_to(x.reshape(bsz, h, HD, 1), (bsz, h, HD, HD)).reshape(rows, HD)
    vx = jnp.broadcast_to(v.reshape(bsz, h, 1, HD), (bsz, h, HD, HD)).reshape(rows, HD)
    bb = 8 if bsz % 8 == 0 else bsz
    br = bb * h * HD
    spec = pl.BlockSpec((br, HD), lambda i: (i, 0))
    ospec = pl.BlockSpec((br // HD, HD), lambda i: (i, 0))
    o, s_new = pl.pallas_call(
        functools.partial(_state_step_body, mode=mode),
        grid=(rows // br,),
        in_specs=[spec] * 5,
        out_specs=[ospec, spec],
        out_shape=[jax.ShapeDtypeStruct((rows // HD, HD), F32), jax.ShapeDtypeStruct((rows, HD), F32)],
        compiler_params=_params("parallel"),
        name="state_step_" + mode,
    )(state.reshape(rows, HD).astype(F32), col(q), col(a), col(b), vx)
    return o.reshape(bsz, h * HD), s_new.reshape(bsz, h, HD, HD)


def _expand_heads(p, width):
    g, n = p.shape
    return jnp.broadcast_to(p[:, None, :], (g, HD, n)).reshape(width, n)


def _head_sums(x, g):
    return x.reshape(g, HD, x.shape[-1]).sum(axis=1)


def _mem_sample_body(q_ref, kt_ref, vt_ref, o_ref):
    kt = kt_ref[0, 0]
    vt = vt_ref[0, 0]
    s = _head_sums(kt * q_ref[0], MEM_H) * SCALE
    e = jnp.exp(s - jnp.max(s, axis=-1, keepdims=True))
    p = e / jnp.sum(e, axis=-1, keepdims=True)
    o_ref[0] = jnp.sum(vt * _expand_heads(p, MEM_W), axis=-1, keepdims=True)


def _mem_sample(q, kvt):
    bsz, w = q.shape
    n = kvt.shape[-1]
    out = pl.pallas_call(
        _mem_sample_body,
        grid=(bsz,),
        in_specs=[pl.BlockSpec((1, w, 1), lambda i: (i, 0, 0)),
                  pl.BlockSpec((1, 1, w, n), lambda i: (i, 0, 0, 0)),
                  pl.BlockSpec((1, 1, w, n), lambda i: (i, 1, 0, 0))],
        out_specs=pl.BlockSpec((1, w, 1), lambda i: (i, 0, 0)),
        out_shape=jax.ShapeDtypeStruct((bsz, w, 1), F32),
        compiler_params=_params("parallel"),
        name="mem_sample",
    )(q.reshape(bsz, w, 1), kvt, kvt)
    return out.reshape(bsz, w)


def _topk_mask_axis0(score, k):
    n = score.shape[0]
    idx = lax.broadcasted_iota(jnp.int32, score.shape, 0)
    sel = jnp.zeros(score.shape, F32)
    work = score
    for _ in range(k):
        m = jnp.max(work, axis=0, keepdims=True)
        first = jnp.min(jnp.where(work == m, idx, n), axis=0, keepdims=True)
        pick = idx == first
        sel = jnp.where(pick, 1.0, sel)
        work = jnp.where(pick, -jnp.inf, work)
    return sel


def _paged_attn_body(pt_ref, pool_ref, qc_ref, qbd_ref, kn_ref, vn_ref, mask_ref, o_ref, buf, sem, s_ref, acc_ref, st_ref,
                     *, layer, groups, reps, chunk, moba):
    b = pl.program_id(0)
    n_pages = pt_ref.shape[1]
    n = n_pages // chunk
    gw = groups * HD

    def copy(kv, c, i, slot):
        return pltpu.make_async_copy(pool_ref.at[pt_ref[b, c * chunk + i], layer, kv], buf.at[slot, i], sem.at[slot])

    def start(kv, c, slot):
        lax.fori_loop(0, chunk, lambda i, _: (copy(kv, c, i, slot).start(), 0)[1], 0)

    def wait(kv, c, slot):
        lax.fori_loop(0, chunk, lambda i, _: (copy(kv, c, i, slot).wait(), 0)[1], 0)

    acc_ref[...] = jnp.zeros_like(acc_ref)
    start(0, 0, 0)

    def k_compute(c, slot):
        for i in range(chunk):
            sc = _dot(qbd_ref[0], buf[slot, i].astype(BF16))
            for r in range(reps):
                s_ref[r, c * chunk + i] = sc[r * groups:(r + 1) * groups]

    def select():
        if moba:
            per = MOBA_BLOCK // PAGE
            nblk = n_pages // per
            sc = s_ref[0].reshape(nblk, per, groups, PAGE)
            gate = jnp.sum(jnp.sum(sc, axis=1, keepdims=True), axis=-1, keepdims=True) * (1.0 / MOBA_BLOCK)
            sel = _topk_mask_axis0(gate, min(MOBA_TOPK, nblk + 1))
            mask = jnp.broadcast_to(sel, (nblk, per, groups, PAGE)).reshape(n_pages, groups, PAGE) > 0.5
        else:
            mask = mask_ref[0] > 0.5
        for r in range(reps):
            qcol = qc_ref[0, :, r:r + 1]
            s_own = _head_sums(jnp.broadcast_to(kn_ref[0] * qcol, (gw, PAGE)), groups) * SCALE
            s = jnp.where(mask, s_ref[r] * SCALE, NEG)
            m = jnp.max(jnp.max(s, axis=0), axis=-1, keepdims=True)
            m = jnp.maximum(m, s_own)
            p = jnp.where(mask, jnp.exp(s - m), 0.0)
            p_own = jnp.exp(s_own - m)
            den = jnp.sum(jnp.sum(p, axis=0), axis=-1, keepdims=True) + p_own
            s_ref[r] = p
            st_ref[r, 0] = p_own
            st_ref[r, 1] = jnp.maximum(den, TINY)

    def v_compute(c, slot):
        for r in range(reps):
            acc = acc_ref[r]
            for i in range(chunk):
                acc = acc + buf[slot, i] * _expand_heads(s_ref[r, c * chunk + i], gw)
            acc_ref[r] = acc

    def step(idx, _):
        slot = idx % 2
        nxt = idx + 1

        @pl.when(nxt < 2 * n)
        def _():
            start(nxt // n, nxt % n, 1 - slot)

        wait(idx // n, idx % n, slot)

        @pl.when(idx < n)
        def _():
            k_compute(idx, slot)

        @pl.when(idx == n - 1)
        def _():
            select()

        @pl.when(idx >= n)
        def _():
            v_compute(idx - n, slot)

        return 0

    lax.fori_loop(0, 2 * n, step, 0)
    for r in range(reps):
        tot = jnp.sum(acc_ref[r], axis=-1, keepdims=True)
        p_own = _expand_heads(st_ref[r, 0], gw)[:, 0:1]
        den = _expand_heads(st_ref[r, 1], gw)[:, 0:1]
        o_ref[0, :, r:r + 1] = (tot + p_own * vn_ref[0]) / den


def _paged_attn(page_table, pool_t, layer, q_cols, k_new, v_new, row_mask, groups, reps, moba):
    bsz, n_pages = page_table.shape
    gw = groups * HD
    chunk = math.gcd(n_pages, 16)
    if row_mask is None:
        row_mask = jnp.zeros((bsz, 1, groups, PAGE), F32)
    mshape = row_mask.shape[1:]
    nq = -(-reps * groups // 8) * 8
    qh = q_cols.reshape(bsz, groups, HD, reps).transpose(0, 3, 1, 2)
    qbd = jnp.einsum("brgd,gh->brghd", qh, jnp.eye(groups, dtype=F32)).reshape(bsz, reps * groups, gw)
    qbd = jnp.pad(qbd, ((0, 0), (0, nq - reps * groups), (0, 0))).astype(BF16)
    body = functools.partial(_paged_attn_body, layer=layer, groups=groups, reps=reps, chunk=chunk, moba=moba)
    grid_spec = pltpu.PrefetchScalarGridSpec(
        num_scalar_prefetch=1,
        grid=(bsz,),
        in_specs=[
            pl.BlockSpec(memory_space=pl.ANY),
            pl.BlockSpec((1, gw, reps), lambda i, pt: (i, 0, 0)),
            pl.BlockSpec((1, nq, gw), lambda i, pt: (i, 0, 0)),
            pl.BlockSpec((1, gw, 1), lambda i, pt: (i, 0, 0)),
            pl.BlockSpec((1, gw, 1), lambda i, pt: (i, 0, 0)),
            pl.BlockSpec((1,) + mshape, lambda i, pt: (i, 0, 0, 0)),
        ],
        out_specs=pl.BlockSpec((1, gw, reps), lambda i, pt: (i, 0, 0)),
        scratch_shapes=[
            pltpu.VMEM((2, chunk, gw, PAGE), F32),
            pltpu.SemaphoreType.DMA((2,)),
            pltpu.VMEM((reps, n_pages, groups, PAGE), F32),
            pltpu.VMEM((reps, gw, PAGE), F32),
            pltpu.VMEM((reps, 2, groups, PAGE), F32),
        ],
    )
    return pl.pallas_call(
        body,
        grid_spec=grid_spec,
        out_shape=jax.ShapeDtypeStruct((bsz, gw, reps), F32),
        compiler_params=_params("arbitrary"),
        name="paged_attn_moba" if moba else "paged_attn_nsa",
    )(page_table, pool_t, q_cols, qbd, k_new.reshape(bsz, gw, 1), v_new.reshape(bsz, gw, 1), row_mask)


CMP_DB = 8


def _compress_weights_t(w1, pe):
    per = PAGE // NSA_BLOCK
    eye = jnp.eye(per, dtype=F32)
    w1r = w1.astype(F32).reshape(2, NSA_BLOCK, HD, HD)
    wd = jnp.einsum("klde,gh->kdglhe", w1r, eye).reshape(2, HD // CMP_DB, CMP_DB * PAGE, per * HD)
    ped = jnp.tile(pe.astype(F32).transpose(0, 2, 1), (1, 1, per)).reshape(2, HD // CMP_DB, 1, CMP_DB * PAGE)
    return wd.astype(BF16), ped


def _nsa_cmp_sample_body(pt_ref, pool_ref, q_ref, wd_ref, ped_ref, w2_ref, oc_ref, sel_ref, buf, sem, tok_ref, acc_ref,
                         *, layer, nseq):
    b = pl.program_id(0)
    j = pl.program_id(1)
    bsz, n_pages = pt_ref.shape
    t = b * 2 + j
    slot = t % 2
    m_rows = nseq * NSA_KVH * n_pages

    def copy(grp, kv, sq, p, g, sl):
        row = pl.multiple_of((((sl * nseq + sq) * NSA_KVH + g) * n_pages + p) * HD, HD)
        return pltpu.make_async_copy(pool_ref.at[pt_ref[grp * nseq + sq, p], layer, kv, g], buf.at[pl.ds(row, HD)],
                                     sem.at[sl])

    def each(fn):
        def run(grp, kv, sl):
            def page(p, _):
                for sq in range(nseq):
                    for g in range(NSA_KVH):
                        fn(copy(grp, kv, sq, p, g, sl))
                return 0
            lax.fori_loop(0, n_pages, page, 0)
        return run

    start = each(lambda c: c.start())
    wait = each(lambda c: c.wait())

    @pl.when(t == 0)
    def _():
        start(0, 0, 0)

    @pl.when(t + 1 < 2 * (bsz // nseq))
    def _():
        start((t + 1) // 2, (t + 1) % 2, 1 - slot)

    wait(b, j, slot)

    base = slot * m_rows * HD
    for dd in range(HD // CMP_DB):
        parts = [buf[pl.ds(base + dd * CMP_DB + u, m_rows, stride=HD), :] for u in range(CMP_DB)]
        a = jnp.concatenate(parts, axis=1) + ped_ref[0, dd]
        part = _dot(a.astype(BF16), wd_ref[0, dd])
        if dd == 0:
            acc_ref[...] = part
        else:
            acc_ref[...] += part
    tok_ref[j] = _dot(jax.nn.gelu(acc_ref[...]).astype(BF16), w2_ref[0])

    @pl.when(j == 1)
    def _():
        pos = n_pages * PAGE
        per = PAGE // NSA_BLOCK
        n_idx = (lax.broadcasted_iota(jnp.int32, (per, n_pages), 1) * per
                 + lax.broadcasted_iota(jnp.int32, (per, n_pages), 0))
        complete = n_idx * NSA_BLOCK + (NSA_BLOCK - 1) <= pos
        cur = pos // NSA_BLOCK
        forced = (n_idx == 0) | (n_idx == cur) | (n_idx == cur - 1)
        k_past = min(NSA_TOPN, per * n_pages + 1) - 1
        for sq, g in ((a, c) for a in range(nseq) for c in range(NSA_KVH)):
            qg = q_ref[sq, g * NSA_REP:(g + 1) * NSA_REP, :].astype(BF16)
            r0 = (sq * NSA_KVH + g) * n_pages
            ck = tok_ref[0, r0:r0 + n_pages, :].astype(BF16)
            cv = tok_ref[1, r0:r0 + n_pages, :].astype(BF16)
            s = [jnp.where(complete[h:h + 1], _dot_nt(qg, ck[:, h * HD:(h + 1) * HD]) * SCALE, NEG) for h in range(per)]
            m = functools.reduce(jnp.maximum, [jnp.max(x, axis=-1, keepdims=True) for x in s])
            e = [jnp.where(complete[h:h + 1], jnp.exp(s[h] - m), 0.0) for h in range(per)]
            den = jnp.maximum(sum(jnp.sum(x, axis=-1, keepdims=True) for x in e), TINY)
            pc = [x / den for x in e]
            oc_ref[sq, g * NSA_REP:(g + 1) * NSA_REP, :] = sum(
                _dot(pc[h].astype(BF16), cv[:, h * HD:(h + 1) * HD]) for h in range(per))
            imp = jnp.concatenate([jnp.sum(x, axis=0, keepdims=True) for x in pc], axis=0)
            score = jnp.where(forced, FORCE, jnp.where(complete, imp, NEG))
            sel = jnp.zeros(score.shape, F32)
            work = score
            for _ in range(k_past):
                mx = jnp.max(jnp.max(work, axis=-1, keepdims=True), axis=0, keepdims=True)
                cand = jnp.where(work == mx, n_idx, per * n_pages)
                first = jnp.min(jnp.min(cand, axis=-1, keepdims=True), axis=0, keepdims=True)
                pick = n_idx == first
                sel = jnp.where(pick, 1.0, sel)
                work = jnp.where(pick, -jnp.inf, work)
            sel_ref[sq, g] = jnp.where(score > 0.5 * NEG, sel, 0.0)


def _nsa_cmp_sample(page_table, pool_t, layer, q, wd, ped, w2b):
    bsz, n_pages = page_table.shape
    per = PAGE // NSA_BLOCK
    nseq = 2 if bsz % 2 == 0 else 1
    m_rows = nseq * NSA_KVH * n_pages
    grid_spec = pltpu.PrefetchScalarGridSpec(
        num_scalar_prefetch=1,
        grid=(bsz // nseq, 2),
        in_specs=[
            pl.BlockSpec(memory_space=pl.ANY),
            pl.BlockSpec((nseq, NSA_H, HD), lambda i, j, pt: (i, 0, 0)),
            pl.BlockSpec((1, HD // CMP_DB, CMP_DB * PAGE, per * HD), lambda i, j, pt: (j, 0, 0, 0)),
            pl.BlockSpec((1, HD // CMP_DB, 1, CMP_DB * PAGE), lambda i, j, pt: (j, 0, 0, 0)),
            pl.BlockSpec((1, per * HD, per * HD), lambda i, j, pt: (j, 0, 0)),
        ],
        out_specs=[pl.BlockSpec((nseq, NSA_H, HD), lambda i, j, pt: (i, 0, 0)),
                   pl.BlockSpec((nseq, NSA_KVH, per, n_pages), lambda i, j, pt: (i, 0, 0, 0))],
        scratch_shapes=[
            pltpu.VMEM((2 * m_rows * HD, PAGE), F32),
            pltpu.SemaphoreType.DMA((2,)),
            pltpu.VMEM((2, m_rows, per * HD), F32),
            pltpu.VMEM((m_rows, per * HD), F32),
        ],
    )
    return pl.pallas_call(
        functools.partial(_nsa_cmp_sample_body, layer=layer, nseq=nseq),
        grid_spec=grid_spec,
        out_shape=[jax.ShapeDtypeStruct((bsz, NSA_H, HD), F32),
                   jax.ShapeDtypeStruct((bsz, NSA_KVH, per, n_pages), F32)],
        compiler_params=_params("arbitrary", "arbitrary"),
        name="nsa_cmp_sample",
    )(page_table, pool_t, q, wd, ped, w2b)


def _win_sample_body(qc_ref, kt_ref, vt_ref, kn_ref, vn_ref, o_ref):
    kt = kt_ref[0, 0]
    vt = vt_ref[0, 0]
    gw, lbuf = kt.shape
    j = lax.broadcasted_iota(jnp.int32, (NSA_KVH, lbuf), 1)
    mask = j > lbuf - NSA_WINDOW
    for r in range(NSA_REP):
        qcol = qc_ref[0, :, r:r + 1]
        s = jnp.where(mask, _head_sums(kt * qcol, NSA_KVH) * SCALE, NEG)
        s_own = _head_sums(jnp.broadcast_to(kn_ref[0] * qcol, (gw, lbuf)), NSA_KVH) * SCALE
        m = jnp.maximum(jnp.max(s, axis=-1, keepdims=True), s_own)
        p = jnp.where(mask, jnp.exp(s - m), 0.0)
        p_own = jnp.exp(s_own - m)
        den = jnp.maximum(jnp.sum(p, axis=-1, keepdims=True) + p_own, TINY)
        tot = jnp.sum(vt * _expand_heads(p, gw), axis=-1, keepdims=True)
        o_ref[0, :, r:r + 1] = (tot + _expand_heads(p_own, gw)[:, 0:1] * vn_ref[0]) / _expand_heads(den, gw)[:, 0:1]


def _win_sample(q_cols, wbuf_t, k_new, v_new):
    bsz, gw, reps = q_cols.shape
    lbuf = wbuf_t.shape[-1]
    col = pl.BlockSpec((1, gw, 1), lambda i: (i, 0, 0))
    return pl.pallas_call(
        _win_sample_body,
        grid=(bsz,),
        in_specs=[pl.BlockSpec((1, gw, reps), lambda i: (i, 0, 0)),
                  pl.BlockSpec((1, 1, gw, lbuf), lambda i: (i, 0, 0, 0)),
                  pl.BlockSpec((1, 1, gw, lbuf), lambda i: (i, 1, 0, 0)), col, col],
        out_specs=pl.BlockSpec((1, gw, reps), lambda i: (i, 0, 0)),
        out_shape=jax.ShapeDtypeStruct((bsz, gw, reps), F32),
        compiler_params=_params("parallel"),
        name="nsa_win_sample",
    )(q_cols, wbuf_t, wbuf_t, k_new.reshape(bsz, gw, 1), v_new.reshape(bsz, gw, 1))


def _gate_combine_body(g_ref, c_ref, s_ref, w_ref, o_ref):
    gt = jax.nn.sigmoid(g_ref[...])
    o_ref[...] = gt[0] * c_ref[...] + gt[1] * s_ref[...] + gt[2] * w_ref[...]


def _gate_combine(gate_logits, o_c, o_s, o_w):
    return pl.pallas_call(
        _gate_combine_body,
        out_shape=jax.ShapeDtypeStruct(o_c.shape, F32),
        name="nsa_gate_combine",
    )(gate_logits, o_c, o_s, o_w)


ODD_NQ, ODD_NQR, ODD_KV = 0, NSA_QW, 2 * NSA_QW
ODD_HQ = ODD_KV + 6 * NSA_KVW
ODD_EQ = ODD_HQ + 3 * HGRN_W
ODD_GATE = ODD_EQ + MEM_W
ODD_NG = ODD_GATE + MIX_W
ODD_N = 4096


def _odd_weights(w_in):
    offs = np.cumsum([0, NSA_QW] + [NSA_KVW] * 6 + [3 * NSA_H] + [HGRN_W] * 3 + [MEM_W, MIX_W])
    nq = w_in[:, offs[0]:offs[1]]
    kv = w_in[:, offs[1]:offs[7]]
    ng = w_in[:, offs[7]:offs[8]]
    rest = w_in[:, offs[8]:]
    pad = jnp.zeros((w_in.shape[0], ODD_N - ODD_NG - 3 * NSA_H), w_in.dtype)
    return jnp.concatenate([nq, nq, kv, rest, ng, pad], axis=1)


def _stack_heads(a, tq, width):
    t = a.shape[0]
    a = a.reshape(t // tq, tq, NSA_KVH, NSA_REP, width).transpose(2, 0, 3, 1, 4)
    return a.reshape(NSA_KVH, t // tq, NSA_REP * tq, width)


def _unstack_heads(a, tq):
    g, nt, _, width = a.shape
    a = a.reshape(g, nt, NSA_REP, tq, width).transpose(1, 3, 0, 2, 4)
    return a.reshape(nt * tq, g * NSA_REP * width)


def _group_kv(k, v):
    t = k.shape[0]
    return jnp.concatenate([k.reshape(t, NSA_KVH, HD), v.reshape(t, NSA_KVH, HD)], axis=-1).transpose(1, 0, 2)


def _pad_lanes(a):
    return jnp.concatenate([a, jnp.zeros_like(a)], axis=-1)


def _odd_prompt(x, cos, sin, g, w_aug_bf16, w_o_bf16, mem_k, mem_v, lb, cmp_w, final_g, tm, tq, kc):
    t = x.shape[0]
    assert t % NSA_BLOCK == 0
    kvo = lambda j: ODD_KV + j * NSA_KVW
    proj = _norm_proj(x, g, w_aug_bf16, cos, sin, ((ODD_NQR, ODD_KV), (kvo(2), kvo(3)), (kvo(4), kvo(5))), tm)
    ck, cv, sk, sv, wk, wv = (proj[:, kvo(j):kvo(j + 1)] for j in range(6))
    w1b, w2b, peb = cmp_w
    nb = t // NSA_BLOCK
    cmp_tok = _compress(jnp.stack([ck, cv]).reshape(2, nb, NSA_BLOCK * NSA_KVW), w1b, w2b, peb)
    ckv = _group_kv(cmp_tok[0], cmp_tok[1])
    q_st = _pad_lanes(_stack_heads(proj[:, ODD_NQ:ODD_NQ + NSA_QW], tq, HD))
    qr_st = _pad_lanes(_stack_heads(proj[:, ODD_NQR:ODD_NQR + NSA_QW], tq, HD))
    gt_st = _stack_heads(proj[:, ODD_NG:ODD_NG + 3 * NSA_H], tq, 3)
    onehot = ((jnp.arange(t)[:, None] // NSA_BLOCK) % HD == jnp.arange(HD)[None, :]).astype(F32)
    ska = jnp.concatenate([sk.reshape(t, NSA_KVH, HD), jnp.broadcast_to(onehot[:, None], (t, NSA_KVH, HD))], axis=-1)
    svo = jnp.concatenate([jnp.ones((t, NSA_KVH, HD), F32), sv.reshape(t, NSA_KVH, HD)], axis=-1)
    o_nsa = _nsa_prompt(q_st, qr_st, gt_st, ckv, ska.transpose(1, 0, 2).astype(BF16),
                        svo.transpose(1, 0, 2).astype(BF16), _group_kv(wk, wv).astype(BF16), tq, kc)
    o_nsa = _unstack_heads(o_nsa[..., HD:], tq)
    hq, hf, hi = (proj[:, ODD_HQ + j * HGRN_W:ODD_HQ + (j + 1) * HGRN_W] for j in range(3))
    o_hg, s_hg = _hgrn_prompt(hq, hf, hi, lb.reshape(1, HGRN_W))
    o_mem = _mem_prompt(proj[:, ODD_EQ:ODD_GATE], mem_k, mem_v, tm)
    y = _mix_out(x, o_nsa, o_hg, o_mem, proj[:, ODD_GATE:ODD_NG], w_o_bf16, final_g, tm)
    rows = lambda a, b: jnp.stack([a.reshape(t, NSA_KVH, HD), b.reshape(t, NSA_KVH, HD)], axis=1)
    return y, rows(ck, cv), rows(sk, sv), rows(wk, wv), _diag_blocks(s_hg, HGRN_H)

def _heads_major(a, h):
    t = a.shape[0]
    return a.reshape(t, h, HD).transpose(1, 0, 2)


def _even_prompt(x, cos, sin, g, w_in_bf16, w_o_bf16, mem_k, mem_v, tm):
    t = x.shape[0]
    o_mq = 3 * RET_W
    o_eq = 3 * RET_W + 3 * MOBA_W
    o_gate = o_eq + MEM_W
    proj = _norm_proj(x, g, w_in_bf16, cos, sin, ((0, 2 * RET_W), (o_mq, o_mq + 2 * MOBA_W)), tm)
    rq, rk, rv = proj[:, :RET_W], proj[:, RET_W:2 * RET_W], proj[:, 2 * RET_W:3 * RET_W]
    mq = proj[:, o_mq:o_mq + MOBA_W]
    mk = proj[:, o_mq + MOBA_W:o_mq + 2 * MOBA_W]
    mv = proj[:, o_mq + 2 * MOBA_W:o_eq]
    eq = proj[:, o_eq:o_gate]
    gate = proj[:, o_gate:]
    o_ret, s_ret = _retention_prompt(rq, rk, rv)
    kmean = _block_mean(mk, MOBA_BLOCK)
    zq = jnp.zeros((MOBA_H, t, HD), F32)
    q_pad = jnp.concatenate([_heads_major(mq, MOBA_H), zq], axis=-1)
    vh = _heads_major(mv, MOBA_H)
    nblk = t // MOBA_BLOCK
    onehot = (jnp.arange(t)[:, None] // MOBA_BLOCK == jnp.arange(HD)[None, :]).astype(F32)
    ka = jnp.concatenate([_heads_major(mk, MOBA_H), jnp.broadcast_to(onehot, (MOBA_H, t, HD))], axis=-1).astype(BF16)
    vo = jnp.concatenate([jnp.ones_like(vh), vh], axis=-1).astype(BF16)
    km = jnp.pad(_heads_major(kmean, MOBA_H), ((0, 0), (HD, LANES - HD - nblk), (0, HD)))
    o_moba = _moba_prompt(q_pad, ka, vo, km)
    o_moba = o_moba[:, :, HD:].transpose(1, 0, 2).reshape(t, MOBA_W)
    o_mem = _mem_prompt(eq, mem_k, mem_v, tm)
    y = _mix_out(x, o_ret, o_moba, o_mem, gate, w_o_bf16, None, tm)
    rows = proj[:, o_mq + MOBA_W:o_eq].reshape(t, 2, MOBA_H, HD)
    return y, rows, _diag_blocks(s_ret, RET_H)


def _pages_t(pool, width):
    n_pool, n_layer = pool.shape[0], pool.shape[1]
    return pool.transpose(0, 1, 3, 4, 5, 2).reshape(n_pool, n_layer, 2, width, PAGE)


def _mem_t(cache):
    bsz, n = cache.shape[0], cache.shape[1]
    return cache.transpose(0, 2, 3, 4, 1).reshape(bsz, 2, MEM_W, n)


def _even_sample(x, cos, sin, g, w_in_bf16, w_o_bf16, mem_cache, state, page_table, pool, layer):
    bsz = x.shape[0]
    o_mq = 3 * RET_W
    o_eq = 3 * RET_W + 3 * MOBA_W
    o_gate = o_eq + MEM_W
    proj = _norm_proj(x, g, w_in_bf16, cos, sin, ((0, 2 * RET_W), (o_mq, o_mq + 2 * MOBA_W)), bsz)
    rq, rk, rv = proj[:, :RET_W], proj[:, RET_W:2 * RET_W], proj[:, 2 * RET_W:3 * RET_W]
    mq = proj[:, o_mq:o_mq + MOBA_W]
    mk = proj[:, o_mq + MOBA_W:o_mq + 2 * MOBA_W]
    mv = proj[:, o_mq + 2 * MOBA_W:o_eq]
    gamma = np.repeat(1.0 - np.power(2.0, -5.0 - np.arange(RET_H, dtype=np.float64)), HD)
    o_ret, s_ret = _state_step(state, rq, rk, jnp.broadcast_to(jnp.asarray(gamma, F32), (bsz, RET_W)), rv, "ret")
    o_moba = _paged_attn(page_table, _pages_t(pool, MOBA_W), layer, mq.reshape(bsz, MOBA_W, 1), mk, mv, None,
                         MOBA_H, 1, True).reshape(bsz, MOBA_W)
    o_mem = _mem_sample(proj[:, o_eq:o_gate], _mem_t(mem_cache))
    y = _mix_out(x, o_ret, o_moba, o_mem, proj[:, o_gate:], w_o_bf16, None, bsz)
    rows = proj[:, o_mq + MOBA_W:o_eq].reshape(bsz, 1, 2, MOBA_H, HD)
    return y, rows, s_ret


def _cols(a):
    bsz = a.shape[0]
    return a.reshape(bsz, NSA_KVH, NSA_REP, HD).transpose(0, 1, 3, 2).reshape(bsz, NSA_KVW, NSA_REP)


def _uncols(a):
    bsz = a.shape[0]
    return a.reshape(bsz, NSA_KVH, HD, NSA_REP).transpose(0, 1, 3, 2).reshape(bsz, NSA_QW)


def _odd_sample(x, cos, sin, g, w_aug_bf16, w_o_bf16, mem_cache, state, lb, page_table, cmp_pool, slc_pool, wbuf,
                layer, cmp_w_t, final_g):
    bsz = x.shape[0]
    lbuf = wbuf.shape[1]
    kvo = lambda j: ODD_KV + j * NSA_KVW
    proj = _norm_proj(x, g, w_aug_bf16, cos, sin, ((ODD_NQR, ODD_KV), (kvo(2), kvo(3)), (kvo(4), kvo(5))), bsz)
    ck, cv, sk, sv, wk, wv = (proj[:, kvo(j):kvo(j + 1)] for j in range(6))
    wd, ped, w2t = cmp_w_t
    n_pool, n_layer = cmp_pool.shape[0], cmp_pool.shape[1]
    cmp_t = cmp_pool.transpose(0, 1, 3, 4, 5, 2)
    o_c, sel = _nsa_cmp_sample(page_table, cmp_t, layer, proj[:, ODD_NQ:ODD_NQ + NSA_QW].reshape(bsz, NSA_H, HD),
                               wd, ped, w2t)
    row_mask = jnp.repeat(sel.transpose(0, 3, 1, 2), NSA_BLOCK, axis=-1)
    q_cols = _cols(proj[:, ODD_NQR:ODD_NQR + NSA_QW])
    o_s = _paged_attn(page_table, _pages_t(slc_pool, NSA_KVW), layer, q_cols, sk, sv, row_mask, NSA_KVH, NSA_REP, False)
    wbuf_t = wbuf.transpose(0, 2, 3, 4, 1).reshape(bsz, 2, NSA_KVW, lbuf)
    o_w = _win_sample(q_cols, wbuf_t, wk, wv)
    ng = proj[:, ODD_NG:ODD_NG + 3 * NSA_H].reshape(bsz, NSA_H, 3)
    gate_logits = jnp.repeat(ng.transpose(2, 0, 1), HD, axis=-1)
    o_nsa = _gate_combine(gate_logits, o_c.reshape(bsz, NSA_QW), _uncols(o_s), _uncols(o_w))
    hq, hf, hi = (proj[:, ODD_HQ + j * HGRN_W:ODD_HQ + (j + 1) * HGRN_W] for j in range(3))
    o_hg, s_hg = _state_step(state, hq, hf, jnp.broadcast_to(lb.reshape(1, HGRN_W), (bsz, HGRN_W)), hi, "hgrn")
    o_mem = _mem_sample(proj[:, ODD_EQ:ODD_GATE], _mem_t(mem_cache))
    y = _mix_out(x, o_nsa, o_hg, o_mem, proj[:, ODD_GATE:ODD_NG], w_o_bf16, final_g, bsz)
    rows = lambda a, b: jnp.stack([a.reshape(bsz, 1, NSA_KVH, HD), b.reshape(bsz, 1, NSA_KVH, HD)], axis=2)
    win = jnp.concatenate([wbuf.astype(F32), rows(wk, wv)], axis=1)[:, -lbuf:]
    return y, rows(ck, cv), rows(sk, sv), win, s_hg


def kernel(x_prompt, x_sample, mem_prompt, cache_moba_kv, state_ret, cache_nsa_cmp_kv, cache_nsa_slc_kv,
           cache_nsa_win_kv, state_hgrn, cache_mem_kv, page_table, norm_g, mem_norm_g, w_mem_kv, w_in_even,
           w_in_odd, w_out, cmp_w1, cmp_w2, cmp_pe, hgrn_lb_logits, final_g):
    bp, tp, d = x_prompt.shape
    bs, ts, _ = x_sample.shape
    depth = w_out.shape[0]
    assert bp == 1 and ts == 1 and depth == 2
    n_mem = mem_prompt.shape[1]
    past_len = page_table.shape[1] * PAGE
    assert past_len % MOBA_BLOCK == 0 and cache_moba_kv.shape[2] == PAGE
    tm, tq, kc = 256, 128, 256
    xp, xs, mem = x_prompt[0], x_sample[:, 0], mem_prompt[0]
    cos_p, sin_p = _rope_tables(jnp.arange(tp, dtype=jnp.int32))
    cos_s, sin_s = (jnp.broadcast_to(a, (bs, LANES)) for a in _rope_tables(jnp.full((1,), past_len, jnp.int32)))
    lb_prob = jax.nn.softmax(hgrn_lb_logits.astype(F32), axis=0)
    lb_all = jnp.cumsum(lb_prob, axis=0) - lb_prob[0]
    lw_p = min(NSA_WINDOW, tp)

    def mem_kv(layer):
        kv = _norm_proj(mem, mem_norm_g[layer], w_mem_kv[layer].astype(BF16), cos_p[:n_mem], sin_p[:n_mem], (), n_mem)
        return kv.reshape(n_mem, 2, MEM_W)

    w_in0 = w_in_even[0].astype(BF16)
    w_o0 = w_out[0].astype(BF16)
    mkv0 = mem_kv(0)
    hp, moba_p, ret_p = _even_prompt(xp, cos_p, sin_p, norm_g[0], w_in0, w_o0, mkv0[:, 0], mkv0[:, 1], tm)
    hs, moba_s, ret_s = _even_sample(xs, cos_s, sin_s, norm_g[0], w_in0, w_o0, cache_mem_kv[0], state_ret[0],
                                     page_table, cache_moba_kv, 0)

    w_in1 = _odd_weights(w_in_odd[0]).astype(BF16)
    w_o1 = w_out[1].astype(BF16)
    mkv1 = mem_kv(1)
    cmp_w = _compress_weights(cmp_w1[0], cmp_w2[0], cmp_pe[0])
    wd, ped = _compress_weights_t(cmp_w1[0], cmp_pe[0])
    yp, cmp_p, slc_p, win_p, hg_p = _odd_prompt(hp, cos_p, sin_p, norm_g[1], w_in1, w_o1, mkv1[:, 0], mkv1[:, 1],
                                                lb_all[1], cmp_w, final_g, tm, tq, kc)
    ys, cmp_s, slc_s, win_s, hg_s = _odd_sample(hs, cos_s, sin_s, norm_g[1], w_in1, w_o1, cache_mem_kv[1],
                                                state_hgrn[0], lb_all[1], page_table, cache_nsa_cmp_kv,
                                                cache_nsa_slc_kv, cache_nsa_win_kv[0], 0, (wd, ped, cmp_w[1]), final_g)

    return (yp[None], ys[:, None], moba_p[None, None], moba_s[:, None], ret_p[None, None], ret_s[None],
            cmp_p[None, None], cmp_s[:, None], slc_p[None, None], slc_s[:, None], win_p[None, tp - lw_p:][None],
            win_s[None], hg_p[None, None], hg_s[None], jnp.stack([mkv0, mkv1]).reshape(depth, 1, n_mem, 2, MEM_H, HD))
```

```python
import functools
import math

import jax
import jax.numpy as jnp
import numpy as np
from jax import lax
from jax.experimental import pallas as pl
from jax.experimental.pallas import tpu as pltpu

F32 = jnp.float32
BF16 = jnp.bfloat16
HIGHEST = lax.Precision.HIGHEST

HD = 64
RET_H, MOBA_H, MEM_H, NSA_H, NSA_KVH, HGRN_H = 6, 6, 4, 8, 2, 4
NSA_REP = NSA_H // NSA_KVH
RET_W, MOBA_W, MEM_W = RET_H * HD, MOBA_H * HD, MEM_H * HD
NSA_QW, NSA_KVW, HGRN_W = NSA_H * HD, NSA_KVH * HD, HGRN_H * HD
MIX_W = RET_W + MOBA_W + MEM_W
PAGE = 128
RET_CHUNK, HGRN_CHUNK = 128, 64
MOBA_BLOCK, MOBA_TOPK = 256, 3
NSA_BLOCK, NSA_TOPN, NSA_WINDOW = 64, 16, 512
ROPE_THETA = 10000.0
EPS = 1e-6
NEG = -1e30
FORCE = 1e30
TINY = 1e-30
SCALE = HD ** -0.5
LANES = 128
VMEM_LIMIT = 56 * 1024 * 1024


def _params(*sem):
    return pltpu.CompilerParams(dimension_semantics=sem, vmem_limit_bytes=VMEM_LIMIT)


def _dot(a, b, precision=None):
    return jnp.dot(a, b, preferred_element_type=F32, precision=precision)


def _dot_nt(a, b, precision=None):
    return lax.dot_general(a, b, (((1,), (1,)), ((), ())), preferred_element_type=F32, precision=precision)


def _dot_tn(a, b, precision=None):
    return lax.dot_general(a, b, (((0,), (0,)), ((), ())), preferred_element_type=F32, precision=precision)


def _block_ones(width, value=1.0):
    r = lax.broadcasted_iota(jnp.int32, (width, width), 0) // HD
    c = lax.broadcasted_iota(jnp.int32, (width, width), 1) // HD
    return jnp.where(r == c, value, 0.0).astype(F32)


def _rope_tile(a, cos, sin, first_half):
    rot = jnp.where(first_half, pltpu.roll(a, LANES - HD // 2, 1), pltpu.roll(a, HD // 2, 1))
    return a * cos + rot * sin


def _norm_proj_body(x_ref, g_ref, w_ref, cos_ref, sin_ref, *o_refs, rope_tiles, n_chunk, starts):
    x = x_ref[...]
    ms = jnp.mean(x * x, axis=-1, keepdims=True)
    y = (x * lax.rsqrt(ms + EPS) * g_ref[...]).astype(BF16)
    n = w_ref.shape[1]
    lane = lax.broadcasted_iota(jnp.int32, (x.shape[0], LANES), 1)
    first_half = (lane % HD) < HD // 2
    for c0 in range(0, n, n_chunk):
        acc = _dot(y, w_ref[:, c0:c0 + n_chunk])
        for j in range(n_chunk // LANES):
            tile = (c0 // LANES) + j
            which = max(i for i, s in enumerate(starts) if s <= tile * LANES)
            local = tile * LANES - starts[which]
            if local >= o_refs[which].shape[1]:
                continue
            a = acc[:, j * LANES:(j + 1) * LANES]
            if tile in rope_tiles:
                a = _rope_tile(a, cos_ref[...], sin_ref[...], first_half)
            o_refs[which][:, local:local + LANES] = a


def _norm_proj(x, g, w_bf16, cos, sin, rope_cols, tm, splits=None):
    m, d = x.shape
    n = w_bf16.shape[1]
    splits = (n,) if splits is None else tuple(splits)
    assert m % tm == 0 and n % LANES == 0 and all(s % LANES == 0 for s in splits) and sum(splits) <= n
    starts = tuple(int(s) for s in np.cumsum((0,) + splits[:-1]))
    tiles = n // LANES
    k = next(c for c in (4, 3, 2, 1) if tiles % c == 0)
    rope_tiles = frozenset(t for a, b in rope_cols for t in range(a // LANES, b // LANES))
    body = functools.partial(_norm_proj_body, rope_tiles=rope_tiles, n_chunk=k * LANES, starts=starts)
    outs = pl.pallas_call(
        body,
        grid=(m // tm,),
        in_specs=[
            pl.BlockSpec((tm, d), lambda i: (i, 0)),
            pl.BlockSpec((1, d), lambda i: (0, 0)),
            pl.BlockSpec((d, n), lambda i: (0, 0)),
            pl.BlockSpec((tm, LANES), lambda i: (i, 0)),
            pl.BlockSpec((tm, LANES), lambda i: (i, 0)),
        ],
        out_specs=[pl.BlockSpec((tm, s), lambda i: (i, 0)) for s in splits],
        out_shape=[jax.ShapeDtypeStruct((m, s), F32) for s in splits],
        compiler_params=_params("parallel"),
        name="norm_proj",
    )(x, g.reshape(1, d), w_bf16, cos, sin)
    return outs if len(splits) > 1 else outs[0]


def _rope_tables(pos):
    half = HD // 2
    inv = ROPE_THETA ** (-jnp.arange(half, dtype=F32) / half)
    ang = pos.astype(F32)[:, None] * inv[None, :]
    cos, sin = jnp.cos(ang), jnp.sin(ang)
    return jnp.concatenate([cos, cos, cos, cos], -1), jnp.concatenate([-sin, sin, -sin, sin], -1)


def _retention_consts(c):
    lg = np.log(1.0 - np.power(2.0, -5.0 - np.arange(RET_H, dtype=np.float64)))
    ti = np.arange(c, dtype=np.float64)
    causal = ti[:, None] >= ti[None, :]
    d_in = np.where(causal[None], np.exp(np.where(causal, ti[:, None] - ti[None, :], 0.0)[None] * lg[:, None, None]), 0.0)
    q_dec = np.repeat(np.exp((ti[:, None] + 1.0) * lg[None, :]), HD, axis=1)
    k_dec = np.repeat(np.exp((c - 1.0 - ti)[:, None] * lg[None, :]), HD, axis=1)
    c_dec = np.repeat(np.exp(c * lg), HD)
    head = np.arange(RET_W) // HD
    bd = (head[:, None] == head[None, :]).astype(np.float64)
    cmat = bd * c_dec[:, None]
    f = lambda a: jnp.asarray(a, dtype=F32)
    return f(d_in), f(q_dec), f(k_dec), f(cmat), f(bd)


def _retention_body(q_ref, k_ref, v_ref, din_ref, qdec_ref, kdec_ref, cmat_ref, bd_ref, o_ref, s_ref):
    @pl.when(pl.program_id(0) == 0)
    def _():
        s_ref[...] = jnp.zeros_like(s_ref)

    q = q_ref[...]
    k = k_ref[...] * SCALE
    v = v_ref[...]
    c, w = q.shape
    head = lax.broadcasted_iota(jnp.int32, (c, w), 1) // HD
    s = s_ref[...]
    qb, kb, vb = q.astype(BF16), k.astype(BF16), v.astype(BF16)
    o = _dot(qb, s.astype(BF16)) * qdec_ref[...]
    for h in range(RET_H):
        mh = head == h
        att = _dot_nt(jnp.where(mh, qb, jnp.zeros_like(qb)), kb) * din_ref[h]
        o = o + jnp.where(mh, _dot(att.astype(BF16), vb), 0.0)
    s_ref[...] = s * cmat_ref[...] + _dot_tn((k * kdec_ref[...]).astype(BF16), vb) * bd_ref[...]
    seg = bd_ref[...] * (1.0 / HD)
    xc = o - _dot(o, seg, HIGHEST)
    var = _dot(xc * xc, seg, HIGHEST)
    o_ref[...] = xc * lax.rsqrt(var + EPS)


def _retention_prompt(qkv):
    t, w = qkv.shape[0], RET_W
    c = math.gcd(t, RET_CHUNK)
    d_in, q_dec, k_dec, cmat, bd = _retention_consts(c)
    col = lambda j: pl.BlockSpec((c, w), lambda i: (i, j))
    const2 = lambda shape: pl.BlockSpec(shape, lambda i: (0,) * len(shape))
    return pl.pallas_call(
        _retention_body,
        grid=(t // c,),
        in_specs=[col(0), col(1), col(2), const2((RET_H, c, c)), const2((c, w)), const2((c, w)), const2((w, w)),
                  const2((w, w))],
        out_specs=[col(0), const2((w, w))],
        out_shape=[jax.ShapeDtypeStruct((t, w), F32), jax.ShapeDtypeStruct((w, w), F32)],
        compiler_params=_params("arbitrary"),
        name="retention_prompt",
    )(qkv, qkv, qkv, d_in, q_dec, k_dec, cmat, bd)


def _diag_blocks(s, h):
    s4 = s.reshape(h, HD, h, HD)
    return jnp.stack([s4[i, :, i, :] for i in range(h)], axis=0)


def _block_mean_body(k_ref, o_ref):
    o_ref[0] = jnp.mean(k_ref[...], axis=0, keepdims=True)


def _block_mean(k, blk, w, col):
    t = k.shape[0]
    assert t % blk == 0
    out = pl.pallas_call(
        _block_mean_body,
        grid=(t // blk,),
        in_specs=[pl.BlockSpec((blk, w), lambda i: (i, col))],
        out_specs=pl.BlockSpec((1, 1, w), lambda i: (i, 0, 0)),
        out_shape=jax.ShapeDtypeStruct((t // blk, 1, w), F32),
        compiler_params=_params("parallel"),
        name="block_mean",
    )(k)
    return out.reshape(t // blk, w)


def _topk_mask(score, k):
    n = score.shape[-1]
    idx = lax.broadcasted_iota(jnp.int32, score.shape, score.ndim - 1)
    sel = jnp.zeros(score.shape, F32)
    work = score
    for _ in range(k):
        m = jnp.max(work, axis=-1, keepdims=True)
        first = jnp.min(jnp.where(work == m, idx, n), axis=-1, keepdims=True)
        pick = idx == first
        sel = jnp.where(pick, 1.0, sel)
        work = jnp.where(pick, -jnp.inf, work)
    return sel


def _tile_loop(lo, hi, fn, unroll=8):
    def group(g, _):
        for u in range(unroll):
            fn(lo + unroll * g + u)
        return 0
    n = jnp.maximum(hi - lo, 0)
    main = n // unroll
    lax.fori_loop(0, main, group, 0)
    done = lo + main * unroll
    rem = n - main * unroll
    size = unroll // 2
    while size >= 1:
        @pl.when((rem & size) != 0)
        def _(done=done, size=size):
            for u in range(size):
                fn(done + u)
        done = done + (rem & size)
        size //= 2


def _lane_fold_max(s):
    out = s[:, 0:LANES]
    for c in range(1, s.shape[1] // LANES):
        out = jnp.maximum(out, s[:, c * LANES:(c + 1) * LANES])
    return out


def _moba_prompt_body(q_ref, ka_ref, vo_ref, kmean_ref, o_ref, s_ref, m_ref, acc_ref, *, topk, bq):
    qi = pl.program_id(1)
    q = q_ref[0]
    rows = q.shape[0]
    kb = MOBA_BLOCK
    gate = _dot_nt(q, kmean_ref[0], HIGHEST)
    blk = lax.broadcasted_iota(jnp.int32, (rows, LANES), 1) - HD
    own = qi * bq + lax.broadcasted_iota(jnp.int32, (rows, LANES), 0) // kb
    past = (blk >= 0) & (blk < own)
    sel = jnp.where(past, _topk_mask(jnp.where(past, gate, NEG), topk), 0.0)
    bias = jnp.where((blk < 0) | (blk == own) | (sel > 0.5), 0.0, NEG)
    qa = (q * SCALE + bias).astype(BF16)
    m_ref[...] = jnp.full(m_ref.shape, NEG, F32)
    n_past = qi * bq

    def score(j):
        start = pl.multiple_of(j * kb, kb)
        s = _dot_nt(qa, ka_ref[0, pl.ds(start, kb), :])
        s_ref[j] = s
        m_ref[...] = jnp.maximum(m_ref[...], _lane_fold_max(s))

    _tile_loop(0, n_past, score)
    qpos = lax.broadcasted_iota(jnp.int32, (rows, kb), 0)
    kpos = lax.broadcasted_iota(jnp.int32, (rows, kb), 1)
    for d in range(bq):
        start = pl.multiple_of((n_past + d) * kb, kb)
        s = jnp.where(kpos + d * kb <= qpos, _dot_nt(qa, ka_ref[0, pl.ds(start, kb), :]), NEG)
        s_ref[n_past + d] = s
        m_ref[...] = jnp.maximum(m_ref[...], _lane_fold_max(s))
    m = jnp.max(m_ref[...], axis=-1, keepdims=True)

    acc_ref[...] = jnp.zeros(acc_ref.shape, F32)

    def accumulate(j):
        start = pl.multiple_of(j * kb, kb)
        p = jnp.exp(s_ref[j] - m).astype(BF16)
        acc_ref[...] += _dot(p, vo_ref[0, pl.ds(start, kb), :])

    _tile_loop(0, n_past + bq, accumulate)
    acc = acc_ref[...]
    o_ref[0] = acc / jnp.maximum(acc[:, 0:1], TINY)


def _moba_prompt(q_pad, ka_bf16, vo_bf16, kmean_rows):
    h, t, _ = q_pad.shape
    nblk = t // MOBA_BLOCK
    assert t % MOBA_BLOCK == 0 and nblk <= LANES - HD
    bq = 2 if nblk % 2 == 0 else 1
    rows = bq * MOBA_BLOCK
    return pl.pallas_call(
        functools.partial(_moba_prompt_body, topk=min(MOBA_TOPK, nblk), bq=bq),
        grid=(h, nblk // bq),
        in_specs=[
            pl.BlockSpec((1, rows, LANES), lambda a, i: (a, i, 0)),
            pl.BlockSpec((1, t, LANES), lambda a, i: (a, 0, 0), pipeline_mode=pl.Buffered(1)),
            pl.BlockSpec((1, t, LANES), lambda a, i: (a, 0, 0), pipeline_mode=pl.Buffered(1)),
            pl.BlockSpec((1, LANES, LANES), lambda a, i: (a, 0, 0)),
        ],
        out_specs=pl.BlockSpec((1, rows, LANES), lambda a, i: (a, i, 0)),
        out_shape=jax.ShapeDtypeStruct((h, t, LANES), F32),
        scratch_shapes=[pltpu.VMEM((nblk, rows, MOBA_BLOCK), F32), pltpu.VMEM((rows, LANES), F32),
                        pltpu.VMEM((rows, LANES), F32)],
        compiler_params=_params("parallel", "arbitrary"),
        name="moba_prompt",
    )(q_pad, ka_bf16, vo_bf16, kmean_rows)


def _mem_prompt_body(q_ref, k_ref, v_ref, o_ref):
    q = q_ref[...]
    kb = k_ref[...].astype(BF16)
    vb = v_ref[...].astype(BF16)
    head = lax.broadcasted_iota(jnp.int32, q.shape, 1) // HD
    o = jnp.zeros(q.shape, F32)
    for h in range(MEM_H):
        mh = head == h
        s = _dot_nt(jnp.where(mh, q, 0.0).astype(BF16), kb) * SCALE
        e = jnp.exp(s - jnp.max(s, axis=-1, keepdims=True))
        p = e / jnp.sum(e, axis=-1, keepdims=True)
        o = o + jnp.where(mh, _dot(p.astype(BF16), vb), 0.0)
    o_ref[...] = o


def _mem_prompt(q, k, v, tm):
    t, w = q.shape
    n = k.shape[0]
    return pl.pallas_call(
        _mem_prompt_body,
        grid=(t // tm,),
        in_specs=[pl.BlockSpec((tm, w), lambda i: (i, 0)), pl.BlockSpec((n, w), lambda i: (0, 0)),
                  pl.BlockSpec((n, w), lambda i: (0, 0))],
        out_specs=pl.BlockSpec((tm, w), lambda i: (i, 0)),
        out_shape=jax.ShapeDtypeStruct((t, w), F32),
        compiler_params=_params("parallel"),
        name="mem_prompt",
    )(q, k, v)


def _mix_out_body(x_ref, a_ref, b_ref, c_ref, gate_ref, wo_ref, fg_ref, o_ref, *, final_norm):
    gate = gate_ref[...]
    gate = gate * jax.nn.sigmoid(gate)
    y = x_ref[...]
    off = 0
    for ref in (a_ref, b_ref, c_ref):
        w = ref.shape[1]
        mix = (ref[...] * gate[:, off:off + w]).astype(BF16)
        y = y + _dot(mix, wo_ref[off:off + w, :])
        off += w
    if final_norm:
        ms = jnp.mean(y * y, axis=-1, keepdims=True)
        y = y * lax.rsqrt(ms + EPS) * fg_ref[...]
    o_ref[...] = y


def _mix_out(x, a, b, c, gate, wo_bf16, final_g, tm):
    m, d = x.shape
    row = lambda w: pl.BlockSpec((tm, w), lambda i: (i, 0))
    fg = jnp.ones((1, d), F32) if final_g is None else final_g.reshape(1, d).astype(F32)
    body = functools.partial(_mix_out_body, final_norm=final_g is not None)
    return pl.pallas_call(
        body,
        grid=(m // tm,),
        in_specs=[row(d), row(a.shape[1]), row(b.shape[1]), row(c.shape[1]), row(gate.shape[1]),
                  pl.BlockSpec(wo_bf16.shape, lambda i: (0, 0)), pl.BlockSpec((1, d), lambda i: (0, 0))],
        out_specs=row(d),
        out_shape=jax.ShapeDtypeStruct((m, d), F32),
        compiler_params=_params("parallel"),
        name="mix_out",
    )(x, a, b, c, gate, wo_bf16, fg)


def _compress_body(x_ref, pe_ref, w1_ref, w2_ref, o_ref):
    x = (x_ref[0] + pe_ref[0]).astype(BF16)
    hid = jax.nn.gelu(_dot(x, w1_ref[0]))
    o_ref[0] = _dot(hid.astype(BF16), w2_ref[0])


def _compress_weights(w1, w2, pe):
    eye = jnp.eye(NSA_KVH, dtype=F32)
    w1r = w1.astype(F32).reshape(2, NSA_BLOCK, HD, HD)
    w1b = jnp.einsum("klde,gh->klgdhe", w1r, eye).reshape(2, NSA_BLOCK * NSA_KVW, NSA_KVW)
    w2b = jnp.einsum("kde,gh->kgdhe", w2.astype(F32), eye).reshape(2, NSA_KVW, NSA_KVW)
    peb = jnp.broadcast_to(pe.astype(F32)[:, :, None, :], (2, NSA_BLOCK, NSA_KVH, HD)).reshape(2, 1, NSA_BLOCK * NSA_KVW)
    return w1b.astype(BF16), w2b.astype(BF16), peb


def _compress(x, w1b, w2b, peb):
    _, nb, kdim = x.shape
    return pl.pallas_call(
        _compress_body,
        grid=(2,),
        in_specs=[pl.BlockSpec((1, nb, kdim), lambda i: (i, 0, 0)), pl.BlockSpec((1, 1, kdim), lambda i: (i, 0, 0)),
                  pl.BlockSpec((1, kdim, NSA_KVW), lambda i: (i, 0, 0)),
                  pl.BlockSpec((1, NSA_KVW, NSA_KVW), lambda i: (i, 0, 0))],
        out_specs=pl.BlockSpec((1, nb, NSA_KVW), lambda i: (i, 0, 0)),
        out_shape=jax.ShapeDtypeStruct((2, nb, NSA_KVW), F32),
        compiler_params=_params("parallel"),
        name="nsa_compress",
    )(x, peb, w1b, w2b)


def _nsa_prompt_body(q_ref, qr_ref, gt_ref, ckv_ref, ska_ref, svo_ref, wkv_ref, o_ref, s_ref, m_ref, acc_ref,
                     *, tq, kc):
    i = pl.program_id(1)
    t0 = i * tq
    rows = q_ref.shape[2]
    nb = ckv_ref.shape[1]
    q = q_ref[0, 0]
    qrb = (qr_ref[0, 0] * SCALE).astype(BF16)
    tpos = t0 + lax.broadcasted_iota(jnp.int32, (rows, 1), 0) % tq

    ckv = ckv_ref[0].astype(BF16)
    blk = lax.broadcasted_iota(jnp.int32, (nb, rows), 0)
    tpos_l = t0 + lax.broadcasted_iota(jnp.int32, (nb, rows), 1) % tq
    complete = blk * NSA_BLOCK + (NSA_BLOCK - 1) <= tpos_l
    s_c = jnp.where(complete, _dot_nt(ckv, q.astype(BF16)) * SCALE, NEG)
    e = jnp.where(complete, jnp.exp(s_c - jnp.max(s_c, axis=0, keepdims=True)), 0.0)
    pc = e / jnp.maximum(jnp.sum(e, axis=0, keepdims=True), TINY)
    o_c = _dot_tn(pc.astype(BF16), ckv)

    imp = pc[:, 0:tq]
    for r in range(1, NSA_REP):
        imp = imp + pc[:, r * tq:(r + 1) * tq]
    blk_q = lax.broadcasted_iota(jnp.int32, (nb, tq), 0)
    tpos_q = t0 + lax.broadcasted_iota(jnp.int32, (nb, tq), 1)
    cur = tpos_q // NSA_BLOCK
    forced = (blk_q == 0) | (blk_q == cur) | (blk_q == cur - 1)
    complete_q = blk_q * NSA_BLOCK + (NSA_BLOCK - 1) <= tpos_q
    score = jnp.where(forced, FORCE, jnp.where(complete_q, imp, NEG))
    sel = jnp.where(score > 0.5 * NEG, _topk_mask_axis0(score, min(NSA_TOPN, nb)), 0.0)

    bias = jnp.where(sel > 0.5, 0.0, NEG).T
    nbp = -(-nb // LANES) * LANES
    if nbp > nb:
        bias = jnp.concatenate([bias, jnp.full((tq, nbp - nb), NEG, F32)], axis=1)
    bias = jnp.concatenate([bias] * NSA_REP, axis=0)
    qrs = qr_ref[0, 0] * SCALE
    upper = lax.broadcasted_iota(jnp.int32, (rows, LANES), 1) >= HD
    n_span = -(-nb // HD)
    qa = []
    for sp in range(n_span):
        col = bias[:, (sp // 2) * LANES:(sp // 2 + 1) * LANES]
        if sp % 2 == 0:
            col = pltpu.roll(col, HD, 1)
        qa.append((qrs + jnp.where(upper, col, 0.0)).astype(BF16))
    cps = HD * NSA_BLOCK // kc
    n_chunks = (t0 + tq + kc - 1) // kc
    last = n_chunks - 1
    qa_last = qa[0]
    for sp in range(1, n_span):
        qa_last = jnp.where(last // cps == sp, qa[sp], qa_last)
    last_start = pl.multiple_of(last * kc, kc)
    kpos = last_start + lax.broadcasted_iota(jnp.int32, (rows, kc), 1)
    s_last = jnp.where(kpos <= tpos, _dot_nt(qa_last, ska_ref[0, pl.ds(last_start, kc), :]), NEG)

    def spans(fn):
        for sp in range(n_span):
            lo = sp * cps
            _tile_loop(jnp.minimum(lo, last), jnp.minimum(lo + cps, last), functools.partial(fn, qa[sp]))

    m_ref[...] = _lane_fold_max(s_last)

    def fold_max(qsp, c):
        start = pl.multiple_of(c * kc, kc)
        s = _dot_nt(qsp, ska_ref[0, pl.ds(start, kc), :])
        s_ref[c] = s
        m_ref[...] = jnp.maximum(m_ref[...], _lane_fold_max(s))

    spans(fold_max)
    m_s = jnp.max(m_ref[...], axis=-1, keepdims=True)
    acc_ref[...] = _dot(jnp.exp(s_last - m_s).astype(BF16), svo_ref[0, pl.ds(last_start, kc), :])

    def accumulate(qsp, c):
        start = pl.multiple_of(c * kc, kc)
        p = jnp.exp(s_ref[c] - m_s).astype(BF16)
        acc_ref[...] += _dot(p, svo_ref[0, pl.ds(start, kc), :])

    spans(accumulate)
    acc_s = acc_ref[...]
    o_s = acc_s / jnp.maximum(acc_s[:, 0:1], TINY)

    nwin = (NSA_WINDOW + tq - 1) // tq + 1
    win = []
    for d in range(nwin):
        c = i - (nwin - 1) + d
        start = pl.multiple_of(jnp.maximum(c, 0) * tq, tq)
        kv = wkv_ref[0, pl.ds(start, tq), :]
        kpos = c * tq + lax.broadcasted_iota(jnp.int32, (rows, tq), 1)
        mask = (kpos <= tpos) & (kpos > tpos - NSA_WINDOW) & (kpos >= 0)
        win.append((jnp.where(mask, _dot_nt(qrb, kv), NEG), kv))
    m_w = jnp.max(functools.reduce(jnp.maximum, [s for s, _ in win]), axis=-1, keepdims=True)
    p_w = [jnp.exp(s - m_w) for s, _ in win]
    l_w = jnp.sum(functools.reduce(jnp.add, p_w), axis=-1, keepdims=True)
    acc_w = functools.reduce(jnp.add, [_dot(p.astype(BF16), kv) for p, (_, kv) in zip(p_w, win)])
    o_w = acc_w / jnp.maximum(l_w, TINY)

    gt = jax.nn.sigmoid(gt_ref[0, 0])
    o_ref[0, 0] = gt[:, 0:1] * o_c + gt[:, 1:2] * o_s + gt[:, 2:3] * o_w


def _nsa_prompt(q_st, qr_st, gt_st, ckv, ska, svo, wkv, tq, kc):
    g, nt, rows, _ = q_st.shape
    t = ska.shape[1]
    nb = ckv.shape[1]
    assert t % kc == 0 and kc % tq == 0 and (HD * NSA_BLOCK) % kc == 0 and t % tq == 0
    qspec = pl.BlockSpec((1, 1, rows, LANES), lambda a, i: (a, i, 0, 0))
    seq = pl.BlockSpec((1, t, LANES), lambda a, i: (a, 0, 0), pipeline_mode=pl.Buffered(1))
    body = functools.partial(_nsa_prompt_body, tq=tq, kc=kc)
    return pl.pallas_call(
        body,
        grid=(g, nt),
        in_specs=[qspec, qspec, pl.BlockSpec((1, 1, rows, 3), lambda a, i: (a, i, 0, 0)),
                  pl.BlockSpec((1, nb, LANES), lambda a, i: (a, 0, 0)), seq, seq, seq],
        out_specs=qspec,
        out_shape=jax.ShapeDtypeStruct((g, nt, rows, LANES), F32),
        scratch_shapes=[pltpu.VMEM((t // kc, rows, kc), F32), pltpu.VMEM((rows, LANES), F32),
                        pltpu.VMEM((rows, LANES), F32)],
        compiler_params=_params("parallel", "arbitrary"),
        name="nsa_prompt",
    )(q_st, qr_st, gt_st, ckv, ska, svo, wkv)


def _hgrn_body(q_ref, f_ref, i_ref, lb_ref, o_ref, s_ref):
    @pl.when(pl.program_id(0) == 0)
    def _():
        s_ref[...] = jnp.zeros_like(s_ref)

    q = q_ref[...]
    v = i_ref[...]
    c, w = q.shape
    lb = lb_ref[...]
    f = lb + (1.0 - lb) * jax.nn.sigmoid(f_ref[...])
    kk = 1.0 - f
    tr = lax.broadcasted_iota(jnp.int32, (c, c), 0)
    tc = lax.broadcasted_iota(jnp.int32, (c, c), 1)
    cum = _dot(jnp.where(tr >= tc, 1.0, 0.0), jnp.log(f), HIGHEST)
    bd = _block_ones(w)
    bd_b = bd.astype(BF16)
    s = s_ref[...]
    o = _dot(q * jnp.exp(cum), s, HIGHEST)
    t_idx = lax.broadcasted_iota(jnp.int32, (c, w), 0)

    sub = 8
    groups = c // sub
    o_blk = [o[i * sub:(i + 1) * sub] for i in range(groups)]
    for g in range(groups):
        r0 = g * sub
        n = c - r0
        t_g = r0 + lax.broadcasted_iota(jnp.int32, (n, w), 0)
        es = [jnp.where(t_g >= j, jnp.exp(cum[r0:] - cum[j:j + 1]) * q[r0:] * kk[j:j + 1], 0.0)
              for j in range(r0, r0 + sub)]
        e = jnp.concatenate(es, axis=0)
        e_hi = e.astype(BF16)
        e_lo = (e - e_hi.astype(F32)).astype(BF16)
        a = _dot(e_hi, bd_b) + _dot(e_lo, bd_b)
        contrib = a[0:n] * v[r0:r0 + 1]
        for u in range(1, sub):
            contrib = contrib + a[u * n:(u + 1) * n] * v[r0 + u:r0 + u + 1]
        for i in range(g, groups):
            o_blk[i] = o_blk[i] + contrib[(i - g) * sub:(i - g + 1) * sub]
    o = jnp.concatenate(o_blk, axis=0)
    last = cum[c - 1:c, :]
    row0 = t_idx == 0
    scale_mat = _dot_tn(jnp.where(row0, jnp.exp(last), 0.0), jnp.where(row0, 1.0, 0.0), HIGHEST)
    s_ref[...] = s * scale_mat + _dot_tn(kk * jnp.exp(last - cum), v, HIGHEST) * bd
    ms = _dot(o * o, bd * (1.0 / HD), HIGHEST)
    o_ref[...] = o * lax.rsqrt(ms + EPS)


def _hgrn_prompt(qfi, lb):
    t, w = qfi.shape[0], HGRN_W
    c = math.gcd(t, HGRN_CHUNK)
    col = lambda j: pl.BlockSpec((c, w), lambda n: (n, j))
    return pl.pallas_call(
        _hgrn_body,
        grid=(t // c,),
        in_specs=[col(0), col(1), col(2), pl.BlockSpec((1, w), lambda n: (0, 0))],
        out_specs=[col(0), pl.BlockSpec((w, w), lambda n: (0, 0))],
        out_shape=[jax.ShapeDtypeStruct((t, w), F32), jax.ShapeDtypeStruct((w, w), F32)],
        compiler_params=_params("arbitrary"),
        name="hgrn_prompt",
    )(qfi, qfi, qfi, lb)


def _state_step_body(s_ref, q_ref, a_ref, b_ref, v_ref, o_ref, so_ref, *, mode):
    s = s_ref[...]
    if mode == "ret":
        k = a_ref[...] * SCALE
        dec = b_ref[...]
    else:
        lb = b_ref[...]
        dec = lb + (1.0 - lb) * jax.nn.sigmoid(a_ref[...])
        k = 1.0 - dec
    kv = k * v_ref[...]
    rows = s.shape[0]
    o = (q_ref[...] * (kv + dec * s)).reshape(rows // HD, HD, HD).sum(axis=1)
    so_ref[...] = dec * s + kv
    if mode == "ret":
        xc = o - jnp.mean(o, axis=-1, keepdims=True)
        o_ref[...] = xc * lax.rsqrt(jnp.mean(xc * xc, axis=-1, keepdims=True) + EPS)
    else:
        o_ref[...] = o * lax.rsqrt(jnp.mean(o * o, axis=-1, keepdims=True) + EPS)


def _state_step(state, q, a, b, v, mode):
    bsz, h = state.shape[0], state.shape[1]
    rows = bsz * h * HD
    col = lambda x: jnp.broadcast_to(x.reshape(bsz, h, HD, 1), (bsz, h, HD, HD)).reshape(rows, HD)
    vx = jnp.broadcast_to(v.reshape(bsz, h, 1, HD), (bsz, h, HD, HD)).reshape(rows, HD)
    bb = 8 if bsz % 8 == 0 else bsz
    br = bb * h * HD
    spec = pl.BlockSpec((br, HD), lambda i: (i, 0))
    ospec = pl.BlockSpec((br // HD, HD), lambda i: (i, 0))
    o, s_new = pl.pallas_call(
        functools.partial(_state_step_body, mode=mode),
        grid=(rows // br,),
        in_specs=[spec] * 5,
        out_specs=[ospec, spec],
        out_shape=[jax.ShapeDtypeStruct((rows // HD, HD), F32), jax.ShapeDtypeStruct((rows, HD), F32)],
        compiler_params=_params("parallel"),
        name="state_step_" + mode,
    )(state.reshape(rows, HD).astype(F32), col(q), col(a), col(b), vx)
    return o.reshape(bsz, h * HD), s_new.reshape(bsz, h, HD, HD)


def _expand_heads(p, width):
    g, n = p.shape
    return jnp.broadcast_to(p[:, None, :], (g, HD, n)).reshape(width, n)


def _head_sums(x, g):
    return x.reshape(g, HD, x.shape[-1]).sum(axis=1)


def _mem_sample_body(q_ref, kt_ref, vt_ref, o_ref):
    kt = kt_ref[0, 0]
    vt = vt_ref[0, 0]
    s = _head_sums(kt * q_ref[0], MEM_H) * SCALE
    e = jnp.exp(s - jnp.max(s, axis=-1, keepdims=True))
    p = e / jnp.sum(e, axis=-1, keepdims=True)
    o_ref[0] = jnp.sum(vt * _expand_heads(p, MEM_W), axis=-1, keepdims=True)


def _mem_sample(q, kvt):
    bsz, w = q.shape
    n = kvt.shape[-1]
    out = pl.pallas_call(
        _mem_sample_body,
        grid=(bsz,),
        in_specs=[pl.BlockSpec((1, w, 1), lambda i: (i, 0, 0)),
                  pl.BlockSpec((1, 1, w, n), lambda i: (i, 0, 0, 0)),
                  pl.BlockSpec((1, 1, w, n), lambda i: (i, 1, 0, 0))],
        out_specs=pl.BlockSpec((1, w, 1), lambda i: (i, 0, 0)),
        out_shape=jax.ShapeDtypeStruct((bsz, w, 1), F32),
        compiler_params=_params("parallel"),
        name="mem_sample",
    )(q.reshape(bsz, w, 1), kvt, kvt)
    return out.reshape(bsz, w)


def _topk_mask_axis0(score, k):
    n = score.shape[0]
    idx = lax.broadcasted_iota(jnp.int32, score.shape, 0)
    sel = jnp.zeros(score.shape, F32)
    work = score
    for _ in range(k):
        m = jnp.max(work, axis=0, keepdims=True)
        first = jnp.min(jnp.where(work == m, idx, n), axis=0, keepdims=True)
        pick = idx == first
        sel = jnp.where(pick, 1.0, sel)
        work = jnp.where(pick, -jnp.inf, work)
    return sel


def _paged_attn_body(pt_ref, pool_ref, qc_ref, qbd_ref, kn_ref, vn_ref, mask_ref, o_ref, buf, sem, s_ref, acc_ref, st_ref,
                     *, layer, groups, reps, chunk, moba):
    b = pl.program_id(0)
    n_pages = pt_ref.shape[1]
    n = n_pages // chunk
    gw = groups * HD

    def copy(kv, c, i, slot):
        return pltpu.make_async_copy(pool_ref.at[pt_ref[b, c * chunk + i], layer, kv], buf.at[slot, i], sem.at[slot])

    def start(kv, c, slot):
        lax.fori_loop(0, chunk, lambda i, _: (copy(kv, c, i, slot).start(), 0)[1], 0)

    def wait(kv, c, slot):
        lax.fori_loop(0, chunk, lambda i, _: (copy(kv, c, i, slot).wait(), 0)[1], 0)

    acc_ref[...] = jnp.zeros_like(acc_ref)
    start(0, 0, 0)

    def k_compute(c, slot):
        for i in range(chunk):
            sc = _dot(qbd_ref[0], buf[slot, i].astype(BF16))
            for r in range(reps):
                s_ref[r, c * chunk + i] = sc[r * groups:(r + 1) * groups]

    def select():
        if moba:
            per = MOBA_BLOCK // PAGE
            nblk = n_pages // per
            sc = s_ref[0].reshape(nblk, per, groups, PAGE)
            gate = jnp.sum(jnp.sum(sc, axis=1, keepdims=True), axis=-1, keepdims=True) * (1.0 / MOBA_BLOCK)
            sel = _topk_mask_axis0(gate, min(MOBA_TOPK, nblk + 1))
            mask = jnp.broadcast_to(sel, (nblk, per, groups, PAGE)).reshape(n_pages, groups, PAGE) > 0.5
        else:
            mask = mask_ref[0] > 0.5
        for r in range(reps):
            qcol = qc_ref[0, :, r:r + 1]
            s_own = _head_sums(jnp.broadcast_to(kn_ref[0] * qcol, (gw, PAGE)), groups) * SCALE
            s = jnp.where(mask, s_ref[r] * SCALE, NEG)
            m = jnp.max(jnp.max(s, axis=0), axis=-1, keepdims=True)
            m = jnp.maximum(m, s_own)
            p = jnp.where(mask, jnp.exp(s - m), 0.0)
            p_own = jnp.exp(s_own - m)
            den = jnp.sum(jnp.sum(p, axis=0), axis=-1, keepdims=True) + p_own
            s_ref[r] = p
            st_ref[r, 0] = p_own
            st_ref[r, 1] = jnp.maximum(den, TINY)

    def v_compute(c, slot):
        for r in range(reps):
            acc = acc_ref[r]
            for i in range(chunk):
                acc = acc + buf[slot, i] * _expand_heads(s_ref[r, c * chunk + i], gw)
            acc_ref[r] = acc

    def step(idx, _):
        slot = idx % 2
        nxt = idx + 1

        @pl.when(nxt < 2 * n)
        def _():
            start(nxt // n, nxt % n, 1 - slot)

        wait(idx // n, idx % n, slot)

        @pl.when(idx < n)
        def _():
            k_compute(idx, slot)

        @pl.when(idx == n - 1)
        def _():
            select()

        @pl.when(idx >= n)
        def _():
            v_compute(idx - n, slot)

        return 0

    lax.fori_loop(0, 2 * n, step, 0)
    for r in range(reps):
        tot = jnp.sum(acc_ref[r], axis=-1, keepdims=True)
        p_own = _expand_heads(st_ref[r, 0], gw)[:, 0:1]
        den = _expand_heads(st_ref[r, 1], gw)[:, 0:1]
        o_ref[0, :, r:r + 1] = (tot + p_own * vn_ref[0]) / den


def _paged_attn(page_table, pool_t, layer, q_cols, k_new, v_new, row_mask, groups, reps, moba):
    bsz, n_pages = page_table.shape
    gw = groups * HD
    chunk = math.gcd(n_pages, 16)
    if row_mask is None:
        row_mask = jnp.zeros((bsz, 1, groups, PAGE), F32)
    mshape = row_mask.shape[1:]
    nq = -(-reps * groups // 8) * 8
    qh = q_cols.reshape(bsz, groups, HD, reps).transpose(0, 3, 1, 2)
    qbd = jnp.einsum("brgd,gh->brghd", qh, jnp.eye(groups, dtype=F32)).reshape(bsz, reps * groups, gw)
    qbd = jnp.pad(qbd, ((0, 0), (0, nq - reps * groups), (0, 0))).astype(BF16)
    body = functools.partial(_paged_attn_body, layer=layer, groups=groups, reps=reps, chunk=chunk, moba=moba)
    grid_spec = pltpu.PrefetchScalarGridSpec(
        num_scalar_prefetch=1,
        grid=(bsz,),
        in_specs=[
            pl.BlockSpec(memory_space=pl.ANY),
            pl.BlockSpec((1, gw, reps), lambda i, pt: (i, 0, 0)),
            pl.BlockSpec((1, nq, gw), lambda i, pt: (i, 0, 0)),
            pl.BlockSpec((1, gw, 1), lambda i, pt: (i, 0, 0)),
            pl.BlockSpec((1, gw, 1), lambda i, pt: (i, 0, 0)),
            pl.BlockSpec((1,) + mshape, lambda i, pt: (i, 0, 0, 0)),
        ],
        out_specs=pl.BlockSpec((1, gw, reps), lambda i, pt: (i, 0, 0)),
        scratch_shapes=[
            pltpu.VMEM((2, chunk, gw, PAGE), F32),
            pltpu.SemaphoreType.DMA((2,)),
            pltpu.VMEM((reps, n_pages, groups, PAGE), F32),
            pltpu.VMEM((reps, gw, PAGE), F32),
            pltpu.VMEM((reps, 2, groups, PAGE), F32),
        ],
    )
    return pl.pallas_call(
        body,
        grid_spec=grid_spec,
        out_shape=jax.ShapeDtypeStruct((bsz, gw, reps), F32),
        compiler_params=_params("arbitrary"),
        name="paged_attn_moba" if moba else "paged_attn_nsa",
    )(page_table, pool_t, q_cols, qbd, k_new.reshape(bsz, gw, 1), v_new.reshape(bsz, gw, 1), row_mask)


CMP_DB = 8


def _compress_weights_t(w1, w2, pe):
    per = PAGE // NSA_BLOCK
    eye = jnp.eye(per, dtype=F32)
    w1r = w1.astype(F32).reshape(2, NSA_BLOCK, HD, HD)
    wd = jnp.einsum("klde,gh->kdglhe", w1r, eye).reshape(2, HD // CMP_DB, CMP_DB * PAGE, per * HD)
    ped = jnp.tile(pe.astype(F32).transpose(0, 2, 1), (1, 1, per)).reshape(2, HD // CMP_DB, 1, CMP_DB * PAGE)
    w2t = jnp.einsum("kde,gh->kgdhe", w2.astype(F32), eye).reshape(2, per * HD, per * HD)
    return wd.astype(BF16), ped, w2t.astype(BF16)


def _nsa_cmp_sample_body(pt_ref, pool_ref, q_ref, wd_ref, ped_ref, w2_ref, oc_ref, sel_ref, buf, sem, tok_ref, acc_ref,
                         *, layer, nseq):
    b = pl.program_id(0)
    j = pl.program_id(1)
    bsz, n_pages = pt_ref.shape
    t = b * 2 + j
    slot = t % 2
    m_rows = nseq * NSA_KVH * n_pages

    def copy(grp, kv, sq, p, g, sl):
        row = pl.multiple_of((((sl * nseq + sq) * NSA_KVH + g) * n_pages + p) * HD, HD)
        return pltpu.make_async_copy(pool_ref.at[pt_ref[grp * nseq + sq, p], layer, kv, g], buf.at[pl.ds(row, HD)],
                                     sem.at[sl])

    def each(fn):
        def run(grp, kv, sl):
            def page(p, _):
                for sq in range(nseq):
                    for g in range(NSA_KVH):
                        fn(copy(grp, kv, sq, p, g, sl))
                return 0
            lax.fori_loop(0, n_pages, page, 0)
        return run

    start = each(lambda c: c.start())
    wait = each(lambda c: c.wait())

    @pl.when(t == 0)
    def _():
        start(0, 0, 0)

    @pl.when(t + 1 < 2 * (bsz // nseq))
    def _():
        start((t + 1) // 2, (t + 1) % 2, 1 - slot)

    wait(b, j, slot)

    base = slot * m_rows * HD
    for dd in range(HD // CMP_DB):
        parts = [buf[pl.ds(base + dd * CMP_DB + u, m_rows, stride=HD), :] for u in range(CMP_DB)]
        a = jnp.concatenate(parts, axis=1) + ped_ref[0, dd]
        part = _dot(a.astype(BF16), wd_ref[0, dd])
        if dd == 0:
            acc_ref[...] = part
        else:
            acc_ref[...] += part
    tok_ref[j] = _dot(jax.nn.gelu(acc_ref[...]).astype(BF16), w2_ref[0])

    @pl.when(j == 1)
    def _():
        pos = n_pages * PAGE
        per = PAGE // NSA_BLOCK
        n_idx = (lax.broadcasted_iota(jnp.int32, (per, n_pages), 1) * per
                 + lax.broadcasted_iota(jnp.int32, (per, n_pages), 0))
        complete = n_idx * NSA_BLOCK + (NSA_BLOCK - 1) <= pos
        cur = pos // NSA_BLOCK
        forced = (n_idx == 0) | (n_idx == cur) | (n_idx == cur - 1)
        k_past = min(NSA_TOPN, per * n_pages + 1) - 1
        for sq, g in ((a, c) for a in range(nseq) for c in range(NSA_KVH)):
            qg = q_ref[sq, g * NSA_REP:(g + 1) * NSA_REP, :].astype(BF16)
            r0 = (sq * NSA_KVH + g) * n_pages
            ck = tok_ref[0, r0:r0 + n_pages, :].astype(BF16)
            cv = tok_ref[1, r0:r0 + n_pages, :].astype(BF16)
            s = [jnp.where(complete[h:h + 1], _dot_nt(qg, ck[:, h * HD:(h + 1) * HD]) * SCALE, NEG) for h in range(per)]
            m = functools.reduce(jnp.maximum, [jnp.max(x, axis=-1, keepdims=True) for x in s])
            e = [jnp.where(complete[h:h + 1], jnp.exp(s[h] - m), 0.0) for h in range(per)]
            den = jnp.maximum(sum(jnp.sum(x, axis=-1, keepdims=True) for x in e), TINY)
            pc = [x / den for x in e]
            oc_ref[sq, g * NSA_REP:(g + 1) * NSA_REP, :] = sum(
                _dot(pc[h].astype(BF16), cv[:, h * HD:(h + 1) * HD]) for h in range(per))
            imp = jnp.concatenate([jnp.sum(x, axis=0, keepdims=True) for x in pc], axis=0)
            score = jnp.where(forced, FORCE, jnp.where(complete, imp, NEG))
            sel = jnp.zeros(score.shape, F32)
            work = score
            for _ in range(k_past):
                mx = jnp.max(jnp.max(work, axis=-1, keepdims=True), axis=0, keepdims=True)
                cand = jnp.where(work == mx, n_idx, per * n_pages)
                first = jnp.min(jnp.min(cand, axis=-1, keepdims=True), axis=0, keepdims=True)
                pick = n_idx == first
                sel = jnp.where(pick, 1.0, sel)
                work = jnp.where(pick, -jnp.inf, work)
            sel_ref[sq, g] = jnp.where(score > 0.5 * NEG, sel, 0.0)


def _nsa_cmp_sample(page_table, pool_t, layer, q, wd, ped, w2b):
    bsz, n_pages = page_table.shape
    per = PAGE // NSA_BLOCK
    nseq = 2 if bsz % 2 == 0 else 1
    m_rows = nseq * NSA_KVH * n_pages
    grid_spec = pltpu.PrefetchScalarGridSpec(
        num_scalar_prefetch=1,
        grid=(bsz // nseq, 2),
        in_specs=[
            pl.BlockSpec(memory_space=pl.ANY),
            pl.BlockSpec((nseq, NSA_H, HD), lambda i, j, pt: (i, 0, 0)),
            pl.BlockSpec((1, HD // CMP_DB, CMP_DB * PAGE, per * HD), lambda i, j, pt: (j, 0, 0, 0)),
            pl.BlockSpec((1, HD // CMP_DB, 1, CMP_DB * PAGE), lambda i, j, pt: (j, 0, 0, 0)),
            pl.BlockSpec((1, per * HD, per * HD), lambda i, j, pt: (j, 0, 0)),
        ],
        out_specs=[pl.BlockSpec((nseq, NSA_H, HD), lambda i, j, pt: (i, 0, 0)),
                   pl.BlockSpec((nseq, NSA_KVH, per, n_pages), lambda i, j, pt: (i, 0, 0, 0))],
        scratch_shapes=[
            pltpu.VMEM((2 * m_rows * HD, PAGE), F32),
            pltpu.SemaphoreType.DMA((2,)),
            pltpu.VMEM((2, m_rows, per * HD), F32),
            pltpu.VMEM((m_rows, per * HD), F32),
        ],
    )
    return pl.pallas_call(
        functools.partial(_nsa_cmp_sample_body, layer=layer, nseq=nseq),
        grid_spec=grid_spec,
        out_shape=[jax.ShapeDtypeStruct((bsz, NSA_H, HD), F32),
                   jax.ShapeDtypeStruct((bsz, NSA_KVH, per, n_pages), F32)],
        compiler_params=_params("arbitrary", "arbitrary"),
        name="nsa_cmp_sample",
    )(page_table, pool_t, q, wd, ped, w2b)


def _win_sample_body(qc_ref, kt_ref, vt_ref, kn_ref, vn_ref, o_ref):
    kt = kt_ref[0, 0]
    vt = vt_ref[0, 0]
    gw, lbuf = kt.shape
    j = lax.broadcasted_iota(jnp.int32, (NSA_KVH, lbuf), 1)
    mask = j > lbuf - NSA_WINDOW
    for r in range(NSA_REP):
        qcol = qc_ref[0, :, r:r + 1]
        s = jnp.where(mask, _head_sums(kt * qcol, NSA_KVH) * SCALE, NEG)
        s_own = _head_sums(jnp.broadcast_to(kn_ref[0] * qcol, (gw, lbuf)), NSA_KVH) * SCALE
        m = jnp.maximum(jnp.max(s, axis=-1, keepdims=True), s_own)
        p = jnp.where(mask, jnp.exp(s - m), 0.0)
        p_own = jnp.exp(s_own - m)
        den = jnp.maximum(jnp.sum(p, axis=-1, keepdims=True) + p_own, TINY)
        tot = jnp.sum(vt * _expand_heads(p, gw), axis=-1, keepdims=True)
        o_ref[0, :, r:r + 1] = (tot + _expand_heads(p_own, gw)[:, 0:1] * vn_ref[0]) / _expand_heads(den, gw)[:, 0:1]


def _win_sample(q_cols, wbuf_t, k_new, v_new):
    bsz, gw, reps = q_cols.shape
    lbuf = wbuf_t.shape[-1]
    col = pl.BlockSpec((1, gw, 1), lambda i: (i, 0, 0))
    return pl.pallas_call(
        _win_sample_body,
        grid=(bsz,),
        in_specs=[pl.BlockSpec((1, gw, reps), lambda i: (i, 0, 0)),
                  pl.BlockSpec((1, 1, gw, lbuf), lambda i: (i, 0, 0, 0)),
                  pl.BlockSpec((1, 1, gw, lbuf), lambda i: (i, 1, 0, 0)), col, col],
        out_specs=pl.BlockSpec((1, gw, reps), lambda i: (i, 0, 0)),
        out_shape=jax.ShapeDtypeStruct((bsz, gw, reps), F32),
        compiler_params=_params("parallel"),
        name="nsa_win_sample",
    )(q_cols, wbuf_t, wbuf_t, k_new.reshape(bsz, gw, 1), v_new.reshape(bsz, gw, 1))


def _gate_combine_body(g_ref, c_ref, s_ref, w_ref, o_ref):
    gt = jax.nn.sigmoid(g_ref[...])
    o_ref[...] = gt[0] * c_ref[...] + gt[1] * s_ref[...] + gt[2] * w_ref[...]


def _gate_combine(gate_logits, o_c, o_s, o_w):
    return pl.pallas_call(
        _gate_combine_body,
        out_shape=jax.ShapeDtypeStruct(o_c.shape, F32),
        name="nsa_gate_combine",
    )(gate_logits, o_c, o_s, o_w)


ODD_NQ, ODD_NQR, ODD_KV = 0, NSA_QW, 2 * NSA_QW
ODD_HQ = ODD_KV + 6 * NSA_KVW
ODD_EQ = ODD_HQ + 3 * HGRN_W
ODD_GATE = ODD_EQ + MEM_W
ODD_NG = ODD_GATE + MIX_W
ODD_N = 4096


def _odd_weights(w_in):
    offs = np.cumsum([0, NSA_QW] + [NSA_KVW] * 6 + [3 * NSA_H] + [HGRN_W] * 3 + [MEM_W, MIX_W])
    nq = w_in[:, offs[0]:offs[1]]
    kv = w_in[:, offs[1]:offs[7]]
    ng = w_in[:, offs[7]:offs[8]]
    rest = w_in[:, offs[8]:]
    pad = jnp.zeros((w_in.shape[0], ODD_N - ODD_NG - 3 * NSA_H), w_in.dtype)
    return jnp.concatenate([nq, nq, kv, rest, ng, pad], axis=1)


def _stack_heads(a, tq, width):
    t = a.shape[0]
    a = a.reshape(t // tq, tq, NSA_KVH, NSA_REP, width).transpose(2, 0, 3, 1, 4)
    return a.reshape(NSA_KVH, t // tq, NSA_REP * tq, width)


def _unstack_heads(a, tq):
    g, nt, _, width = a.shape
    a = a.reshape(g, nt, NSA_REP, tq, width).transpose(1, 3, 0, 2, 4)
    return a.reshape(nt * tq, g * NSA_REP * width)


def _group_kv(k, v):
    t = k.shape[0]
    return jnp.concatenate([k.reshape(t, NSA_KVH, HD), v.reshape(t, NSA_KVH, HD)], axis=-1).transpose(1, 0, 2)


def _pad_lanes(a):
    return jnp.concatenate([a, jnp.zeros_like(a)], axis=-1)


def _odd_prompt(x, cos, sin, g, w_aug_bf16, w_o_bf16, mem_k, mem_v, lb, cmp_w, final_g, tm, tq, kc):
    t = x.shape[0]
    assert t % NSA_BLOCK == 0
    kvo = lambda j: ODD_KV + j * NSA_KVW
    nq, nqr, kv6, qfi, eq, gate, ng = _norm_proj(
        x, g, w_aug_bf16, cos, sin, ((ODD_NQR, ODD_KV), (kvo(2), kvo(3)), (kvo(4), kvo(5))), tm,
        splits=(NSA_QW, NSA_QW, 6 * NSA_KVW, 3 * HGRN_W, MEM_W, MIX_W, LANES))
    ck, cv, sk, sv, wk, wv = (kv6[:, j * NSA_KVW:(j + 1) * NSA_KVW] for j in range(6))
    w1b, w2b, peb = cmp_w
    nb = t // NSA_BLOCK
    cmp_tok = _compress(jnp.stack([ck, cv]).reshape(2, nb, NSA_BLOCK * NSA_KVW), w1b, w2b, peb)
    ckv = _group_kv(cmp_tok[0], cmp_tok[1])
    q_st = _pad_lanes(_stack_heads(nq, tq, HD))
    qr_st = _pad_lanes(_stack_heads(nqr, tq, HD))
    gt_st = _stack_heads(ng[:, :3 * NSA_H], tq, 3)
    onehot = ((jnp.arange(t)[:, None] // NSA_BLOCK) % HD == jnp.arange(HD)[None, :]).astype(F32)
    ska = jnp.concatenate([sk.reshape(t, NSA_KVH, HD), jnp.broadcast_to(onehot[:, None], (t, NSA_KVH, HD))], axis=-1)
    svo = jnp.concatenate([jnp.ones((t, NSA_KVH, HD), F32), sv.reshape(t, NSA_KVH, HD)], axis=-1)
    o_nsa = _nsa_prompt(q_st, qr_st, gt_st, ckv, ska.transpose(1, 0, 2).astype(BF16),
                        svo.transpose(1, 0, 2).astype(BF16), _group_kv(wk, wv).astype(BF16), tq, kc)
    o_nsa = _unstack_heads(o_nsa[..., HD:], tq)
    o_hg, s_hg = _hgrn_prompt(qfi, lb.reshape(1, HGRN_W))
    o_mem = _mem_prompt(eq, mem_k, mem_v, tm)
    y = _mix_out(x, o_nsa, o_hg, o_mem, gate, w_o_bf16, final_g, tm)
    rows = lambda a, b: jnp.stack([a.reshape(t, NSA_KVH, HD), b.reshape(t, NSA_KVH, HD)], axis=1)
    return y, rows(ck, cv), rows(sk, sv), rows(wk, wv), _diag_blocks(s_hg, HGRN_H)

def _heads_major(a, h):
    t = a.shape[0]
    return a.reshape(t, h, HD).transpose(1, 0, 2)


def _even_prompt(x, cos, sin, g, w_in_bf16, w_o_bf16, mem_k, mem_v, tm):
    t = x.shape[0]
    o_mq = 3 * RET_W
    o_eq = 3 * RET_W + 3 * MOBA_W
    o_gate = o_eq + MEM_W
    ret, moba, eq, gate = _norm_proj(x, g, w_in_bf16, cos, sin, ((0, 2 * RET_W), (o_mq, o_mq + 2 * MOBA_W)), tm,
                                     splits=(3 * RET_W, 3 * MOBA_W, MEM_W, MIX_W))
    mq, mk, mv = moba[:, :MOBA_W], moba[:, MOBA_W:2 * MOBA_W], moba[:, 2 * MOBA_W:]
    o_ret, s_ret = _retention_prompt(ret)
    kmean = _block_mean(moba, MOBA_BLOCK, MOBA_W, 1)
    zq = jnp.zeros((MOBA_H, t, HD), F32)
    q_pad = jnp.concatenate([_heads_major(mq, MOBA_H), zq], axis=-1)
    vh = _heads_major(mv, MOBA_H)
    nblk = t // MOBA_BLOCK
    onehot = (jnp.arange(t)[:, None] // MOBA_BLOCK == jnp.arange(HD)[None, :]).astype(F32)
    ka = jnp.concatenate([_heads_major(mk, MOBA_H), jnp.broadcast_to(onehot, (MOBA_H, t, HD))], axis=-1).astype(BF16)
    vo = jnp.concatenate([jnp.ones_like(vh), vh], axis=-1).astype(BF16)
    km = jnp.pad(_heads_major(kmean, MOBA_H), ((0, 0), (HD, LANES - HD - nblk), (0, HD)))
    o_moba = _moba_prompt(q_pad, ka, vo, km)
    o_moba = o_moba[:, :, HD:].transpose(1, 0, 2).reshape(t, MOBA_W)
    o_mem = _mem_prompt(eq, mem_k, mem_v, tm)
    y = _mix_out(x, o_ret, o_moba, o_mem, gate, w_o_bf16, None, tm)
    rows = moba[:, MOBA_W:].reshape(t, 2, MOBA_H, HD)
    return y, rows, _diag_blocks(s_ret, RET_H)


def _pages_t(pool, width):
    n_pool, n_layer = pool.shape[0], pool.shape[1]
    return pool.transpose(0, 1, 3, 4, 5, 2).reshape(n_pool, n_layer, 2, width, PAGE)


def _mem_t(cache):
    bsz, n = cache.shape[0], cache.shape[1]
    return cache.transpose(0, 2, 3, 4, 1).reshape(bsz, 2, MEM_W, n)


def _even_sample(x, cos, sin, g, w_in_bf16, w_o_bf16, mem_cache, state, page_table, pool, layer):
    bsz = x.shape[0]
    o_mq = 3 * RET_W
    o_eq = 3 * RET_W + 3 * MOBA_W
    o_gate = o_eq + MEM_W
    proj = _norm_proj(x, g, w_in_bf16, cos, sin, ((0, 2 * RET_W), (o_mq, o_mq + 2 * MOBA_W)), bsz)
    rq, rk, rv = proj[:, :RET_W], proj[:, RET_W:2 * RET_W], proj[:, 2 * RET_W:3 * RET_W]
    mq = proj[:, o_mq:o_mq + MOBA_W]
    mk = proj[:, o_mq + MOBA_W:o_mq + 2 * MOBA_W]
    mv = proj[:, o_mq + 2 * MOBA_W:o_eq]
    gamma = np.repeat(1.0 - np.power(2.0, -5.0 - np.arange(RET_H, dtype=np.float64)), HD)
    o_ret, s_ret = _state_step(state, rq, rk, jnp.broadcast_to(jnp.asarray(gamma, F32), (bsz, RET_W)), rv, "ret")
    o_moba = _paged_attn(page_table, _pages_t(pool, MOBA_W), layer, mq.reshape(bsz, MOBA_W, 1), mk, mv, None,
                         MOBA_H, 1, True).reshape(bsz, MOBA_W)
    o_mem = _mem_sample(proj[:, o_eq:o_gate], _mem_t(mem_cache))
    y = _mix_out(x, o_ret, o_moba, o_mem, proj[:, o_gate:], w_o_bf16, None, bsz)
    rows = proj[:, o_mq + MOBA_W:o_eq].reshape(bsz, 1, 2, MOBA_H, HD)
    return y, rows, s_ret


def _cols(a):
    bsz = a.shape[0]
    return a.reshape(bsz, NSA_KVH, NSA_REP, HD).transpose(0, 1, 3, 2).reshape(bsz, NSA_KVW, NSA_REP)


def _uncols(a):
    bsz = a.shape[0]
    return a.reshape(bsz, NSA_KVH, HD, NSA_REP).transpose(0, 1, 3, 2).reshape(bsz, NSA_QW)


def _odd_sample(x, cos, sin, g, w_aug_bf16, w_o_bf16, mem_cache, state, lb, page_table, cmp_pool, slc_pool, wbuf,
                layer, cmp_w_t, final_g):
    bsz = x.shape[0]
    lbuf = wbuf.shape[1]
    kvo = lambda j: ODD_KV + j * NSA_KVW
    proj = _norm_proj(x, g, w_aug_bf16, cos, sin, ((ODD_NQR, ODD_KV), (kvo(2), kvo(3)), (kvo(4), kvo(5))), bsz)
    ck, cv, sk, sv, wk, wv = (proj[:, kvo(j):kvo(j + 1)] for j in range(6))
    wd, ped, w2t = cmp_w_t
    n_pool, n_layer = cmp_pool.shape[0], cmp_pool.shape[1]
    cmp_t = cmp_pool.transpose(0, 1, 3, 4, 5, 2)
    o_c, sel = _nsa_cmp_sample(page_table, cmp_t, layer, proj[:, ODD_NQ:ODD_NQ + NSA_QW].reshape(bsz, NSA_H, HD),
                               wd, ped, w2t)
    row_mask = jnp.repeat(sel.transpose(0, 3, 1, 2), NSA_BLOCK, axis=-1)
    q_cols = _cols(proj[:, ODD_NQR:ODD_NQR + NSA_QW])
    o_s = _paged_attn(page_table, _pages_t(slc_pool, NSA_KVW), layer, q_cols, sk, sv, row_mask, NSA_KVH, NSA_REP, False)
    wbuf_t = wbuf.transpose(0, 2, 3, 4, 1).reshape(bsz, 2, NSA_KVW, lbuf)
    o_w = _win_sample(q_cols, wbuf_t, wk, wv)
    ng = proj[:, ODD_NG:ODD_NG + 3 * NSA_H].reshape(bsz, NSA_H, 3)
    gate_logits = jnp.repeat(ng.transpose(2, 0, 1), HD, axis=-1)
    o_nsa = _gate_combine(gate_logits, o_c.reshape(bsz, NSA_QW), _uncols(o_s), _uncols(o_w))
    hq, hf, hi = (proj[:, ODD_HQ + j * HGRN_W:ODD_HQ + (j + 1) * HGRN_W] for j in range(3))
    o_hg, s_hg = _state_step(state, hq, hf, jnp.broadcast_to(lb.reshape(1, HGRN_W), (bsz, HGRN_W)), hi, "hgrn")
    o_mem = _mem_sample(proj[:, ODD_EQ:ODD_GATE], _mem_t(mem_cache))
    y = _mix_out(x, o_nsa, o_hg, o_mem, proj[:, ODD_GATE:ODD_NG], w_o_bf16, final_g, bsz)
    rows = lambda a, b: jnp.stack([a.reshape(bsz, 1, NSA_KVH, HD), b.reshape(bsz, 1, NSA_KVH, HD)], axis=2)
    win = jnp.concatenate([wbuf.astype(F32), rows(wk, wv)], axis=1)[:, -lbuf:]
    return y, rows(ck, cv), rows(sk, sv), win, s_hg


def kernel(x_prompt, x_sample, mem_prompt, cache_moba_kv, state_ret, cache_nsa_cmp_kv, cache_nsa_slc_kv,
           cache_nsa_win_kv, state_hgrn, cache_mem_kv, page_table, norm_g, mem_norm_g, w_mem_kv, w_in_even,
           w_in_odd, w_out, cmp_w1, cmp_w2, cmp_pe, hgrn_lb_logits, final_g):
    bp, tp, d = x_prompt.shape
    bs, ts, _ = x_sample.shape
    depth = w_out.shape[0]
    assert bp == 1 and ts == 1 and depth == 2
    n_mem = mem_prompt.shape[1]
    past_len = page_table.shape[1] * PAGE
    assert past_len % MOBA_BLOCK == 0 and cache_moba_kv.shape[2] == PAGE
    tm, tq, kc = 256, 128, 256
    xp, xs, mem = x_prompt[0], x_sample[:, 0], mem_prompt[0]
    cos_p, sin_p = _rope_tables(jnp.arange(tp, dtype=jnp.int32))
    cos_s, sin_s = (jnp.broadcast_to(a, (bs, LANES)) for a in _rope_tables(jnp.full((1,), past_len, jnp.int32)))
    lb_prob = jax.nn.softmax(hgrn_lb_logits.astype(F32), axis=0)
    lb_all = jnp.cumsum(lb_prob, axis=0) - lb_prob[0]
    lw_p = min(NSA_WINDOW, tp)

    def mem_kv(layer):
        kv = _norm_proj(mem, mem_norm_g[layer], w_mem_kv[layer].astype(BF16), cos_p[:n_mem], sin_p[:n_mem], (), n_mem)
        return kv.reshape(n_mem, 2, MEM_W)

    w_in0 = w_in_even[0].astype(BF16)
    w_o0 = w_out[0].astype(BF16)
    mkv0 = mem_kv(0)
    hp, moba_p, ret_p = _even_prompt(xp, cos_p, sin_p, norm_g[0], w_in0, w_o0, mkv0[:, 0], mkv0[:, 1], tm)
    hs, moba_s, ret_s = _even_sample(xs, cos_s, sin_s, norm_g[0], w_in0, w_o0, cache_mem_kv[0], state_ret[0],
                                     page_table, cache_moba_kv, 0)

    w_in1 = _odd_weights(w_in_odd[0]).astype(BF16)
    w_o1 = w_out[1].astype(BF16)
    mkv1 = mem_kv(1)
    cmp_w = _compress_weights(cmp_w1[0], cmp_w2[0], cmp_pe[0])
    cmp_w_t = _compress_weights_t(cmp_w1[0], cmp_w2[0], cmp_pe[0])
    yp, cmp_p, slc_p, win_p, hg_p = _odd_prompt(hp, cos_p, sin_p, norm_g[1], w_in1, w_o1, mkv1[:, 0], mkv1[:, 1],
                                                lb_all[1], cmp_w, final_g, tm, tq, kc)
    ys, cmp_s, slc_s, win_s, hg_s = _odd_sample(hs, cos_s, sin_s, norm_g[1], w_in1, w_o1, cache_mem_kv[1],
                                                state_hgrn[0], lb_all[1], page_table, cache_nsa_cmp_kv,
                                                cache_nsa_slc_kv, cache_nsa_win_kv[0], 0, cmp_w_t, final_g)

    return (yp[None], ys[:, None], moba_p[None, None], moba_s[:, None], ret_p[None, None], ret_s[None],
            cmp_p[None, None], cmp_s[:, None], slc_p[None, None], slc_s[:, None], win_p[None, tp - lw_p:][None],
            win_s[None], hg_p[None, None], hg_s[None], jnp.stack([mkv0, mkv1]).reshape(depth, 1, n_mem, 2, MEM_H, HD))
```

```python
import functools
import math

import jax
import jax.numpy as jnp
import numpy as np
from jax import lax
from jax.experimental import pallas as pl
from jax.experimental.pallas import tpu as pltpu

F32 = jnp.float32
BF16 = jnp.bfloat16
HIGHEST = lax.Precision.HIGHEST

HD = 64
RET_H, MOBA_H, MEM_H, NSA_H, NSA_KVH, HGRN_H = 6, 6, 4, 8, 2, 4
NSA_REP = NSA_H // NSA_KVH
RET_W, MOBA_W, MEM_W = RET_H * HD, MOBA_H * HD, MEM_H * HD
NSA_QW, NSA_KVW, HGRN_W = NSA_H * HD, NSA_KVH * HD, HGRN_H * HD
MIX_W = RET_W + MOBA_W + MEM_W
PAGE = 128
RET_CHUNK, HGRN_CHUNK = 128, 64
MOBA_BLOCK, MOBA_TOPK = 256, 3
NSA_BLOCK, NSA_TOPN, NSA_WINDOW = 64, 16, 512
ROPE_THETA = 10000.0
EPS = 1e-6
NEG = -1e30
FORCE = 1e30
TINY = 1e-30
SCALE = HD ** -0.5
LANES = 128
SHIFT_SLACK = 1.01
SHIFT_LIMIT = 20.0
VMEM_LIMIT = 56 * 1024 * 1024


def _params(*sem):
    return pltpu.CompilerParams(dimension_semantics=sem, vmem_limit_bytes=VMEM_LIMIT)


def _dot(a, b, precision=None):
    return jnp.dot(a, b, preferred_element_type=F32, precision=precision)


def _dot_nt(a, b, precision=None):
    return lax.dot_general(a, b, (((1,), (1,)), ((), ())), preferred_element_type=F32, precision=precision)


def _dot_tn(a, b, precision=None):
    return lax.dot_general(a, b, (((0,), (0,)), ((), ())), preferred_element_type=F32, precision=precision)


def _block_ones(width, value=1.0):
    r = lax.broadcasted_iota(jnp.int32, (width, width), 0) // HD
    c = lax.broadcasted_iota(jnp.int32, (width, width), 1) // HD
    return jnp.where(r == c, value, 0.0).astype(F32)


def _rope_tile(a, cos, sin, first_half):
    rot = jnp.where(first_half, pltpu.roll(a, LANES - HD // 2, 1), pltpu.roll(a, HD // 2, 1))
    return a * cos + rot * sin


def _norm_proj_body(x_ref, g_ref, w_ref, cos_ref, sin_ref, *o_refs, rope_tiles, n_chunk, starts):
    x = x_ref[...]
    ms = jnp.mean(x * x, axis=-1, keepdims=True)
    y = (x * lax.rsqrt(ms + EPS) * g_ref[...]).astype(BF16)
    n = w_ref.shape[1]
    lane = lax.broadcasted_iota(jnp.int32, (x.shape[0], LANES), 1)
    first_half = (lane % HD) < HD // 2
    for c0 in range(0, n, n_chunk):
        acc = _dot(y, w_ref[:, c0:c0 + n_chunk])
        for j in range(n_chunk // LANES):
            tile = (c0 // LANES) + j
            which = max(i for i, s in enumerate(starts) if s <= tile * LANES)
            local = tile * LANES - starts[which]
            if local >= o_refs[which].shape[1]:
                continue
            a = acc[:, j * LANES:(j + 1) * LANES]
            if tile in rope_tiles:
                a = _rope_tile(a, cos_ref[...], sin_ref[...], first_half)
            o_refs[which][:, local:local + LANES] = a


def _norm_proj(x, g, w_bf16, cos, sin, rope_cols, tm, splits=None):
    m, d = x.shape
    n = w_bf16.shape[1]
    splits = (n,) if splits is None else tuple(splits)
    assert m % tm == 0 and n % LANES == 0 and all(s % LANES == 0 for s in splits) and sum(splits) <= n
    starts = tuple(int(s) for s in np.cumsum((0,) + splits[:-1]))
    tiles = n // LANES
    k = next(c for c in (4, 3, 2, 1) if tiles % c == 0)
    rope_tiles = frozenset(t for a, b in rope_cols for t in range(a // LANES, b // LANES))
    body = functools.partial(_norm_proj_body, rope_tiles=rope_tiles, n_chunk=k * LANES, starts=starts)
    outs = pl.pallas_call(
        body,
        grid=(m // tm,),
        in_specs=[
            pl.BlockSpec((tm, d), lambda i: (i, 0)),
            pl.BlockSpec((1, d), lambda i: (0, 0)),
            pl.BlockSpec((d, n), lambda i: (0, 0)),
            pl.BlockSpec((tm, LANES), lambda i: (i, 0)),
            pl.BlockSpec((tm, LANES), lambda i: (i, 0)),
        ],
        out_specs=[pl.BlockSpec((tm, s), lambda i: (i, 0)) for s in splits],
        out_shape=[jax.ShapeDtypeStruct((m, s), F32) for s in splits],
        compiler_params=_params("parallel"),
        name="norm_proj",
    )(x, g.reshape(1, d), w_bf16, cos, sin)
    return outs if len(splits) > 1 else outs[0]


def _rope_tables(pos):
    half = HD // 2
    inv = ROPE_THETA ** (-jnp.arange(half, dtype=F32) / half)
    ang = pos.astype(F32)[:, None] * inv[None, :]
    cos, sin = jnp.cos(ang), jnp.sin(ang)
    return jnp.concatenate([cos, cos, cos, cos], -1), jnp.concatenate([-sin, sin, -sin, sin], -1)


def _retention_consts(c):
    lg = np.log(1.0 - np.power(2.0, -5.0 - np.arange(RET_H, dtype=np.float64)))
    ti = np.arange(c, dtype=np.float64)
    causal = ti[:, None] >= ti[None, :]
    d_in = np.where(causal[None], np.exp(np.where(causal, ti[:, None] - ti[None, :], 0.0)[None] * lg[:, None, None]), 0.0)
    q_dec = np.repeat(np.exp((ti[:, None] + 1.0) * lg[None, :]), HD, axis=1)
    k_dec = np.repeat(np.exp((c - 1.0 - ti)[:, None] * lg[None, :]), HD, axis=1)
    c_dec = np.repeat(np.exp(c * lg), HD)
    head = np.arange(RET_W) // HD
    bd = (head[:, None] == head[None, :]).astype(np.float64)
    cmat = bd * c_dec[:, None]
    f = lambda a: jnp.asarray(a, dtype=F32)
    return f(d_in), f(q_dec), f(k_dec), f(cmat), f(bd)


def _retention_body(q_ref, k_ref, v_ref, din_ref, qdec_ref, kdec_ref, cmat_ref, bd_ref, o_ref, s_ref):
    @pl.when(pl.program_id(0) == 0)
    def _():
        s_ref[...] = jnp.zeros_like(s_ref)

    q = q_ref[...]
    k = k_ref[...] * SCALE
    v = v_ref[...]
    c, w = q.shape
    head = lax.broadcasted_iota(jnp.int32, (c, w), 1) // HD
    s = s_ref[...]
    qb, kb, vb = q.astype(BF16), k.astype(BF16), v.astype(BF16)
    o = _dot(qb, s.astype(BF16)) * qdec_ref[...]
    for h in range(RET_H):
        mh = head == h
        att = _dot_nt(jnp.where(mh, qb, jnp.zeros_like(qb)), kb) * din_ref[h]
        o = o + jnp.where(mh, _dot(att.astype(BF16), vb), 0.0)
    s_ref[...] = s * cmat_ref[...] + _dot_tn((k * kdec_ref[...]).astype(BF16), vb) * bd_ref[...]
    seg = bd_ref[...] * (1.0 / HD)
    xc = o - _dot(o, seg, HIGHEST)
    var = _dot(xc * xc, seg, HIGHEST)
    o_ref[...] = xc * lax.rsqrt(var + EPS)


def _retention_prompt(qkv):
    t, w = qkv.shape[0], RET_W
    c = math.gcd(t, RET_CHUNK)
    d_in, q_dec, k_dec, cmat, bd = _retention_consts(c)
    col = lambda j: pl.BlockSpec((c, w), lambda i: (i, j))
    const2 = lambda shape: pl.BlockSpec(shape, lambda i: (0,) * len(shape))
    return pl.pallas_call(
        _retention_body,
        grid=(t // c,),
        in_specs=[col(0), col(1), col(2), const2((RET_H, c, c)), const2((c, w)), const2((c, w)), const2((w, w)),
                  const2((w, w))],
        out_specs=[col(0), const2((w, w))],
        out_shape=[jax.ShapeDtypeStruct((t, w), F32), jax.ShapeDtypeStruct((w, w), F32)],
        compiler_params=_params("arbitrary"),
        name="retention_prompt",
    )(qkv, qkv, qkv, d_in, q_dec, k_dec, cmat, bd)


def _diag_blocks(s, h):
    s4 = s.reshape(h, HD, h, HD)
    return jnp.stack([s4[i, :, i, :] for i in range(h)], axis=0)


def _block_stats_body(k_ref, mean_ref, norm_ref):
    k = k_ref[...]
    mean_ref[0] = jnp.mean(k, axis=0, keepdims=True)
    n2 = _dot(k * k, _block_ones(k.shape[1]), HIGHEST)
    norm_ref[0] = jnp.max(n2, axis=0, keepdims=True)


def _block_stats(k, blk, w, col):
    t = k.shape[0]
    assert t % blk == 0
    mean, norm = pl.pallas_call(
        _block_stats_body,
        grid=(t // blk,),
        in_specs=[pl.BlockSpec((blk, w), lambda i: (i, col))],
        out_specs=[pl.BlockSpec((1, 1, w), lambda i: (i, 0, 0))] * 2,
        out_shape=[jax.ShapeDtypeStruct((t // blk, 1, w), F32)] * 2,
        compiler_params=_params("parallel"),
        name="block_stats",
    )(k)
    return mean.reshape(t // blk, w), norm.reshape(t // blk, w)


def _key_norm_bound(norm2, heads):
    kmax = jnp.sqrt(jnp.max(norm2, axis=0)).reshape(heads, HD)[:, :1]
    return jnp.broadcast_to(kmax[:, :, None], (heads, 1, LANES))


def _topk_mask(score, k):
    n = score.shape[-1]
    idx = lax.broadcasted_iota(jnp.int32, score.shape, score.ndim - 1)
    sel = jnp.zeros(score.shape, F32)
    work = score
    for _ in range(k):
        m = jnp.max(work, axis=-1, keepdims=True)
        first = jnp.min(jnp.where(work == m, idx, n), axis=-1, keepdims=True)
        pick = idx == first
        sel = jnp.where(pick, 1.0, sel)
        work = jnp.where(pick, -jnp.inf, work)
    return sel


def _tile_loop(lo, hi, fn, unroll=8):
    def group(g, _):
        for u in range(unroll):
            fn(lo + unroll * g + u)
        return 0
    n = jnp.maximum(hi - lo, 0)
    main = n // unroll
    lax.fori_loop(0, main, group, 0)
    done = lo + main * unroll
    rem = n - main * unroll
    size = unroll // 2
    while size >= 1:
        @pl.when((rem & size) != 0)
        def _(done=done, size=size):
            for u in range(size):
                fn(done + u)
        done = done + (rem & size)
        size //= 2


def _lane_fold_max(s):
    out = s[:, 0:LANES]
    for c in range(1, s.shape[1] // LANES):
        out = jnp.maximum(out, s[:, c * LANES:(c + 1) * LANES])
    return out


def _moba_prompt_body(q_ref, ka_ref, vo_ref, kmean_ref, kmax_ref, o_ref, s_ref, m_ref, acc_ref, *, topk, bq):
    qi = pl.program_id(1)
    q = q_ref[0]
    rows = q.shape[0]
    kb = MOBA_BLOCK
    gate = _dot_nt(q, kmean_ref[0], HIGHEST)
    blk = lax.broadcasted_iota(jnp.int32, (rows, LANES), 1) - HD
    own = qi * bq + lax.broadcasted_iota(jnp.int32, (rows, LANES), 0) // kb
    past = (blk >= 0) & (blk < own)
    sel = jnp.where(past, _topk_mask(jnp.where(past, gate, NEG), topk), 0.0)
    bias = jnp.where((blk < 0) | (blk == own) | (sel > 0.5), 0.0, NEG)
    qa = (q * SCALE + bias).astype(BF16)
    n_past = qi * bq
    qpos = lax.broadcasted_iota(jnp.int32, (rows, kb), 0)
    kpos = lax.broadcasted_iota(jnp.int32, (rows, kb), 1)
    acc_ref[...] = jnp.zeros(acc_ref.shape, F32)

    def scores(j):
        start = pl.multiple_of(j * kb, kb)
        return _dot_nt(qa, ka_ref[0, pl.ds(start, kb), :])

    def own_scores(d):
        return jnp.where(kpos + d * kb <= qpos, scores(n_past + d), NEG)

    def add_values(j, p):
        start = pl.multiple_of(j * kb, kb)
        acc_ref[...] += _dot(p.astype(BF16), vo_ref[0, pl.ds(start, kb), :])

    bound = jnp.sqrt(jnp.sum(q * q, axis=-1, keepdims=True)) * kmax_ref[0][:, 0:1] * (SCALE * SHIFT_SLACK)
    small = jnp.max(bound) <= SHIFT_LIMIT

    @pl.when(small)
    def _():
        _tile_loop(0, n_past, lambda j: add_values(j, jnp.exp(scores(j) - bound)))
        for d in range(bq):
            add_values(n_past + d, jnp.exp(own_scores(d) - bound))

    @pl.when(jnp.logical_not(small))
    def _():
        m_ref[...] = jnp.full(m_ref.shape, NEG, F32)

        def score(j, s):
            s_ref[j] = s
            m_ref[...] = jnp.maximum(m_ref[...], _lane_fold_max(s))

        _tile_loop(0, n_past, lambda j: score(j, scores(j)))
        for d in range(bq):
            score(n_past + d, own_scores(d))
        m = jnp.max(m_ref[...], axis=-1, keepdims=True)
        _tile_loop(0, n_past + bq, lambda j: add_values(j, jnp.exp(s_ref[j] - m)))

    acc = acc_ref[...]
    o_ref[0] = acc / jnp.maximum(acc[:, 0:1], TINY)


def _moba_prompt(q_pad, ka_bf16, vo_bf16, kmean_rows, kmax):
    h, t, _ = q_pad.shape
    nblk = t // MOBA_BLOCK
    assert t % MOBA_BLOCK == 0 and nblk <= LANES - HD
    bq = 2 if nblk % 2 == 0 else 1
    rows = bq * MOBA_BLOCK
    return pl.pallas_call(
        functools.partial(_moba_prompt_body, topk=min(MOBA_TOPK, nblk), bq=bq),
        grid=(h, nblk // bq),
        in_specs=[
            pl.BlockSpec((1, rows, LANES), lambda a, i: (a, i, 0)),
            pl.BlockSpec((1, t, LANES), lambda a, i: (a, 0, 0), pipeline_mode=pl.Buffered(1)),
            pl.BlockSpec((1, t, LANES), lambda a, i: (a, 0, 0), pipeline_mode=pl.Buffered(1)),
            pl.BlockSpec((1, LANES, LANES), lambda a, i: (a, 0, 0)),
            pl.BlockSpec((1, 1, LANES), lambda a, i: (a, 0, 0)),
        ],
        out_specs=pl.BlockSpec((1, rows, LANES), lambda a, i: (a, i, 0)),
        out_shape=jax.ShapeDtypeStruct((h, t, LANES), F32),
        scratch_shapes=[pltpu.VMEM((nblk, rows, MOBA_BLOCK), F32), pltpu.VMEM((rows, LANES), F32),
                        pltpu.VMEM((rows, LANES), F32)],
        compiler_params=_params("parallel", "arbitrary"),
        name="moba_prompt",
    )(q_pad, ka_bf16, vo_bf16, kmean_rows, kmax)


def _mem_prompt_body(q_ref, k_ref, v_ref, o_ref):
    q = q_ref[...]
    kb = k_ref[...].astype(BF16)
    vb = v_ref[...].astype(BF16)
    head = lax.broadcasted_iota(jnp.int32, q.shape, 1) // HD
    o = jnp.zeros(q.shape, F32)
    for h in range(MEM_H):
        mh = head == h
        s = _dot_nt(jnp.where(mh, q, 0.0).astype(BF16), kb) * SCALE
        e = jnp.exp(s - jnp.max(s, axis=-1, keepdims=True))
        p = e / jnp.sum(e, axis=-1, keepdims=True)
        o = o + jnp.where(mh, _dot(p.astype(BF16), vb), 0.0)
    o_ref[...] = o


def _mem_prompt(q, k, v, tm):
    t, w = q.shape
    n = k.shape[0]
    return pl.pallas_call(
        _mem_prompt_body,
        grid=(t // tm,),
        in_specs=[pl.BlockSpec((tm, w), lambda i: (i, 0)), pl.BlockSpec((n, w), lambda i: (0, 0)),
                  pl.BlockSpec((n, w), lambda i: (0, 0))],
        out_specs=pl.BlockSpec((tm, w), lambda i: (i, 0)),
        out_shape=jax.ShapeDtypeStruct((t, w), F32),
        compiler_params=_params("parallel"),
        name="mem_prompt",
    )(q, k, v)


def _mix_out_body(x_ref, a_ref, b_ref, c_ref, gate_ref, wo_ref, fg_ref, o_ref, *, final_norm):
    gate = gate_ref[...]
    gate = gate * jax.nn.sigmoid(gate)
    y = x_ref[...]
    off = 0
    for ref in (a_ref, b_ref, c_ref):
        w = ref.shape[1]
        mix = (ref[...] * gate[:, off:off + w]).astype(BF16)
        y = y + _dot(mix, wo_ref[off:off + w, :])
        off += w
    if final_norm:
        ms = jnp.mean(y * y, axis=-1, keepdims=True)
        y = y * lax.rsqrt(ms + EPS) * fg_ref[...]
    o_ref[...] = y


def _mix_out(x, a, b, c, gate, wo_bf16, final_g, tm):
    m, d = x.shape
    row = lambda w: pl.BlockSpec((tm, w), lambda i: (i, 0))
    fg = jnp.ones((1, d), F32) if final_g is None else final_g.reshape(1, d).astype(F32)
    body = functools.partial(_mix_out_body, final_norm=final_g is not None)
    return pl.pallas_call(
        body,
        grid=(m // tm,),
        in_specs=[row(d), row(a.shape[1]), row(b.shape[1]), row(c.shape[1]), row(gate.shape[1]),
                  pl.BlockSpec(wo_bf16.shape, lambda i: (0, 0)), pl.BlockSpec((1, d), lambda i: (0, 0))],
        out_specs=row(d),
        out_shape=jax.ShapeDtypeStruct((m, d), F32),
        compiler_params=_params("parallel"),
        name="mix_out",
    )(x, a, b, c, gate, wo_bf16, fg)


def _compress_body(x_ref, pe_ref, w1_ref, w2_ref, o_ref):
    x = (x_ref[0] + pe_ref[0]).astype(BF16)
    hid = jax.nn.gelu(_dot(x, w1_ref[0]))
    o_ref[0] = _dot(hid.astype(BF16), w2_ref[0])


def _compress_weights(w1, w2, pe):
    eye = jnp.eye(NSA_KVH, dtype=F32)
    w1r = w1.astype(F32).reshape(2, NSA_BLOCK, HD, HD)
    w1b = jnp.einsum("klde,gh->klgdhe", w1r, eye).reshape(2, NSA_BLOCK * NSA_KVW, NSA_KVW)
    w2b = jnp.einsum("kde,gh->kgdhe", w2.astype(F32), eye).reshape(2, NSA_KVW, NSA_KVW)
    peb = jnp.broadcast_to(pe.astype(F32)[:, :, None, :], (2, NSA_BLOCK, NSA_KVH, HD)).reshape(2, 1, NSA_BLOCK * NSA_KVW)
    return w1b.astype(BF16), w2b.astype(BF16), peb


def _compress(x, w1b, w2b, peb):
    _, nb, kdim = x.shape
    return pl.pallas_call(
        _compress_body,
        grid=(2,),
        in_specs=[pl.BlockSpec((1, nb, kdim), lambda i: (i, 0, 0)), pl.BlockSpec((1, 1, kdim), lambda i: (i, 0, 0)),
                  pl.BlockSpec((1, kdim, NSA_KVW), lambda i: (i, 0, 0)),
                  pl.BlockSpec((1, NSA_KVW, NSA_KVW), lambda i: (i, 0, 0))],
        out_specs=pl.BlockSpec((1, nb, NSA_KVW), lambda i: (i, 0, 0)),
        out_shape=jax.ShapeDtypeStruct((2, nb, NSA_KVW), F32),
        compiler_params=_params("parallel"),
        name="nsa_compress",
    )(x, peb, w1b, w2b)


def _nsa_prompt_body(q_ref, qr_ref, gt_ref, ckv_ref, ska_ref, svo_ref, wkv_ref, kmax_ref, o_ref, s_ref, m_ref, acc_ref,
                     *, tq, kc):
    i = pl.program_id(1)
    t0 = i * tq
    rows = q_ref.shape[2]
    nb = ckv_ref.shape[1]
    q = q_ref[0, 0]
    qrb = (qr_ref[0, 0] * SCALE).astype(BF16)
    tpos = t0 + lax.broadcasted_iota(jnp.int32, (rows, 1), 0) % tq

    ckv = ckv_ref[0].astype(BF16)
    blk = lax.broadcasted_iota(jnp.int32, (nb, rows), 0)
    tpos_l = t0 + lax.broadcasted_iota(jnp.int32, (nb, rows), 1) % tq
    complete = blk * NSA_BLOCK + (NSA_BLOCK - 1) <= tpos_l
    s_c = jnp.where(complete, _dot_nt(ckv, q.astype(BF16)) * SCALE, NEG)
    e = jnp.where(complete, jnp.exp(s_c - jnp.max(s_c, axis=0, keepdims=True)), 0.0)
    pc = e / jnp.maximum(jnp.sum(e, axis=0, keepdims=True), TINY)
    o_c = _dot_tn(pc.astype(BF16), ckv)

    imp = pc[:, 0:tq]
    for r in range(1, NSA_REP):
        imp = imp + pc[:, r * tq:(r + 1) * tq]
    blk_q = lax.broadcasted_iota(jnp.int32, (nb, tq), 0)
    tpos_q = t0 + lax.broadcasted_iota(jnp.int32, (nb, tq), 1)
    cur = tpos_q // NSA_BLOCK
    forced = (blk_q == 0) | (blk_q == cur) | (blk_q == cur - 1)
    complete_q = blk_q * NSA_BLOCK + (NSA_BLOCK - 1) <= tpos_q
    score = jnp.where(forced, FORCE, jnp.where(complete_q, imp, NEG))
    sel = jnp.where(score > 0.5 * NEG, _topk_mask_axis0(score, min(NSA_TOPN, nb)), 0.0)

    bias = jnp.where(sel > 0.5, 0.0, NEG).T
    nbp = -(-nb // LANES) * LANES
    if nbp > nb:
        bias = jnp.concatenate([bias, jnp.full((tq, nbp - nb), NEG, F32)], axis=1)
    bias = jnp.concatenate([bias] * NSA_REP, axis=0)
    qrs = qr_ref[0, 0] * SCALE
    upper = lax.broadcasted_iota(jnp.int32, (rows, LANES), 1) >= HD
    n_span = -(-nb // HD)
    qa = []
    for sp in range(n_span):
        col = bias[:, (sp // 2) * LANES:(sp // 2 + 1) * LANES]
        if sp % 2 == 0:
            col = pltpu.roll(col, HD, 1)
        qa.append((qrs + jnp.where(upper, col, 0.0)).astype(BF16))
    cps = HD * NSA_BLOCK // kc
    n_chunks = (t0 + tq + kc - 1) // kc
    last = n_chunks - 1
    qa_last = qa[0]
    for sp in range(1, n_span):
        qa_last = jnp.where(last // cps == sp, qa[sp], qa_last)
    last_start = pl.multiple_of(last * kc, kc)
    kpos = last_start + lax.broadcasted_iota(jnp.int32, (rows, kc), 1)
    s_last = jnp.where(kpos <= tpos, _dot_nt(qa_last, ska_ref[0, pl.ds(last_start, kc), :]), NEG)

    def spans(fn):
        for sp in range(n_span):
            lo = sp * cps
            _tile_loop(jnp.minimum(lo, last), jnp.minimum(lo + cps, last), functools.partial(fn, qa[sp]))

    def scores(qsp, c):
        start = pl.multiple_of(c * kc, kc)
        return _dot_nt(qsp, ska_ref[0, pl.ds(start, kc), :])

    def add_values(c, p):
        start = pl.multiple_of(c * kc, kc)
        acc_ref[...] += _dot(p.astype(BF16), svo_ref[0, pl.ds(start, kc), :])

    acc_ref[...] = jnp.zeros(acc_ref.shape, F32)
    qr = qr_ref[0, 0]
    bound = jnp.sqrt(jnp.sum(qr * qr, axis=-1, keepdims=True)) * kmax_ref[0][:, 0:1] * (SCALE * SHIFT_SLACK)
    small = jnp.max(bound) <= SHIFT_LIMIT

    @pl.when(small)
    def _():
        add_values(last, jnp.exp(s_last - bound))
        spans(lambda qsp, c: add_values(c, jnp.exp(scores(qsp, c) - bound)))

    @pl.when(jnp.logical_not(small))
    def _():
        m_ref[...] = _lane_fold_max(s_last)

        def fold_max(qsp, c):
            s = scores(qsp, c)
            s_ref[c] = s
            m_ref[...] = jnp.maximum(m_ref[...], _lane_fold_max(s))

        spans(fold_max)
        m_s = jnp.max(m_ref[...], axis=-1, keepdims=True)
        add_values(last, jnp.exp(s_last - m_s))
        spans(lambda qsp, c: add_values(c, jnp.exp(s_ref[c] - m_s)))

    acc_s = acc_ref[...]
    o_s = acc_s / jnp.maximum(acc_s[:, 0:1], TINY)

    nwin = (NSA_WINDOW + tq - 1) // tq + 1
    win = []
    for d in range(nwin):
        c = i - (nwin - 1) + d
        start = pl.multiple_of(jnp.maximum(c, 0) * tq, tq)
        kv = wkv_ref[0, pl.ds(start, tq), :]
        kpos = c * tq + lax.broadcasted_iota(jnp.int32, (rows, tq), 1)
        mask = (kpos <= tpos) & (kpos > tpos - NSA_WINDOW) & (kpos >= 0)
        win.append((jnp.where(mask, _dot_nt(qrb, kv), NEG), kv))
    m_w = jnp.max(functools.reduce(jnp.maximum, [s for s, _ in win]), axis=-1, keepdims=True)
    p_w = [jnp.exp(s - m_w) for s, _ in win]
    l_w = jnp.sum(functools.reduce(jnp.add, p_w), axis=-1, keepdims=True)
    acc_w = functools.reduce(jnp.add, [_dot(p.astype(BF16), kv) for p, (_, kv) in zip(p_w, win)])
    o_w = acc_w / jnp.maximum(l_w, TINY)

    gt = jax.nn.sigmoid(gt_ref[0, 0])
    o_ref[0, 0] = gt[:, 0:1] * o_c + gt[:, 1:2] * o_s + gt[:, 2:3] * o_w


def _nsa_prompt(q_st, qr_st, gt_st, ckv, ska, svo, wkv, kmax, tq, kc):
    g, nt, rows, _ = q_st.shape
    t = ska.shape[1]
    nb = ckv.shape[1]
    assert t % kc == 0 and kc % tq == 0 and (HD * NSA_BLOCK) % kc == 0 and t % tq == 0
    qspec = pl.BlockSpec((1, 1, rows, LANES), lambda a, i: (a, i, 0, 0))
    seq = pl.BlockSpec((1, t, LANES), lambda a, i: (a, 0, 0), pipeline_mode=pl.Buffered(1))
    body = functools.partial(_nsa_prompt_body, tq=tq, kc=kc)
    return pl.pallas_call(
        body,
        grid=(g, nt),
        in_specs=[qspec, qspec, pl.BlockSpec((1, 1, rows, 3), lambda a, i: (a, i, 0, 0)),
                  pl.BlockSpec((1, nb, LANES), lambda a, i: (a, 0, 0)), seq, seq, seq,
                  pl.BlockSpec((1, 1, LANES), lambda a, i: (a, 0, 0))],
        out_specs=qspec,
        out_shape=jax.ShapeDtypeStruct((g, nt, rows, LANES), F32),
        scratch_shapes=[pltpu.VMEM((t // kc, rows, kc), F32), pltpu.VMEM((rows, LANES), F32),
                        pltpu.VMEM((rows, LANES), F32)],
        compiler_params=_params("parallel", "arbitrary"),
        name="nsa_prompt",
    )(q_st, qr_st, gt_st, ckv, ska, svo, wkv, kmax)


def _hgrn_body(q_ref, f_ref, i_ref, lb_ref, o_ref, s_ref):
    @pl.when(pl.program_id(0) == 0)
    def _():
        s_ref[...] = jnp.zeros_like(s_ref)

    q = q_ref[...]
    v = i_ref[...]
    c, w = q.shape
    lb = lb_ref[...]
    f = lb + (1.0 - lb) * jax.nn.sigmoid(f_ref[...])
    kk = 1.0 - f
    tr = lax.broadcasted_iota(jnp.int32, (c, c), 0)
    tc = lax.broadcasted_iota(jnp.int32, (c, c), 1)
    cum = _dot(jnp.where(tr >= tc, 1.0, 0.0), jnp.log(f), HIGHEST)
    bd = _block_ones(w)
    bd_b = bd.astype(BF16)
    s = s_ref[...]
    o = _dot(q * jnp.exp(cum), s, HIGHEST)
    t_idx = lax.broadcasted_iota(jnp.int32, (c, w), 0)

    sub = 8
    groups = c // sub
    o_blk = [o[i * sub:(i + 1) * sub] for i in range(groups)]
    for g in range(groups):
        r0 = g * sub
        n = c - r0
        t_g = r0 + lax.broadcasted_iota(jnp.int32, (n, w), 0)
        es = [jnp.where(t_g >= j, jnp.exp(cum[r0:] - cum[j:j + 1]) * q[r0:] * kk[j:j + 1], 0.0)
              for j in range(r0, r0 + sub)]
        e = jnp.concatenate(es, axis=0)
        e_hi = e.astype(BF16)
        e_lo = (e - e_hi.astype(F32)).astype(BF16)
        a = _dot(e_hi, bd_b) + _dot(e_lo, bd_b)
        contrib = a[0:n] * v[r0:r0 + 1]
        for u in range(1, sub):
            contrib = contrib + a[u * n:(u + 1) * n] * v[r0 + u:r0 + u + 1]
        for i in range(g, groups):
            o_blk[i] = o_blk[i] + contrib[(i - g) * sub:(i - g + 1) * sub]
    o = jnp.concatenate(o_blk, axis=0)
    last = cum[c - 1:c, :]
    row0 = t_idx == 0
    scale_mat = _dot_tn(jnp.where(row0, jnp.exp(last), 0.0), jnp.where(row0, 1.0, 0.0), HIGHEST)
    s_ref[...] = s * scale_mat + _dot_tn(kk * jnp.exp(last - cum), v, HIGHEST) * bd
    ms = _dot(o * o, bd * (1.0 / HD), HIGHEST)
    o_ref[...] = o * lax.rsqrt(ms + EPS)


def _hgrn_prompt(qfi, lb):
    t, w = qfi.shape[0], HGRN_W
    c = math.gcd(t, HGRN_CHUNK)
    col = lambda j: pl.BlockSpec((c, w), lambda n: (n, j))
    return pl.pallas_call(
        _hgrn_body,
        grid=(t // c,),
        in_specs=[col(0), col(1), col(2), pl.BlockSpec((1, w), lambda n: (0, 0))],
        out_specs=[col(0), pl.BlockSpec((w, w), lambda n: (0, 0))],
        out_shape=[jax.ShapeDtypeStruct((t, w), F32), jax.ShapeDtypeStruct((w, w), F32)],
        compiler_params=_params("arbitrary"),
        name="hgrn_prompt",
    )(qfi, qfi, qfi, lb)


def _state_step_body(s_ref, q_ref, a_ref, b_ref, v_ref, o_ref, so_ref, *, mode):
    s = s_ref[...]
    if mode == "ret":
        k = a_ref[...] * SCALE
        dec = b_ref[...]
    else:
        lb = b_ref[...]
        dec = lb + (1.0 - lb) * jax.nn.sigmoid(a_ref[...])
        k = 1.0 - dec
    kv = k * v_ref[...]
    rows = s.shape[0]
    o = (q_ref[...] * (kv + dec * s)).reshape(rows // HD, HD, HD).sum(axis=1)
    so_ref[...] = dec * s + kv
    if mode == "ret":
        xc = o - jnp.mean(o, axis=-1, keepdims=True)
        o_ref[...] = xc * lax.rsqrt(jnp.mean(xc * xc, axis=-1, keepdims=True) + EPS)
    else:
        o_ref[...] = o * lax.rsqrt(jnp.mean(o * o, axis=-1, keepdims=True) + EPS)


def _state_step(state, q, a, b, v, mode):
    bsz, h = state.shape[0], state.shape[1]
    rows = bsz * h * HD
    col = lambda x: jnp.broadcast_to(x.reshape(bsz, h, HD, 1), (bsz, h, HD, HD)).reshape(rows, HD)
    vx = jnp.broadcast_to(v.reshape(bsz, h, 1, HD), (bsz, h, HD, HD)).reshape(rows, HD)
    bb = 8 if bsz % 8 == 0 else bsz
    br = bb * h * HD
    spec = pl.BlockSpec((br, HD), lambda i: (i, 0))
    ospec = pl.BlockSpec((br // HD, HD), lambda i: (i, 0))
    o, s_new = pl.pallas_call(
        functools.partial(_state_step_body, mode=mode),
        grid=(rows // br,),
        in_specs=[spec] * 5,
        out_specs=[ospec, spec],
        out_shape=[jax.ShapeDtypeStruct((rows // HD, HD), F32), jax.ShapeDtypeStruct((rows, HD), F32)],
        compiler_params=_params("parallel"),
        name="state_step_" + mode,
    )(state.reshape(rows, HD).astype(F32), col(q), col(a), col(b), vx)
    return o.reshape(bsz, h * HD), s_new.reshape(bsz, h, HD, HD)


def _expand_heads(p, width):
    g, n = p.shape
    return jnp.broadcast_to(p[:, None, :], (g, HD, n)).reshape(width, n)


def _head_sums(x, g):
    return x.reshape(g, HD, x.shape[-1]).sum(axis=1)


def _mem_sample_body(q_ref, kt_ref, vt_ref, o_ref):
    kt = kt_ref[0, 0]
    vt = vt_ref[0, 0]
    s = _head_sums(kt * q_ref[0], MEM_H) * SCALE
    e = jnp.exp(s - jnp.max(s, axis=-1, keepdims=True))
    p = e / jnp.sum(e, axis=-1, keepdims=True)
    o_ref[0] = jnp.sum(vt * _expand_heads(p, MEM_W), axis=-1, keepdims=True)


def _mem_sample(q, kvt):
    bsz, w = q.shape
    n = kvt.shape[-1]
    out = pl.pallas_call(
        _mem_sample_body,
        grid=(bsz,),
        in_specs=[pl.BlockSpec((1, w, 1), lambda i: (i, 0, 0)),
                  pl.BlockSpec((1, 1, w, n), lambda i: (i, 0, 0, 0)),
                  pl.BlockSpec((1, 1, w, n), lambda i: (i, 1, 0, 0))],
        out_specs=pl.BlockSpec((1, w, 1), lambda i: (i, 0, 0)),
        out_shape=jax.ShapeDtypeStruct((bsz, w, 1), F32),
        compiler_params=_params("parallel"),
        name="mem_sample",
    )(q.reshape(bsz, w, 1), kvt, kvt)
    return out.reshape(bsz, w)


def _topk_mask_axis0(score, k):
    n = score.shape[0]
    idx = lax.broadcasted_iota(jnp.int32, score.shape, 0)
    sel = jnp.zeros(score.shape, F32)
    work = score
    for _ in range(k):
        m = jnp.max(work, axis=0, keepdims=True)
        first = jnp.min(jnp.where(work == m, idx, n), axis=0, keepdims=True)
        pick = idx == first
        sel = jnp.where(pick, 1.0, sel)
        work = jnp.where(pick, -jnp.inf, work)
    return sel


def _paged_attn_body(pt_ref, pool_ref, qc_ref, qbd_ref, kn_ref, vn_ref, mask_ref, o_ref, buf, sem, s_ref, acc_ref, st_ref,
                     *, layer, groups, reps, chunk, moba):
    b = pl.program_id(0)
    n_pages = pt_ref.shape[1]
    n = n_pages // chunk
    gw = groups * HD

    def copy(kv, c, i, slot):
        return pltpu.make_async_copy(pool_ref.at[pt_ref[b, c * chunk + i], layer, kv], buf.at[slot, i], sem.at[slot])

    def start(kv, c, slot):
        lax.fori_loop(0, chunk, lambda i, _: (copy(kv, c, i, slot).start(), 0)[1], 0)

    def wait(kv, c, slot):
        lax.fori_loop(0, chunk, lambda i, _: (copy(kv, c, i, slot).wait(), 0)[1], 0)

    acc_ref[...] = jnp.zeros_like(acc_ref)
    start(0, 0, 0)

    def k_compute(c, slot):
        for i in range(chunk):
            sc = _dot(qbd_ref[0], buf[slot, i].astype(BF16))
            for r in range(reps):
                s_ref[r, c * chunk + i] = sc[r * groups:(r + 1) * groups]

    def select():
        if moba:
            per = MOBA_BLOCK // PAGE
            nblk = n_pages // per
            sc = s_ref[0].reshape(nblk, per, groups, PAGE)
            gate = jnp.sum(jnp.sum(sc, axis=1, keepdims=True), axis=-1, keepdims=True) * (1.0 / MOBA_BLOCK)
            sel = _topk_mask_axis0(gate, min(MOBA_TOPK, nblk + 1))
            mask = jnp.broadcast_to(sel, (nblk, per, groups, PAGE)).reshape(n_pages, groups, PAGE) > 0.5
        else:
            mask = mask_ref[0] > 0.5
        for r in range(reps):
            qcol = qc_ref[0, :, r:r + 1]
            s_own = _head_sums(jnp.broadcast_to(kn_ref[0] * qcol, (gw, PAGE)), groups) * SCALE
            s = jnp.where(mask, s_ref[r] * SCALE, NEG)
            m = jnp.max(jnp.max(s, axis=0), axis=-1, keepdims=True)
            m = jnp.maximum(m, s_own)
            p = jnp.where(mask, jnp.exp(s - m), 0.0)
            p_own = jnp.exp(s_own - m)
            den = jnp.sum(jnp.sum(p, axis=0), axis=-1, keepdims=True) + p_own
            s_ref[r] = p
            st_ref[r, 0] = p_own
            st_ref[r, 1] = jnp.maximum(den, TINY)

    def v_compute(c, slot):
        for r in range(reps):
            acc = acc_ref[r]
            for i in range(chunk):
                acc = acc + buf[slot, i] * _expand_heads(s_ref[r, c * chunk + i], gw)
            acc_ref[r] = acc

    def step(idx, _):
        slot = idx % 2
        nxt = idx + 1

        @pl.when(nxt < 2 * n)
        def _():
            start(nxt // n, nxt % n, 1 - slot)

        wait(idx // n, idx % n, slot)

        @pl.when(idx < n)
        def _():
            k_compute(idx, slot)

        @pl.when(idx == n - 1)
        def _():
            select()

        @pl.when(idx >= n)
        def _():
            v_compute(idx - n, slot)

        return 0

    lax.fori_loop(0, 2 * n, step, 0)
    for r in range(reps):
        tot = jnp.sum(acc_ref[r], axis=-1, keepdims=True)
        p_own = _expand_heads(st_ref[r, 0], gw)[:, 0:1]
        den = _expand_heads(st_ref[r, 1], gw)[:, 0:1]
        o_ref[0, :, r:r + 1] = (tot + p_own * vn_ref[0]) / den


def _paged_attn(page_table, pool_t, layer, q_cols, k_new, v_new, row_mask, groups, reps, moba):
    bsz, n_pages = page_table.shape
    gw = groups * HD
    chunk = math.gcd(n_pages, 16)
    if row_mask is None:
        row_mask = jnp.zeros((bsz, 1, groups, PAGE), F32)
    mshape = row_mask.shape[1:]
    nq = -(-reps * groups // 8) * 8
    qh = q_cols.reshape(bsz, groups, HD, reps).transpose(0, 3, 1, 2)
    qbd = jnp.einsum("brgd,gh->brghd", qh, jnp.eye(groups, dtype=F32)).reshape(bsz, reps * groups, gw)
    qbd = jnp.pad(qbd, ((0, 0), (0, nq - reps * groups), (0, 0))).astype(BF16)
    body = functools.partial(_paged_attn_body, layer=layer, groups=groups, reps=reps, chunk=chunk, moba=moba)
    grid_spec = pltpu.PrefetchScalarGridSpec(
        num_scalar_prefetch=1,
        grid=(bsz,),
        in_specs=[
            pl.BlockSpec(memory_space=pl.ANY),
            pl.BlockSpec((1, gw, reps), lambda i, pt: (i, 0, 0)),
            pl.BlockSpec((1, nq, gw), lambda i, pt: (i, 0, 0)),
            pl.BlockSpec((1, gw, 1), lambda i, pt: (i, 0, 0)),
            pl.BlockSpec((1, gw, 1), lambda i, pt: (i, 0, 0)),
            pl.BlockSpec((1,) + mshape, lambda i, pt: (i, 0, 0, 0)),
        ],
        out_specs=pl.BlockSpec((1, gw, reps), lambda i, pt: (i, 0, 0)),
        scratch_shapes=[
            pltpu.VMEM((2, chunk, gw, PAGE), F32),
            pltpu.SemaphoreType.DMA((2,)),
            pltpu.VMEM((reps, n_pages, groups, PAGE), F32),
            pltpu.VMEM((reps, gw, PAGE), F32),
            pltpu.VMEM((reps, 2, groups, PAGE), F32),
        ],
    )
    return pl.pallas_call(
        body,
        grid_spec=grid_spec,
        out_shape=jax.ShapeDtypeStruct((bsz, gw, reps), F32),
        compiler_params=_params("arbitrary"),
        name="paged_attn_moba" if moba else "paged_attn_nsa",
    )(page_table, pool_t, q_cols, qbd, k_new.reshape(bsz, gw, 1), v_new.reshape(bsz, gw, 1), row_mask)


CMP_DB = 8


def _compress_weights_t(w1, w2, pe):
    per = PAGE // NSA_BLOCK
    eye = jnp.eye(per, dtype=F32)
    w1r = w1.astype(F32).reshape(2, NSA_BLOCK, HD, HD)
    wd = jnp.einsum("klde,gh->kdglhe", w1r, eye).reshape(2, HD // CMP_DB, CMP_DB * PAGE, per * HD)
    ped = jnp.tile(pe.astype(F32).transpose(0, 2, 1), (1, 1, per)).reshape(2, HD // CMP_DB, 1, CMP_DB * PAGE)
    w2t = jnp.einsum("kde,gh->kgdhe", w2.astype(F32), eye).reshape(2, per * HD, per * HD)
    return wd.astype(BF16), ped, w2t.astype(BF16)


def _nsa_cmp_sample_body(pt_ref, pool_ref, q_ref, wd_ref, ped_ref, w2_ref, oc_ref, sel_ref, buf, sem, tok_ref, acc_ref,
                         *, layer, nseq):
    b = pl.program_id(0)
    j = pl.program_id(1)
    bsz, n_pages = pt_ref.shape
    t = b * 2 + j
    slot = t % 2
    m_rows = nseq * NSA_KVH * n_pages

    def copy(grp, kv, sq, p, g, sl):
        row = pl.multiple_of((((sl * nseq + sq) * NSA_KVH + g) * n_pages + p) * HD, HD)
        return pltpu.make_async_copy(pool_ref.at[pt_ref[grp * nseq + sq, p], layer, kv, g], buf.at[pl.ds(row, HD)],
                                     sem.at[sl])

    def each(fn):
        def run(grp, kv, sl):
            def page(p, _):
                for sq in range(nseq):
                    for g in range(NSA_KVH):
                        fn(copy(grp, kv, sq, p, g, sl))
                return 0
            lax.fori_loop(0, n_pages, page, 0)
        return run

    start = each(lambda c: c.start())
    wait = each(lambda c: c.wait())

    @pl.when(t == 0)
    def _():
        start(0, 0, 0)

    @pl.when(t + 1 < 2 * (bsz // nseq))
    def _():
        start((t + 1) // 2, (t + 1) % 2, 1 - slot)

    wait(b, j, slot)

    base = slot * m_rows * HD
    for dd in range(HD // CMP_DB):
        parts = [buf[pl.ds(base + dd * CMP_DB + u, m_rows, stride=HD), :] for u in range(CMP_DB)]
        a = jnp.concatenate(parts, axis=1) + ped_ref[0, dd]
        part = _dot(a.astype(BF16), wd_ref[0, dd])
        if dd == 0:
            acc_ref[...] = part
        else:
            acc_ref[...] += part
    tok_ref[j] = _dot(jax.nn.gelu(acc_ref[...]).astype(BF16), w2_ref[0])

    @pl.when(j == 1)
    def _():
        pos = n_pages * PAGE
        per = PAGE // NSA_BLOCK
        n_idx = (lax.broadcasted_iota(jnp.int32, (per, n_pages), 1) * per
                 + lax.broadcasted_iota(jnp.int32, (per, n_pages), 0))
        complete = n_idx * NSA_BLOCK + (NSA_BLOCK - 1) <= pos
        cur = pos // NSA_BLOCK
        forced = (n_idx == 0) | (n_idx == cur) | (n_idx == cur - 1)
        k_past = min(NSA_TOPN, per * n_pages + 1) - 1
        for sq, g in ((a, c) for a in range(nseq) for c in range(NSA_KVH)):
            qg = q_ref[sq, g * NSA_REP:(g + 1) * NSA_REP, :].astype(BF16)
            r0 = (sq * NSA_KVH + g) * n_pages
            ck = tok_ref[0, r0:r0 + n_pages, :].astype(BF16)
            cv = tok_ref[1, r0:r0 + n_pages, :].astype(BF16)
            s = [jnp.where(complete[h:h + 1], _dot_nt(qg, ck[:, h * HD:(h + 1) * HD]) * SCALE, NEG) for h in range(per)]
            m = functools.reduce(jnp.maximum, [jnp.max(x, axis=-1, keepdims=True) for x in s])
            e = [jnp.where(complete[h:h + 1], jnp.exp(s[h] - m), 0.0) for h in range(per)]
            den = jnp.maximum(sum(jnp.sum(x, axis=-1, keepdims=True) for x in e), TINY)
            pc = [x / den for x in e]
            oc_ref[sq, g * NSA_REP:(g + 1) * NSA_REP, :] = sum(
                _dot(pc[h].astype(BF16), cv[:, h * HD:(h + 1) * HD]) for h in range(per))
            imp = jnp.concatenate([jnp.sum(x, axis=0, keepdims=True) for x in pc], axis=0)
            score = jnp.where(forced, FORCE, jnp.where(complete, imp, NEG))
            sel = jnp.zeros(score.shape, F32)
            work = score
            for _ in range(k_past):
                mx = jnp.max(jnp.max(work, axis=-1, keepdims=True), axis=0, keepdims=True)
                cand = jnp.where(work == mx, n_idx, per * n_pages)
                first = jnp.min(jnp.min(cand, axis=-1, keepdims=True), axis=0, keepdims=True)
                pick = n_idx == first
                sel = jnp.where(pick, 1.0, sel)
                work = jnp.where(pick, -jnp.inf, work)
            sel_ref[sq, g] = jnp.where(score > 0.5 * NEG, sel, 0.0)


def _nsa_cmp_sample(page_table, pool_t, layer, q, wd, ped, w2b):
    bsz, n_pages = page_table.shape
    per = PAGE // NSA_BLOCK
    nseq = 2 if bsz % 2 == 0 else 1
    m_rows = nseq * NSA_KVH * n_pages
    grid_spec = pltpu.PrefetchScalarGridSpec(
        num_scalar_prefetch=1,
        grid=(bsz // nseq, 2),
        in_specs=[
            pl.BlockSpec(memory_space=pl.ANY),
            pl.BlockSpec((nseq, NSA_H, HD), lambda i, j, pt: (i, 0, 0)),
            pl.BlockSpec((1, HD // CMP_DB, CMP_DB * PAGE, per * HD), lambda i, j, pt: (j, 0, 0, 0)),
            pl.BlockSpec((1, HD // CMP_DB, 1, CMP_DB * PAGE), lambda i, j, pt: (j, 0, 0, 0)),
            pl.BlockSpec((1, per * HD, per * HD), lambda i, j, pt: (j, 0, 0)),
        ],
        out_specs=[pl.BlockSpec((nseq, NSA_H, HD), lambda i, j, pt: (i, 0, 0)),
                   pl.BlockSpec((nseq, NSA_KVH, per, n_pages), lambda i, j, pt: (i, 0, 0, 0))],
        scratch_shapes=[
            pltpu.VMEM((2 * m_rows * HD, PAGE), F32),
            pltpu.SemaphoreType.DMA((2,)),
            pltpu.VMEM((2, m_rows, per * HD), F32),
            pltpu.VMEM((m_rows, per * HD), F32),
        ],
    )
    return pl.pallas_call(
        functools.partial(_nsa_cmp_sample_body, layer=layer, nseq=nseq),
        grid_spec=grid_spec,
        out_shape=[jax.ShapeDtypeStruct((bsz, NSA_H, HD), F32),
                   jax.ShapeDtypeStruct((bsz, NSA_KVH, per, n_pages), F32)],
        compiler_params=_params("arbitrary", "arbitrary"),
        name="nsa_cmp_sample",
    )(page_table, pool_t, q, wd, ped, w2b)


def _win_sample_body(qc_ref, kt_ref, vt_ref, kn_ref, vn_ref, o_ref):
    kt = kt_ref[0, 0]
    vt = vt_ref[0, 0]
    gw, lbuf = kt.shape
    j = lax.broadcasted_iota(jnp.int32, (NSA_KVH, lbuf), 1)
    mask = j > lbuf - NSA_WINDOW
    for r in range(NSA_REP):
        qcol = qc_ref[0, :, r:r + 1]
        s = jnp.where(mask, _head_sums(kt * qcol, NSA_KVH) * SCALE, NEG)
        s_own = _head_sums(jnp.broadcast_to(kn_ref[0] * qcol, (gw, lbuf)), NSA_KVH) * SCALE
        m = jnp.maximum(jnp.max(s, axis=-1, keepdims=True), s_own)
        p = jnp.where(mask, jnp.exp(s - m), 0.0)
        p_own = jnp.exp(s_own - m)
        den = jnp.maximum(jnp.sum(p, axis=-1, keepdims=True) + p_own, TINY)
        tot = jnp.sum(vt * _expand_heads(p, gw), axis=-1, keepdims=True)
        o_ref[0, :, r:r + 1] = (tot + _expand_heads(p_own, gw)[:, 0:1] * vn_ref[0]) / _expand_heads(den, gw)[:, 0:1]


def _win_sample(q_cols, wbuf_t, k_new, v_new):
    bsz, gw, reps = q_cols.shape
    lbuf = wbuf_t.shape[-1]
    col = pl.BlockSpec((1, gw, 1), lambda i: (i, 0, 0))
    return pl.pallas_call(
        _win_sample_body,
        grid=(bsz,),
        in_specs=[pl.BlockSpec((1, gw, reps), lambda i: (i, 0, 0)),
                  pl.BlockSpec((1, 1, gw, lbuf), lambda i: (i, 0, 0, 0)),
                  pl.BlockSpec((1, 1, gw, lbuf), lambda i: (i, 1, 0, 0)), col, col],
        out_specs=pl.BlockSpec((1, gw, reps), lambda i: (i, 0, 0)),
        out_shape=jax.ShapeDtypeStruct((bsz, gw, reps), F32),
        compiler_params=_params("parallel"),
        name="nsa_win_sample",
    )(q_cols, wbuf_t, wbuf_t, k_new.reshape(bsz, gw, 1), v_new.reshape(bsz, gw, 1))


def _gate_combine_body(g_ref, c_ref, s_ref, w_ref, o_ref):
    gt = jax.nn.sigmoid(g_ref[...])
    o_ref[...] = gt[0] * c_ref[...] + gt[1] * s_ref[...] + gt[2] * w_ref[...]


def _gate_combine(gate_logits, o_c, o_s, o_w):
    return pl.pallas_call(
        _gate_combine_body,
        out_shape=jax.ShapeDtypeStruct(o_c.shape, F32),
        name="nsa_gate_combine",
    )(gate_logits, o_c, o_s, o_w)


ODD_NQ, ODD_NQR, ODD_KV = 0, NSA_QW, 2 * NSA_QW
ODD_HQ = ODD_KV + 6 * NSA_KVW
ODD_EQ = ODD_HQ + 3 * HGRN_W
ODD_GATE = ODD_EQ + MEM_W
ODD_NG = ODD_GATE + MIX_W
ODD_N = 4096


def _odd_weights(w_in):
    offs = np.cumsum([0, NSA_QW] + [NSA_KVW] * 6 + [3 * NSA_H] + [HGRN_W] * 3 + [MEM_W, MIX_W])
    nq = w_in[:, offs[0]:offs[1]]
    kv = w_in[:, offs[1]:offs[7]]
    ng = w_in[:, offs[7]:offs[8]]
    rest = w_in[:, offs[8]:]
    pad = jnp.zeros((w_in.shape[0], ODD_N - ODD_NG - 3 * NSA_H), w_in.dtype)
    return jnp.concatenate([nq, nq, kv, rest, ng, pad], axis=1)


def _stack_heads(a, tq, width):
    t = a.shape[0]
    a = a.reshape(t // tq, tq, NSA_KVH, NSA_REP, width).transpose(2, 0, 3, 1, 4)
    return a.reshape(NSA_KVH, t // tq, NSA_REP * tq, width)


def _unstack_heads(a, tq):
    g, nt, _, width = a.shape
    a = a.reshape(g, nt, NSA_REP, tq, width).transpose(1, 3, 0, 2, 4)
    return a.reshape(nt * tq, g * NSA_REP * width)


def _group_kv(k, v):
    t = k.shape[0]
    return jnp.concatenate([k.reshape(t, NSA_KVH, HD), v.reshape(t, NSA_KVH, HD)], axis=-1).transpose(1, 0, 2)


def _pad_lanes(a):
    return jnp.concatenate([a, jnp.zeros_like(a)], axis=-1)


def _odd_prompt(x, cos, sin, g, w_aug_bf16, w_o_bf16, mem_k, mem_v, lb, cmp_w, final_g, tm, tq, kc):
    t = x.shape[0]
    assert t % NSA_BLOCK == 0
    kvo = lambda j: ODD_KV + j * NSA_KVW
    nq, nqr, kv6, qfi, eq, gate, ng = _norm_proj(
        x, g, w_aug_bf16, cos, sin, ((ODD_NQR, ODD_KV), (kvo(2), kvo(3)), (kvo(4), kvo(5))), tm,
        splits=(NSA_QW, NSA_QW, 6 * NSA_KVW, 3 * HGRN_W, MEM_W, MIX_W, LANES))
    ck, cv, sk, sv, wk, wv = (kv6[:, j * NSA_KVW:(j + 1) * NSA_KVW] for j in range(6))
    w1b, w2b, peb = cmp_w
    nb = t // NSA_BLOCK
    cmp_tok = _compress(jnp.stack([ck, cv]).reshape(2, nb, NSA_BLOCK * NSA_KVW), w1b, w2b, peb)
    ckv = _group_kv(cmp_tok[0], cmp_tok[1])
    q_st = _pad_lanes(_stack_heads(nq, tq, HD))
    qr_st = _pad_lanes(_stack_heads(nqr, tq, HD))
    gt_st = _stack_heads(ng[:, :3 * NSA_H], tq, 3)
    onehot = ((jnp.arange(t)[:, None] // NSA_BLOCK) % HD == jnp.arange(HD)[None, :]).astype(F32)
    ska = jnp.concatenate([sk.reshape(t, NSA_KVH, HD), jnp.broadcast_to(onehot[:, None], (t, NSA_KVH, HD))], axis=-1)
    svo = jnp.concatenate([jnp.ones((t, NSA_KVH, HD), F32), sv.reshape(t, NSA_KVH, HD)], axis=-1)
    _, sk_norm2 = _block_stats(kv6, math.gcd(t, MOBA_BLOCK), NSA_KVW, 2)
    o_nsa = _nsa_prompt(q_st, qr_st, gt_st, ckv, ska.transpose(1, 0, 2).astype(BF16),
                        svo.transpose(1, 0, 2).astype(BF16), _group_kv(wk, wv).astype(BF16),
                        _key_norm_bound(sk_norm2, NSA_KVH), tq, kc)
    o_nsa = _unstack_heads(o_nsa[..., HD:], tq)
    o_hg, s_hg = _hgrn_prompt(qfi, lb.reshape(1, HGRN_W))
    o_mem = _mem_prompt(eq, mem_k, mem_v, tm)
    y = _mix_out(x, o_nsa, o_hg, o_mem, gate, w_o_bf16, final_g, tm)
    rows = lambda a, b: jnp.stack([a.reshape(t, NSA_KVH, HD), b.reshape(t, NSA_KVH, HD)], axis=1)
    return y, rows(ck, cv), rows(sk, sv), rows(wk, wv), _diag_blocks(s_hg, HGRN_H)

def _heads_major(a, h):
    t = a.shape[0]
    return a.reshape(t, h, HD).transpose(1, 0, 2)


def _even_prompt(x, cos, sin, g, w_in_bf16, w_o_bf16, mem_k, mem_v, tm):
    t = x.shape[0]
    o_mq = 3 * RET_W
    o_eq = 3 * RET_W + 3 * MOBA_W
    o_gate = o_eq + MEM_W
    ret, moba, eq, gate = _norm_proj(x, g, w_in_bf16, cos, sin, ((0, 2 * RET_W), (o_mq, o_mq + 2 * MOBA_W)), tm,
                                     splits=(3 * RET_W, 3 * MOBA_W, MEM_W, MIX_W))
    mq, mk, mv = moba[:, :MOBA_W], moba[:, MOBA_W:2 * MOBA_W], moba[:, 2 * MOBA_W:]
    o_ret, s_ret = _retention_prompt(ret)
    kmean, knorm2 = _block_stats(moba, MOBA_BLOCK, MOBA_W, 1)
    zq = jnp.zeros((MOBA_H, t, HD), F32)
    q_pad = jnp.concatenate([_heads_major(mq, MOBA_H), zq], axis=-1)
    vh = _heads_major(mv, MOBA_H)
    nblk = t // MOBA_BLOCK
    onehot = (jnp.arange(t)[:, None] // MOBA_BLOCK == jnp.arange(HD)[None, :]).astype(F32)
    ka = jnp.concatenate([_heads_major(mk, MOBA_H), jnp.broadcast_to(onehot, (MOBA_H, t, HD))], axis=-1).astype(BF16)
    vo = jnp.concatenate([jnp.ones_like(vh), vh], axis=-1).astype(BF16)
    km = jnp.pad(_heads_major(kmean, MOBA_H), ((0, 0), (HD, LANES - HD - nblk), (0, HD)))
    o_moba = _moba_prompt(q_pad, ka, vo, km, _key_norm_bound(knorm2, MOBA_H))
    o_moba = o_moba[:, :, HD:].transpose(1, 0, 2).reshape(t, MOBA_W)
    o_mem = _mem_prompt(eq, mem_k, mem_v, tm)
    y = _mix_out(x, o_ret, o_moba, o_mem, gate, w_o_bf16, None, tm)
    rows = moba[:, MOBA_W:].reshape(t, 2, MOBA_H, HD)
    return y, rows, _diag_blocks(s_ret, RET_H)


def _pages_t(pool, width):
    n_pool, n_layer = pool.shape[0], pool.shape[1]
    return pool.transpose(0, 1, 3, 4, 5, 2).reshape(n_pool, n_layer, 2, width, PAGE)


def _mem_t(cache):
    bsz, n = cache.shape[0], cache.shape[1]
    return cache.transpose(0, 2, 3, 4, 1).reshape(bsz, 2, MEM_W, n)


def _even_sample(x, cos, sin, g, w_in_bf16, w_o_bf16, mem_cache, state, page_table, pool, layer):
    bsz = x.shape[0]
    o_mq = 3 * RET_W
    o_eq = 3 * RET_W + 3 * MOBA_W
    o_gate = o_eq + MEM_W
    proj = _norm_proj(x, g, w_in_bf16, cos, sin, ((0, 2 * RET_W), (o_mq, o_mq + 2 * MOBA_W)), bsz)
    rq, rk, rv = proj[:, :RET_W], proj[:, RET_W:2 * RET_W], proj[:, 2 * RET_W:3 * RET_W]
    mq = proj[:, o_mq:o_mq + MOBA_W]
    mk = proj[:, o_mq + MOBA_W:o_mq + 2 * MOBA_W]
    mv = proj[:, o_mq + 2 * MOBA_W:o_eq]
    gamma = np.repeat(1.0 - np.power(2.0, -5.0 - np.arange(RET_H, dtype=np.float64)), HD)
    o_ret, s_ret = _state_step(state, rq, rk, jnp.broadcast_to(jnp.asarray(gamma, F32), (bsz, RET_W)), rv, "ret")
    o_moba = _paged_attn(page_table, _pages_t(pool, MOBA_W), layer, mq.reshape(bsz, MOBA_W, 1), mk, mv, None,
                         MOBA_H, 1, True).reshape(bsz, MOBA_W)
    o_mem = _mem_sample(proj[:, o_eq:o_gate], _mem_t(mem_cache))
    y = _mix_out(x, o_ret, o_moba, o_mem, proj[:, o_gate:], w_o_bf16, None, bsz)
    rows = proj[:, o_mq + MOBA_W:o_eq].reshape(bsz, 1, 2, MOBA_H, HD)
    return y, rows, s_ret


def _cols(a):
    bsz = a.shape[0]
    return a.reshape(bsz, NSA_KVH, NSA_REP, HD).transpose(0, 1, 3, 2).reshape(bsz, NSA_KVW, NSA_REP)


def _uncols(a):
    bsz = a.shape[0]
    return a.reshape(bsz, NSA_KVH, HD, NSA_REP).transpose(0, 1, 3, 2).reshape(bsz, NSA_QW)


def _odd_sample(x, cos, sin, g, w_aug_bf16, w_o_bf16, mem_cache, state, lb, page_table, cmp_pool, slc_pool, wbuf,
                layer, cmp_w_t, final_g):
    bsz = x.shape[0]
    lbuf = wbuf.shape[1]
    kvo = lambda j: ODD_KV + j * NSA_KVW
    proj = _norm_proj(x, g, w_aug_bf16, cos, sin, ((ODD_NQR, ODD_KV), (kvo(2), kvo(3)), (kvo(4), kvo(5))), bsz)
    ck, cv, sk, sv, wk, wv = (proj[:, kvo(j):kvo(j + 1)] for j in range(6))
    wd, ped, w2t = cmp_w_t
    n_pool, n_layer = cmp_pool.shape[0], cmp_pool.shape[1]
    cmp_t = cmp_pool.transpose(0, 1, 3, 4, 5, 2)
    o_c, sel = _nsa_cmp_sample(page_table, cmp_t, layer, proj[:, ODD_NQ:ODD_NQ + NSA_QW].reshape(bsz, NSA_H, HD),
                               wd, ped, w2t)
    row_mask = jnp.repeat(sel.transpose(0, 3, 1, 2), NSA_BLOCK, axis=-1)
    q_cols = _cols(proj[:, ODD_NQR:ODD_NQR + NSA_QW])
    o_s = _paged_attn(page_table, _pages_t(slc_pool, NSA_KVW), layer, q_cols, sk, sv, row_mask, NSA_KVH, NSA_REP, False)
    wbuf_t = wbuf.transpose(0, 2, 3, 4, 1).reshape(bsz, 2, NSA_KVW, lbuf)
    o_w = _win_sample(q_cols, wbuf_t, wk, wv)
    ng = proj[:, ODD_NG:ODD_NG + 3 * NSA_H].reshape(bsz, NSA_H, 3)
    gate_logits = jnp.repeat(ng.transpose(2, 0, 1), HD, axis=-1)
    o_nsa = _gate_combine(gate_logits, o_c.reshape(bsz, NSA_QW), _uncols(o_s), _uncols(o_w))
    hq, hf, hi = (proj[:, ODD_HQ + j * HGRN_W:ODD_HQ + (j + 1) * HGRN_W] for j in range(3))
    o_hg, s_hg = _state_step(state, hq, hf, jnp.broadcast_to(lb.reshape(1, HGRN_W), (bsz, HGRN_W)), hi, "hgrn")
    o_mem = _mem_sample(proj[:, ODD_EQ:ODD_GATE], _mem_t(mem_cache))
    y = _mix_out(x, o_nsa, o_hg, o_mem, proj[:, ODD_GATE:ODD_NG], w_o_bf16, final_g, bsz)
    rows = lambda a, b: jnp.stack([a.reshape(bsz, 1, NSA_KVH, HD), b.reshape(bsz, 1, NSA_KVH, HD)], axis=2)
    win = jnp.concatenate([wbuf.astype(F32), rows(wk, wv)], axis=1)[:, -lbuf:]
    return y, rows(ck, cv), rows(sk, sv), win, s_hg


def kernel(x_prompt, x_sample, mem_prompt, cache_moba_kv, state_ret, cache_nsa_cmp_kv, cache_nsa_slc_kv,
           cache_nsa_win_kv, state_hgrn, cache_mem_kv, page_table, norm_g, mem_norm_g, w_mem_kv, w_in_even,
           w_in_odd, w_out, cmp_w1, cmp_w2, cmp_pe, hgrn_lb_logits, final_g):
    bp, tp, d = x_prompt.shape
    bs, ts, _ = x_sample.shape
    depth = w_out.shape[0]
    assert bp == 1 and ts == 1 and depth == 2
    n_mem = mem_prompt.shape[1]
    past_len = page_table.shape[1] * PAGE
    assert past_len % MOBA_BLOCK == 0 and cache_moba_kv.shape[2] == PAGE
    tm, tq, kc = 256, 128, 256
    xp, xs, mem = x_prompt[0], x_sample[:, 0], mem_prompt[0]
    cos_p, sin_p = _rope_tables(jnp.arange(tp, dtype=jnp.int32))
    cos_s, sin_s = (jnp.broadcast_to(a, (bs, LANES)) for a in _rope_tables(jnp.full((1,), past_len, jnp.int32)))
    lb_prob = jax.nn.softmax(hgrn_lb_logits.astype(F32), axis=0)
    lb_all = jnp.cumsum(lb_prob, axis=0) - lb_prob[0]
    lw_p = min(NSA_WINDOW, tp)

    def mem_kv(layer):
        kv = _norm_proj(mem, mem_norm_g[layer], w_mem_kv[layer].astype(BF16), cos_p[:n_mem], sin_p[:n_mem], (), n_mem)
        return kv.reshape(n_mem, 2, MEM_W)

    w_in0 = w_in_even[0].astype(BF16)
    w_o0 = w_out[0].astype(BF16)
    mkv0 = mem_kv(0)
    hp, moba_p, ret_p = _even_prompt(xp, cos_p, sin_p, norm_g[0], w_in0, w_o0, mkv0[:, 0], mkv0[:, 1], tm)
    hs, moba_s, ret_s = _even_sample(xs, cos_s, sin_s, norm_g[0], w_in0, w_o0, cache_mem_kv[0], state_ret[0],
                                     page_table, cache_moba_kv, 0)

    w_in1 = _odd_weights(w_in_odd[0]).astype(BF16)
    w_o1 = w_out[1].astype(BF16)
    mkv1 = mem_kv(1)
    cmp_w = _compress_weights(cmp_w1[0], cmp_w2[0], cmp_pe[0])
    cmp_w_t = _compress_weights_t(cmp_w1[0], cmp_w2[0], cmp_pe[0])
    yp, cmp_p, slc_p, win_p, hg_p = _odd_prompt(hp, cos_p, sin_p, norm_g[1], w_in1, w_o1, mkv1[:, 0], mkv1[:, 1],
                                                lb_all[1], cmp_w, final_g, tm, tq, kc)
    ys, cmp_s, slc_s, win_s, hg_s = _odd_sample(hs, cos_s, sin_s, norm_g[1], w_in1, w_o1, cache_mem_kv[1],
                                                state_hgrn[0], lb_all[1], page_table, cache_nsa_cmp_kv,
                                                cache_nsa_slc_kv, cache_nsa_win_kv[0], 0, cmp_w_t, final_g)

    return (yp[None], ys[:, None], moba_p[None, None], moba_s[:, None], ret_p[None, None], ret_s[None],
            cmp_p[None, None], cmp_s[:, None], slc_p[None, None], slc_s[:, None], win_p[None, tp - lw_p:][None],
            win_s[None], hg_p[None, None], hg_s[None], jnp.stack([mkv0, mkv1]).reshape(depth, 1, n_mem, 2, MEM_H, HD))
```

```python
import functools
import math

import jax
import jax.numpy as jnp
import numpy as np
from jax import lax
from jax.experimental import pallas as pl
from jax.experimental.pallas import tpu as pltpu

F32 = jnp.float32
BF16 = jnp.bfloat16
HIGHEST = lax.Precision.HIGHEST

HD = 64
RET_H, MOBA_H, MEM_H, NSA_H, NSA_KVH, HGRN_H = 6, 6, 4, 8, 2, 4
NSA_REP = NSA_H // NSA_KVH
RET_W, MOBA_W, MEM_W = RET_H * HD, MOBA_H * HD, MEM_H * HD
NSA_QW, NSA_KVW, HGRN_W = NSA_H * HD, NSA_KVH * HD, HGRN_H * HD
MIX_W = RET_W + MOBA_W + MEM_W
PAGE = 128
RET_CHUNK, HGRN_CHUNK = 128, 64
MOBA_BLOCK, MOBA_TOPK = 256, 3
NSA_BLOCK, NSA_TOPN, NSA_WINDOW = 64, 16, 512
ROPE_THETA = 10000.0
EPS = 1e-6
NEG = -1e30
FORCE = 1e30
TINY = 1e-30
SCALE = HD ** -0.5
LANES = 128
SHIFT_SLACK = 1.01
SHIFT_LIMIT = 20.0
VMEM_LIMIT = 56 * 1024 * 1024


def _params(*sem):
    return pltpu.CompilerParams(dimension_semantics=sem, vmem_limit_bytes=VMEM_LIMIT)


def _dot(a, b, precision=None):
    return jnp.dot(a, b, preferred_element_type=F32, precision=precision)


def _dot_nt(a, b, precision=None):
    return lax.dot_general(a, b, (((1,), (1,)), ((), ())), preferred_element_type=F32, precision=precision)


def _dot_tn(a, b, precision=None):
    return lax.dot_general(a, b, (((0,), (0,)), ((), ())), preferred_element_type=F32, precision=precision)


def _block_ones(width, value=1.0):
    r = lax.broadcasted_iota(jnp.int32, (width, width), 0) // HD
    c = lax.broadcasted_iota(jnp.int32, (width, width), 1) // HD
    return jnp.where(r == c, value, 0.0).astype(F32)


def _rope_tile(a, cos, sin, first_half):
    rot = jnp.where(first_half, pltpu.roll(a, LANES - HD // 2, 1), pltpu.roll(a, HD // 2, 1))
    return a * cos + rot * sin


def _norm_proj_body(x_ref, g_ref, w_ref, cos_ref, sin_ref, *o_refs, rope_tiles, n_chunk, starts):
    x = x_ref[...]
    ms = jnp.mean(x * x, axis=-1, keepdims=True)
    y = (x * lax.rsqrt(ms + EPS) * g_ref[...]).astype(BF16)
    n = w_ref.shape[1]
    lane = lax.broadcasted_iota(jnp.int32, (x.shape[0], LANES), 1)
    first_half = (lane % HD) < HD // 2
    for c0 in range(0, n, n_chunk):
        acc = _dot(y, w_ref[:, c0:c0 + n_chunk])
        for j in range(n_chunk // LANES):
            tile = (c0 // LANES) + j
            which = max(i for i, s in enumerate(starts) if s <= tile * LANES)
            local = tile * LANES - starts[which]
            if local >= o_refs[which].shape[1]:
                continue
            a = acc[:, j * LANES:(j + 1) * LANES]
            if tile in rope_tiles:
                a = _rope_tile(a, cos_ref[...], sin_ref[...], first_half)
            o_refs[which][:, local:local + LANES] = a


def _norm_proj(x, g, w_bf16, cos, sin, rope_cols, tm, splits=None):
    m, d = x.shape
    n = w_bf16.shape[1]
    splits = (n,) if splits is None else tuple(splits)
    assert m % tm == 0 and n % LANES == 0 and all(s % LANES == 0 for s in splits) and sum(splits) <= n
    starts = tuple(int(s) for s in np.cumsum((0,) + splits[:-1]))
    tiles = n // LANES
    k = next(c for c in (4, 3, 2, 1) if tiles % c == 0)
    rope_tiles = frozenset(t for a, b in rope_cols for t in range(a // LANES, b // LANES))
    body = functools.partial(_norm_proj_body, rope_tiles=rope_tiles, n_chunk=k * LANES, starts=starts)
    outs = pl.pallas_call(
        body,
        grid=(m // tm,),
        in_specs=[
            pl.BlockSpec((tm, d), lambda i: (i, 0)),
            pl.BlockSpec((1, d), lambda i: (0, 0)),
            pl.BlockSpec((d, n), lambda i: (0, 0)),
            pl.BlockSpec((tm, LANES), lambda i: (i, 0)),
            pl.BlockSpec((tm, LANES), lambda i: (i, 0)),
        ],
        out_specs=[pl.BlockSpec((tm, s), lambda i: (i, 0)) for s in splits],
        out_shape=[jax.ShapeDtypeStruct((m, s), F32) for s in splits],
        compiler_params=_params("parallel"),
        name="norm_proj",
    )(x, g.reshape(1, d), w_bf16, cos, sin)
    return outs if len(splits) > 1 else outs[0]


def _rope_tables(pos):
    half = HD // 2
    inv = ROPE_THETA ** (-jnp.arange(half, dtype=F32) / half)
    ang = pos.astype(F32)[:, None] * inv[None, :]
    cos, sin = jnp.cos(ang), jnp.sin(ang)
    return jnp.concatenate([cos, cos, cos, cos], -1), jnp.concatenate([-sin, sin, -sin, sin], -1)


def _retention_consts(c):
    lg = np.log(1.0 - np.power(2.0, -5.0 - np.arange(RET_H, dtype=np.float64)))
    ti = np.arange(c, dtype=np.float64)
    causal = ti[:, None] >= ti[None, :]
    d_in = np.where(causal[None], np.exp(np.where(causal, ti[:, None] - ti[None, :], 0.0)[None] * lg[:, None, None]), 0.0)
    q_dec = np.repeat(np.exp((ti[:, None] + 1.0) * lg[None, :]), HD, axis=1)
    k_dec = np.repeat(np.exp((c - 1.0 - ti)[:, None] * lg[None, :]), HD, axis=1)
    c_dec = np.repeat(np.exp(c * lg), HD)
    head = np.arange(RET_W) // HD
    bd = (head[:, None] == head[None, :]).astype(np.float64)
    cmat = bd * c_dec[:, None]
    f = lambda a: jnp.asarray(a, dtype=F32)
    return f(d_in), f(q_dec), f(k_dec), f(cmat), f(bd)


def _retention_body(q_ref, k_ref, v_ref, din_ref, qdec_ref, kdec_ref, cmat_ref, bd_ref, o_ref, s_ref):
    @pl.when(pl.program_id(0) == 0)
    def _():
        s_ref[...] = jnp.zeros_like(s_ref)

    q = q_ref[...]
    k = k_ref[...] * SCALE
    v = v_ref[...]
    c, w = q.shape
    head = lax.broadcasted_iota(jnp.int32, (c, w), 1) // HD
    s = s_ref[...]
    qb, kb, vb = q.astype(BF16), k.astype(BF16), v.astype(BF16)
    o = _dot(qb, s.astype(BF16)) * qdec_ref[...]
    for h in range(RET_H):
        mh = head == h
        att = _dot_nt(jnp.where(mh, qb, jnp.zeros_like(qb)), kb) * din_ref[h]
        o = o + jnp.where(mh, _dot(att.astype(BF16), vb), 0.0)
    s_ref[...] = s * cmat_ref[...] + _dot_tn((k * kdec_ref[...]).astype(BF16), vb) * bd_ref[...]
    seg = bd_ref[...] * (1.0 / HD)
    xc = o - _dot(o, seg, HIGHEST)
    var = _dot(xc * xc, seg, HIGHEST)
    o_ref[...] = xc * lax.rsqrt(var + EPS)


def _retention_prompt(qkv):
    t, w = qkv.shape[0], RET_W
    c = math.gcd(t, RET_CHUNK)
    d_in, q_dec, k_dec, cmat, bd = _retention_consts(c)
    col = lambda j: pl.BlockSpec((c, w), lambda i: (i, j))
    const2 = lambda shape: pl.BlockSpec(shape, lambda i: (0,) * len(shape))
    return pl.pallas_call(
        _retention_body,
        grid=(t // c,),
        in_specs=[col(0), col(1), col(2), const2((RET_H, c, c)), const2((c, w)), const2((c, w)), const2((w, w)),
                  const2((w, w))],
        out_specs=[col(0), const2((w, w))],
        out_shape=[jax.ShapeDtypeStruct((t, w), F32), jax.ShapeDtypeStruct((w, w), F32)],
        compiler_params=_params("arbitrary"),
        name="retention_prompt",
    )(qkv, qkv, qkv, d_in, q_dec, k_dec, cmat, bd)


def _diag_blocks(s, h):
    s4 = s.reshape(h, HD, h, HD)
    return jnp.stack([s4[i, :, i, :] for i in range(h)], axis=0)


def _block_stats_body(k_ref, mean_ref, norm_ref):
    k = k_ref[...]
    mean_ref[0] = jnp.mean(k, axis=0, keepdims=True)
    n2 = _dot(k * k, _block_ones(k.shape[1]), HIGHEST)
    norm_ref[0] = jnp.max(n2, axis=0, keepdims=True)


def _block_stats(k, blk, w, col):
    t = k.shape[0]
    assert t % blk == 0
    mean, norm = pl.pallas_call(
        _block_stats_body,
        grid=(t // blk,),
        in_specs=[pl.BlockSpec((blk, w), lambda i: (i, col))],
        out_specs=[pl.BlockSpec((1, 1, w), lambda i: (i, 0, 0))] * 2,
        out_shape=[jax.ShapeDtypeStruct((t // blk, 1, w), F32)] * 2,
        compiler_params=_params("parallel"),
        name="block_stats",
    )(k)
    return mean.reshape(t // blk, w), norm.reshape(t // blk, w)


def _key_norm_bound(norm2, heads):
    kmax = jnp.sqrt(jnp.max(norm2, axis=0)).reshape(heads, HD)[:, :1]
    return jnp.broadcast_to(kmax[:, :, None], (heads, 1, LANES))


def _topk_mask(score, k):
    n = score.shape[-1]
    idx = lax.broadcasted_iota(jnp.int32, score.shape, score.ndim - 1)
    sel = jnp.zeros(score.shape, F32)
    work = score
    for _ in range(k):
        m = jnp.max(work, axis=-1, keepdims=True)
        first = jnp.min(jnp.where(work == m, idx, n), axis=-1, keepdims=True)
        pick = idx == first
        sel = jnp.where(pick, 1.0, sel)
        work = jnp.where(pick, -jnp.inf, work)
    return sel


def _tile_loop(lo, hi, fn, unroll=8):
    def group(g, _):
        for u in range(unroll):
            fn(lo + unroll * g + u)
        return 0
    n = jnp.maximum(hi - lo, 0)
    main = n // unroll
    lax.fori_loop(0, main, group, 0)
    done = lo + main * unroll
    rem = n - main * unroll
    size = unroll // 2
    while size >= 1:
        @pl.when((rem & size) != 0)
        def _(done=done, size=size):
            for u in range(size):
                fn(done + u)
        done = done + (rem & size)
        size //= 2


def _lane_fold_max(s):
    out = s[:, 0:LANES]
    for c in range(1, s.shape[1] // LANES):
        out = jnp.maximum(out, s[:, c * LANES:(c + 1) * LANES])
    return out


def _moba_prompt_body(q_ref, ka_ref, vo_ref, kmean_ref, kmax_ref, o_ref, s_ref, m_ref, acc_ref, *, topk, bq):
    qi = pl.program_id(1)
    q = jnp.concatenate([q_ref[0], jnp.zeros(q_ref.shape[1:], F32)], axis=1)
    rows = q.shape[0]
    kb = MOBA_BLOCK
    gate = _dot_nt(q, kmean_ref[0], HIGHEST)
    blk = lax.broadcasted_iota(jnp.int32, (rows, LANES), 1) - HD
    own = qi * bq + lax.broadcasted_iota(jnp.int32, (rows, LANES), 0) // kb
    past = (blk >= 0) & (blk < own)
    sel = jnp.where(past, _topk_mask(jnp.where(past, gate, NEG), topk), 0.0)
    bias = jnp.where((blk < 0) | (blk == own) | (sel > 0.5), 0.0, NEG)
    qa = (q * SCALE + bias).astype(BF16)
    n_past = qi * bq
    qpos = lax.broadcasted_iota(jnp.int32, (rows, kb), 0)
    kpos = lax.broadcasted_iota(jnp.int32, (rows, kb), 1)
    acc_ref[...] = jnp.zeros(acc_ref.shape, F32)

    def scores(j):
        start = pl.multiple_of(j * kb, kb)
        return _dot_nt(qa, ka_ref[0, pl.ds(start, kb), :])

    def own_scores(d):
        return jnp.where(kpos + d * kb <= qpos, scores(n_past + d), NEG)

    def add_values(j, p):
        start = pl.multiple_of(j * kb, kb)
        acc_ref[...] += _dot(p.astype(BF16), vo_ref[0, pl.ds(start, kb), :])

    bound = jnp.sqrt(jnp.sum(q * q, axis=-1, keepdims=True)) * kmax_ref[0][:, 0:1] * (SCALE * SHIFT_SLACK)
    small = jnp.max(bound) <= SHIFT_LIMIT

    @pl.when(small)
    def _():
        _tile_loop(0, n_past, lambda j: add_values(j, jnp.exp(scores(j) - bound)))
        for d in range(bq):
            add_values(n_past + d, jnp.exp(own_scores(d) - bound))

    @pl.when(jnp.logical_not(small))
    def _():
        m_ref[...] = jnp.full(m_ref.shape, NEG, F32)

        def score(j, s):
            s_ref[j] = s
            m_ref[...] = jnp.maximum(m_ref[...], _lane_fold_max(s))

        _tile_loop(0, n_past, lambda j: score(j, scores(j)))
        for d in range(bq):
            score(n_past + d, own_scores(d))
        m = jnp.max(m_ref[...], axis=-1, keepdims=True)
        _tile_loop(0, n_past + bq, lambda j: add_values(j, jnp.exp(s_ref[j] - m)))

    acc = acc_ref[...]
    o_ref[0] = (acc / jnp.maximum(acc[:, 0:1], TINY))[:, HD:]


def _moba_prompt(q_pad, ka_bf16, vo_bf16, kmean_rows, kmax):
    h, t, _ = q_pad.shape
    nblk = t // MOBA_BLOCK
    assert t % MOBA_BLOCK == 0 and nblk <= LANES - HD
    bq = 2 if nblk % 2 == 0 else 1
    rows = bq * MOBA_BLOCK
    return pl.pallas_call(
        functools.partial(_moba_prompt_body, topk=min(MOBA_TOPK, nblk), bq=bq),
        grid=(h, nblk // bq),
        in_specs=[
            pl.BlockSpec((1, rows, HD), lambda a, i: (a, i, 0)),
            pl.BlockSpec((1, t, LANES), lambda a, i: (a, 0, 0), pipeline_mode=pl.Buffered(1)),
            pl.BlockSpec((1, t, LANES), lambda a, i: (a, 0, 0), pipeline_mode=pl.Buffered(1)),
            pl.BlockSpec((1, LANES, LANES), lambda a, i: (a, 0, 0)),
            pl.BlockSpec((1, 1, LANES), lambda a, i: (a, 0, 0)),
        ],
        out_specs=pl.BlockSpec((1, rows, HD), lambda a, i: (a, i, 0)),
        out_shape=jax.ShapeDtypeStruct((h, t, HD), F32),
        scratch_shapes=[pltpu.VMEM((nblk, rows, MOBA_BLOCK), F32), pltpu.VMEM((rows, LANES), F32),
                        pltpu.VMEM((rows, LANES), F32)],
        compiler_params=_params("parallel", "arbitrary"),
        name="moba_prompt",
    )(q_pad, ka_bf16, vo_bf16, kmean_rows, kmax)


def _mem_prompt_body(q_ref, k_ref, v_ref, o_ref):
    q = q_ref[...]
    kb = k_ref[...].astype(BF16)
    vb = v_ref[...].astype(BF16)
    head = lax.broadcasted_iota(jnp.int32, q.shape, 1) // HD
    o = jnp.zeros(q.shape, F32)
    for h in range(MEM_H):
        mh = head == h
        s = _dot_nt(jnp.where(mh, q, 0.0).astype(BF16), kb) * SCALE
        e = jnp.exp(s - jnp.max(s, axis=-1, keepdims=True))
        p = e / jnp.sum(e, axis=-1, keepdims=True)
        o = o + jnp.where(mh, _dot(p.astype(BF16), vb), 0.0)
    o_ref[...] = o


def _mem_prompt(q, k, v, tm):
    t, w = q.shape
    n = k.shape[0]
    return pl.pallas_call(
        _mem_prompt_body,
        grid=(t // tm,),
        in_specs=[pl.BlockSpec((tm, w), lambda i: (i, 0)), pl.BlockSpec((n, w), lambda i: (0, 0)),
                  pl.BlockSpec((n, w), lambda i: (0, 0))],
        out_specs=pl.BlockSpec((tm, w), lambda i: (i, 0)),
        out_shape=jax.ShapeDtypeStruct((t, w), F32),
        compiler_params=_params("parallel"),
        name="mem_prompt",
    )(q, k, v)


def _mix_out_body(x_ref, a_ref, b_ref, c_ref, gate_ref, wo_ref, fg_ref, o_ref, *, final_norm):
    gate = gate_ref[...]
    gate = gate * jax.nn.sigmoid(gate)
    y = x_ref[...]
    off = 0
    for ref in (a_ref, b_ref, c_ref):
        w = ref.shape[1]
        mix = (ref[...] * gate[:, off:off + w]).astype(BF16)
        y = y + _dot(mix, wo_ref[off:off + w, :])
        off += w
    if final_norm:
        ms = jnp.mean(y * y, axis=-1, keepdims=True)
        y = y * lax.rsqrt(ms + EPS) * fg_ref[...]
    o_ref[...] = y


def _mix_out(x, a, b, c, gate, wo_bf16, final_g, tm):
    m, d = x.shape
    row = lambda w: pl.BlockSpec((tm, w), lambda i: (i, 0))
    fg = jnp.ones((1, d), F32) if final_g is None else final_g.reshape(1, d).astype(F32)
    body = functools.partial(_mix_out_body, final_norm=final_g is not None)
    return pl.pallas_call(
        body,
        grid=(m // tm,),
        in_specs=[row(d), row(a.shape[1]), row(b.shape[1]), row(c.shape[1]), row(gate.shape[1]),
                  pl.BlockSpec(wo_bf16.shape, lambda i: (0, 0)), pl.BlockSpec((1, d), lambda i: (0, 0))],
        out_specs=row(d),
        out_shape=jax.ShapeDtypeStruct((m, d), F32),
        compiler_params=_params("parallel"),
        name="mix_out",
    )(x, a, b, c, gate, wo_bf16, fg)


def _compress_body(x_ref, pe_ref, w1_ref, w2_ref, o_ref):
    x = (x_ref[0] + pe_ref[0]).astype(BF16)
    hid = jax.nn.gelu(_dot(x, w1_ref[0]))
    o_ref[0] = _dot(hid.astype(BF16), w2_ref[0])


def _compress_weights(w1, w2, pe):
    eye = jnp.eye(NSA_KVH, dtype=F32)
    w1r = w1.astype(F32).reshape(2, NSA_BLOCK, HD, HD)
    w1b = jnp.einsum("klde,gh->klgdhe", w1r, eye).reshape(2, NSA_BLOCK * NSA_KVW, NSA_KVW)
    w2b = jnp.einsum("kde,gh->kgdhe", w2.astype(F32), eye).reshape(2, NSA_KVW, NSA_KVW)
    peb = jnp.broadcast_to(pe.astype(F32)[:, :, None, :], (2, NSA_BLOCK, NSA_KVH, HD)).reshape(2, 1, NSA_BLOCK * NSA_KVW)
    return w1b.astype(BF16), w2b.astype(BF16), peb


def _compress(x, w1b, w2b, peb):
    _, nb, kdim = x.shape
    return pl.pallas_call(
        _compress_body,
        grid=(2,),
        in_specs=[pl.BlockSpec((1, nb, kdim), lambda i: (i, 0, 0)), pl.BlockSpec((1, 1, kdim), lambda i: (i, 0, 0)),
                  pl.BlockSpec((1, kdim, NSA_KVW), lambda i: (i, 0, 0)),
                  pl.BlockSpec((1, NSA_KVW, NSA_KVW), lambda i: (i, 0, 0))],
        out_specs=pl.BlockSpec((1, nb, NSA_KVW), lambda i: (i, 0, 0)),
        out_shape=jax.ShapeDtypeStruct((2, nb, NSA_KVW), F32),
        compiler_params=_params("parallel"),
        name="nsa_compress",
    )(x, peb, w1b, w2b)


def _nsa_prompt_body(q_ref, qr_ref, gt_ref, ckv_ref, ska_ref, svo_ref, wkv_ref, kmax_ref, o_ref, s_ref, m_ref, acc_ref,
                     *, tq, kc):
    i = pl.program_id(1)
    t0 = i * tq
    rows = q_ref.shape[2]
    nb = ckv_ref.shape[1]
    pad = lambda a: jnp.concatenate([a, jnp.zeros_like(a)], axis=1)
    q = pad(q_ref[0, 0])
    qr = pad(qr_ref[0, 0]).astype(F32)
    qrs = qr * SCALE
    qrb = qrs.astype(BF16)
    tpos = t0 + lax.broadcasted_iota(jnp.int32, (rows, 1), 0) % tq

    ckv = ckv_ref[0].astype(BF16)
    blk = lax.broadcasted_iota(jnp.int32, (nb, rows), 0)
    tpos_l = t0 + lax.broadcasted_iota(jnp.int32, (nb, rows), 1) % tq
    complete = blk * NSA_BLOCK + (NSA_BLOCK - 1) <= tpos_l
    s_c = jnp.where(complete, _dot_nt(ckv, q) * SCALE, NEG)
    e = jnp.where(complete, jnp.exp(s_c - jnp.max(s_c, axis=0, keepdims=True)), 0.0)
    pc = e / jnp.maximum(jnp.sum(e, axis=0, keepdims=True), TINY)
    o_c = _dot_tn(pc.astype(BF16), ckv)

    imp = pc[:, 0:tq]
    for r in range(1, NSA_REP):
        imp = imp + pc[:, r * tq:(r + 1) * tq]
    blk_q = lax.broadcasted_iota(jnp.int32, (nb, tq), 0)
    tpos_q = t0 + lax.broadcasted_iota(jnp.int32, (nb, tq), 1)
    cur = tpos_q // NSA_BLOCK
    forced = (blk_q == 0) | (blk_q == cur) | (blk_q == cur - 1)
    complete_q = blk_q * NSA_BLOCK + (NSA_BLOCK - 1) <= tpos_q
    score = jnp.where(forced, FORCE, jnp.where(complete_q, imp, NEG))
    sel = jnp.where(score > 0.5 * NEG, _topk_mask_axis0(score, min(NSA_TOPN, nb)), 0.0)

    bias = jnp.where(sel > 0.5, 0.0, NEG).T
    nbp = -(-nb // LANES) * LANES
    if nbp > nb:
        bias = jnp.concatenate([bias, jnp.full((tq, nbp - nb), NEG, F32)], axis=1)
    bias = jnp.concatenate([bias] * NSA_REP, axis=0)
    upper =lax.broadcasted_iota(jnp.int32, (rows, LANES), 1) >= HD
    n_span = -(-nb // HD)
    qa = []
    for sp in range(n_span):
        col = bias[:, (sp // 2) * LANES:(sp // 2 + 1) * LANES]
        if sp % 2 == 0:
            col = pltpu.roll(col, HD, 1)
        qa.append((qrs + jnp.where(upper, col, 0.0)).astype(BF16))
    cps = HD * NSA_BLOCK // kc
    n_chunks = (t0 + tq + kc - 1) // kc
    last = n_chunks - 1
    qa_last = qa[0]
    for sp in range(1, n_span):
        qa_last = jnp.where(last // cps == sp, qa[sp], qa_last)
    last_start = pl.multiple_of(last * kc, kc)
    kpos = last_start + lax.broadcasted_iota(jnp.int32, (rows, kc), 1)
    s_last = jnp.where(kpos <= tpos, _dot_nt(qa_last, ska_ref[0, pl.ds(last_start, kc), :]), NEG)

    def spans(fn):
        for sp in range(n_span):
            lo = sp * cps
            _tile_loop(jnp.minimum(lo, last), jnp.minimum(lo + cps, last), functools.partial(fn, qa[sp]))

    def scores(qsp, c):
        start = pl.multiple_of(c * kc, kc)
        return _dot_nt(qsp, ska_ref[0, pl.ds(start, kc), :])

    def add_values(c, p):
        start = pl.multiple_of(c * kc, kc)
        acc_ref[...] += _dot(p.astype(BF16), svo_ref[0, pl.ds(start, kc), :])

    acc_ref[...] = jnp.zeros(acc_ref.shape, F32)
    bound =jnp.sqrt(jnp.sum(qr * qr, axis=-1, keepdims=True)) * kmax_ref[0][:, 0:1] * (SCALE * SHIFT_SLACK)
    small = jnp.max(bound) <= SHIFT_LIMIT

    @pl.when(small)
    def _():
        add_values(last, jnp.exp(s_last - bound))
        spans(lambda qsp, c: add_values(c, jnp.exp(scores(qsp, c) - bound)))

    @pl.when(jnp.logical_not(small))
    def _():
        m_ref[...] = _lane_fold_max(s_last)

        def fold_max(qsp, c):
            s = scores(qsp, c)
            s_ref[c] = s
            m_ref[...] = jnp.maximum(m_ref[...], _lane_fold_max(s))

        spans(fold_max)
        m_s = jnp.max(m_ref[...], axis=-1, keepdims=True)
        add_values(last, jnp.exp(s_last - m_s))
        spans(lambda qsp, c: add_values(c, jnp.exp(s_ref[c] - m_s)))

    acc_s = acc_ref[...]
    o_s = acc_s / jnp.maximum(acc_s[:, 0:1], TINY)

    nwin = (NSA_WINDOW + tq - 1) // tq + 1
    win = []
    for d in range(nwin):
        c = i - (nwin - 1) + d
        start = pl.multiple_of(jnp.maximum(c, 0) * tq, tq)
        kv = wkv_ref[0, pl.ds(start, tq), :]
        kpos = c * tq + lax.broadcasted_iota(jnp.int32, (rows, tq), 1)
        mask = (kpos <= tpos) & (kpos > tpos - NSA_WINDOW) & (kpos >= 0)
        win.append((jnp.where(mask, _dot_nt(qrb, kv), NEG), kv))
    m_w = jnp.max(functools.reduce(jnp.maximum, [s for s, _ in win]), axis=-1, keepdims=True)
    p_w = [jnp.exp(s - m_w) for s, _ in win]
    l_w = jnp.sum(functools.reduce(jnp.add, p_w), axis=-1, keepdims=True)
    acc_w = functools.reduce(jnp.add, [_dot(p.astype(BF16), kv) for p, (_, kv) in zip(p_w, win)])
    o_w = acc_w / jnp.maximum(l_w, TINY)

    gt = jax.nn.sigmoid(gt_ref[0, 0])
    o_ref[0, 0] = (gt[:, 0:1] * o_c + gt[:, 1:2] * o_s + gt[:, 2:3] * o_w)[:, HD:]


def _nsa_prompt(q_st, qr_st, gt_st, ckv, ska, svo, wkv, kmax, tq, kc):
    g, nt, rows, _ = q_st.shape
    t = ska.shape[1]
    nb = ckv.shape[1]
    assert t % kc == 0 and kc % tq == 0 and (HD * NSA_BLOCK) % kc == 0 and t % tq == 0
    qspec = pl.BlockSpec((1, 1, rows, HD), lambda a, i: (a, i, 0, 0))
    seq = pl.BlockSpec((1, t, LANES), lambda a, i: (a, 0, 0), pipeline_mode=pl.Buffered(1))
    body = functools.partial(_nsa_prompt_body, tq=tq, kc=kc)
    return pl.pallas_call(
        body,
        grid=(g, nt),
        in_specs=[qspec, qspec, pl.BlockSpec((1, 1, rows, 3), lambda a, i: (a, i, 0, 0)),
                  pl.BlockSpec((1, nb, LANES), lambda a, i: (a, 0, 0)), seq, seq, seq,
                  pl.BlockSpec((1, 1, LANES), lambda a, i: (a, 0, 0))],
        out_specs=qspec,
        out_shape=jax.ShapeDtypeStruct((g, nt, rows, HD), F32),
        scratch_shapes=[pltpu.VMEM((t // kc, rows, kc), F32), pltpu.VMEM((rows, LANES), F32),
                        pltpu.VMEM((rows, LANES), F32)],
        compiler_params=_params("parallel", "arbitrary"),
        name="nsa_prompt",
    )(q_st, qr_st, gt_st, ckv, ska, svo, wkv, kmax)


def _hgrn_body(q_ref, f_ref, i_ref, lb_ref, o_ref, s_ref):
    @pl.when(pl.program_id(0) == 0)
    def _():
        s_ref[...] = jnp.zeros_like(s_ref)

    q = q_ref[...]
    v = i_ref[...]
    c, w = q.shape
    lb = lb_ref[...]
    f = lb + (1.0 - lb) * jax.nn.sigmoid(f_ref[...])
    kk = 1.0 - f
    tr = lax.broadcasted_iota(jnp.int32, (c, c), 0)
    tc = lax.broadcasted_iota(jnp.int32, (c, c), 1)
    cum = _dot(jnp.where(tr >= tc, 1.0, 0.0), jnp.log(f), HIGHEST)
    bd = _block_ones(w)
    bd_b = bd.astype(BF16)
    s = s_ref[...]
    o = _dot(q * jnp.exp(cum), s, HIGHEST)
    t_idx = lax.broadcasted_iota(jnp.int32, (c, w), 0)

    sub = 8
    groups = c // sub
    o_blk = [o[i * sub:(i + 1) * sub] for i in range(groups)]
    for g in range(groups):
        r0 = g * sub
        n = c - r0
        t_g = r0 + lax.broadcasted_iota(jnp.int32, (n, w), 0)
        es = [jnp.where(t_g >= j, jnp.exp(cum[r0:] - cum[j:j + 1]) * q[r0:] * kk[j:j + 1], 0.0)
              for j in range(r0, r0 + sub)]
        e = jnp.concatenate(es, axis=0)
        e_hi = e.astype(BF16)
        e_lo = (e - e_hi.astype(F32)).astype(BF16)
        a = _dot(e_hi, bd_b) + _dot(e_lo, bd_b)
        contrib = a[0:n] * v[r0:r0 + 1]
        for u in range(1, sub):
            contrib = contrib + a[u * n:(u + 1) * n] * v[r0 + u:r0 + u + 1]
        for i in range(g, groups):
            o_blk[i] = o_blk[i] + contrib[(i - g) * sub:(i - g + 1) * sub]
    o = jnp.concatenate(o_blk, axis=0)
    last = cum[c - 1:c, :]
    row0 = t_idx == 0
    scale_mat = _dot_tn(jnp.where(row0, jnp.exp(last), 0.0), jnp.where(row0, 1.0, 0.0), HIGHEST)
    s_ref[...] = s * scale_mat + _dot_tn(kk * jnp.exp(last - cum), v, HIGHEST) * bd
    ms = _dot(o * o, bd * (1.0 / HD), HIGHEST)
    o_ref[...] = o * lax.rsqrt(ms + EPS)


def _hgrn_prompt(qfi, lb):
    t, w = qfi.shape[0], HGRN_W
    c = math.gcd(t, HGRN_CHUNK)
    col = lambda j: pl.BlockSpec((c, w), lambda n: (n, j))
    return pl.pallas_call(
        _hgrn_body,
        grid=(t // c,),
        in_specs=[col(0), col(1), col(2), pl.BlockSpec((1, w), lambda n: (0, 0))],
        out_specs=[col(0), pl.BlockSpec((w, w), lambda n: (0, 0))],
        out_shape=[jax.ShapeDtypeStruct((t, w), F32), jax.ShapeDtypeStruct((w, w), F32)],
        compiler_params=_params("arbitrary"),
        name="hgrn_prompt",
    )(qfi, qfi, qfi, lb)


def _state_step_body(s_ref, q_ref, a_ref, b_ref, v_ref, o_ref, so_ref, *, mode):
    s = s_ref[...]
    if mode == "ret":
        k = a_ref[...] * SCALE
        dec = b_ref[...]
    else:
        lb = b_ref[...]
        dec = lb + (1.0 - lb) * jax.nn.sigmoid(a_ref[...])
        k = 1.0 - dec
    kv = k * v_ref[...]
    rows = s.shape[0]
    o = (q_ref[...] * (kv + dec * s)).reshape(rows // HD, HD, HD).sum(axis=1)
    so_ref[...] = dec * s + kv
    if mode == "ret":
        xc = o - jnp.mean(o, axis=-1, keepdims=True)
        o_ref[...] = xc * lax.rsqrt(jnp.mean(xc * xc, axis=-1, keepdims=True) + EPS)
    else:
        o_ref[...] = o * lax.rsqrt(jnp.mean(o * o, axis=-1, keepdims=True) + EPS)


def _state_step(state, q, a, b, v, mode):
    bsz, h = state.shape[0], state.shape[1]
    rows = bsz * h * HD
    col = lambda x: jnp.broadcast_to(x.reshape(bsz, h, HD, 1), (bsz, h, HD, HD)).reshape(rows, HD)
    vx = jnp.broadcast_to(v.reshape(bsz, h, 1, HD), (bsz, h, HD, HD)).reshape(rows, HD)
    bb = 8 if bsz % 8 == 0 else bsz
    br = bb * h * HD
    spec = pl.BlockSpec((br, HD), lambda i: (i, 0))
    ospec = pl.BlockSpec((br // HD, HD), lambda i: (i, 0))
    o, s_new = pl.pallas_call(
        functools.partial(_state_step_body, mode=mode),
        grid=(rows // br,),
        in_specs=[spec] * 5,
        out_specs=[ospec, spec],
        out_shape=[jax.ShapeDtypeStruct((rows // HD, HD), F32), jax.ShapeDtypeStruct((rows, HD), F32)],
        compiler_params=_params("parallel"),
        name="state_step_" + mode,
    )(state.reshape(rows, HD).astype(F32), col(q), col(a), col(b), vx)
    return o.reshape(bsz, h * HD), s_new.reshape(bsz, h, HD, HD)


def _expand_heads(p, width):
    g, n = p.shape
    return jnp.broadcast_to(p[:, None, :], (g, HD, n)).reshape(width, n)


def _head_sums(x, g):
    return x.reshape(g, HD, x.shape[-1]).sum(axis=1)


def _mem_sample_body(q_ref, kt_ref, vt_ref, o_ref):
    kt = kt_ref[0, 0]
    vt = vt_ref[0, 0]
    s = _head_sums(kt * q_ref[0], MEM_H) * SCALE
    e = jnp.exp(s - jnp.max(s, axis=-1, keepdims=True))
    p = e / jnp.sum(e, axis=-1, keepdims=True)
    o_ref[0] = jnp.sum(vt * _expand_heads(p, MEM_W), axis=-1, keepdims=True)


def _mem_sample(q, kvt):
    bsz, w = q.shape
    n = kvt.shape[-1]
    out = pl.pallas_call(
        _mem_sample_body,
        grid=(bsz,),
        in_specs=[pl.BlockSpec((1, w, 1), lambda i: (i, 0, 0)),
                  pl.BlockSpec((1, 1, w, n), lambda i: (i, 0, 0, 0)),
                  pl.BlockSpec((1, 1, w, n), lambda i: (i, 1, 0, 0))],
        out_specs=pl.BlockSpec((1, w, 1), lambda i: (i, 0, 0)),
        out_shape=jax.ShapeDtypeStruct((bsz, w, 1), F32),
        compiler_params=_params("parallel"),
        name="mem_sample",
    )(q.reshape(bsz, w, 1), kvt, kvt)
    return out.reshape(bsz, w)


def _topk_mask_axis0(score, k):
    n = score.shape[0]
    idx = lax.broadcasted_iota(jnp.int32, score.shape, 0)
    sel = jnp.zeros(score.shape, F32)
    work = score
    for _ in range(k):
        m = jnp.max(work, axis=0, keepdims=True)
        first = jnp.min(jnp.where(work == m, idx, n), axis=0, keepdims=True)
        pick = idx == first
        sel = jnp.where(pick, 1.0, sel)
        work = jnp.where(pick, -jnp.inf, work)
    return sel


def _moba_sample_body(pt_ref, pool_ref, qc_ref, qbd_ref, kn_ref, vn_ref, o_ref, buf, sem, s_ref, acc_ref, st_ref,
                      *, layer, chunk):
    b = pl.program_id(0)
    n_pages = pt_ref.shape[1]
    n = n_pages // chunk
    groups, gw = MOBA_H, MOBA_W

    def copy(kv, c, i, slot):
        return pltpu.make_async_copy(pool_ref.at[pt_ref[b, c * chunk + i], layer, kv], buf.at[slot, i], sem.at[slot])

    def start(kv, c, slot):
        lax.fori_loop(0, chunk, lambda i, _: (copy(kv, c, i, slot).start(), 0)[1], 0)

    def wait(kv, c, slot):
        lax.fori_loop(0, chunk, lambda i, _: (copy(kv, c, i, slot).wait(), 0)[1], 0)

    acc_ref[...] = jnp.zeros_like(acc_ref)
    start(0, 0, 0)

    def k_compute(c, slot):
        for i in range(chunk):
            sc = _dot(qbd_ref[0], buf[slot, i].astype(BF16))
            s_ref[c * chunk + i] = sc[0:groups]

    def select():
        per = MOBA_BLOCK // PAGE
        nblk = n_pages // per
        sc = s_ref[...].reshape(nblk, per, groups, PAGE)
        gate = jnp.sum(jnp.sum(sc, axis=1, keepdims=True), axis=-1, keepdims=True) * (1.0 / MOBA_BLOCK)
        sel = _topk_mask_axis0(gate, min(MOBA_TOPK, nblk))
        mask = jnp.broadcast_to(sel, (nblk, per, groups, PAGE)).reshape(n_pages, groups, PAGE) > 0.5
        s_own = _head_sums(jnp.broadcast_to(kn_ref[0] * qc_ref[0], (gw, PAGE)), groups) * SCALE
        s = jnp.where(mask, s_ref[...] * SCALE, NEG)
        m = jnp.maximum(jnp.max(jnp.max(s, axis=0), axis=-1, keepdims=True), s_own)
        p = jnp.where(mask, jnp.exp(s - m), 0.0)
        p_own = jnp.exp(s_own - m)
        s_ref[...] = p
        st_ref[0] = p_own
        st_ref[1] = jnp.maximum(jnp.sum(jnp.sum(p, axis=0), axis=-1, keepdims=True) + p_own, TINY)

    def v_compute(c, slot):
        acc = acc_ref[...]
        for i in range(chunk):
            acc = acc + buf[slot, i] * _expand_heads(s_ref[c * chunk + i], gw)
        acc_ref[...] = acc

    def step(idx, _):
        slot = idx % 2
        nxt = idx + 1

        @pl.when(nxt < 2 * n)
        def _():
            start(nxt // n, nxt % n, 1 - slot)

        wait(idx // n, idx % n, slot)

        @pl.when(idx < n)
        def _():
            k_compute(idx, slot)

        @pl.when(idx == n - 1)
        def _():
            select()

        @pl.when(idx >= n)
        def _():
            v_compute(idx - n, slot)

        return 0

    lax.fori_loop(0, 2 * n, step, 0)
    tot = jnp.sum(acc_ref[...], axis=-1, keepdims=True)
    p_own = _expand_heads(st_ref[0], gw)[:, 0:1]
    den = _expand_heads(st_ref[1], gw)[:, 0:1]
    o_ref[0] = (tot + p_own * vn_ref[0]) / den


def _moba_sample(page_table, pool_t, layer, q, k_new, v_new):
    bsz, n_pages = page_table.shape
    gw = MOBA_W
    chunk = math.gcd(n_pages, 16)
    nq = -(-MOBA_H // 8) * 8
    qbd = jnp.einsum("bgd,gh->bghd", q.reshape(bsz, MOBA_H, HD), jnp.eye(MOBA_H, dtype=F32)).reshape(bsz, MOBA_H, gw)
    qbd = jnp.pad(qbd, ((0, 0), (0, nq - MOBA_H), (0, 0))).astype(BF16)
    col = pl.BlockSpec((1, gw, 1), lambda i, pt: (i, 0, 0))
    grid_spec = pltpu.PrefetchScalarGridSpec(
        num_scalar_prefetch=1,
        grid=(bsz,),
        in_specs=[pl.BlockSpec(memory_space=pl.ANY), col, pl.BlockSpec((1, nq, gw), lambda i, pt: (i, 0, 0)), col, col],
        out_specs=col,
        scratch_shapes=[
            pltpu.VMEM((2, chunk, gw, PAGE), F32),
            pltpu.SemaphoreType.DMA((2,)),
            pltpu.VMEM((n_pages, MOBA_H, PAGE), F32),
            pltpu.VMEM((gw, PAGE), F32),
            pltpu.VMEM((2, MOBA_H, PAGE), F32),
        ],
    )
    out = pl.pallas_call(
        functools.partial(_moba_sample_body, layer=layer, chunk=chunk),
        grid_spec=grid_spec,
        out_shape=jax.ShapeDtypeStruct((bsz, gw, 1), F32),
        compiler_params=_params("arbitrary"),
        name="moba_sample",
    )(page_table, pool_t, q.reshape(bsz, gw, 1), qbd, k_new.reshape(bsz, gw, 1), v_new.reshape(bsz, gw, 1))
    return out.reshape(bsz, gw)


CMP_DB = 8


def _compress_weights_t(w1, w2, pe):
    per = PAGE // NSA_BLOCK
    eye = jnp.eye(per, dtype=F32)
    w1r = w1.astype(F32).reshape(2, NSA_BLOCK, HD, HD)
    wd = jnp.einsum("klde,gh->kdglhe", w1r, eye).reshape(2, HD // CMP_DB, CMP_DB * PAGE, per * HD)
    ped = jnp.tile(pe.astype(F32).transpose(0, 2, 1), (1, 1, per)).reshape(2, HD // CMP_DB, 1, CMP_DB * PAGE)
    w2t = jnp.einsum("kde,gh->kgdhe", w2.astype(F32), eye).reshape(2, per * HD, per * HD)
    return wd.astype(BF16), ped, w2t.astype(BF16)


def _nsa_cmp_sample_body(pt_ref, pool_ref, q_ref, wd_ref, ped_ref, w2_ref, oc_ref, sel_ref, buf, sem, tok_ref, acc_ref,
                         *, layer, nseq):
    b = pl.program_id(0)
    j = pl.program_id(1)
    bsz, n_pages = pt_ref.shape
    t = b * 2 + j
    slot = t % 2
    m_rows = nseq * NSA_KVH * n_pages

    def copy(grp, kv, sq, p, g, sl):
        row = pl.multiple_of((((sl * nseq + sq) * NSA_KVH + g) * n_pages + p) * HD, HD)
        return pltpu.make_async_copy(pool_ref.at[pt_ref[grp * nseq + sq, p], layer, kv, g], buf.at[pl.ds(row, HD)],
                                     sem.at[sl])

    def each(fn):
        def run(grp, kv, sl):
            def page(p, _):
                for sq in range(nseq):
                    for g in range(NSA_KVH):
                        fn(copy(grp, kv, sq, p, g, sl))
                return 0
            lax.fori_loop(0, n_pages, page, 0)
        return run

    start = each(lambda c: c.start())
    wait = each(lambda c: c.wait())

    @pl.when(t == 0)
    def _():
        start(0, 0, 0)

    @pl.when(t + 1 < 2 * (bsz // nseq))
    def _():
        start((t + 1) // 2, (t + 1) % 2, 1 - slot)

    wait(b, j, slot)

    base = slot * m_rows * HD
    for dd in range(HD // CMP_DB):
        parts = [buf[pl.ds(base + dd * CMP_DB + u, m_rows, stride=HD), :] for u in range(CMP_DB)]
        a = jnp.concatenate(parts, axis=1) + ped_ref[0, dd]
        part = _dot(a.astype(BF16), wd_ref[0, dd])
        if dd == 0:
            acc_ref[...] = part
        else:
            acc_ref[...] += part
    tok_ref[j] = _dot(jax.nn.gelu(acc_ref[...]).astype(BF16), w2_ref[0])

    @pl.when(j == 1)
    def _():
        pos = n_pages * PAGE
        per = PAGE // NSA_BLOCK
        n_idx = (lax.broadcasted_iota(jnp.int32, (per, n_pages), 1) * per
                 + lax.broadcasted_iota(jnp.int32, (per, n_pages), 0))
        complete = n_idx * NSA_BLOCK + (NSA_BLOCK - 1) <= pos
        cur = pos // NSA_BLOCK
        forced = (n_idx == 0) | (n_idx == cur) | (n_idx == cur - 1)
        k_past = min(NSA_TOPN, per * n_pages + 1) - 1
        for sq, g in ((a, c) for a in range(nseq) for c in range(NSA_KVH)):
            qg = q_ref[sq, g * NSA_REP:(g + 1) * NSA_REP, :].astype(BF16)
            r0 = (sq * NSA_KVH + g) * n_pages
            ck = tok_ref[0, r0:r0 + n_pages, :].astype(BF16)
            cv = tok_ref[1, r0:r0 + n_pages, :].astype(BF16)
            s = [jnp.where(complete[h:h + 1], _dot_nt(qg, ck[:, h * HD:(h + 1) * HD]) * SCALE, NEG) for h in range(per)]
            m = functools.reduce(jnp.maximum, [jnp.max(x, axis=-1, keepdims=True) for x in s])
            e = [jnp.where(complete[h:h + 1], jnp.exp(s[h] - m), 0.0) for h in range(per)]
            den = jnp.maximum(sum(jnp.sum(x, axis=-1, keepdims=True) for x in e), TINY)
            pc = [x / den for x in e]
            oc_ref[sq, g * NSA_REP:(g + 1) * NSA_REP, :] = sum(
                _dot(pc[h].astype(BF16), cv[:, h * HD:(h + 1) * HD]) for h in range(per))
            imp = jnp.concatenate([jnp.sum(x, axis=0, keepdims=True) for x in pc], axis=0)
            score = jnp.where(forced, FORCE, jnp.where(complete, imp, NEG))
            sel = jnp.zeros(score.shape, F32)
            work = score
            for _ in range(k_past):
                mx = jnp.max(jnp.max(work, axis=-1, keepdims=True), axis=0, keepdims=True)
                cand = jnp.where(work == mx, n_idx, per * n_pages)
                first = jnp.min(jnp.min(cand, axis=-1, keepdims=True), axis=0, keepdims=True)
                pick = n_idx == first
                sel = jnp.where(pick, 1.0, sel)
                work = jnp.where(pick, -jnp.inf, work)
            sel_ref[sq, g] = jnp.where(score > 0.5 * NEG, sel, 0.0)


def _nsa_cmp_sample(page_table, pool_t, layer, q, wd, ped, w2b):
    bsz, n_pages = page_table.shape
    per = PAGE // NSA_BLOCK
    nseq = 2 if bsz % 2 == 0 else 1
    m_rows = nseq * NSA_KVH * n_pages
    grid_spec = pltpu.PrefetchScalarGridSpec(
        num_scalar_prefetch=1,
        grid=(bsz // nseq, 2),
        in_specs=[
            pl.BlockSpec(memory_space=pl.ANY),
            pl.BlockSpec((nseq, NSA_H, HD), lambda i, j, pt: (i, 0, 0)),
            pl.BlockSpec((1, HD // CMP_DB, CMP_DB * PAGE, per * HD), lambda i, j, pt: (j, 0, 0, 0)),
            pl.BlockSpec((1, HD // CMP_DB, 1, CMP_DB * PAGE), lambda i, j, pt: (j, 0, 0, 0)),
            pl.BlockSpec((1, per * HD, per * HD), lambda i, j, pt: (j, 0, 0)),
        ],
        out_specs=[pl.BlockSpec((nseq, NSA_H, HD), lambda i, j, pt: (i, 0, 0)),
                   pl.BlockSpec((nseq, NSA_KVH, per, n_pages), lambda i, j, pt: (i, 0, 0, 0))],
        scratch_shapes=[
            pltpu.VMEM((2 * m_rows * HD, PAGE), F32),
            pltpu.SemaphoreType.DMA((2,)),
            pltpu.VMEM((2, m_rows, per * HD), F32),
            pltpu.VMEM((m_rows, per * HD), F32),
        ],
    )
    return pl.pallas_call(
        functools.partial(_nsa_cmp_sample_body, layer=layer, nseq=nseq),
        grid_spec=grid_spec,
        out_shape=[jax.ShapeDtypeStruct((bsz, NSA_H, HD), F32),
                   jax.ShapeDtypeStruct((bsz, NSA_KVH, per, n_pages), F32)],
        compiler_params=_params("arbitrary", "arbitrary"),
        name="nsa_cmp_sample",
    )(page_table, pool_t, q, wd, ped, w2b)


def _nsa_sel_sample_body(pt_ref, idx_ref, ok_ref, pool_ref, q_ref, kn_ref, vn_ref, o_ref, buf, sem, *, layer, ksel):
    b = pl.program_id(0)
    bsz = pt_ref.shape[0]
    slot = b % 2
    per = PAGE // NSA_BLOCK

    def copy(bb, kv, g, j, sl):
        page = pt_ref[bb, idx_ref[bb, g * ksel + j] // per]
        return pltpu.make_async_copy(pool_ref.at[page, layer, kv, g], buf.at[sl, kv, g, j], sem.at[sl])

    def each(fn):
        def run(bb, sl):
            def block(j, _):
                for kv in range(2):
                    for g in range(NSA_KVH):
                        fn(copy(bb, kv, g, j, sl))
                return 0
            lax.fori_loop(0, ksel, block, 0)
        return run

    start = each(lambda c: c.start())
    wait = each(lambda c: c.wait())

    @pl.when(b == 0)
    def _():
        start(0, 0)

    @pl.when(b + 1 < bsz)
    def _():
        start(b + 1, 1 - slot)

    wait(b, slot)
    lane_half = lax.broadcasted_iota(jnp.int32, (NSA_REP, PAGE), 1) // NSA_BLOCK
    for g in range(NSA_KVH):
        qg = q_ref[0, g * NSA_REP:(g + 1) * NSA_REP, :] * SCALE
        qb = qg.astype(BF16)
        s = []
        for j in range(ksel):
            blk = idx_ref[b, g * ksel + j]
            valid = (lane_half == blk % per) & (ok_ref[b, g * ksel + j] > 0)
            s.append(jnp.where(valid, _dot(qb, buf[slot, 0, g, j].astype(BF16)), NEG))
        s_own = jnp.sum(qg * kn_ref[0, g:g + 1, :], axis=-1, keepdims=True)
        m = functools.reduce(jnp.maximum, [jnp.max(x, axis=-1, keepdims=True) for x in s] + [s_own])
        p = [jnp.exp(x - m) for x in s]
        p_own = jnp.exp(s_own - m)
        den = functools.reduce(jnp.add, [jnp.sum(x, axis=-1, keepdims=True) for x in p]) + p_own
        o = functools.reduce(jnp.add, [_dot_nt(p[j].astype(BF16), buf[slot, 1, g, j].astype(BF16))
                                       for j in range(ksel)])
        o_ref[0, g * NSA_REP:(g + 1) * NSA_REP, :] = (o + p_own * vn_ref[0, g:g + 1, :]) / jnp.maximum(den, TINY)


def _nsa_sel_sample(page_table, pool_t, layer, q, k_new, v_new, sel):
    bsz, n_pages = page_table.shape
    per = PAGE // NSA_BLOCK
    ksel = min(NSA_TOPN, per * n_pages + 1) - 1
    flat = sel.transpose(0, 1, 3, 2).reshape(bsz, NSA_KVH, n_pages * per)
    vals, idx = lax.top_k(flat, ksel)
    idx = idx.reshape(bsz, NSA_KVH * ksel).astype(jnp.int32)
    ok = (vals > 0.5).reshape(bsz, NSA_KVH * ksel).astype(jnp.int32)
    grid_spec = pltpu.PrefetchScalarGridSpec(
        num_scalar_prefetch=3,
        grid=(bsz,),
        in_specs=[
            pl.BlockSpec(memory_space=pl.ANY),
            pl.BlockSpec((1, NSA_H, HD), lambda i, *_: (i, 0, 0)),
            pl.BlockSpec((1, NSA_KVH, HD), lambda i, *_: (i, 0, 0)),
            pl.BlockSpec((1, NSA_KVH, HD), lambda i, *_: (i, 0, 0)),
        ],
        out_specs=pl.BlockSpec((1, NSA_H, HD), lambda i, *_: (i, 0, 0)),
        scratch_shapes=[pltpu.VMEM((2, 2, NSA_KVH, ksel, HD, PAGE), F32), pltpu.SemaphoreType.DMA((2,))],
    )
    return pl.pallas_call(
        functools.partial(_nsa_sel_sample_body, layer=layer, ksel=ksel),
        grid_spec=grid_spec,
        out_shape=jax.ShapeDtypeStruct((bsz, NSA_H, HD), F32),
        compiler_params=_params("arbitrary"),
        name="nsa_sel_sample",
    )(page_table, idx, ok, pool_t, q, k_new, v_new)


def _win_sample_body(qc_ref, kt_ref, vt_ref, kn_ref, vn_ref, o_ref):
    kt = kt_ref[0, 0]
    vt = vt_ref[0, 0]
    gw, lbuf = kt.shape
    j = lax.broadcasted_iota(jnp.int32, (NSA_KVH, lbuf), 1)
    mask = j > lbuf - NSA_WINDOW
    for r in range(NSA_REP):
        qcol = qc_ref[0, :, r:r + 1]
        s = jnp.where(mask, _head_sums(kt * qcol, NSA_KVH) * SCALE, NEG)
        s_own = _head_sums(jnp.broadcast_to(kn_ref[0] * qcol, (gw, lbuf)), NSA_KVH) * SCALE
        m = jnp.maximum(jnp.max(s, axis=-1, keepdims=True), s_own)
        p = jnp.where(mask, jnp.exp(s - m), 0.0)
        p_own = jnp.exp(s_own - m)
        den = jnp.maximum(jnp.sum(p, axis=-1, keepdims=True) + p_own, TINY)
        tot = jnp.sum(vt * _expand_heads(p, gw), axis=-1, keepdims=True)
        o_ref[0, :, r:r + 1] = (tot + _expand_heads(p_own, gw)[:, 0:1] * vn_ref[0]) / _expand_heads(den, gw)[:, 0:1]


def _win_sample(q_cols, wbuf_t, k_new, v_new):
    bsz, gw, reps = q_cols.shape
    lbuf = wbuf_t.shape[-1]
    col = pl.BlockSpec((1, gw, 1), lambda i: (i, 0, 0))
    return pl.pallas_call(
        _win_sample_body,
        grid=(bsz,),
        in_specs=[pl.BlockSpec((1, gw, reps), lambda i: (i, 0, 0)),
                  pl.BlockSpec((1, 1, gw, lbuf), lambda i: (i, 0, 0, 0)),
                  pl.BlockSpec((1, 1, gw, lbuf), lambda i: (i, 1, 0, 0)), col, col],
        out_specs=pl.BlockSpec((1, gw, reps), lambda i: (i, 0, 0)),
        out_shape=jax.ShapeDtypeStruct((bsz, gw, reps), F32),
        compiler_params=_params("parallel"),
        name="nsa_win_sample",
    )(q_cols, wbuf_t, wbuf_t, k_new.reshape(bsz, gw, 1), v_new.reshape(bsz, gw, 1))


def _gate_combine_body(g_ref, c_ref, s_ref, w_ref, o_ref):
    gt = jax.nn.sigmoid(g_ref[...])
    o_ref[...] = gt[0] * c_ref[...] + gt[1] * s_ref[...] + gt[2] * w_ref[...]


def _gate_combine(gate_logits, o_c, o_s, o_w):
    return pl.pallas_call(
        _gate_combine_body,
        out_shape=jax.ShapeDtypeStruct(o_c.shape, F32),
        name="nsa_gate_combine",
    )(gate_logits, o_c, o_s, o_w)


ODD_NQ, ODD_NQR, ODD_KV = 0, NSA_QW, 2 * NSA_QW
ODD_HQ = ODD_KV + 6 * NSA_KVW
ODD_EQ = ODD_HQ + 3 * HGRN_W
ODD_GATE = ODD_EQ + MEM_W
ODD_NG = ODD_GATE + MIX_W
ODD_N = 4096


def _odd_weights(w_in):
    offs = np.cumsum([0, NSA_QW] + [NSA_KVW] * 6 + [3 * NSA_H] + [HGRN_W] * 3 + [MEM_W, MIX_W])
    nq = w_in[:, offs[0]:offs[1]]
    kv = w_in[:, offs[1]:offs[7]]
    ng = w_in[:, offs[7]:offs[8]]
    rest = w_in[:, offs[8]:]
    pad = jnp.zeros((w_in.shape[0], ODD_N - ODD_NG - 3 * NSA_H), w_in.dtype)
    return jnp.concatenate([nq, nq, kv, rest, ng, pad], axis=1)


def _stack_heads(a, tq, width):
    t = a.shape[0]
    a = a.reshape(t // tq, tq, NSA_KVH, NSA_REP, width).transpose(2, 0, 3, 1, 4)
    return a.reshape(NSA_KVH, t // tq, NSA_REP * tq, width)


def _unstack_heads(a, tq):
    g, nt, _, width = a.shape
    a = a.reshape(g, nt, NSA_REP, tq, width).transpose(1, 3, 0, 2, 4)
    return a.reshape(nt * tq, g * NSA_REP * width)


def _group_kv(k, v):
    t = k.shape[0]
    return jnp.concatenate([k.reshape(t, NSA_KVH, HD), v.reshape(t, NSA_KVH, HD)], axis=-1).transpose(1, 0, 2)


def _odd_prompt(x, cos, sin, g, w_aug_bf16, w_o_bf16, mem_k, mem_v, lb, cmp_w, final_g, tm, tq, kc):
    t = x.shape[0]
    assert t % NSA_BLOCK == 0
    kvo = lambda j: ODD_KV + j * NSA_KVW
    nq, nqr, kv6, qfi, eq, gate, ng = _norm_proj(
        x, g, w_aug_bf16, cos, sin, ((ODD_NQR, ODD_KV), (kvo(2), kvo(3)), (kvo(4), kvo(5))), tm,
        splits=(NSA_QW, NSA_QW, 6 * NSA_KVW, 3 * HGRN_W, MEM_W, MIX_W, LANES))
    ck, cv, sk, sv, wk, wv = (kv6[:, j * NSA_KVW:(j + 1) * NSA_KVW] for j in range(6))
    w1b, w2b, peb = cmp_w
    nb = t // NSA_BLOCK
    cmp_tok = _compress(jnp.stack([ck, cv]).reshape(2, nb, NSA_BLOCK * NSA_KVW), w1b, w2b, peb)
    ckv = _group_kv(cmp_tok[0], cmp_tok[1])
    q_st = _stack_heads(nq, tq, HD).astype(BF16)
    qr_st = _stack_heads(nqr, tq, HD).astype(BF16)
    gt_st = _stack_heads(ng[:, :3 * NSA_H], tq, 3)
    onehot = ((jnp.arange(t)[:, None] // NSA_BLOCK) % HD == jnp.arange(HD)[None, :]).astype(F32)
    ska = jnp.concatenate([sk.reshape(t, NSA_KVH, HD), jnp.broadcast_to(onehot[:, None], (t, NSA_KVH, HD))], axis=-1)
    svo = jnp.concatenate([jnp.ones((t, NSA_KVH, HD), F32), sv.reshape(t, NSA_KVH, HD)], axis=-1)
    _, sk_norm2 = _block_stats(kv6, math.gcd(t, MOBA_BLOCK), NSA_KVW, 2)
    o_nsa = _nsa_prompt(q_st, qr_st, gt_st, ckv, ska.transpose(1, 0, 2).astype(BF16),
                        svo.transpose(1, 0, 2).astype(BF16), _group_kv(wk, wv).astype(BF16),
                        _key_norm_bound(sk_norm2, NSA_KVH), tq, kc)
    o_nsa = _unstack_heads(o_nsa, tq)
    o_hg, s_hg = _hgrn_prompt(qfi, lb.reshape(1, HGRN_W))
    o_mem = _mem_prompt(eq, mem_k, mem_v, tm)
    y = _mix_out(x, o_nsa, o_hg, o_mem, gate, w_o_bf16, final_g, tm)
    rows = lambda a, b: jnp.stack([a.reshape(t, NSA_KVH, HD), b.reshape(t, NSA_KVH, HD)], axis=1)
    return y, rows(ck, cv), rows(sk, sv), rows(wk, wv), _diag_blocks(s_hg, HGRN_H)

def _heads_major(a, h):
    t = a.shape[0]
    return a.reshape(t, h, HD).transpose(1, 0, 2)


def _even_prompt(x, cos, sin, g, w_in_bf16, w_o_bf16, mem_k, mem_v, tm):
    t = x.shape[0]
    o_mq = 3 * RET_W
    o_eq = 3 * RET_W + 3 * MOBA_W
    o_gate = o_eq + MEM_W
    ret, moba, eq, gate = _norm_proj(x, g, w_in_bf16, cos, sin, ((0, 2 * RET_W), (o_mq, o_mq + 2 * MOBA_W)), tm,
                                     splits=(3 * RET_W, 3 * MOBA_W, MEM_W, MIX_W))
    mq, mk, mv = moba[:, :MOBA_W], moba[:, MOBA_W:2 * MOBA_W], moba[:, 2 * MOBA_W:]
    o_ret, s_ret = _retention_prompt(ret)
    kmean, knorm2 = _block_stats(moba, MOBA_BLOCK, MOBA_W, 1)
    q_pad = _heads_major(mq, MOBA_H)
    vh = _heads_major(mv, MOBA_H)
    nblk = t // MOBA_BLOCK
    onehot = (jnp.arange(t)[:, None] // MOBA_BLOCK == jnp.arange(HD)[None, :]).astype(F32)
    ka = jnp.concatenate([_heads_major(mk, MOBA_H), jnp.broadcast_to(onehot, (MOBA_H, t, HD))], axis=-1).astype(BF16)
    vo = jnp.concatenate([jnp.ones_like(vh), vh], axis=-1).astype(BF16)
    km = jnp.pad(_heads_major(kmean, MOBA_H), ((0, 0), (HD, LANES - HD - nblk), (0, HD)))
    o_moba = _moba_prompt(q_pad, ka, vo, km, _key_norm_bound(knorm2, MOBA_H))
    o_moba = o_moba.transpose(1, 0, 2).reshape(t, MOBA_W)
    o_mem = _mem_prompt(eq, mem_k, mem_v, tm)
    y = _mix_out(x, o_ret, o_moba, o_mem, gate, w_o_bf16, None, tm)
    rows = moba[:, MOBA_W:].reshape(t, 2, MOBA_H, HD)
    return y, rows, _diag_blocks(s_ret, RET_H)


def _pages_t(pool, width):
    n_pool, n_layer = pool.shape[0], pool.shape[1]
    return pool.transpose(0, 1, 3, 4, 5, 2).reshape(n_pool, n_layer, 2, width, PAGE)


def _mem_t(cache):
    bsz, n = cache.shape[0], cache.shape[1]
    return cache.transpose(0, 2, 3, 4, 1).reshape(bsz, 2, MEM_W, n)


def _even_sample(x, cos, sin, g, w_in_bf16, w_o_bf16, mem_cache, state, page_table, pool, layer):
    bsz = x.shape[0]
    o_mq = 3 * RET_W
    o_eq = 3 * RET_W + 3 * MOBA_W
    o_gate = o_eq + MEM_W
    proj = _norm_proj(x, g, w_in_bf16, cos, sin, ((0, 2 * RET_W), (o_mq, o_mq + 2 * MOBA_W)), bsz)
    rq, rk, rv = proj[:, :RET_W], proj[:, RET_W:2 * RET_W], proj[:, 2 * RET_W:3 * RET_W]
    mq = proj[:, o_mq:o_mq + MOBA_W]
    mk = proj[:, o_mq + MOBA_W:o_mq + 2 * MOBA_W]
    mv = proj[:, o_mq + 2 * MOBA_W:o_eq]
    gamma = np.repeat(1.0 - np.power(2.0, -5.0 - np.arange(RET_H, dtype=np.float64)), HD)
    o_ret, s_ret = _state_step(state, rq, rk, jnp.broadcast_to(jnp.asarray(gamma, F32), (bsz, RET_W)), rv, "ret")
    o_moba = _moba_sample(page_table, _pages_t(pool, MOBA_W), layer, mq, mk, mv)
    o_mem = _mem_sample(proj[:, o_eq:o_gate], _mem_t(mem_cache))
    y = _mix_out(x, o_ret, o_moba, o_mem, proj[:, o_gate:], w_o_bf16, None, bsz)
    rows = proj[:, o_mq + MOBA_W:o_eq].reshape(bsz, 1, 2, MOBA_H, HD)
    return y, rows, s_ret


def _cols(a):
    bsz = a.shape[0]
    return a.reshape(bsz, NSA_KVH, NSA_REP, HD).transpose(0, 1, 3, 2).reshape(bsz, NSA_KVW, NSA_REP)


def _uncols(a):
    bsz = a.shape[0]
    return a.reshape(bsz, NSA_KVH, HD, NSA_REP).transpose(0, 1, 3, 2).reshape(bsz, NSA_QW)


def _odd_sample(x, cos, sin, g, w_aug_bf16, w_o_bf16, mem_cache, state, lb, page_table, cmp_pool, slc_pool, wbuf,
                layer, cmp_w_t, final_g):
    bsz = x.shape[0]
    lbuf = wbuf.shape[1]
    kvo = lambda j: ODD_KV + j * NSA_KVW
    proj = _norm_proj(x, g, w_aug_bf16, cos, sin, ((ODD_NQR, ODD_KV), (kvo(2), kvo(3)), (kvo(4), kvo(5))), bsz)
    ck, cv, sk, sv, wk, wv = (proj[:, kvo(j):kvo(j + 1)] for j in range(6))
    wd, ped, w2t = cmp_w_t
    n_pool, n_layer = cmp_pool.shape[0], cmp_pool.shape[1]
    cmp_t = cmp_pool.transpose(0, 1, 3, 4, 5, 2)
    o_c, sel = _nsa_cmp_sample(page_table, cmp_t, layer, proj[:, ODD_NQ:ODD_NQ + NSA_QW].reshape(bsz, NSA_H, HD),
                               wd, ped, w2t)
    qr = proj[:, ODD_NQR:ODD_NQR + NSA_QW]
    q_cols = _cols(qr)
    o_s = _nsa_sel_sample(page_table, slc_pool.transpose(0, 1, 3, 4, 5, 2), layer, qr.reshape(bsz, NSA_H, HD),
                          sk.reshape(bsz, NSA_KVH, HD), sv.reshape(bsz, NSA_KVH, HD), sel).reshape(bsz, NSA_QW)
    wbuf_t = wbuf.transpose(0, 2, 3, 4, 1).reshape(bsz, 2, NSA_KVW, lbuf)
    o_w = _win_sample(q_cols, wbuf_t, wk, wv)
    ng = proj[:, ODD_NG:ODD_NG + 3 * NSA_H].reshape(bsz, NSA_H, 3)
    gate_logits = jnp.repeat(ng.transpose(2, 0, 1), HD, axis=-1)
    o_nsa = _gate_combine(gate_logits, o_c.reshape(bsz, NSA_QW), o_s, _uncols(o_w))
    hq, hf, hi = (proj[:, ODD_HQ + j * HGRN_W:ODD_HQ + (j + 1) * HGRN_W] for j in range(3))
    o_hg, s_hg = _state_step(state, hq, hf, jnp.broadcast_to(lb.reshape(1, HGRN_W), (bsz, HGRN_W)), hi, "hgrn")
    o_mem = _mem_sample(proj[:, ODD_EQ:ODD_GATE], _mem_t(mem_cache))
    y = _mix_out(x, o_nsa, o_hg, o_mem, proj[:, ODD_GATE:ODD_NG], w_o_bf16, final_g, bsz)
    rows = lambda a, b: jnp.stack([a.reshape(bsz, 1, NSA_KVH, HD), b.reshape(bsz, 1, NSA_KVH, HD)], axis=2)
    win = jnp.concatenate([wbuf.astype(F32), rows(wk, wv)], axis=1)[:, -lbuf:]
    return y, rows(ck, cv), rows(sk, sv), win, s_hg


def kernel(x_prompt, x_sample, mem_prompt, cache_moba_kv, state_ret, cache_nsa_cmp_kv, cache_nsa_slc_kv,
           cache_nsa_win_kv, state_hgrn, cache_mem_kv, page_table, norm_g, mem_norm_g, w_mem_kv, w_in_even,
           w_in_odd, w_out, cmp_w1, cmp_w2, cmp_pe, hgrn_lb_logits, final_g):
    bp, tp, d = x_prompt.shape
    bs, ts, _ = x_sample.shape
    depth = w_out.shape[0]
    assert bp == 1 and ts == 1 and depth == 2
    n_mem = mem_prompt.shape[1]
    past_len = page_table.shape[1] * PAGE
    assert past_len % MOBA_BLOCK == 0 and cache_moba_kv.shape[2] == PAGE
    tm, tq, kc = 256, 128, 256
    xp, xs, mem = x_prompt[0], x_sample[:, 0], mem_prompt[0]
    cos_p, sin_p = _rope_tables(jnp.arange(tp, dtype=jnp.int32))
    cos_s, sin_s = (jnp.broadcast_to(a, (bs, LANES)) for a in _rope_tables(jnp.full((1,), past_len, jnp.int32)))
    lb_prob = jax.nn.softmax(hgrn_lb_logits.astype(F32), axis=0)
    lb_all = jnp.cumsum(lb_prob, axis=0) - lb_prob[0]
    lw_p = min(NSA_WINDOW, tp)

    def mem_kv(layer):
        kv = _norm_proj(mem, mem_norm_g[layer], w_mem_kv[layer].astype(BF16), cos_p[:n_mem], sin_p[:n_mem], (), n_mem)
        return kv.reshape(n_mem, 2, MEM_W)

    w_in0 = w_in_even[0].astype(BF16)
    w_o0 = w_out[0].astype(BF16)
    mkv0 = mem_kv(0)
    hp, moba_p, ret_p = _even_prompt(xp, cos_p, sin_p, norm_g[0], w_in0, w_o0, mkv0[:, 0], mkv0[:, 1], tm)
    hs, moba_s, ret_s = _even_sample(xs, cos_s, sin_s, norm_g[0], w_in0, w_o0, cache_mem_kv[0], state_ret[0],
                                     page_table, cache_moba_kv, 0)

    w_in1 = _odd_weights(w_in_odd[0]).astype(BF16)
    w_o1 = w_out[1].astype(BF16)
    mkv1 = mem_kv(1)
    cmp_w = _compress_weights(cmp_w1[0], cmp_w2[0], cmp_pe[0])
    cmp_w_t = _compress_weights_t(cmp_w1[0], cmp_w2[0], cmp_pe[0])
    yp, cmp_p, slc_p, win_p, hg_p = _odd_prompt(hp, cos_p, sin_p, norm_g[1], w_in1, w_o1, mkv1[:, 0], mkv1[:, 1],
                                                lb_all[1], cmp_w, final_g, tm, tq, kc)
    ys, cmp_s, slc_s, win_s, hg_s = _odd_sample(hs, cos_s, sin_s, norm_g[1], w_in1, w_o1, cache_mem_kv[1],
                                                state_hgrn[0], lb_all[1], page_table, cache_nsa_cmp_kv,
                                                cache_nsa_slc_kv, cache_nsa_win_kv[0], 0, cmp_w_t, final_g)

    return (yp[None], ys[:, None], moba_p[None, None], moba_s[:, None], ret_p[None, None], ret_s[None],
            cmp_p[None, None], cmp_s[:, None], slc_p[None, None], slc_s[:, None], win_p[None, tp - lw_p:][None],
            win_s[None], hg_p[None, None], hg_s[None], jnp.stack([mkv0, mkv1]).reshape(depth, 1, n_mem, 2, MEM_H, HD))
```

```python
import functools
import math

import jax
import jax.numpy as jnp
import numpy as np
from jax import lax
from jax.experimental import pallas as pl
from jax.experimental.pallas import tpu as pltpu

F32 = jnp.float32
BF16 = jnp.bfloat16
HIGHEST = lax.Precision.HIGHEST

HD = 64
RET_H, MOBA_H, MEM_H, NSA_H, NSA_KVH, HGRN_H = 6, 6, 4, 8, 2, 4
NSA_REP = NSA_H // NSA_KVH
RET_W, MOBA_W, MEM_W = RET_H * HD, MOBA_H * HD, MEM_H * HD
NSA_QW, NSA_KVW, HGRN_W = NSA_H * HD, NSA_KVH * HD, HGRN_H * HD
MIX_W = RET_W + MOBA_W + MEM_W
PAGE = 128
RET_CHUNK, HGRN_CHUNK = 128, 64
MOBA_BLOCK, MOBA_TOPK = 256, 3
NSA_BLOCK, NSA_TOPN, NSA_WINDOW = 64, 16, 512
ROPE_THETA = 10000.0
EPS = 1e-6
NEG = -1e30
FORCE = 1e30
TINY = 1e-30
SCALE = HD ** -0.5
LANES = 128
SHIFT_SLACK = 1.02
SHIFT_LIMIT = 20.0
VMEM_LIMIT = 56 * 1024 * 1024


def _params(*sem):
    return pltpu.CompilerParams(dimension_semantics=sem, vmem_limit_bytes=VMEM_LIMIT)


def _dot(a, b, precision=None):
    return jnp.dot(a, b, preferred_element_type=F32, precision=precision)


def _dot_nt(a, b, precision=None):
    return lax.dot_general(a, b, (((1,), (1,)), ((), ())), preferred_element_type=F32, precision=precision)


def _dot_tn(a, b, precision=None):
    return lax.dot_general(a, b, (((0,), (0,)), ((), ())), preferred_element_type=F32, precision=precision)


def _block_ones(width, value=1.0):
    r = lax.broadcasted_iota(jnp.int32, (width, width), 0) // HD
    c = lax.broadcasted_iota(jnp.int32, (width, width), 1) // HD
    return jnp.where(r == c, value, 0.0).astype(F32)


def _rope_tile(a, cos, sin, first_half):
    rot = jnp.where(first_half, pltpu.roll(a, LANES - HD // 2, 1), pltpu.roll(a, HD // 2, 1))
    return a * cos + rot * sin


def _norm_proj_body(x_ref, g_ref, w_ref, cos_ref, sin_ref, *o_refs, rope_tiles, n_chunk, starts):
    x = x_ref[...]
    ms = jnp.mean(x * x, axis=-1, keepdims=True)
    y = (x * lax.rsqrt(ms + EPS) * g_ref[...]).astype(BF16)
    n = w_ref.shape[1]
    lane = lax.broadcasted_iota(jnp.int32, (x.shape[0], LANES), 1)
    first_half = (lane % HD) < HD // 2
    for c0 in range(0, n, n_chunk):
        acc = _dot(y, w_ref[:, c0:c0 + n_chunk])
        for j in range(n_chunk // LANES):
            tile = (c0 // LANES) + j
            which = max(i for i, s in enumerate(starts) if s <= tile * LANES)
            local = tile * LANES - starts[which]
            if local >= o_refs[which].shape[1]:
                continue
            a = acc[:, j * LANES:(j + 1) * LANES]
            if tile in rope_tiles:
                a = _rope_tile(a, cos_ref[...], sin_ref[...], first_half)
            o_refs[which][:, local:local + LANES] = a


def _norm_proj(x, g, w_bf16, cos, sin, rope_cols, tm, splits=None):
    m, d = x.shape
    n = w_bf16.shape[1]
    splits = (n,) if splits is None else tuple(splits)
    assert m % tm == 0 and n % LANES == 0 and all(s % LANES == 0 for s in splits) and sum(splits) <= n
    starts = tuple(int(s) for s in np.cumsum((0,) + splits[:-1]))
    tiles = n // LANES
    k = next(c for c in (4, 3, 2, 1) if tiles % c == 0)
    rope_tiles = frozenset(t for a, b in rope_cols for t in range(a // LANES, b // LANES))
    body = functools.partial(_norm_proj_body, rope_tiles=rope_tiles, n_chunk=k * LANES, starts=starts)
    outs = pl.pallas_call(
        body,
        grid=(m // tm,),
        in_specs=[
            pl.BlockSpec((tm, d), lambda i: (i, 0)),
            pl.BlockSpec((1, d), lambda i: (0, 0)),
            pl.BlockSpec((d, n), lambda i: (0, 0)),
            pl.BlockSpec((tm, LANES), lambda i: (i, 0)),
            pl.BlockSpec((tm, LANES), lambda i: (i, 0)),
        ],
        out_specs=[pl.BlockSpec((tm, s), lambda i: (i, 0)) for s in splits],
        out_shape=[jax.ShapeDtypeStruct((m, s), F32) for s in splits],
        compiler_params=_params("parallel"),
        name="norm_proj",
    )(x, g.reshape(1, d), w_bf16, cos, sin)
    return outs if len(splits) > 1 else outs[0]


def _rope_tables(pos):
    half = HD // 2
    inv = ROPE_THETA ** (-jnp.arange(half, dtype=F32) / half)
    ang = pos.astype(F32)[:, None] * inv[None, :]
    cos, sin = jnp.cos(ang), jnp.sin(ang)
    return jnp.concatenate([cos, cos, cos, cos], -1), jnp.concatenate([-sin, sin, -sin, sin], -1)


def _retention_consts(c):
    lg = np.log(1.0 - np.power(2.0, -5.0 - np.arange(RET_H, dtype=np.float64)))
    ti = np.arange(c, dtype=np.float64)
    causal = ti[:, None] >= ti[None, :]
    d_in = np.where(causal[None], np.exp(np.where(causal, ti[:, None] - ti[None, :], 0.0)[None] * lg[:, None, None]), 0.0)
    q_dec = np.repeat(np.exp((ti[:, None] + 1.0) * lg[None, :]), HD, axis=1)
    k_dec = np.repeat(np.exp((c - 1.0 - ti)[:, None] * lg[None, :]), HD, axis=1)
    c_dec = np.repeat(np.exp(c * lg), HD)
    head = np.arange(RET_W) // HD
    bd = (head[:, None] == head[None, :]).astype(np.float64)
    cmat = bd * c_dec[:, None]
    f = lambda a: jnp.asarray(a, dtype=F32)
    return f(d_in), f(q_dec), f(k_dec), f(cmat), f(bd)


def _retention_body(q_ref, k_ref, v_ref, din_ref, qdec_ref, kdec_ref, cmat_ref, bd_ref, o_ref, s_ref):
    @pl.when(pl.program_id(0) == 0)
    def _():
        s_ref[...] = jnp.zeros_like(s_ref)

    q = q_ref[...]
    k = k_ref[...] * SCALE
    v = v_ref[...]
    c, w = q.shape
    head = lax.broadcasted_iota(jnp.int32, (c, w), 1) // HD
    s = s_ref[...]
    qb, kb, vb = q.astype(BF16), k.astype(BF16), v.astype(BF16)
    o = _dot(qb, s.astype(BF16)) * qdec_ref[...]
    for h in range(RET_H):
        mh = head == h
        att = _dot_nt(jnp.where(mh, qb, jnp.zeros_like(qb)), kb) * din_ref[h]
        o = o + jnp.where(mh, _dot(att.astype(BF16), vb), 0.0)
    s_ref[...] = s * cmat_ref[...] + _dot_tn((k * kdec_ref[...]).astype(BF16), vb) * bd_ref[...]
    seg = bd_ref[...] * (1.0 / HD)
    xc = o - _dot(o, seg, HIGHEST)
    var = _dot(xc * xc, seg, HIGHEST)
    o_ref[...] = xc * lax.rsqrt(var + EPS)


def _retention_prompt(qkv):
    t, w = qkv.shape[0], RET_W
    c = math.gcd(t, RET_CHUNK)
    d_in, q_dec, k_dec, cmat, bd = _retention_consts(c)
    col = lambda j: pl.BlockSpec((c, w), lambda i: (i, j))
    const2 = lambda shape: pl.BlockSpec(shape, lambda i: (0,) * len(shape))
    return pl.pallas_call(
        _retention_body,
        grid=(t // c,),
        in_specs=[col(0), col(1), col(2), const2((RET_H, c, c)), const2((c, w)), const2((c, w)), const2((w, w)),
                  const2((w, w))],
        out_specs=[col(0), const2((w, w))],
        out_shape=[jax.ShapeDtypeStruct((t, w), F32), jax.ShapeDtypeStruct((w, w), F32)],
        compiler_params=_params("arbitrary"),
        name="retention_prompt",
    )(qkv, qkv, qkv, d_in, q_dec, k_dec, cmat, bd)


def _diag_blocks(s, h):
    s4 = s.reshape(h, HD, h, HD)
    return jnp.stack([s4[i, :, i, :] for i in range(h)], axis=0)


def _block_stats_body(k_ref, mean_ref, norm_ref):
    k = k_ref[...]
    mean_ref[0] = jnp.mean(k, axis=0, keepdims=True)
    n2 = _dot((k * k).astype(BF16), _block_ones(k.shape[1]).astype(BF16))
    norm_ref[0] = jnp.max(n2, axis=0, keepdims=True)


def _block_stats(k, blk, w, col):
    t = k.shape[0]
    assert t % blk == 0
    mean, norm = pl.pallas_call(
        _block_stats_body,
        grid=(t // blk,),
        in_specs=[pl.BlockSpec((blk, w), lambda i: (i, col))],
        out_specs=[pl.BlockSpec((1, 1, w), lambda i: (i, 0, 0))] * 2,
        out_shape=[jax.ShapeDtypeStruct((t // blk, 1, w), F32)] * 2,
        compiler_params=_params("parallel"),
        name="block_stats",
    )(k)
    return mean.reshape(t // blk, w), norm.reshape(t // blk, w)


def _key_norm_bound(norm2, heads):
    kmax = jnp.sqrt(jnp.max(norm2, axis=0)).reshape(heads, HD)[:, :1]
    return jnp.broadcast_to(kmax[:, :, None], (heads, 1, LANES))


def _topk_mask(score, k):
    n = score.shape[-1]
    idx = lax.broadcasted_iota(jnp.int32, score.shape, score.ndim - 1)
    sel = jnp.zeros(score.shape, F32)
    work = score
    for _ in range(k):
        m = jnp.max(work, axis=-1, keepdims=True)
        first = jnp.min(jnp.where(work == m, idx, n), axis=-1, keepdims=True)
        pick = idx == first
        sel = jnp.where(pick, 1.0, sel)
        work = jnp.where(pick, -jnp.inf, work)
    return sel


def _tile_loop(lo, hi, fn, unroll=8):
    def group(g, _):
        for u in range(unroll):
            fn(lo + unroll * g + u)
        return 0
    n = jnp.maximum(hi - lo, 0)
    main = n // unroll
    lax.fori_loop(0, main, group, 0)
    done = lo + main * unroll
    rem = n - main * unroll
    size = unroll // 2
    while size >= 1:
        @pl.when((rem & size) != 0)
        def _(done=done, size=size):
            for u in range(size):
                fn(done + u)
        done = done + (rem & size)
        size //= 2


def _lane_fold_max(s):
    out = s[:, 0:LANES]
    for c in range(1, s.shape[1] // LANES):
        out = jnp.maximum(out, s[:, c * LANES:(c + 1) * LANES])
    return out


def _moba_prompt_body(q_ref, ka_ref, vo_ref, kmean_ref, kmax_ref, o_ref, s_ref, m_ref, acc_ref, *, topk, bq):
    qi = pl.program_id(1)
    q = jnp.concatenate([q_ref[0], jnp.zeros(q_ref.shape[1:], F32)], axis=1)
    rows = q.shape[0]
    kb = MOBA_BLOCK
    gate = _dot_nt(q, kmean_ref[0], HIGHEST)
    blk = lax.broadcasted_iota(jnp.int32, (rows, LANES), 1) - HD
    own = qi * bq + lax.broadcasted_iota(jnp.int32, (rows, LANES), 0) // kb
    past = (blk >= 0) & (blk < own)
    sel = jnp.where(past, _topk_mask(jnp.where(past, gate, NEG), topk), 0.0)
    bias = jnp.where((blk < 0) | (blk == own) | (sel > 0.5), 0.0, NEG)
    qa = (q * SCALE + bias).astype(BF16)
    n_past = qi * bq
    qpos = lax.broadcasted_iota(jnp.int32, (rows, kb), 0)
    kpos = lax.broadcasted_iota(jnp.int32, (rows, kb), 1)
    acc_ref[...] = jnp.zeros(acc_ref.shape, F32)

    def scores(j):
        start = pl.multiple_of(j * kb, kb)
        return _dot_nt(qa, ka_ref[0, pl.ds(start, kb), :])

    def own_scores(d):
        return jnp.where(kpos + d * kb <= qpos, scores(n_past + d), NEG)

    def add_values(j, p):
        start = pl.multiple_of(j * kb, kb)
        acc_ref[...] += _dot(p.astype(BF16), vo_ref[0, pl.ds(start, kb), :])

    bound = jnp.sqrt(jnp.sum(q * q, axis=-1, keepdims=True)) * kmax_ref[0][:, 0:1] * (SCALE * SHIFT_SLACK)
    small = jnp.max(bound) <= SHIFT_LIMIT

    @pl.when(small)
    def _():
        _tile_loop(0, n_past, lambda j: add_values(j, jnp.exp(scores(j) - bound)))
        for d in range(bq):
            add_values(n_past + d, jnp.exp(own_scores(d) - bound))

    @pl.when(jnp.logical_not(small))
    def _():
        m_ref[...] = jnp.full(m_ref.shape, NEG, F32)

        def score(j, s):
            s_ref[j] = s
            m_ref[...] = jnp.maximum(m_ref[...], _lane_fold_max(s))

        _tile_loop(0, n_past, lambda j: score(j, scores(j)))
        for d in range(bq):
            score(n_past + d, own_scores(d))
        m = jnp.max(m_ref[...], axis=-1, keepdims=True)
        _tile_loop(0, n_past + bq, lambda j: add_values(j, jnp.exp(s_ref[j] - m)))

    acc = acc_ref[...]
    o_ref[0] = (acc / jnp.maximum(acc[:, 0:1], TINY))[:, HD:]


def _moba_prompt(q_pad, ka_bf16, vo_bf16, kmean_rows, kmax):
    h, t, _ = q_pad.shape
    nblk = t // MOBA_BLOCK
    assert t % MOBA_BLOCK == 0 and nblk <= LANES - HD
    bq = 2 if nblk % 2 == 0 else 1
    rows = bq * MOBA_BLOCK
    return pl.pallas_call(
        functools.partial(_moba_prompt_body, topk=min(MOBA_TOPK, nblk), bq=bq),
        grid=(h, nblk // bq),
        in_specs=[
            pl.BlockSpec((1, rows, HD), lambda a, i: (a, i, 0)),
            pl.BlockSpec((1, t, LANES), lambda a, i: (a, 0, 0), pipeline_mode=pl.Buffered(1)),
            pl.BlockSpec((1, t, LANES), lambda a, i: (a, 0, 0), pipeline_mode=pl.Buffered(1)),
            pl.BlockSpec((1, LANES, LANES), lambda a, i: (a, 0, 0)),
            pl.BlockSpec((1, 1, LANES), lambda a, i: (a, 0, 0)),
        ],
        out_specs=pl.BlockSpec((1, rows, HD), lambda a, i: (a, i, 0)),
        out_shape=jax.ShapeDtypeStruct((h, t, HD), F32),
        scratch_shapes=[pltpu.VMEM((nblk, rows, MOBA_BLOCK), F32), pltpu.VMEM((rows, LANES), F32),
                        pltpu.VMEM((rows, LANES), F32)],
        compiler_params=_params("parallel", "arbitrary"),
        name="moba_prompt",
    )(q_pad, ka_bf16, vo_bf16, kmean_rows, kmax)


def _mem_prompt_body(q_ref, k_ref, v_ref, o_ref):
    q = q_ref[...]
    kb = k_ref[...].astype(BF16)
    vb = v_ref[...].astype(BF16)
    head = lax.broadcasted_iota(jnp.int32, q.shape, 1) // HD
    o = jnp.zeros(q.shape, F32)
    for h in range(MEM_H):
        mh = head == h
        s = _dot_nt(jnp.where(mh, q, 0.0).astype(BF16), kb) * SCALE
        e = jnp.exp(s - jnp.max(s, axis=-1, keepdims=True))
        p = e / jnp.sum(e, axis=-1, keepdims=True)
        o = o + jnp.where(mh, _dot(p.astype(BF16), vb), 0.0)
    o_ref[...] = o


def _mem_prompt(q, k, v, tm):
    t, w = q.shape
    n = k.shape[0]
    return pl.pallas_call(
        _mem_prompt_body,
        grid=(t // tm,),
        in_specs=[pl.BlockSpec((tm, w), lambda i: (i, 0)), pl.BlockSpec((n, w), lambda i: (0, 0)),
                  pl.BlockSpec((n, w), lambda i: (0, 0))],
        out_specs=pl.BlockSpec((tm, w), lambda i: (i, 0)),
        out_shape=jax.ShapeDtypeStruct((t, w), F32),
        compiler_params=_params("parallel"),
        name="mem_prompt",
    )(q, k, v)


def _mix_out_body(x_ref, a_ref, b_ref, c_ref, gate_ref, wo_ref, fg_ref, o_ref, *, final_norm):
    gate = gate_ref[...]
    gate = gate * jax.nn.sigmoid(gate)
    y = x_ref[...]
    off = 0
    for ref in (a_ref, b_ref, c_ref):
        w = ref.shape[1]
        mix = (ref[...] * gate[:, off:off + w]).astype(BF16)
        y = y + _dot(mix, wo_ref[off:off + w, :])
        off += w
    if final_norm:
        ms = jnp.mean(y * y, axis=-1, keepdims=True)
        y = y * lax.rsqrt(ms + EPS) * fg_ref[...]
    o_ref[...] = y


def _mix_out(x, a, b, c, gate, wo_bf16, final_g, tm):
    m, d = x.shape
    row = lambda w: pl.BlockSpec((tm, w), lambda i: (i, 0))
    fg = jnp.ones((1, d), F32) if final_g is None else final_g.reshape(1, d).astype(F32)
    body = functools.partial(_mix_out_body, final_norm=final_g is not None)
    return pl.pallas_call(
        body,
        grid=(m // tm,),
        in_specs=[row(d), row(a.shape[1]), row(b.shape[1]), row(c.shape[1]), row(gate.shape[1]),
                  pl.BlockSpec(wo_bf16.shape, lambda i: (0, 0)), pl.BlockSpec((1, d), lambda i: (0, 0))],
        out_specs=row(d),
        out_shape=jax.ShapeDtypeStruct((m, d), F32),
        compiler_params=_params("parallel"),
        name="mix_out",
    )(x, a, b, c, gate, wo_bf16, fg)


def _compress_body(x_ref, pe_ref, w1_ref, w2_ref, o_ref):
    x = (x_ref[0] + pe_ref[0]).astype(BF16)
    hid = jax.nn.gelu(_dot(x, w1_ref[0]))
    o_ref[0] = _dot(hid.astype(BF16), w2_ref[0])


def _compress_weights(w1, w2, pe):
    eye = jnp.eye(NSA_KVH, dtype=F32)
    w1r = w1.astype(F32).reshape(2, NSA_BLOCK, HD, HD)
    w1b = jnp.einsum("klde,gh->klgdhe", w1r, eye).reshape(2, NSA_BLOCK * NSA_KVW, NSA_KVW)
    w2b = jnp.einsum("kde,gh->kgdhe", w2.astype(F32), eye).reshape(2, NSA_KVW, NSA_KVW)
    peb = jnp.broadcast_to(pe.astype(F32)[:, :, None, :], (2, NSA_BLOCK, NSA_KVH, HD)).reshape(2, 1, NSA_BLOCK * NSA_KVW)
    return w1b.astype(BF16), w2b.astype(BF16), peb


def _compress(x, w1b, w2b, peb):
    _, nb, kdim = x.shape
    return pl.pallas_call(
        _compress_body,
        grid=(2,),
        in_specs=[pl.BlockSpec((1, nb, kdim), lambda i: (i, 0, 0)), pl.BlockSpec((1, 1, kdim), lambda i: (i, 0, 0)),
                  pl.BlockSpec((1, kdim, NSA_KVW), lambda i: (i, 0, 0)),
                  pl.BlockSpec((1, NSA_KVW, NSA_KVW), lambda i: (i, 0, 0))],
        out_specs=pl.BlockSpec((1, nb, NSA_KVW), lambda i: (i, 0, 0)),
        out_shape=jax.ShapeDtypeStruct((2, nb, NSA_KVW), F32),
        compiler_params=_params("parallel"),
        name="nsa_compress",
    )(x, peb, w1b, w2b)


def _nsa_prompt_body(q_ref, qr_ref, gt_ref, ckv_ref, ska_ref, svo_ref, wkv_ref, kmax_ref, o_ref, s_ref, m_ref, acc_ref,
                     *, tq, kc):
    i = pl.program_id(1)
    t0 = i * tq
    rows = q_ref.shape[2]
    nb = ckv_ref.shape[1]
    pad = lambda a: jnp.concatenate([a, jnp.zeros_like(a)], axis=1)
    q = pad(q_ref[0, 0])
    qr = pad(qr_ref[0, 0]).astype(F32)
    qrs = qr * SCALE
    qrb = qrs.astype(BF16)
    tpos = t0 + lax.broadcasted_iota(jnp.int32, (rows, 1), 0) % tq

    ckv = ckv_ref[0].astype(BF16)
    blk = lax.broadcasted_iota(jnp.int32, (nb, rows), 0)
    tpos_l = t0 + lax.broadcasted_iota(jnp.int32, (nb, rows), 1) % tq
    complete = blk * NSA_BLOCK + (NSA_BLOCK - 1) <= tpos_l
    s_c = jnp.where(complete, _dot_nt(ckv, q) * SCALE, NEG)
    e = jnp.where(complete, jnp.exp(s_c - jnp.max(s_c, axis=0, keepdims=True)), 0.0)
    pc = e / jnp.maximum(jnp.sum(e, axis=0, keepdims=True), TINY)
    o_c = _dot_tn(pc.astype(BF16), ckv)

    imp = pc[:, 0:tq]
    for r in range(1, NSA_REP):
        imp = imp + pc[:, r * tq:(r + 1) * tq]
    blk_q = lax.broadcasted_iota(jnp.int32, (nb, tq), 0)
    tpos_q = t0 + lax.broadcasted_iota(jnp.int32, (nb, tq), 1)
    cur = tpos_q // NSA_BLOCK
    forced = (blk_q == 0) | (blk_q == cur) | (blk_q == cur - 1)
    complete_q = blk_q * NSA_BLOCK + (NSA_BLOCK - 1) <= tpos_q
    score = jnp.where(forced, FORCE, jnp.where(complete_q, imp, NEG))
    sel = jnp.where(score > 0.5 * NEG, _topk_mask_axis0(score, min(NSA_TOPN, nb)), 0.0)

    bias = jnp.where(sel > 0.5, 0.0, NEG).T
    nbp = -(-nb // LANES) * LANES
    if nbp > nb:
        bias = jnp.concatenate([bias, jnp.full((tq, nbp - nb), NEG, F32)], axis=1)
    bias = jnp.concatenate([bias] * NSA_REP, axis=0)
    upper =lax.broadcasted_iota(jnp.int32, (rows, LANES), 1) >= HD
    n_span = -(-nb // HD)
    qa = []
    for sp in range(n_span):
        col = bias[:, (sp // 2) * LANES:(sp // 2 + 1) * LANES]
        if sp % 2 == 0:
            col = pltpu.roll(col, HD, 1)
        qa.append((qrs + jnp.where(upper, col, 0.0)).astype(BF16))
    cps = HD * NSA_BLOCK // kc
    n_chunks = (t0 + tq + kc - 1) // kc
    last = n_chunks - 1
    qa_last = qa[0]
    for sp in range(1, n_span):
        qa_last = jnp.where(last // cps == sp, qa[sp], qa_last)
    last_start = pl.multiple_of(last * kc, kc)
    kpos = last_start + lax.broadcasted_iota(jnp.int32, (rows, kc), 1)
    s_last = jnp.where(kpos <= tpos, _dot_nt(qa_last, ska_ref[0, pl.ds(last_start, kc), :]), NEG)

    def spans(fn):
        for sp in range(n_span):
            lo = sp * cps
            _tile_loop(jnp.minimum(lo, last), jnp.minimum(lo + cps, last), functools.partial(fn, qa[sp]))

    def scores(qsp, c):
        start = pl.multiple_of(c * kc, kc)
        return _dot_nt(qsp, ska_ref[0, pl.ds(start, kc), :])

    def add_values(c, p):
        start = pl.multiple_of(c * kc, kc)
        acc_ref[...] += _dot(p.astype(BF16), svo_ref[0, pl.ds(start, kc), :])

    acc_ref[...] = jnp.zeros(acc_ref.shape, F32)
    bound =jnp.sqrt(jnp.sum(qr * qr, axis=-1, keepdims=True)) * kmax_ref[0][:, 0:1] * (SCALE * SHIFT_SLACK)
    small = jnp.max(bound) <= SHIFT_LIMIT

    @pl.when(small)
    def _():
        add_values(last, jnp.exp(s_last - bound))
        spans(lambda qsp, c: add_values(c, jnp.exp(scores(qsp, c) - bound)))

    @pl.when(jnp.logical_not(small))
    def _():
        m_ref[...] = _lane_fold_max(s_last)

        def fold_max(qsp, c):
            s = scores(qsp, c)
            s_ref[c] = s
            m_ref[...] = jnp.maximum(m_ref[...], _lane_fold_max(s))

        spans(fold_max)
        m_s = jnp.max(m_ref[...], axis=-1, keepdims=True)
        add_values(last, jnp.exp(s_last - m_s))
        spans(lambda qsp, c: add_values(c, jnp.exp(s_ref[c] - m_s)))

    acc_s = acc_ref[...]
    o_s = acc_s / jnp.maximum(acc_s[:, 0:1], TINY)

    nwin = (NSA_WINDOW + tq - 1) // tq + 1
    win = []
    for d in range(nwin):
        c = i - (nwin - 1) + d
        start = pl.multiple_of(jnp.maximum(c, 0) * tq, tq)
        kv = wkv_ref[0, pl.ds(start, tq), :]
        kpos = c * tq + lax.broadcasted_iota(jnp.int32, (rows, tq), 1)
        mask = (kpos <= tpos) & (kpos > tpos - NSA_WINDOW) & (kpos >= 0)
        win.append((jnp.where(mask, _dot_nt(qrb, kv), NEG), kv))
    m_w = jnp.max(functools.reduce(jnp.maximum, [s for s, _ in win]), axis=-1, keepdims=True)
    p_w = [jnp.exp(s - m_w) for s, _ in win]
    l_w = jnp.sum(functools.reduce(jnp.add, p_w), axis=-1, keepdims=True)
    acc_w = functools.reduce(jnp.add, [_dot(p.astype(BF16), kv) for p, (_, kv) in zip(p_w, win)])
    o_w = acc_w / jnp.maximum(l_w, TINY)

    gt = jax.nn.sigmoid(gt_ref[0, 0])
    o_ref[0, 0] = (gt[:, 0:1] * o_c + gt[:, 1:2] * o_s + gt[:, 2:3] * o_w)[:, HD:]


def _nsa_prompt(q_st, qr_st, gt_st, ckv, ska, svo, wkv, kmax, tq, kc):
    g, nt, rows, _ = q_st.shape
    t = ska.shape[1]
    nb = ckv.shape[1]
    assert t % kc == 0 and kc % tq == 0 and (HD * NSA_BLOCK) % kc == 0 and t % tq == 0
    qspec = pl.BlockSpec((1, 1, rows, HD), lambda a, i: (a, i, 0, 0))
    seq = pl.BlockSpec((1, t, LANES), lambda a, i: (a, 0, 0), pipeline_mode=pl.Buffered(1))
    body = functools.partial(_nsa_prompt_body, tq=tq, kc=kc)
    return pl.pallas_call(
        body,
        grid=(g, nt),
        in_specs=[qspec, qspec, pl.BlockSpec((1, 1, rows, 3), lambda a, i: (a, i, 0, 0)),
                  pl.BlockSpec((1, nb, LANES), lambda a, i: (a, 0, 0)), seq, seq, seq,
                  pl.BlockSpec((1, 1, LANES), lambda a, i: (a, 0, 0))],
        out_specs=qspec,
        out_shape=jax.ShapeDtypeStruct((g, nt, rows, HD), F32),
        scratch_shapes=[pltpu.VMEM((t // kc, rows, kc), F32), pltpu.VMEM((rows, LANES), F32),
                        pltpu.VMEM((rows, LANES), F32)],
        compiler_params=_params("parallel", "arbitrary"),
        name="nsa_prompt",
    )(q_st, qr_st, gt_st, ckv, ska, svo, wkv, kmax)


def _hgrn_body(q_ref, f_ref, i_ref, lb_ref, o_ref, s_ref):
    @pl.when(pl.program_id(0) == 0)
    def _():
        s_ref[...] = jnp.zeros_like(s_ref)

    q = q_ref[...]
    v = i_ref[...]
    c, w = q.shape
    lb = lb_ref[...]
    f = lb + (1.0 - lb) * jax.nn.sigmoid(f_ref[...])
    kk = 1.0 - f
    tr = lax.broadcasted_iota(jnp.int32, (c, c), 0)
    tc = lax.broadcasted_iota(jnp.int32, (c, c), 1)
    cum = _dot(jnp.where(tr >= tc, 1.0, 0.0), jnp.log(f), HIGHEST)
    bd = _block_ones(w)
    bd_b = bd.astype(BF16)
    s = s_ref[...]
    o = _dot(q * jnp.exp(cum), s, HIGHEST)
    t_idx = lax.broadcasted_iota(jnp.int32, (c, w), 0)

    sub = 8
    groups = c // sub
    o_blk = [o[i * sub:(i + 1) * sub] for i in range(groups)]
    for g in range(groups):
        r0 = g * sub
        n = c - r0
        t_g = r0 + lax.broadcasted_iota(jnp.int32, (n, w), 0)
        es = [jnp.where(t_g >= j, jnp.exp(cum[r0:] - cum[j:j + 1]) * q[r0:] * kk[j:j + 1], 0.0)
              for j in range(r0, r0 + sub)]
        e = jnp.concatenate(es, axis=0)
        e_hi = e.astype(BF16)
        e_lo = (e - e_hi.astype(F32)).astype(BF16)
        a = _dot(e_hi, bd_b) + _dot(e_lo, bd_b)
        contrib = a[0:n] * v[r0:r0 + 1]
        for u in range(1, sub):
            contrib = contrib + a[u * n:(u + 1) * n] * v[r0 + u:r0 + u + 1]
        for i in range(g, groups):
            o_blk[i] = o_blk[i] + contrib[(i - g) * sub:(i - g + 1) * sub]
    o = jnp.concatenate(o_blk, axis=0)
    last = cum[c - 1:c, :]
    row0 = t_idx == 0
    scale_mat = _dot_tn(jnp.where(row0, jnp.exp(last), 0.0), jnp.where(row0, 1.0, 0.0), HIGHEST)
    s_ref[...] = s * scale_mat + _dot_tn(kk * jnp.exp(last - cum), v, HIGHEST) * bd
    ms = _dot(o * o, bd * (1.0 / HD), HIGHEST)
    o_ref[...] = o * lax.rsqrt(ms + EPS)


def _hgrn_prompt(qfi, lb):
    t, w = qfi.shape[0], HGRN_W
    c = math.gcd(t, HGRN_CHUNK)
    col = lambda j: pl.BlockSpec((c, w), lambda n: (n, j))
    return pl.pallas_call(
        _hgrn_body,
        grid=(t // c,),
        in_specs=[col(0), col(1), col(2), pl.BlockSpec((1, w), lambda n: (0, 0))],
        out_specs=[col(0), pl.BlockSpec((w, w), lambda n: (0, 0))],
        out_shape=[jax.ShapeDtypeStruct((t, w), F32), jax.ShapeDtypeStruct((w, w), F32)],
        compiler_params=_params("arbitrary"),
        name="hgrn_prompt",
    )(qfi, qfi, qfi, lb)


def _state_step_body(s_ref, q_ref, a_ref, b_ref, v_ref, o_ref, so_ref, *, mode):
    s = s_ref[...]
    if mode == "ret":
        k = a_ref[...] * SCALE
        dec = b_ref[...]
    else:
        lb = b_ref[...]
        dec = lb + (1.0 - lb) * jax.nn.sigmoid(a_ref[...])
        k = 1.0 - dec
    kv = k * v_ref[...]
    rows = s.shape[0]
    o = (q_ref[...] * (kv + dec * s)).reshape(rows // HD, HD, HD).sum(axis=1)
    so_ref[...] = dec * s + kv
    if mode == "ret":
        xc = o - jnp.mean(o, axis=-1, keepdims=True)
        o_ref[...] = xc * lax.rsqrt(jnp.mean(xc * xc, axis=-1, keepdims=True) + EPS)
    else:
        o_ref[...] = o * lax.rsqrt(jnp.mean(o * o, axis=-1, keepdims=True) + EPS)


def _state_step(state, q, a, b, v, mode):
    bsz, h = state.shape[0], state.shape[1]
    rows = bsz * h * HD
    col = lambda x: jnp.broadcast_to(x.reshape(bsz, h, HD, 1), (bsz, h, HD, HD)).reshape(rows, HD)
    vx = jnp.broadcast_to(v.reshape(bsz, h, 1, HD), (bsz, h, HD, HD)).reshape(rows, HD)
    bb = 8 if bsz % 8 == 0 else bsz
    br = bb * h * HD
    spec = pl.BlockSpec((br, HD), lambda i: (i, 0))
    ospec = pl.BlockSpec((br // HD, HD), lambda i: (i, 0))
    o, s_new = pl.pallas_call(
        functools.partial(_state_step_body, mode=mode),
        grid=(rows // br,),
        in_specs=[spec] * 5,
        out_specs=[ospec, spec],
        out_shape=[jax.ShapeDtypeStruct((rows // HD, HD), F32), jax.ShapeDtypeStruct((rows, HD), F32)],
        compiler_params=_params("parallel"),
        name="state_step_" + mode,
    )(state.reshape(rows, HD).astype(F32), col(q), col(a), col(b), vx)
    return o.reshape(bsz, h * HD), s_new.reshape(bsz, h, HD, HD)


def _expand_heads(p, width):
    g, n = p.shape
    return jnp.broadcast_to(p[:, None, :], (g, HD, n)).reshape(width, n)


def _head_sums(x, g):
    return x.reshape(g, HD, x.shape[-1]).sum(axis=1)


def _mem_sample_body(q_ref, kt_ref, vt_ref, o_ref):
    kt = kt_ref[0, 0]
    vt = vt_ref[0, 0]
    s = _head_sums(kt * q_ref[0], MEM_H) * SCALE
    e = jnp.exp(s - jnp.max(s, axis=-1, keepdims=True))
    p = e / jnp.sum(e, axis=-1, keepdims=True)
    o_ref[0] = jnp.sum(vt * _expand_heads(p, MEM_W), axis=-1, keepdims=True)


def _mem_sample(q, kvt):
    bsz, w = q.shape
    n = kvt.shape[-1]
    out = pl.pallas_call(
        _mem_sample_body,
        grid=(bsz,),
        in_specs=[pl.BlockSpec((1, w, 1), lambda i: (i, 0, 0)),
                  pl.BlockSpec((1, 1, w, n), lambda i: (i, 0, 0, 0)),
                  pl.BlockSpec((1, 1, w, n), lambda i: (i, 1, 0, 0))],
        out_specs=pl.BlockSpec((1, w, 1), lambda i: (i, 0, 0)),
        out_shape=jax.ShapeDtypeStruct((bsz, w, 1), F32),
        compiler_params=_params("parallel"),
        name="mem_sample",
    )(q.reshape(bsz, w, 1), kvt, kvt)
    return out.reshape(bsz, w)


def _topk_mask_axis0(score, k):
    n = score.shape[0]
    idx = lax.broadcasted_iota(jnp.int32, score.shape, 0)
    sel = jnp.zeros(score.shape, F32)
    work = score
    for _ in range(k):
        m = jnp.max(work, axis=0, keepdims=True)
        first = jnp.min(jnp.where(work == m, idx, n), axis=0, keepdims=True)
        pick = idx == first
        sel = jnp.where(pick, 1.0, sel)
        work = jnp.where(pick, -jnp.inf, work)
    return sel


def _moba_sample_body(pt_ref, pool_ref, qc_ref, qbd_ref, kn_ref, vn_ref, o_ref, buf, sem, s_ref, vbuf, vsem,
                      *, layer, chunk, topk):
    b = pl.program_id(0)
    n_pages = pt_ref.shape[1]
    n = n_pages // chunk
    groups, gw = MOBA_H, MOBA_W
    per = MOBA_BLOCK // PAGE
    nblk = n_pages // per

    def copy(t, i):
        page = pt_ref[t // n, (t % n) * chunk + i]
        return pltpu.make_async_copy(pool_ref.at[page, layer, 0], buf.at[t % 2, i], sem.at[t % 2])

    def start(t):
        lax.fori_loop(0, chunk, lambda i, _: (copy(t, i).start(), 0)[1], 0)

    def wait(t):
        lax.fori_loop(0, chunk, lambda i, _: (copy(t, i).wait(), 0)[1], 0)

    @pl.when(b == 0)
    def _():
        start(0)

    def k_step(c, _):
        t = b * n + c
        slot = t % 2

        @pl.when(t + 1 < pl.num_programs(0) * n)
        def _():
            start(t + 1)

        wait(t)
        for i in range(chunk):
            sc = _dot(qbd_ref[0], buf[slot, i].astype(BF16))
            s_ref[c * chunk + i] = sc[0:groups]
        return 0

    lax.fori_loop(0, n, k_step, 0)

    sc = s_ref[...].reshape(nblk, per, groups, PAGE)
    gate = jnp.sum(jnp.sum(sc, axis=1, keepdims=True), axis=-1, keepdims=True) * (1.0 / MOBA_BLOCK)
    blk = lax.broadcasted_iota(jnp.int32, gate.shape, 0)
    sel = jnp.zeros(gate.shape, F32)
    picks = []
    for _ in range(topk):
        first = jnp.min(jnp.where(gate == jnp.max(gate, axis=0, keepdims=True), blk, nblk), axis=0, keepdims=True)
        pick = blk == first
        sel = jnp.where(pick, 1.0, sel)
        gate = jnp.where(pick, -jnp.inf, gate)
        picks.append(first)

    v_copies = []
    for h in range(groups):
        for j in range(topk):
            first_page = picks[j][0, 0, h, 0] * per
            for half in range(per):
                v_copies.append((h, j * per + half, first_page + half))
    descs = [pltpu.make_async_copy(pool_ref.at[pt_ref[b, page], layer, 1, pl.ds(h * HD, HD)], vbuf.at[h, slot_j], vsem)
             for h, slot_j, page in v_copies]
    for d in descs:
        d.start()

    mask = jnp.broadcast_to(sel, (nblk, per, groups, PAGE)).reshape(n_pages, groups, PAGE) > 0.5
    s_own = _head_sums(jnp.broadcast_to(kn_ref[0] * qc_ref[0], (gw, PAGE)), groups) * SCALE
    s = jnp.where(mask, s_ref[...] * SCALE, NEG)
    m = jnp.maximum(jnp.max(jnp.max(s, axis=0), axis=-1, keepdims=True), s_own)
    p = jnp.where(mask, jnp.exp(s - m), 0.0)
    p_own = jnp.exp(s_own - m)
    den = jnp.maximum(jnp.sum(jnp.sum(p, axis=0), axis=-1, keepdims=True) + p_own, TINY)
    s_ref[...] = p

    for d in descs:
        d.wait()
    tot = []
    for h in range(groups):
        acc = jnp.zeros((HD, PAGE), F32)
        for hh, slot_j, page in v_copies:
            if hh == h:
                acc = acc + vbuf[h, slot_j] * s_ref[page][h:h + 1]
        tot.append(jnp.sum(acc, axis=-1, keepdims=True))
    tot = jnp.concatenate(tot, axis=0)
    o_ref[0] = (tot + _expand_heads(p_own, gw)[:, 0:1] * vn_ref[0]) / _expand_heads(den, gw)[:, 0:1]


def _moba_sample(page_table, pool_t, layer, q, k_new, v_new):
    bsz, n_pages = page_table.shape
    gw = MOBA_W
    chunk = math.gcd(n_pages, 16)
    per = MOBA_BLOCK // PAGE
    topk = min(MOBA_TOPK, n_pages // per)
    nq = -(-MOBA_H // 8) * 8
    qbd = jnp.einsum("bgd,gh->bghd", q.reshape(bsz, MOBA_H, HD), jnp.eye(MOBA_H, dtype=F32)).reshape(bsz, MOBA_H, gw)
    qbd = jnp.pad(qbd, ((0, 0), (0, nq - MOBA_H), (0, 0))).astype(BF16)
    col = pl.BlockSpec((1, gw, 1), lambda i, pt: (i, 0, 0))
    grid_spec = pltpu.PrefetchScalarGridSpec(
        num_scalar_prefetch=1,
        grid=(bsz,),
        in_specs=[pl.BlockSpec(memory_space=pl.ANY), col, pl.BlockSpec((1, nq, gw), lambda i, pt: (i, 0, 0)), col, col],
        out_specs=col,
        scratch_shapes=[
            pltpu.VMEM((2, chunk, gw, PAGE), F32),
            pltpu.SemaphoreType.DMA((2,)),
            pltpu.VMEM((n_pages, MOBA_H, PAGE), F32),
            pltpu.VMEM((MOBA_H, topk * per, HD, PAGE), F32),
            pltpu.SemaphoreType.DMA(()),
        ],
    )
    out = pl.pallas_call(
        functools.partial(_moba_sample_body, layer=layer, chunk=chunk, topk=topk),
        grid_spec=grid_spec,
        out_shape=jax.ShapeDtypeStruct((bsz, gw, 1), F32),
        compiler_params=_params("arbitrary"),
        name="moba_sample",
    )(page_table, pool_t, q.reshape(bsz, gw, 1), qbd, k_new.reshape(bsz, gw, 1), v_new.reshape(bsz, gw, 1))
    return out.reshape(bsz, gw)


CMP_DB = 8


def _compress_weights_t(w1, w2, pe):
    per = PAGE // NSA_BLOCK
    eye = jnp.eye(per, dtype=F32)
    w1r = w1.astype(F32).reshape(2, NSA_BLOCK, HD, HD)
    wd = jnp.einsum("klde,gh->kdglhe", w1r, eye).reshape(2, HD // CMP_DB, CMP_DB * PAGE, per * HD)
    ped = jnp.tile(pe.astype(F32).transpose(0, 2, 1), (1, 1, per)).reshape(2, HD // CMP_DB, 1, CMP_DB * PAGE)
    w2t = jnp.einsum("kde,gh->kgdhe", w2.astype(F32), eye).reshape(2, per * HD, per * HD)
    return wd.astype(BF16), ped, w2t.astype(BF16)


def _nsa_cmp_sample_body(pt_ref, pool_ref, q_ref, wd_ref, ped_ref, w2_ref, oc_ref, sel_ref, buf, sem, tok_ref, acc_ref,
                         *, layer, nseq):
    b = pl.program_id(0)
    j = pl.program_id(1)
    bsz, n_pages = pt_ref.shape
    t = b * 2 + j
    slot = t % 2
    m_rows = nseq * NSA_KVH * n_pages

    def copy(grp, kv, sq, p, g, sl):
        row = pl.multiple_of((((sl * nseq + sq) * NSA_KVH + g) * n_pages + p) * HD, HD)
        return pltpu.make_async_copy(pool_ref.at[pt_ref[grp * nseq + sq, p], layer, kv, g], buf.at[pl.ds(row, HD)],
                                     sem.at[sl])

    def each(fn):
        def run(grp, kv, sl):
            def page(p, _):
                for sq in range(nseq):
                    for g in range(NSA_KVH):
                        fn(copy(grp, kv, sq, p, g, sl))
                return 0
            lax.fori_loop(0, n_pages, page, 0)
        return run

    start = each(lambda c: c.start())
    wait = each(lambda c: c.wait())

    @pl.when(t == 0)
    def _():
        start(0, 0, 0)

    @pl.when(t + 1 < 2 * (bsz // nseq))
    def _():
        start((t + 1) // 2, (t + 1) % 2, 1 - slot)

    wait(b, j, slot)

    base = slot * m_rows * HD
    for dd in range(HD // CMP_DB):
        parts = [buf[pl.ds(base + dd * CMP_DB + u, m_rows, stride=HD), :] for u in range(CMP_DB)]
        a = jnp.concatenate(parts, axis=1) + ped_ref[0, dd]
        part = _dot(a.astype(BF16), wd_ref[0, dd])
        if dd == 0:
            acc_ref[...] = part
        else:
            acc_ref[...] += part
    tok_ref[j] = _dot(jax.nn.gelu(acc_ref[...]).astype(BF16), w2_ref[0])

    @pl.when(j == 1)
    def _():
        pos = n_pages * PAGE
        per = PAGE // NSA_BLOCK
        n_idx = (lax.broadcasted_iota(jnp.int32, (per, n_pages), 1) * per
                 + lax.broadcasted_iota(jnp.int32, (per, n_pages), 0))
        complete = n_idx * NSA_BLOCK + (NSA_BLOCK - 1) <= pos
        cur = pos // NSA_BLOCK
        forced = (n_idx == 0) | (n_idx == cur) | (n_idx == cur - 1)
        k_past = min(NSA_TOPN, per * n_pages + 1) - 1
        for sq, g in ((a, c) for a in range(nseq) for c in range(NSA_KVH)):
            qg = q_ref[sq, g * NSA_REP:(g + 1) * NSA_REP, :].astype(BF16)
            r0 = (sq * NSA_KVH + g) * n_pages
            ck = tok_ref[0, r0:r0 + n_pages, :].astype(BF16)
            cv = tok_ref[1, r0:r0 + n_pages, :].astype(BF16)
            s = [jnp.where(complete[h:h + 1], _dot_nt(qg, ck[:, h * HD:(h + 1) * HD]) * SCALE, NEG) for h in range(per)]
            m = functools.reduce(jnp.maximum, [jnp.max(x, axis=-1, keepdims=True) for x in s])
            e = [jnp.where(complete[h:h + 1], jnp.exp(s[h] - m), 0.0) for h in range(per)]
            den = jnp.maximum(sum(jnp.sum(x, axis=-1, keepdims=True) for x in e), TINY)
            pc = [x / den for x in e]
            oc_ref[sq, g * NSA_REP:(g + 1) * NSA_REP, :] = sum(
                _dot(pc[h].astype(BF16), cv[:, h * HD:(h + 1) * HD]) for h in range(per))
            imp = jnp.concatenate([jnp.sum(x, axis=0, keepdims=True) for x in pc], axis=0)
            score = jnp.where(forced, FORCE, jnp.where(complete, imp, NEG))
            sel = jnp.zeros(score.shape, F32)
            work = score
            for _ in range(k_past):
                mx = jnp.max(jnp.max(work, axis=-1, keepdims=True), axis=0, keepdims=True)
                cand = jnp.where(work == mx, n_idx, per * n_pages)
                first = jnp.min(jnp.min(cand, axis=-1, keepdims=True), axis=0, keepdims=True)
                pick = n_idx == first
                sel = jnp.where(pick, 1.0, sel)
                work = jnp.where(pick, -jnp.inf, work)
            sel_ref[sq, g] = jnp.where(score > 0.5 * NEG, sel, 0.0)


def _nsa_cmp_sample(page_table, pool_t, layer, q, wd, ped, w2b):
    bsz, n_pages = page_table.shape
    per = PAGE // NSA_BLOCK
    nseq = 2 if bsz % 2 == 0 else 1
    m_rows = nseq * NSA_KVH * n_pages
    grid_spec = pltpu.PrefetchScalarGridSpec(
        num_scalar_prefetch=1,
        grid=(bsz // nseq, 2),
        in_specs=[
            pl.BlockSpec(memory_space=pl.ANY),
            pl.BlockSpec((nseq, NSA_H, HD), lambda i, j, pt: (i, 0, 0)),
            pl.BlockSpec((1, HD // CMP_DB, CMP_DB * PAGE, per * HD), lambda i, j, pt: (j, 0, 0, 0)),
            pl.BlockSpec((1, HD // CMP_DB, 1, CMP_DB * PAGE), lambda i, j, pt: (j, 0, 0, 0)),
            pl.BlockSpec((1, per * HD, per * HD), lambda i, j, pt: (j, 0, 0)),
        ],
        out_specs=[pl.BlockSpec((nseq, NSA_H, HD), lambda i, j, pt: (i, 0, 0)),
                   pl.BlockSpec((nseq, NSA_KVH, per, n_pages), lambda i, j, pt: (i, 0, 0, 0))],
        scratch_shapes=[
            pltpu.VMEM((2 * m_rows * HD, PAGE), F32),
            pltpu.SemaphoreType.DMA((2,)),
            pltpu.VMEM((2, m_rows, per * HD), F32),
            pltpu.VMEM((m_rows, per * HD), F32),
        ],
    )
    return pl.pallas_call(
        functools.partial(_nsa_cmp_sample_body, layer=layer, nseq=nseq),
        grid_spec=grid_spec,
        out_shape=[jax.ShapeDtypeStruct((bsz, NSA_H, HD), F32),
                   jax.ShapeDtypeStruct((bsz, NSA_KVH, per, n_pages), F32)],
        compiler_params=_params("arbitrary", "arbitrary"),
        name="nsa_cmp_sample",
    )(page_table, pool_t, q, wd, ped, w2b)


def _nsa_sel_sample_body(pt_ref, idx_ref, ok_ref, pool_ref, q_ref, kn_ref, vn_ref, o_ref, buf, sem, *, layer, ksel):
    b = pl.program_id(0)
    bsz = pt_ref.shape[0]
    slot = b % 2
    per = PAGE // NSA_BLOCK

    def copy(bb, kv, g, j, sl):
        page = pt_ref[bb, idx_ref[bb, g * ksel + j] // per]
        return pltpu.make_async_copy(pool_ref.at[page, layer, kv, g], buf.at[sl, kv, g, j], sem.at[sl])

    def each(fn):
        def run(bb, sl):
            def block(j, _):
                for kv in range(2):
                    for g in range(NSA_KVH):
                        fn(copy(bb, kv, g, j, sl))
                return 0
            lax.fori_loop(0, ksel, block, 0)
        return run

    start = each(lambda c: c.start())
    wait = each(lambda c: c.wait())

    @pl.when(b == 0)
    def _():
        start(0, 0)

    @pl.when(b + 1 < bsz)
    def _():
        start(b + 1, 1 - slot)

    wait(b, slot)
    lane_half = lax.broadcasted_iota(jnp.int32, (NSA_REP, PAGE), 1) // NSA_BLOCK
    for g in range(NSA_KVH):
        qg = q_ref[0, g * NSA_REP:(g + 1) * NSA_REP, :] * SCALE
        qb = qg.astype(BF16)
        s = []
        for j in range(ksel):
            blk = idx_ref[b, g * ksel + j]
            valid = (lane_half == blk % per) & (ok_ref[b, g * ksel + j] > 0)
            s.append(jnp.where(valid, _dot(qb, buf[slot, 0, g, j].astype(BF16)), NEG))
        s_own = jnp.sum(qg * kn_ref[0, g:g + 1, :], axis=-1, keepdims=True)
        m = functools.reduce(jnp.maximum, [jnp.max(x, axis=-1, keepdims=True) for x in s] + [s_own])
        p = [jnp.exp(x - m) for x in s]
        p_own = jnp.exp(s_own - m)
        den = functools.reduce(jnp.add, [jnp.sum(x, axis=-1, keepdims=True) for x in p]) + p_own
        o = functools.reduce(jnp.add, [_dot_nt(p[j].astype(BF16), buf[slot, 1, g, j].astype(BF16))
                                       for j in range(ksel)])
        o_ref[0, g * NSA_REP:(g + 1) * NSA_REP, :] = (o + p_own * vn_ref[0, g:g + 1, :]) / jnp.maximum(den, TINY)


def _nsa_sel_sample(page_table, pool_t, layer, q, k_new, v_new, sel):
    bsz, n_pages = page_table.shape
    per = PAGE // NSA_BLOCK
    ksel = min(NSA_TOPN, per * n_pages + 1) - 1
    flat = sel.transpose(0, 1, 3, 2).reshape(bsz, NSA_KVH, n_pages * per)
    vals, idx = lax.top_k(flat, ksel)
    idx = idx.reshape(bsz, NSA_KVH * ksel).astype(jnp.int32)
    ok = (vals > 0.5).reshape(bsz, NSA_KVH * ksel).astype(jnp.int32)
    grid_spec = pltpu.PrefetchScalarGridSpec(
        num_scalar_prefetch=3,
        grid=(bsz,),
        in_specs=[
            pl.BlockSpec(memory_space=pl.ANY),
            pl.BlockSpec((1, NSA_H, HD), lambda i, *_: (i, 0, 0)),
            pl.BlockSpec((1, NSA_KVH, HD), lambda i, *_: (i, 0, 0)),
            pl.BlockSpec((1, NSA_KVH, HD), lambda i, *_: (i, 0, 0)),
        ],
        out_specs=pl.BlockSpec((1, NSA_H, HD), lambda i, *_: (i, 0, 0)),
        scratch_shapes=[pltpu.VMEM((2, 2, NSA_KVH, ksel, HD, PAGE), F32), pltpu.SemaphoreType.DMA((2,))],
    )
    return pl.pallas_call(
        functools.partial(_nsa_sel_sample_body, layer=layer, ksel=ksel),
        grid_spec=grid_spec,
        out_shape=jax.ShapeDtypeStruct((bsz, NSA_H, HD), F32),
        compiler_params=_params("arbitrary"),
        name="nsa_sel_sample",
    )(page_table, idx, ok, pool_t, q, k_new, v_new)


def _win_sample_body(qc_ref, kt_ref, vt_ref, kn_ref, vn_ref, o_ref):
    kt = kt_ref[0, 0]
    vt = vt_ref[0, 0]
    gw, lbuf = kt.shape
    j = lax.broadcasted_iota(jnp.int32, (NSA_KVH, lbuf), 1)
    mask = j > lbuf - NSA_WINDOW
    for r in range(NSA_REP):
        qcol = qc_ref[0, :, r:r + 1]
        s = jnp.where(mask, _head_sums(kt * qcol, NSA_KVH) * SCALE, NEG)
        s_own = _head_sums(jnp.broadcast_to(kn_ref[0] * qcol, (gw, lbuf)), NSA_KVH) * SCALE
        m = jnp.maximum(jnp.max(s, axis=-1, keepdims=True), s_own)
        p = jnp.where(mask, jnp.exp(s - m), 0.0)
        p_own = jnp.exp(s_own - m)
        den = jnp.maximum(jnp.sum(p, axis=-1, keepdims=True) + p_own, TINY)
        tot = jnp.sum(vt * _expand_heads(p, gw), axis=-1, keepdims=True)
        o_ref[0, :, r:r + 1] = (tot + _expand_heads(p_own, gw)[:, 0:1] * vn_ref[0]) / _expand_heads(den, gw)[:, 0:1]


def _win_sample(q_cols, wbuf_t, k_new, v_new):
    bsz, gw, reps = q_cols.shape
    lbuf = wbuf_t.shape[-1]
    col = pl.BlockSpec((1, gw, 1), lambda i: (i, 0, 0))
    return pl.pallas_call(
        _win_sample_body,
        grid=(bsz,),
        in_specs=[pl.BlockSpec((1, gw, reps), lambda i: (i, 0, 0)),
                  pl.BlockSpec((1, 1, gw, lbuf), lambda i: (i, 0, 0, 0)),
                  pl.BlockSpec((1, 1, gw, lbuf), lambda i: (i, 1, 0, 0)), col, col],
        out_specs=pl.BlockSpec((1, gw, reps), lambda i: (i, 0, 0)),
        out_shape=jax.ShapeDtypeStruct((bsz, gw, reps), F32),
        compiler_params=_params("parallel"),
        name="nsa_win_sample",
    )(q_cols, wbuf_t, wbuf_t, k_new.reshape(bsz, gw, 1), v_new.reshape(bsz, gw, 1))


def _gate_combine_body(g_ref, c_ref, s_ref, w_ref, o_ref):
    gt = jax.nn.sigmoid(g_ref[...])
    o_ref[...] = gt[0] * c_ref[...] + gt[1] * s_ref[...] + gt[2] * w_ref[...]


def _gate_combine(gate_logits, o_c, o_s, o_w):
    return pl.pallas_call(
        _gate_combine_body,
        out_shape=jax.ShapeDtypeStruct(o_c.shape, F32),
        name="nsa_gate_combine",
    )(gate_logits, o_c, o_s, o_w)


ODD_NQ, ODD_NQR, ODD_KV = 0, NSA_QW, 2 * NSA_QW
ODD_HQ = ODD_KV + 6 * NSA_KVW
ODD_EQ = ODD_HQ + 3 * HGRN_W
ODD_GATE = ODD_EQ + MEM_W
ODD_NG = ODD_GATE + MIX_W
ODD_N = 4096


def _odd_weights(w_in):
    offs = np.cumsum([0, NSA_QW] + [NSA_KVW] * 6 + [3 * NSA_H] + [HGRN_W] * 3 + [MEM_W, MIX_W])
    nq = w_in[:, offs[0]:offs[1]]
    kv = w_in[:, offs[1]:offs[7]]
    ng = w_in[:, offs[7]:offs[8]]
    rest = w_in[:, offs[8]:]
    pad = jnp.zeros((w_in.shape[0], ODD_N - ODD_NG - 3 * NSA_H), w_in.dtype)
    return jnp.concatenate([nq, nq, kv, rest, ng, pad], axis=1)


def _stack_heads(a, tq, width):
    t = a.shape[0]
    a = a.reshape(t // tq, tq, NSA_KVH, NSA_REP, width).transpose(2, 0, 3, 1, 4)
    return a.reshape(NSA_KVH, t // tq, NSA_REP * tq, width)


def _unstack_heads(a, tq):
    g, nt, _, width = a.shape
    a = a.reshape(g, nt, NSA_REP, tq, width).transpose(1, 3, 0, 2, 4)
    return a.reshape(nt * tq, g * NSA_REP * width)


def _group_kv(k, v):
    t = k.shape[0]
    return jnp.concatenate([k.reshape(t, NSA_KVH, HD), v.reshape(t, NSA_KVH, HD)], axis=-1).transpose(1, 0, 2)


def _odd_prompt(x, cos, sin, g, w_aug_bf16, w_o_bf16, mem_k, mem_v, lb, cmp_w, final_g, tm, tq, kc):
    t = x.shape[0]
    assert t % NSA_BLOCK == 0
    kvo = lambda j: ODD_KV + j * NSA_KVW
    nq, nqr, kv6, qfi, eq, gate, ng = _norm_proj(
        x, g, w_aug_bf16, cos, sin, ((ODD_NQR, ODD_KV), (kvo(2), kvo(3)), (kvo(4), kvo(5))), tm,
        splits=(NSA_QW, NSA_QW, 6 * NSA_KVW, 3 * HGRN_W, MEM_W, MIX_W, LANES))
    ck, cv, sk, sv, wk, wv = (kv6[:, j * NSA_KVW:(j + 1) * NSA_KVW] for j in range(6))
    w1b, w2b, peb = cmp_w
    nb = t // NSA_BLOCK
    cmp_tok = _compress(jnp.stack([ck, cv]).reshape(2, nb, NSA_BLOCK * NSA_KVW), w1b, w2b, peb)
    ckv = _group_kv(cmp_tok[0], cmp_tok[1])
    q_st = _stack_heads(nq, tq, HD).astype(BF16)
    qr_st = _stack_heads(nqr, tq, HD).astype(BF16)
    gt_st = _stack_heads(ng[:, :3 * NSA_H], tq, 3)
    onehot = ((jnp.arange(t)[:, None] // NSA_BLOCK) % HD == jnp.arange(HD)[None, :]).astype(F32)
    ska = jnp.concatenate([sk.reshape(t, NSA_KVH, HD), jnp.broadcast_to(onehot[:, None], (t, NSA_KVH, HD))], axis=-1)
    svo = jnp.concatenate([jnp.ones((t, NSA_KVH, HD), F32), sv.reshape(t, NSA_KVH, HD)], axis=-1)
    _, sk_norm2 = _block_stats(kv6, math.gcd(t, MOBA_BLOCK), NSA_KVW, 2)
    o_nsa = _nsa_prompt(q_st, qr_st, gt_st, ckv, ska.transpose(1, 0, 2).astype(BF16),
                        svo.transpose(1, 0, 2).astype(BF16), _group_kv(wk, wv).astype(BF16),
                        _key_norm_bound(sk_norm2, NSA_KVH), tq, kc)
    o_nsa = _unstack_heads(o_nsa, tq)
    o_hg, s_hg = _hgrn_prompt(qfi, lb.reshape(1, HGRN_W))
    o_mem = _mem_prompt(eq, mem_k, mem_v, tm)
    y = _mix_out(x, o_nsa, o_hg, o_mem, gate, w_o_bf16, final_g, tm)
    rows = lambda a, b: jnp.stack([a.reshape(t, NSA_KVH, HD), b.reshape(t, NSA_KVH, HD)], axis=1)
    return y, rows(ck, cv), rows(sk, sv), rows(wk, wv), _diag_blocks(s_hg, HGRN_H)

def _heads_major(a, h):
    t = a.shape[0]
    return a.reshape(t, h, HD).transpose(1, 0, 2)


def _even_prompt(x, cos, sin, g, w_in_bf16, w_o_bf16, mem_k, mem_v, tm):
    t = x.shape[0]
    o_mq = 3 * RET_W
    o_eq = 3 * RET_W + 3 * MOBA_W
    o_gate = o_eq + MEM_W
    ret, moba, eq, gate = _norm_proj(x, g, w_in_bf16, cos, sin, ((0, 2 * RET_W), (o_mq, o_mq + 2 * MOBA_W)), tm,
                                     splits=(3 * RET_W, 3 * MOBA_W, MEM_W, MIX_W))
    mq, mk, mv = moba[:, :MOBA_W], moba[:, MOBA_W:2 * MOBA_W], moba[:, 2 * MOBA_W:]
    o_ret, s_ret = _retention_prompt(ret)
    kmean, knorm2 = _block_stats(moba, MOBA_BLOCK, MOBA_W, 1)
    q_pad = _heads_major(mq, MOBA_H)
    vh = _heads_major(mv, MOBA_H)
    nblk = t // MOBA_BLOCK
    onehot = (jnp.arange(t)[:, None] // MOBA_BLOCK == jnp.arange(HD)[None, :]).astype(F32)
    ka = jnp.concatenate([_heads_major(mk, MOBA_H), jnp.broadcast_to(onehot, (MOBA_H, t, HD))], axis=-1).astype(BF16)
    vo = jnp.concatenate([jnp.ones_like(vh), vh], axis=-1).astype(BF16)
    km = jnp.pad(_heads_major(kmean, MOBA_H), ((0, 0), (HD, LANES - HD - nblk), (0, HD)))
    o_moba = _moba_prompt(q_pad, ka, vo, km, _key_norm_bound(knorm2, MOBA_H))
    o_moba = o_moba.transpose(1, 0, 2).reshape(t, MOBA_W)
    o_mem = _mem_prompt(eq, mem_k, mem_v, tm)
    y = _mix_out(x, o_ret, o_moba, o_mem, gate, w_o_bf16, None, tm)
    rows = moba[:, MOBA_W:].reshape(t, 2, MOBA_H, HD)
    return y, rows, _diag_blocks(s_ret, RET_H)


def _pages_t(pool, width):
    n_pool, n_layer = pool.shape[0], pool.shape[1]
    return pool.transpose(0, 1, 3, 4, 5, 2).reshape(n_pool, n_layer, 2, width, PAGE)


def _mem_t(cache):
    bsz, n = cache.shape[0], cache.shape[1]
    return cache.transpose(0, 2, 3, 4, 1).reshape(bsz, 2, MEM_W, n)


def _even_sample(x, cos, sin, g, w_in_bf16, w_o_bf16, mem_cache, state, page_table, pool, layer):
    bsz = x.shape[0]
    o_mq = 3 * RET_W
    o_eq = 3 * RET_W + 3 * MOBA_W
    o_gate = o_eq + MEM_W
    proj = _norm_proj(x, g, w_in_bf16, cos, sin, ((0, 2 * RET_W), (o_mq, o_mq + 2 * MOBA_W)), bsz)
    rq, rk, rv = proj[:, :RET_W], proj[:, RET_W:2 * RET_W], proj[:, 2 * RET_W:3 * RET_W]
    mq = proj[:, o_mq:o_mq + MOBA_W]
    mk = proj[:, o_mq + MOBA_W:o_mq + 2 * MOBA_W]
    mv = proj[:, o_mq + 2 * MOBA_W:o_eq]
    gamma = np.repeat(1.0 - np.power(2.0, -5.0 - np.arange(RET_H, dtype=np.float64)), HD)
    o_ret, s_ret = _state_step(state, rq, rk, jnp.broadcast_to(jnp.asarray(gamma, F32), (bsz, RET_W)), rv, "ret")
    o_moba = _moba_sample(page_table, _pages_t(pool, MOBA_W), layer, mq, mk, mv)
    o_mem = _mem_sample(proj[:, o_eq:o_gate], _mem_t(mem_cache))
    y = _mix_out(x, o_ret, o_moba, o_mem, proj[:, o_gate:], w_o_bf16, None, bsz)
    rows = proj[:, o_mq + MOBA_W:o_eq].reshape(bsz, 1, 2, MOBA_H, HD)
    return y, rows, s_ret


def _cols(a):
    bsz = a.shape[0]
    return a.reshape(bsz, NSA_KVH, NSA_REP, HD).transpose(0, 1, 3, 2).reshape(bsz, NSA_KVW, NSA_REP)


def _uncols(a):
    bsz = a.shape[0]
    return a.reshape(bsz, NSA_KVH, HD, NSA_REP).transpose(0, 1, 3, 2).reshape(bsz, NSA_QW)


def _odd_sample(x, cos, sin, g, w_aug_bf16, w_o_bf16, mem_cache, state, lb, page_table, cmp_pool, slc_pool, wbuf,
                layer, cmp_w_t, final_g):
    bsz = x.shape[0]
    lbuf = wbuf.shape[1]
    kvo = lambda j: ODD_KV + j * NSA_KVW
    proj = _norm_proj(x, g, w_aug_bf16, cos, sin, ((ODD_NQR, ODD_KV), (kvo(2), kvo(3)), (kvo(4), kvo(5))), bsz)
    ck, cv, sk, sv, wk, wv = (proj[:, kvo(j):kvo(j + 1)] for j in range(6))
    wd, ped, w2t = cmp_w_t
    n_pool, n_layer = cmp_pool.shape[0], cmp_pool.shape[1]
    cmp_t = cmp_pool.transpose(0, 1, 3, 4, 5, 2)
    o_c, sel = _nsa_cmp_sample(page_table, cmp_t, layer, proj[:, ODD_NQ:ODD_NQ + NSA_QW].reshape(bsz, NSA_H, HD),
                               wd, ped, w2t)
    qr = proj[:, ODD_NQR:ODD_NQR + NSA_QW]
    q_cols = _cols(qr)
    o_s = _nsa_sel_sample(page_table, slc_pool.transpose(0, 1, 3, 4, 5, 2), layer, qr.reshape(bsz, NSA_H, HD),
                          sk.reshape(bsz, NSA_KVH, HD), sv.reshape(bsz, NSA_KVH, HD), sel).reshape(bsz, NSA_QW)
    wbuf_t = wbuf.transpose(0, 2, 3, 4, 1).reshape(bsz, 2, NSA_KVW, lbuf)
    o_w = _win_sample(q_cols, wbuf_t, wk, wv)
    ng = proj[:, ODD_NG:ODD_NG + 3 * NSA_H].reshape(bsz, NSA_H, 3)
    gate_logits = jnp.repeat(ng.transpose(2, 0, 1), HD, axis=-1)
    o_nsa = _gate_combine(gate_logits, o_c.reshape(bsz, NSA_QW), o_s, _uncols(o_w))
    hq, hf, hi = (proj[:, ODD_HQ + j * HGRN_W:ODD_HQ + (j + 1) * HGRN_W] for j in range(3))
    o_hg, s_hg = _state_step(state, hq, hf, jnp.broadcast_to(lb.reshape(1, HGRN_W), (bsz, HGRN_W)), hi, "hgrn")
    o_mem = _mem_sample(proj[:, ODD_EQ:ODD_GATE], _mem_t(mem_cache))
    y = _mix_out(x, o_nsa, o_hg, o_mem, proj[:, ODD_GATE:ODD_NG], w_o_bf16, final_g, bsz)
    rows = lambda a, b: jnp.stack([a.reshape(bsz, 1, NSA_KVH, HD), b.reshape(bsz, 1, NSA_KVH, HD)], axis=2)
    win = jnp.concatenate([wbuf.astype(F32), rows(wk, wv)], axis=1)[:, -lbuf:]
    return y, rows(ck, cv), rows(sk, sv), win, s_hg


def kernel(x_prompt, x_sample, mem_prompt, cache_moba_kv, state_ret, cache_nsa_cmp_kv, cache_nsa_slc_kv,
           cache_nsa_win_kv, state_hgrn, cache_mem_kv, page_table, norm_g, mem_norm_g, w_mem_kv, w_in_even,
           w_in_odd, w_out, cmp_w1, cmp_w2, cmp_pe, hgrn_lb_logits, final_g):
    bp, tp, d = x_prompt.shape
    bs, ts, _ = x_sample.shape
    depth = w_out.shape[0]
    assert bp == 1 and ts == 1 and depth == 2
    n_mem = mem_prompt.shape[1]
    past_len = page_table.shape[1] * PAGE
    assert past_len % MOBA_BLOCK == 0 and cache_moba_kv.shape[2] == PAGE
    tm, tq, kc = 256, 128, 256
    xp, xs, mem = x_prompt[0], x_sample[:, 0], mem_prompt[0]
    cos_p, sin_p = _rope_tables(jnp.arange(tp, dtype=jnp.int32))
    cos_s, sin_s = (jnp.broadcast_to(a, (bs, LANES)) for a in _rope_tables(jnp.full((1,), past_len, jnp.int32)))
    lb_prob = jax.nn.softmax(hgrn_lb_logits.astype(F32), axis=0)
    lb_all = jnp.cumsum(lb_prob, axis=0) - lb_prob[0]
    lw_p = min(NSA_WINDOW, tp)

    def mem_kv(layer):
        kv = _norm_proj(mem, mem_norm_g[layer], w_mem_kv[layer].astype(BF16), cos_p[:n_mem], sin_p[:n_mem], (), n_mem)
        return kv.reshape(n_mem, 2, MEM_W)

    w_in0 = w_in_even[0].astype(BF16)
    w_o0 = w_out[0].astype(BF16)
    mkv0 = mem_kv(0)
    hp, moba_p, ret_p = _even_prompt(xp, cos_p, sin_p, norm_g[0], w_in0, w_o0, mkv0[:, 0], mkv0[:, 1], tm)
    hs, moba_s, ret_s = _even_sample(xs, cos_s, sin_s, norm_g[0], w_in0, w_o0, cache_mem_kv[0], state_ret[0],
                                     page_table, cache_moba_kv, 0)

    w_in1 = _odd_weights(w_in_odd[0]).astype(BF16)
    w_o1 = w_out[1].astype(BF16)
    mkv1 = mem_kv(1)
    cmp_w = _compress_weights(cmp_w1[0], cmp_w2[0], cmp_pe[0])
    cmp_w_t = _compress_weights_t(cmp_w1[0], cmp_w2[0], cmp_pe[0])
    yp, cmp_p, slc_p, win_p, hg_p = _odd_prompt(hp, cos_p, sin_p, norm_g[1], w_in1, w_o1, mkv1[:, 0], mkv1[:, 1],
                                                lb_all[1], cmp_w, final_g, tm, tq, kc)
    ys, cmp_s, slc_s, win_s, hg_s = _odd_sample(hs, cos_s, sin_s, norm_g[1], w_in1, w_o1, cache_mem_kv[1],
                                                state_hgrn[0], lb_all[1], page_table, cache_nsa_cmp_kv,
                                                cache_nsa_slc_kv, cache_nsa_win_kv[0], 0, cmp_w_t, final_g)

    return (yp[None], ys[:, None], moba_p[None, None], moba_s[:, None], ret_p[None, None], ret_s[None],
            cmp_p[None, None], cmp_s[:, None], slc_p[None, None], slc_s[:, None], win_p[None, tp - lw_p:][None],
            win_s[None], hg_p[None, None], hg_s[None], jnp.stack([mkv0, mkv1]).reshape(depth, 1, n_mem, 2, MEM_H, HD))
```

```python
import functools
import math

import jax
import jax.numpy as jnp
import numpy as np
from jax import lax
from jax.experimental import pallas as pl
from jax.experimental.pallas import tpu as pltpu

F32 = jnp.float32
BF16 = jnp.bfloat16
HIGHEST = lax.Precision.HIGHEST

HD = 64
RET_H, MOBA_H, MEM_H, NSA_H, NSA_KVH, HGRN_H = 6, 6, 4, 8, 2, 4
NSA_REP = NSA_H // NSA_KVH
RET_W, MOBA_W, MEM_W = RET_H * HD, MOBA_H * HD, MEM_H * HD
NSA_QW, NSA_KVW, HGRN_W = NSA_H * HD, NSA_KVH * HD, HGRN_H * HD
MIX_W = RET_W + MOBA_W + MEM_W
PAGE = 128
RET_CHUNK, HGRN_CHUNK = 128, 64
MOBA_BLOCK, MOBA_TOPK = 256, 3
NSA_BLOCK, NSA_TOPN, NSA_WINDOW = 64, 16, 512
ROPE_THETA = 10000.0
EPS = 1e-6
NEG = -1e30
FORCE = 1e30
TINY = 1e-30
SCALE = HD ** -0.5
LANES = 128
SHIFT_SLACK = 1.02
SHIFT_LIMIT = 20.0
VMEM_LIMIT = 56 * 1024 * 1024


def _params(*sem):
    return pltpu.CompilerParams(dimension_semantics=sem, vmem_limit_bytes=VMEM_LIMIT)


def _dot(a, b, precision=None):
    return jnp.dot(a, b, preferred_element_type=F32, precision=precision)


def _dot_nt(a, b, precision=None):
    return lax.dot_general(a, b, (((1,), (1,)), ((), ())), preferred_element_type=F32, precision=precision)


def _dot_tn(a, b, precision=None):
    return lax.dot_general(a, b, (((0,), (0,)), ((), ())), preferred_element_type=F32, precision=precision)


def _block_ones(width, value=1.0):
    r = lax.broadcasted_iota(jnp.int32, (width, width), 0) // HD
    c = lax.broadcasted_iota(jnp.int32, (width, width), 1) // HD
    return jnp.where(r == c, value, 0.0).astype(F32)


def _rope_tile(a, cos, sin, first_half):
    rot = jnp.where(first_half, pltpu.roll(a, LANES - HD // 2, 1), pltpu.roll(a, HD // 2, 1))
    return a * cos + rot * sin


def _norm_proj_body(x_ref, g_ref, w_ref, cos_ref, sin_ref, *o_refs, rope_tiles, n_chunk, starts):
    x = x_ref[...]
    ms = jnp.mean(x * x, axis=-1, keepdims=True)
    y = (x * lax.rsqrt(ms + EPS) * g_ref[...]).astype(BF16)
    n = w_ref.shape[1]
    lane = lax.broadcasted_iota(jnp.int32, (x.shape[0], LANES), 1)
    first_half = (lane % HD) < HD // 2
    for c0 in range(0, n, n_chunk):
        acc = _dot(y, w_ref[:, c0:c0 + n_chunk])
        for j in range(n_chunk // LANES):
            tile = (c0 // LANES) + j
            which = max(i for i, s in enumerate(starts) if s <= tile * LANES)
            local = tile * LANES - starts[which]
            if local >= o_refs[which].shape[1]:
                continue
            a = acc[:, j * LANES:(j + 1) * LANES]
            if tile in rope_tiles:
                a = _rope_tile(a, cos_ref[...], sin_ref[...], first_half)
            o_refs[which][:, local:local + LANES] = a


def _norm_proj(x, g, w_bf16, cos, sin, rope_cols, tm, splits=None):
    m, d = x.shape
    n = w_bf16.shape[1]
    splits = (n,) if splits is None else tuple(splits)
    assert m % tm == 0 and n % LANES == 0 and all(s % LANES == 0 for s in splits) and sum(splits) <= n
    starts = tuple(int(s) for s in np.cumsum((0,) + splits[:-1]))
    tiles = n // LANES
    k = next(c for c in (4, 3, 2, 1) if tiles % c == 0)
    rope_tiles = frozenset(t for a, b in rope_cols for t in range(a // LANES, b // LANES))
    body = functools.partial(_norm_proj_body, rope_tiles=rope_tiles, n_chunk=k * LANES, starts=starts)
    outs = pl.pallas_call(
        body,
        grid=(m // tm,),
        in_specs=[
            pl.BlockSpec((tm, d), lambda i: (i, 0)),
            pl.BlockSpec((1, d), lambda i: (0, 0)),
            pl.BlockSpec((d, n), lambda i: (0, 0)),
            pl.BlockSpec((tm, LANES), lambda i: (i, 0)),
            pl.BlockSpec((tm, LANES), lambda i: (i, 0)),
        ],
        out_specs=[pl.BlockSpec((tm, s), lambda i: (i, 0)) for s in splits],
        out_shape=[jax.ShapeDtypeStruct((m, s), F32) for s in splits],
        compiler_params=_params("parallel"),
        name="norm_proj",
    )(x, g.reshape(1, d), w_bf16, cos, sin)
    return outs if len(splits) > 1 else outs[0]


def _rope_tables(pos):
    half = HD // 2
    inv = ROPE_THETA ** (-jnp.arange(half, dtype=F32) / half)
    ang = pos.astype(F32)[:, None] * inv[None, :]
    cos, sin = jnp.cos(ang), jnp.sin(ang)
    return jnp.concatenate([cos, cos, cos, cos], -1), jnp.concatenate([-sin, sin, -sin, sin], -1)


def _retention_consts(c):
    lg = np.log(1.0 - np.power(2.0, -5.0 - np.arange(RET_H, dtype=np.float64)))
    ti = np.arange(c, dtype=np.float64)
    causal = ti[:, None] >= ti[None, :]
    d_in = np.where(causal[None], np.exp(np.where(causal, ti[:, None] - ti[None, :], 0.0)[None] * lg[:, None, None]), 0.0)
    q_dec = np.repeat(np.exp((ti[:, None] + 1.0) * lg[None, :]), HD, axis=1)
    k_dec = np.repeat(np.exp((c - 1.0 - ti)[:, None] * lg[None, :]), HD, axis=1)
    c_dec = np.repeat(np.exp(c * lg), HD)
    head = np.arange(RET_W) // HD
    bd = (head[:, None] == head[None, :]).astype(np.float64)
    cmat = bd * c_dec[:, None]
    f = lambda a: jnp.asarray(a, dtype=F32)
    return f(d_in), f(q_dec), f(k_dec), f(cmat), f(bd)


def _retention_body(q_ref, k_ref, v_ref, din_ref, qdec_ref, kdec_ref, cmat_ref, bd_ref, o_ref, s_ref):
    @pl.when(pl.program_id(0) == 0)
    def _():
        s_ref[...] = jnp.zeros_like(s_ref)

    q = q_ref[...]
    k = k_ref[...] * SCALE
    v = v_ref[...]
    c, w = q.shape
    head = lax.broadcasted_iota(jnp.int32, (c, w), 1) // HD
    s = s_ref[...]
    qb, kb, vb = q.astype(BF16), k.astype(BF16), v.astype(BF16)
    o = _dot(qb, s.astype(BF16)) * qdec_ref[...]
    for h in range(RET_H):
        mh = head == h
        att = _dot_nt(jnp.where(mh, qb, jnp.zeros_like(qb)), kb) * din_ref[h]
        o = o + jnp.where(mh, _dot(att.astype(BF16), vb), 0.0)
    s_ref[...] = s * cmat_ref[...] + _dot_tn((k * kdec_ref[...]).astype(BF16), vb) * bd_ref[...]
    seg = bd_ref[...] * (1.0 / HD)
    xc = o - _dot(o, seg, HIGHEST)
    var = _dot(xc * xc, seg, HIGHEST)
    o_ref[...] = xc * lax.rsqrt(var + EPS)


def _retention_prompt(qkv):
    t, w = qkv.shape[0], RET_W
    c = math.gcd(t, RET_CHUNK)
    d_in, q_dec, k_dec, cmat, bd = _retention_consts(c)
    col = lambda j: pl.BlockSpec((c, w), lambda i: (i, j))
    const2 = lambda shape: pl.BlockSpec(shape, lambda i: (0,) * len(shape))
    return pl.pallas_call(
        _retention_body,
        grid=(t // c,),
        in_specs=[col(0), col(1), col(2), const2((RET_H, c, c)), const2((c, w)), const2((c, w)), const2((w, w)),
                  const2((w, w))],
        out_specs=[col(0), const2((w, w))],
        out_shape=[jax.ShapeDtypeStruct((t, w), F32), jax.ShapeDtypeStruct((w, w), F32)],
        compiler_params=_params("arbitrary"),
        name="retention_prompt",
    )(qkv, qkv, qkv, d_in, q_dec, k_dec, cmat, bd)


def _diag_blocks(s, h):
    s4 = s.reshape(h, HD, h, HD)
    return jnp.stack([s4[i, :, i, :] for i in range(h)], axis=0)


def _block_stats_body(k_ref, mean_ref, norm_ref):
    k = k_ref[...]
    mean_ref[0] = jnp.mean(k, axis=0, keepdims=True)
    n2 = _dot((k * k).astype(BF16), _block_ones(k.shape[1]).astype(BF16))
    norm_ref[0] = jnp.max(n2, axis=0, keepdims=True)


def _block_stats(k, blk, w, col):
    t = k.shape[0]
    assert t % blk == 0
    mean, norm = pl.pallas_call(
        _block_stats_body,
        grid=(t // blk,),
        in_specs=[pl.BlockSpec((blk, w), lambda i: (i, col))],
        out_specs=[pl.BlockSpec((1, 1, w), lambda i: (i, 0, 0))] * 2,
        out_shape=[jax.ShapeDtypeStruct((t // blk, 1, w), F32)] * 2,
        compiler_params=_params("parallel"),
        name="block_stats",
    )(k)
    return mean.reshape(t // blk, w), norm.reshape(t // blk, w)


def _key_norm_bound(norm2, heads):
    kmax = jnp.sqrt(jnp.max(norm2, axis=0)).reshape(heads, HD)[:, :1]
    return jnp.broadcast_to(kmax[:, :, None], (heads, 1, LANES))


def _topk_mask(score, k):
    n = score.shape[-1]
    idx = lax.broadcasted_iota(jnp.int32, score.shape, score.ndim - 1)
    sel = jnp.zeros(score.shape, F32)
    work = score
    for _ in range(k):
        m = jnp.max(work, axis=-1, keepdims=True)
        first = jnp.min(jnp.where(work == m, idx, n), axis=-1, keepdims=True)
        pick = idx == first
        sel = jnp.where(pick, 1.0, sel)
        work = jnp.where(pick, -jnp.inf, work)
    return sel


def _tile_loop(lo, hi, fn, unroll=8):
    def group(g, _):
        for u in range(unroll):
            fn(lo + unroll * g + u)
        return 0
    n = jnp.maximum(hi - lo, 0)
    main = n // unroll
    lax.fori_loop(0, main, group, 0)
    done = lo + main * unroll
    rem = n - main * unroll
    size = unroll // 2
    while size >= 1:
        @pl.when((rem & size) != 0)
        def _(done=done, size=size):
            for u in range(size):
                fn(done + u)
        done = done + (rem & size)
        size //= 2


def _lane_fold_max(s):
    out = s[:, 0:LANES]
    for c in range(1, s.shape[1] // LANES):
        out = jnp.maximum(out, s[:, c * LANES:(c + 1) * LANES])
    return out


def _moba_prompt_body(q_ref, ka_ref, vo_ref, kmean_ref, kmax_ref, o_ref, m_ref, acc_ref, *, topk, bq):
    qi = pl.program_id(1)
    q = jnp.concatenate([q_ref[0], jnp.zeros(q_ref.shape[1:], F32)], axis=1)
    rows = q.shape[0]
    kb = MOBA_BLOCK
    gate = _dot_nt(q, kmean_ref[0], HIGHEST)
    blk = lax.broadcasted_iota(jnp.int32, (rows, LANES), 1) - HD
    own = qi * bq + lax.broadcasted_iota(jnp.int32, (rows, LANES), 0) // kb
    past = (blk >= 0) & (blk < own)
    sel = jnp.where(past, _topk_mask(jnp.where(past, gate, NEG), topk), 0.0)
    bias = jnp.where((blk < 0) | (blk == own) | (sel > 0.5), 0.0, NEG)
    qa = (q * SCALE + bias).astype(BF16)
    n_past = qi * bq
    qpos = lax.broadcasted_iota(jnp.int32, (rows, kb), 0)
    kpos = lax.broadcasted_iota(jnp.int32, (rows, kb), 1)
    acc_ref[...] = jnp.zeros(acc_ref.shape, F32)

    def scores(j):
        start = pl.multiple_of(j * kb, kb)
        return _dot_nt(qa, ka_ref[0, pl.ds(start, kb), :])

    def own_scores(d):
        return jnp.where(kpos + d * kb <= qpos, scores(n_past + d), NEG)

    def add_values(j, p):
        start = pl.multiple_of(j * kb, kb)
        acc_ref[...] += _dot(p.astype(BF16), vo_ref[0, pl.ds(start, kb), :])

    bound = jnp.sqrt(jnp.sum(q * q, axis=-1, keepdims=True)) * kmax_ref[0][:, 0:1] * (SCALE * SHIFT_SLACK)
    small = jnp.max(bound) <= SHIFT_LIMIT

    @pl.when(small)
    def _():
        _tile_loop(0, n_past, lambda j: add_values(j, jnp.exp(scores(j) - bound)))
        for d in range(bq):
            add_values(n_past + d, jnp.exp(own_scores(d) - bound))

    @pl.when(jnp.logical_not(small))
    def _():
        m_ref[...] = jnp.full(m_ref.shape, NEG, F32)

        def fold(s):
            m_ref[...] = jnp.maximum(m_ref[...], _lane_fold_max(s))

        _tile_loop(0, n_past, lambda j: fold(scores(j)))
        for d in range(bq):
            fold(own_scores(d))
        m = jnp.max(m_ref[...], axis=-1, keepdims=True)
        _tile_loop(0, n_past, lambda j: add_values(j, jnp.exp(scores(j) - m)))
        for d in range(bq):
            add_values(n_past + d, jnp.exp(own_scores(d) - m))

    acc = acc_ref[...]
    o_ref[0] = (acc / jnp.maximum(acc[:, 0:1], TINY))[:, HD:]


def _moba_prompt(q_pad, ka_bf16, vo_bf16, kmean_rows, kmax):
    h, t, _ = q_pad.shape
    nblk = t // MOBA_BLOCK
    assert t % MOBA_BLOCK == 0 and nblk <= LANES - HD
    bq = next(c for c in (4, 2, 1) if nblk % c == 0)
    rows = bq * MOBA_BLOCK
    return pl.pallas_call(
        functools.partial(_moba_prompt_body, topk=min(MOBA_TOPK, nblk), bq=bq),
        grid=(h, nblk // bq),
        in_specs=[
            pl.BlockSpec((1, rows, HD), lambda a, i: (a, i, 0)),
            pl.BlockSpec((1, t, LANES), lambda a, i: (a, 0, 0), pipeline_mode=pl.Buffered(1)),
            pl.BlockSpec((1, t, LANES), lambda a, i: (a, 0, 0), pipeline_mode=pl.Buffered(1)),
            pl.BlockSpec((1, LANES, LANES), lambda a, i: (a, 0, 0)),
            pl.BlockSpec((1, 1, LANES), lambda a, i: (a, 0, 0)),
        ],
        out_specs=pl.BlockSpec((1, rows, HD), lambda a, i: (a, i, 0)),
        out_shape=jax.ShapeDtypeStruct((h, t, HD), F32),
        scratch_shapes=[pltpu.VMEM((rows, LANES), F32), pltpu.VMEM((rows, LANES), F32)],
        compiler_params=_params("parallel", "arbitrary"),
        name="moba_prompt",
    )(q_pad, ka_bf16, vo_bf16, kmean_rows, kmax)


def _mem_prompt_body(q_ref, k_ref, v_ref, o_ref):
    q = q_ref[...]
    kb = k_ref[...].astype(BF16)
    vb = v_ref[...].astype(BF16)
    head = lax.broadcasted_iota(jnp.int32, q.shape, 1) // HD
    o = jnp.zeros(q.shape, F32)
    for h in range(MEM_H):
        mh = head == h
        s = _dot_nt(jnp.where(mh, q, 0.0).astype(BF16), kb) * SCALE
        e = jnp.exp(s - jnp.max(s, axis=-1, keepdims=True))
        p = e / jnp.sum(e, axis=-1, keepdims=True)
        o = o + jnp.where(mh, _dot(p.astype(BF16), vb), 0.0)
    o_ref[...] = o


def _mem_prompt(q, k, v, tm):
    t, w = q.shape
    n = k.shape[0]
    return pl.pallas_call(
        _mem_prompt_body,
        grid=(t // tm,),
        in_specs=[pl.BlockSpec((tm, w), lambda i: (i, 0)), pl.BlockSpec((n, w), lambda i: (0, 0)),
                  pl.BlockSpec((n, w), lambda i: (0, 0))],
        out_specs=pl.BlockSpec((tm, w), lambda i: (i, 0)),
        out_shape=jax.ShapeDtypeStruct((t, w), F32),
        compiler_params=_params("parallel"),
        name="mem_prompt",
    )(q, k, v)


def _mix_out_body(x_ref, a_ref, b_ref, c_ref, gate_ref, wo_ref, fg_ref, o_ref, *, final_norm):
    gate = gate_ref[...]
    gate = gate * jax.nn.sigmoid(gate)
    y = x_ref[...]
    off = 0
    for ref in (a_ref, b_ref, c_ref):
        w = ref.shape[1]
        mix = (ref[...] * gate[:, off:off + w]).astype(BF16)
        y = y + _dot(mix, wo_ref[off:off + w, :])
        off += w
    if final_norm:
        ms = jnp.mean(y * y, axis=-1, keepdims=True)
        y = y * lax.rsqrt(ms + EPS) * fg_ref[...]
    o_ref[...] = y


def _mix_out(x, a, b, c, gate, wo_bf16, final_g, tm):
    m, d = x.shape
    row = lambda w: pl.BlockSpec((tm, w), lambda i: (i, 0))
    fg = jnp.ones((1, d), F32) if final_g is None else final_g.reshape(1, d).astype(F32)
    body = functools.partial(_mix_out_body, final_norm=final_g is not None)
    return pl.pallas_call(
        body,
        grid=(m // tm,),
        in_specs=[row(d), row(a.shape[1]), row(b.shape[1]), row(c.shape[1]), row(gate.shape[1]),
                  pl.BlockSpec(wo_bf16.shape, lambda i: (0, 0)), pl.BlockSpec((1, d), lambda i: (0, 0))],
        out_specs=row(d),
        out_shape=jax.ShapeDtypeStruct((m, d), F32),
        compiler_params=_params("parallel"),
        name="mix_out",
    )(x, a, b, c, gate, wo_bf16, fg)


def _compress_body(x_ref, pe_ref, w1_ref, w2_ref, o_ref):
    x = (x_ref[0] + pe_ref[0]).astype(BF16)
    hid = jax.nn.gelu(_dot(x, w1_ref[0]))
    o_ref[0] = _dot(hid.astype(BF16), w2_ref[0])


def _compress_weights(w1, w2, pe):
    eye = jnp.eye(NSA_KVH, dtype=F32)
    w1r = w1.astype(F32).reshape(2, NSA_BLOCK, HD, HD)
    w1b = jnp.einsum("klde,gh->klgdhe", w1r, eye).reshape(2, NSA_BLOCK * NSA_KVW, NSA_KVW)
    w2b = jnp.einsum("kde,gh->kgdhe", w2.astype(F32), eye).reshape(2, NSA_KVW, NSA_KVW)
    peb = jnp.broadcast_to(pe.astype(F32)[:, :, None, :], (2, NSA_BLOCK, NSA_KVH, HD)).reshape(2, 1, NSA_BLOCK * NSA_KVW)
    return w1b.astype(BF16), w2b.astype(BF16), peb


def _compress(x, w1b, w2b, peb):
    _, nb, kdim = x.shape
    return pl.pallas_call(
        _compress_body,
        grid=(2,),
        in_specs=[pl.BlockSpec((1, nb, kdim), lambda i: (i, 0, 0)), pl.BlockSpec((1, 1, kdim), lambda i: (i, 0, 0)),
                  pl.BlockSpec((1, kdim, NSA_KVW), lambda i: (i, 0, 0)),
                  pl.BlockSpec((1, NSA_KVW, NSA_KVW), lambda i: (i, 0, 0))],
        out_specs=pl.BlockSpec((1, nb, NSA_KVW), lambda i: (i, 0, 0)),
        out_shape=jax.ShapeDtypeStruct((2, nb, NSA_KVW), F32),
        compiler_params=_params("parallel"),
        name="nsa_compress",
    )(x, peb, w1b, w2b)


def _nsa_prompt_body(q_ref, qr_ref, gt_ref, ckv_ref, ska_ref, svo_ref, wkv_ref, kmax_ref, o_ref, m_ref, acc_ref,
                     *, tq, kc):
    i = pl.program_id(1)
    t0 = i * tq
    rows = q_ref.shape[2]
    nb = ckv_ref.shape[1]
    pad = lambda a: jnp.concatenate([a, jnp.zeros_like(a)], axis=1)
    q = pad(q_ref[0, 0])
    qr = pad(qr_ref[0, 0]).astype(F32)
    qrs = qr * SCALE
    qrb = qrs.astype(BF16)
    tpos = t0 + lax.broadcasted_iota(jnp.int32, (rows, 1), 0) % tq

    ckv = ckv_ref[0].astype(BF16)
    blk = lax.broadcasted_iota(jnp.int32, (nb, rows), 0)
    tpos_l = t0 + lax.broadcasted_iota(jnp.int32, (nb, rows), 1) % tq
    complete = blk * NSA_BLOCK + (NSA_BLOCK - 1) <= tpos_l
    s_c = jnp.where(complete, _dot_nt(ckv, q) * SCALE, NEG)
    e = jnp.where(complete, jnp.exp(s_c - jnp.max(s_c, axis=0, keepdims=True)), 0.0)
    pc = e / jnp.maximum(jnp.sum(e, axis=0, keepdims=True), TINY)
    o_c = _dot_tn(pc.astype(BF16), ckv)

    imp = pc[:, 0:tq]
    for r in range(1, NSA_REP):
        imp = imp + pc[:, r * tq:(r + 1) * tq]
    blk_q = lax.broadcasted_iota(jnp.int32, (nb, tq), 0)
    tpos_q = t0 + lax.broadcasted_iota(jnp.int32, (nb, tq), 1)
    cur = tpos_q // NSA_BLOCK
    forced = (blk_q == 0) | (blk_q == cur) | (blk_q == cur - 1)
    complete_q = blk_q * NSA_BLOCK + (NSA_BLOCK - 1) <= tpos_q
    score = jnp.where(forced, FORCE, jnp.where(complete_q, imp, NEG))
    sel = jnp.where(score > 0.5 * NEG, _topk_mask_axis0(score, min(NSA_TOPN, nb)), 0.0)

    bias = jnp.where(sel > 0.5, 0.0, NEG).T
    nbp = -(-nb // LANES) * LANES
    if nbp > nb:
        bias = jnp.concatenate([bias, jnp.full((tq, nbp - nb), NEG, F32)], axis=1)
    bias = jnp.concatenate([bias] * NSA_REP, axis=0)
    upper =lax.broadcasted_iota(jnp.int32, (rows, LANES), 1) >= HD
    n_span = -(-nb // HD)
    qa = []
    for sp in range(n_span):
        col = bias[:, (sp // 2) * LANES:(sp // 2 + 1) * LANES]
        if sp % 2 == 0:
            col = pltpu.roll(col, HD, 1)
        qa.append((qrs + jnp.where(upper, col, 0.0)).astype(BF16))
    cps = HD * NSA_BLOCK // kc
    n_chunks = (t0 + tq + kc - 1) // kc
    last = n_chunks - 1
    qa_last = qa[0]
    for sp in range(1, n_span):
        qa_last = jnp.where(last // cps == sp, qa[sp], qa_last)
    last_start = pl.multiple_of(last * kc, kc)
    kpos = last_start + lax.broadcasted_iota(jnp.int32, (rows, kc), 1)
    s_last = jnp.where(kpos <= tpos, _dot_nt(qa_last, ska_ref[0, pl.ds(last_start, kc), :]), NEG)

    def spans(fn):
        for sp in range(n_span):
            lo = sp * cps
            _tile_loop(jnp.minimum(lo, last), jnp.minimum(lo + cps, last), functools.partial(fn, qa[sp]))

    def scores(qsp, c):
        start = pl.multiple_of(c * kc, kc)
        return _dot_nt(qsp, ska_ref[0, pl.ds(start, kc), :])

    def add_values(c, p):
        start = pl.multiple_of(c * kc, kc)
        acc_ref[...] += _dot(p.astype(BF16), svo_ref[0, pl.ds(start, kc), :])

    acc_ref[...] = jnp.zeros(acc_ref.shape, F32)
    bound =jnp.sqrt(jnp.sum(qr * qr, axis=-1, keepdims=True)) * kmax_ref[0][:, 0:1] * (SCALE * SHIFT_SLACK)
    small = jnp.max(bound) <= SHIFT_LIMIT

    @pl.when(small)
    def _():
        add_values(last, jnp.exp(s_last - bound))
        spans(lambda qsp, c: add_values(c, jnp.exp(scores(qsp, c) - bound)))

    @pl.when(jnp.logical_not(small))
    def _():
        m_ref[...] = _lane_fold_max(s_last)

        def fold_max(qsp, c):
            m_ref[...] = jnp.maximum(m_ref[...], _lane_fold_max(scores(qsp, c)))

        spans(fold_max)
        m_s = jnp.max(m_ref[...], axis=-1, keepdims=True)
        add_values(last, jnp.exp(s_last - m_s))
        spans(lambda qsp, c: add_values(c, jnp.exp(scores(qsp, c) - m_s)))

    acc_s = acc_ref[...]
    o_s = acc_s / jnp.maximum(acc_s[:, 0:1], TINY)

    nwin = (NSA_WINDOW + tq - 1) // tq + 1
    win = []
    for d in range(nwin):
        c = i - (nwin - 1) + d
        start = pl.multiple_of(jnp.maximum(c, 0) * tq, tq)
        kv = wkv_ref[0, pl.ds(start, tq), :]
        kpos = c * tq + lax.broadcasted_iota(jnp.int32, (rows, tq), 1)
        mask = (kpos <= tpos) & (kpos > tpos - NSA_WINDOW) & (kpos >= 0)
        win.append((jnp.where(mask, _dot_nt(qrb, kv), NEG), kv))
    m_w = jnp.max(functools.reduce(jnp.maximum, [s for s, _ in win]), axis=-1, keepdims=True)
    p_w = [jnp.exp(s - m_w) for s, _ in win]
    l_w = jnp.sum(functools.reduce(jnp.add, p_w), axis=-1, keepdims=True)
    acc_w = functools.reduce(jnp.add, [_dot(p.astype(BF16), kv) for p, (_, kv) in zip(p_w, win)])
    o_w = acc_w / jnp.maximum(l_w, TINY)

    gt = jax.nn.sigmoid(gt_ref[0, 0])
    o_ref[0, 0] = (gt[:, 0:1] * o_c + gt[:, 1:2] * o_s + gt[:, 2:3] * o_w)[:, HD:]


def _nsa_prompt(q_st, qr_st, gt_st, ckv, ska, svo, wkv, kmax, tq, kc):
    g, nt, rows, _ = q_st.shape
    t = ska.shape[1]
    nb = ckv.shape[1]
    assert t % kc == 0 and kc % tq == 0 and (HD * NSA_BLOCK) % kc == 0 and t % tq == 0
    qspec = pl.BlockSpec((1, 1, rows, HD), lambda a, i: (a, i, 0, 0))
    seq = pl.BlockSpec((1, t, LANES), lambda a, i: (a, 0, 0), pipeline_mode=pl.Buffered(1))
    body = functools.partial(_nsa_prompt_body, tq=tq, kc=kc)
    return pl.pallas_call(
        body,
        grid=(g, nt),
        in_specs=[qspec, qspec, pl.BlockSpec((1, 1, rows, 3), lambda a, i: (a, i, 0, 0)),
                  pl.BlockSpec((1, nb, LANES), lambda a, i: (a, 0, 0)), seq, seq, seq,
                  pl.BlockSpec((1, 1, LANES), lambda a, i: (a, 0, 0))],
        out_specs=qspec,
        out_shape=jax.ShapeDtypeStruct((g, nt, rows, HD), F32),
        scratch_shapes=[pltpu.VMEM((rows, LANES), F32), pltpu.VMEM((rows, LANES), F32)],
        compiler_params=_params("parallel", "arbitrary"),
        name="nsa_prompt",
    )(q_st, qr_st, gt_st, ckv, ska, svo, wkv, kmax)


def _hgrn_body(q_ref, f_ref, i_ref, lb_ref, o_ref, s_ref):
    @pl.when(pl.program_id(0) == 0)
    def _():
        s_ref[...] = jnp.zeros_like(s_ref)

    q = q_ref[...]
    v = i_ref[...]
    c, w = q.shape
    lb = lb_ref[...]
    f = lb + (1.0 - lb) * jax.nn.sigmoid(f_ref[...])
    kk = 1.0 - f
    tr = lax.broadcasted_iota(jnp.int32, (c, c), 0)
    tc = lax.broadcasted_iota(jnp.int32, (c, c), 1)
    cum = _dot(jnp.where(tr >= tc, 1.0, 0.0), jnp.log(f), HIGHEST)
    bd = _block_ones(w)
    bd_b = bd.astype(BF16)
    s = s_ref[...]
    o = _dot(q * jnp.exp(cum), s, HIGHEST)
    t_idx = lax.broadcasted_iota(jnp.int32, (c, w), 0)

    sub = 8
    groups = c // sub
    o_blk = [o[i * sub:(i + 1) * sub] for i in range(groups)]
    for g in range(groups):
        r0 = g * sub
        n = c - r0
        t_g = r0 + lax.broadcasted_iota(jnp.int32, (n, w), 0)
        es = [jnp.where(t_g >= j, jnp.exp(cum[r0:] - cum[j:j + 1]) * q[r0:] * kk[j:j + 1], 0.0)
              for j in range(r0, r0 + sub)]
        e = jnp.concatenate(es, axis=0)
        a = _dot(e.astype(BF16), bd_b)
        contrib = a[0:n] * v[r0:r0 + 1]
        for u in range(1, sub):
            contrib = contrib + a[u * n:(u + 1) * n] * v[r0 + u:r0 + u + 1]
        for i in range(g, groups):
            o_blk[i] = o_blk[i] + contrib[(i - g) * sub:(i - g + 1) * sub]
    o = jnp.concatenate(o_blk, axis=0)
    last = cum[c - 1:c, :]
    row0 = t_idx == 0
    scale_mat = _dot_tn(jnp.where(row0, jnp.exp(last), 0.0), jnp.where(row0, 1.0, 0.0), HIGHEST)
    s_ref[...] = s * scale_mat + _dot_tn(kk * jnp.exp(last - cum), v, HIGHEST) * bd
    ms = _dot(o * o, bd * (1.0 / HD), HIGHEST)
    o_ref[...] = o * lax.rsqrt(ms + EPS)


def _hgrn_prompt(qfi, lb):
    t, w = qfi.shape[0], HGRN_W
    c = math.gcd(t, HGRN_CHUNK)
    col = lambda j: pl.BlockSpec((c, w), lambda n: (n, j))
    return pl.pallas_call(
        _hgrn_body,
        grid=(t // c,),
        in_specs=[col(0), col(1), col(2), pl.BlockSpec((1, w), lambda n: (0, 0))],
        out_specs=[col(0), pl.BlockSpec((w, w), lambda n: (0, 0))],
        out_shape=[jax.ShapeDtypeStruct((t, w), F32), jax.ShapeDtypeStruct((w, w), F32)],
        compiler_params=_params("arbitrary"),
        name="hgrn_prompt",
    )(qfi, qfi, qfi, lb)


def _state_step_body(s_ref, q_ref, a_ref, b_ref, v_ref, o_ref, so_ref, *, mode):
    s = s_ref[...]
    if mode == "ret":
        k = a_ref[...] * SCALE
        dec = b_ref[...]
    else:
        lb = b_ref[...]
        dec = lb + (1.0 - lb) * jax.nn.sigmoid(a_ref[...])
        k = 1.0 - dec
    kv = k * v_ref[...]
    rows = s.shape[0]
    o = (q_ref[...] * (kv + dec * s)).reshape(rows // HD, HD, HD).sum(axis=1)
    so_ref[...] = dec * s + kv
    if mode == "ret":
        xc = o - jnp.mean(o, axis=-1, keepdims=True)
        o_ref[...] = xc * lax.rsqrt(jnp.mean(xc * xc, axis=-1, keepdims=True) + EPS)
    else:
        o_ref[...] = o * lax.rsqrt(jnp.mean(o * o, axis=-1, keepdims=True) + EPS)


def _state_step(state, q, a, b, v, mode):
    bsz, h = state.shape[0], state.shape[1]
    rows = bsz * h * HD
    col = lambda x: jnp.broadcast_to(x.reshape(bsz, h, HD, 1), (bsz, h, HD, HD)).reshape(rows, HD)
    vx = jnp.broadcast_to(v.reshape(bsz, h, 1, HD), (bsz, h, HD, HD)).reshape(rows, HD)
    bb = 8 if bsz % 8 == 0 else bsz
    br = bb * h * HD
    spec = pl.BlockSpec((br, HD), lambda i: (i, 0))
    ospec = pl.BlockSpec((br // HD, HD), lambda i: (i, 0))
    o, s_new = pl.pallas_call(
        functools.partial(_state_step_body, mode=mode),
        grid=(rows // br,),
        in_specs=[spec] * 5,
        out_specs=[ospec, spec],
        out_shape=[jax.ShapeDtypeStruct((rows // HD, HD), F32), jax.ShapeDtypeStruct((rows, HD), F32)],
        compiler_params=_params("parallel"),
        name="state_step_" + mode,
    )(state.reshape(rows, HD).astype(F32), col(q), col(a), col(b), vx)
    return o.reshape(bsz, h * HD), s_new.reshape(bsz, h, HD, HD)


def _expand_heads(p, width):
    g, n = p.shape
    return jnp.broadcast_to(p[:, None, :], (g, HD, n)).reshape(width, n)


def _head_sums(x, g):
    return x.reshape(g, HD, x.shape[-1]).sum(axis=1)


def _mem_sample_body(q_ref, kt_ref, vt_ref, o_ref):
    kt = kt_ref[0, 0]
    vt = vt_ref[0, 0]
    s = _head_sums(kt * q_ref[0], MEM_H) * SCALE
    e = jnp.exp(s - jnp.max(s, axis=-1, keepdims=True))
    p = e / jnp.sum(e, axis=-1, keepdims=True)
    o_ref[0] = jnp.sum(vt * _expand_heads(p, MEM_W), axis=-1, keepdims=True)


def _mem_sample(q, kvt):
    bsz, w = q.shape
    n = kvt.shape[-1]
    out = pl.pallas_call(
        _mem_sample_body,
        grid=(bsz,),
        in_specs=[pl.BlockSpec((1, w, 1), lambda i: (i, 0, 0)),
                  pl.BlockSpec((1, 1, w, n), lambda i: (i, 0, 0, 0)),
                  pl.BlockSpec((1, 1, w, n), lambda i: (i, 1, 0, 0))],
        out_specs=pl.BlockSpec((1, w, 1), lambda i: (i, 0, 0)),
        out_shape=jax.ShapeDtypeStruct((bsz, w, 1), F32),
        compiler_params=_params("parallel"),
        name="mem_sample",
    )(q.reshape(bsz, w, 1), kvt, kvt)
    return out.reshape(bsz, w)


def _topk_mask_axis0(score, k):
    n = score.shape[0]
    idx = lax.broadcasted_iota(jnp.int32, score.shape, 0)
    sel = jnp.zeros(score.shape, F32)
    work = score
    for _ in range(k):
        m = jnp.max(work, axis=0, keepdims=True)
        first = jnp.min(jnp.where(work == m, idx, n), axis=0, keepdims=True)
        pick = idx == first
        sel = jnp.where(pick, 1.0, sel)
        work = jnp.where(pick, -jnp.inf, work)
    return sel


def _moba_sample_body(pt_ref, pool_ref, qc_ref, qbd_ref, kn_ref, vn_ref, o_ref, buf, sem, s_ref, vbuf, vsem,
                      *, layer, chunk, topk):
    b = pl.program_id(0)
    n_pages = pt_ref.shape[1]
    n = n_pages // chunk
    groups, gw = MOBA_H, MOBA_W
    per = MOBA_BLOCK // PAGE
    nblk = n_pages // per

    def copy(t, i):
        page = pt_ref[t // n, (t % n) * chunk + i]
        return pltpu.make_async_copy(pool_ref.at[page, layer, 0], buf.at[t % 2, i], sem.at[t % 2])

    def start(t):
        lax.fori_loop(0, chunk, lambda i, _: (copy(t, i).start(), 0)[1], 0)

    def wait(t):
        lax.fori_loop(0, chunk, lambda i, _: (copy(t, i).wait(), 0)[1], 0)

    @pl.when(b == 0)
    def _():
        start(0)

    def k_step(c, _):
        t = b * n + c
        slot = t % 2

        @pl.when(t + 1 < pl.num_programs(0) * n)
        def _():
            start(t + 1)

        wait(t)
        for i in range(chunk):
            sc = _dot(qbd_ref[0], buf[slot, i].astype(BF16))
            s_ref[c * chunk + i] = sc[0:groups]
        return 0

    lax.fori_loop(0, n, k_step, 0)

    sc = s_ref[...].reshape(nblk, per, groups, PAGE)
    gate = jnp.sum(jnp.sum(sc, axis=1, keepdims=True), axis=-1, keepdims=True) * (1.0 / MOBA_BLOCK)
    blk = lax.broadcasted_iota(jnp.int32, gate.shape, 0)
    sel = jnp.zeros(gate.shape, F32)
    picks = []
    for _ in range(topk):
        first = jnp.min(jnp.where(gate == jnp.max(gate, axis=0, keepdims=True), blk, nblk), axis=0, keepdims=True)
        pick = blk == first
        sel = jnp.where(pick, 1.0, sel)
        gate = jnp.where(pick, -jnp.inf, gate)
        picks.append(first)

    v_copies = []
    for h in range(groups):
        for j in range(topk):
            first_page = picks[j][0, 0, h, 0] * per
            for half in range(per):
                v_copies.append((h, j * per + half, first_page + half))
    descs = [pltpu.make_async_copy(pool_ref.at[pt_ref[b, page], layer, 1, pl.ds(h * HD, HD)], vbuf.at[h, slot_j], vsem)
             for h, slot_j, page in v_copies]
    for d in descs:
        d.start()

    mask = jnp.broadcast_to(sel, (nblk, per, groups, PAGE)).reshape(n_pages, groups, PAGE) > 0.5
    s_own = _head_sums(jnp.broadcast_to(kn_ref[0] * qc_ref[0], (gw, PAGE)), groups) * SCALE
    s = jnp.where(mask, s_ref[...] * SCALE, NEG)
    m = jnp.maximum(jnp.max(jnp.max(s, axis=0), axis=-1, keepdims=True), s_own)
    p = jnp.where(mask, jnp.exp(s - m), 0.0)
    p_own = jnp.exp(s_own - m)
    den = jnp.maximum(jnp.sum(jnp.sum(p, axis=0), axis=-1, keepdims=True) + p_own, TINY)
    s_ref[...] = p

    for d in descs:
        d.wait()
    tot = []
    for h in range(groups):
        acc = jnp.zeros((HD, PAGE), F32)
        for hh, slot_j, page in v_copies:
            if hh == h:
                acc = acc + vbuf[h, slot_j] * s_ref[page][h:h + 1]
        tot.append(jnp.sum(acc, axis=-1, keepdims=True))
    tot = jnp.concatenate(tot, axis=0)
    o_ref[0] = (tot + _expand_heads(p_own, gw)[:, 0:1] * vn_ref[0]) / _expand_heads(den, gw)[:, 0:1]


def _moba_sample(page_table, pool_t, layer, q, k_new, v_new):
    bsz, n_pages = page_table.shape
    gw = MOBA_W
    chunk = math.gcd(n_pages, 16)
    per = MOBA_BLOCK // PAGE
    topk = min(MOBA_TOPK, n_pages // per)
    nq = -(-MOBA_H // 8) * 8
    qbd = jnp.einsum("bgd,gh->bghd", q.reshape(bsz, MOBA_H, HD), jnp.eye(MOBA_H, dtype=F32)).reshape(bsz, MOBA_H, gw)
    qbd = jnp.pad(qbd, ((0, 0), (0, nq - MOBA_H), (0, 0))).astype(BF16)
    col = pl.BlockSpec((1, gw, 1), lambda i, pt: (i, 0, 0))
    grid_spec = pltpu.PrefetchScalarGridSpec(
        num_scalar_prefetch=1,
        grid=(bsz,),
        in_specs=[pl.BlockSpec(memory_space=pl.ANY), col, pl.BlockSpec((1, nq, gw), lambda i, pt: (i, 0, 0)), col, col],
        out_specs=col,
        scratch_shapes=[
            pltpu.VMEM((2, chunk, gw, PAGE), F32),
            pltpu.SemaphoreType.DMA((2,)),
            pltpu.VMEM((n_pages, MOBA_H, PAGE), F32),
            pltpu.VMEM((MOBA_H, topk * per, HD, PAGE), F32),
            pltpu.SemaphoreType.DMA(()),
        ],
    )
    out = pl.pallas_call(
        functools.partial(_moba_sample_body, layer=layer, chunk=chunk, topk=topk),
        grid_spec=grid_spec,
        out_shape=jax.ShapeDtypeStruct((bsz, gw, 1), F32),
        compiler_params=_params("arbitrary"),
        name="moba_sample",
    )(page_table, pool_t, q.reshape(bsz, gw, 1), qbd, k_new.reshape(bsz, gw, 1), v_new.reshape(bsz, gw, 1))
    return out.reshape(bsz, gw)


CMP_DB = 8


def _compress_weights_t(w1, w2, pe):
    per = PAGE // NSA_BLOCK
    eye = jnp.eye(per, dtype=F32)
    w1r = w1.astype(F32).reshape(2, NSA_BLOCK, HD, HD)
    wd = jnp.einsum("klde,gh->kdglhe", w1r, eye).reshape(2, HD // CMP_DB, CMP_DB * PAGE, per * HD)
    ped = jnp.tile(pe.astype(F32).transpose(0, 2, 1), (1, 1, per)).reshape(2, HD // CMP_DB, 1, CMP_DB * PAGE)
    w2t = jnp.einsum("kde,gh->kgdhe", w2.astype(F32), eye).reshape(2, per * HD, per * HD)
    return wd.astype(BF16), ped, w2t.astype(BF16)


def _nsa_cmp_sample_body(pt_ref, pool_ref, q_ref, wd_ref, ped_ref, w2_ref, oc_ref, sel_ref, buf, sem, tok_ref, acc_ref,
                         *, layer, nseq):
    b = pl.program_id(0)
    j = pl.program_id(1)
    bsz, n_pages = pt_ref.shape
    t = b * 2 + j
    slot = t % 2
    m_rows = nseq * NSA_KVH * n_pages

    def copy(grp, kv, sq, p, g, sl):
        row = pl.multiple_of((((sl * nseq + sq) * NSA_KVH + g) * n_pages + p) * HD, HD)
        return pltpu.make_async_copy(pool_ref.at[pt_ref[grp * nseq + sq, p], layer, kv, g], buf.at[pl.ds(row, HD)],
                                     sem.at[sl])

    def each(fn):
        def run(grp, kv, sl):
            def page(p, _):
                for sq in range(nseq):
                    for g in range(NSA_KVH):
                        fn(copy(grp, kv, sq, p, g, sl))
                return 0
            lax.fori_loop(0, n_pages, page, 0)
        return run

    start = each(lambda c: c.start())
    wait = each(lambda c: c.wait())

    @pl.when(t == 0)
    def _():
        start(0, 0, 0)

    @pl.when(t + 1 < 2 * (bsz // nseq))
    def _():
        start((t + 1) // 2, (t + 1) % 2, 1 - slot)

    wait(b, j, slot)

    base = slot * m_rows * HD
    for dd in range(HD // CMP_DB):
        parts = [buf[pl.ds(base + dd * CMP_DB + u, m_rows, stride=HD), :] for u in range(CMP_DB)]
        a = jnp.concatenate(parts, axis=1) + ped_ref[0, dd]
        part = _dot(a.astype(BF16), wd_ref[0, dd])
        if dd == 0:
            acc_ref[...] = part
        else:
            acc_ref[...] += part
    tok_ref[j] = _dot(jax.nn.gelu(acc_ref[...]).astype(BF16), w2_ref[0])

    @pl.when(j == 1)
    def _():
        pos = n_pages * PAGE
        per = PAGE // NSA_BLOCK
        n_idx = (lax.broadcasted_iota(jnp.int32, (per, n_pages), 1) * per
                 + lax.broadcasted_iota(jnp.int32, (per, n_pages), 0))
        complete = n_idx * NSA_BLOCK + (NSA_BLOCK - 1) <= pos
        cur = pos // NSA_BLOCK
        forced = (n_idx == 0) | (n_idx == cur) | (n_idx == cur - 1)
        k_past = min(NSA_TOPN, per * n_pages + 1) - 1
        for sq, g in ((a, c) for a in range(nseq) for c in range(NSA_KVH)):
            qg = q_ref[sq, g * NSA_REP:(g + 1) * NSA_REP, :].astype(BF16)
            r0 = (sq * NSA_KVH + g) * n_pages
            ck = tok_ref[0, r0:r0 + n_pages, :].astype(BF16)
            cv = tok_ref[1, r0:r0 + n_pages, :].astype(BF16)
            s = [jnp.where(complete[h:h + 1], _dot_nt(qg, ck[:, h * HD:(h + 1) * HD]) * SCALE, NEG) for h in range(per)]
            m = functools.reduce(jnp.maximum, [jnp.max(x, axis=-1, keepdims=True) for x in s])
            e = [jnp.where(complete[h:h + 1], jnp.exp(s[h] - m), 0.0) for h in range(per)]
            den = jnp.maximum(sum(jnp.sum(x, axis=-1, keepdims=True) for x in e), TINY)
            pc = [x / den for x in e]
            oc_ref[sq, g * NSA_REP:(g + 1) * NSA_REP, :] = sum(
                _dot(pc[h].astype(BF16), cv[:, h * HD:(h + 1) * HD]) for h in range(per))
            imp = jnp.concatenate([jnp.sum(x, axis=0, keepdims=True) for x in pc], axis=0)
            score = jnp.where(forced, FORCE, jnp.where(complete, imp, NEG))
            sel = jnp.zeros(score.shape, F32)
            work = score
            for _ in range(k_past):
                mx = jnp.max(jnp.max(work, axis=-1, keepdims=True), axis=0, keepdims=True)
                cand = jnp.where(work == mx, n_idx, per * n_pages)
                first = jnp.min(jnp.min(cand, axis=-1, keepdims=True), axis=0, keepdims=True)
                pick = n_idx == first
                sel = jnp.where(pick, 1.0, sel)
                work = jnp.where(pick, -jnp.inf, work)
            sel_ref[sq, g] = jnp.where(score > 0.5 * NEG, sel, 0.0)


def _nsa_cmp_sample(page_table, pool_t, layer, q, wd, ped, w2b):
    bsz, n_pages = page_table.shape
    per = PAGE // NSA_BLOCK
    nseq = 2 if bsz % 2 == 0 else 1
    m_rows = nseq * NSA_KVH * n_pages
    grid_spec = pltpu.PrefetchScalarGridSpec(
        num_scalar_prefetch=1,
        grid=(bsz // nseq, 2),
        in_specs=[
            pl.BlockSpec(memory_space=pl.ANY),
            pl.BlockSpec((nseq, NSA_H, HD), lambda i, j, pt: (i, 0, 0)),
            pl.BlockSpec((1, HD // CMP_DB, CMP_DB * PAGE, per * HD), lambda i, j, pt: (j, 0, 0, 0)),
            pl.BlockSpec((1, HD // CMP_DB, 1, CMP_DB * PAGE), lambda i, j, pt: (j, 0, 0, 0)),
            pl.BlockSpec((1, per * HD, per * HD), lambda i, j, pt: (j, 0, 0)),
        ],
        out_specs=[pl.BlockSpec((nseq, NSA_H, HD), lambda i, j, pt: (i, 0, 0)),
                   pl.BlockSpec((nseq, NSA_KVH, per, n_pages), lambda i, j, pt: (i, 0, 0, 0))],
        scratch_shapes=[
            pltpu.VMEM((2 * m_rows * HD, PAGE), F32),
            pltpu.SemaphoreType.DMA((2,)),
            pltpu.VMEM((2, m_rows, per * HD), F32),
            pltpu.VMEM((m_rows, per * HD), F32),
        ],
    )
    return pl.pallas_call(
        functools.partial(_nsa_cmp_sample_body, layer=layer, nseq=nseq),
        grid_spec=grid_spec,
        out_shape=[jax.ShapeDtypeStruct((bsz, NSA_H, HD), F32),
                   jax.ShapeDtypeStruct((bsz, NSA_KVH, per, n_pages), F32)],
        compiler_params=_params("arbitrary", "arbitrary"),
        name="nsa_cmp_sample",
    )(page_table, pool_t, q, wd, ped, w2b)


def _nsa_sel_sample_body(pt_ref, idx_ref, ok_ref, pool_ref, q_ref, kn_ref, vn_ref, o_ref, buf, sem, *, layer, ksel):
    b = pl.program_id(0)
    bsz = pt_ref.shape[0]
    slot = b % 2
    per = PAGE // NSA_BLOCK

    def copy(bb, kv, g, j, sl):
        page = pt_ref[bb, idx_ref[bb, g * ksel + j] // per]
        return pltpu.make_async_copy(pool_ref.at[page, layer, kv, g], buf.at[sl, kv, g, j], sem.at[sl])

    def each(fn):
        def run(bb, sl):
            def block(j, _):
                for kv in range(2):
                    for g in range(NSA_KVH):
                        fn(copy(bb, kv, g, j, sl))
                return 0
            lax.fori_loop(0, ksel, block, 0)
        return run

    start = each(lambda c: c.start())
    wait = each(lambda c: c.wait())

    @pl.when(b == 0)
    def _():
        start(0, 0)

    @pl.when(b + 1 < bsz)
    def _():
        start(b + 1, 1 - slot)

    wait(b, slot)
    lane_half = lax.broadcasted_iota(jnp.int32, (NSA_REP, PAGE), 1) // NSA_BLOCK
    for g in range(NSA_KVH):
        qg = q_ref[0, g * NSA_REP:(g + 1) * NSA_REP, :] * SCALE
        qb = qg.astype(BF16)
        s = []
        for j in range(ksel):
            blk = idx_ref[b, g * ksel + j]
            valid = (lane_half == blk % per) & (ok_ref[b, g * ksel + j] > 0)
            s.append(jnp.where(valid, _dot(qb, buf[slot, 0, g, j].astype(BF16)), NEG))
        s_own = jnp.sum(qg * kn_ref[0, g:g + 1, :], axis=-1, keepdims=True)
        m = functools.reduce(jnp.maximum, [jnp.max(x, axis=-1, keepdims=True) for x in s] + [s_own])
        p = [jnp.exp(x - m) for x in s]
        p_own = jnp.exp(s_own - m)
        den = functools.reduce(jnp.add, [jnp.sum(x, axis=-1, keepdims=True) for x in p]) + p_own
        o = functools.reduce(jnp.add, [_dot_nt(p[j].astype(BF16), buf[slot, 1, g, j].astype(BF16))
                                       for j in range(ksel)])
        o_ref[0, g * NSA_REP:(g + 1) * NSA_REP, :] = (o + p_own * vn_ref[0, g:g + 1, :]) / jnp.maximum(den, TINY)


def _nsa_sel_sample(page_table, pool_t, layer, q, k_new, v_new, sel):
    bsz, n_pages = page_table.shape
    per = PAGE // NSA_BLOCK
    ksel = min(NSA_TOPN, per * n_pages + 1) - 1
    flat = sel.transpose(0, 1, 3, 2).reshape(bsz, NSA_KVH, n_pages * per)
    vals, idx = lax.top_k(flat, ksel)
    idx = idx.reshape(bsz, NSA_KVH * ksel).astype(jnp.int32)
    ok = (vals > 0.5).reshape(bsz, NSA_KVH * ksel).astype(jnp.int32)
    grid_spec = pltpu.PrefetchScalarGridSpec(
        num_scalar_prefetch=3,
        grid=(bsz,),
        in_specs=[
            pl.BlockSpec(memory_space=pl.ANY),
            pl.BlockSpec((1, NSA_H, HD), lambda i, *_: (i, 0, 0)),
            pl.BlockSpec((1, NSA_KVH, HD), lambda i, *_: (i, 0, 0)),
            pl.BlockSpec((1, NSA_KVH, HD), lambda i, *_: (i, 0, 0)),
        ],
        out_specs=pl.BlockSpec((1, NSA_H, HD), lambda i, *_: (i, 0, 0)),
        scratch_shapes=[pltpu.VMEM((2, 2, NSA_KVH, ksel, HD, PAGE), F32), pltpu.SemaphoreType.DMA((2,))],
    )
    return pl.pallas_call(
        functools.partial(_nsa_sel_sample_body, layer=layer, ksel=ksel),
        grid_spec=grid_spec,
        out_shape=jax.ShapeDtypeStruct((bsz, NSA_H, HD), F32),
        compiler_params=_params("arbitrary"),
        name="nsa_sel_sample",
    )(page_table, idx, ok, pool_t, q, k_new, v_new)


def _win_sample_body(qc_ref, kt_ref, vt_ref, kn_ref, vn_ref, o_ref):
    kt = kt_ref[0, 0]
    vt = vt_ref[0, 0]
    gw, lbuf = kt.shape
    j = lax.broadcasted_iota(jnp.int32, (NSA_KVH, lbuf), 1)
    mask = j > lbuf - NSA_WINDOW
    for r in range(NSA_REP):
        qcol = qc_ref[0, :, r:r + 1]
        s = jnp.where(mask, _head_sums(kt * qcol, NSA_KVH) * SCALE, NEG)
        s_own = _head_sums(jnp.broadcast_to(kn_ref[0] * qcol, (gw, lbuf)), NSA_KVH) * SCALE
        m = jnp.maximum(jnp.max(s, axis=-1, keepdims=True), s_own)
        p = jnp.where(mask, jnp.exp(s - m), 0.0)
        p_own = jnp.exp(s_own - m)
        den = jnp.maximum(jnp.sum(p, axis=-1, keepdims=True) + p_own, TINY)
        tot = jnp.sum(vt * _expand_heads(p, gw), axis=-1, keepdims=True)
        o_ref[0, :, r:r + 1] = (tot + _expand_heads(p_own, gw)[:, 0:1] * vn_ref[0]) / _expand_heads(den, gw)[:, 0:1]


def _win_sample(q_cols, wbuf_t, k_new, v_new):
    bsz, gw, reps = q_cols.shape
    lbuf = wbuf_t.shape[-1]
    col = pl.BlockSpec((1, gw, 1), lambda i: (i, 0, 0))
    return pl.pallas_call(
        _win_sample_body,
        grid=(bsz,),
        in_specs=[pl.BlockSpec((1, gw, reps), lambda i: (i, 0, 0)),
                  pl.BlockSpec((1, 1, gw, lbuf), lambda i: (i, 0, 0, 0)),
                  pl.BlockSpec((1, 1, gw, lbuf), lambda i: (i, 1, 0, 0)), col, col],
        out_specs=pl.BlockSpec((1, gw, reps), lambda i: (i, 0, 0)),
        out_shape=jax.ShapeDtypeStruct((bsz, gw, reps), F32),
        compiler_params=_params("parallel"),
        name="nsa_win_sample",
    )(q_cols, wbuf_t, wbuf_t, k_new.reshape(bsz, gw, 1), v_new.reshape(bsz, gw, 1))


def _gate_combine_body(g_ref, c_ref, s_ref, w_ref, o_ref):
    gt = jax.nn.sigmoid(g_ref[...])
    o_ref[...] = gt[0] * c_ref[...] + gt[1] * s_ref[...] + gt[2] * w_ref[...]


def _gate_combine(gate_logits, o_c, o_s, o_w):
    return pl.pallas_call(
        _gate_combine_body,
        out_shape=jax.ShapeDtypeStruct(o_c.shape, F32),
        name="nsa_gate_combine",
    )(gate_logits, o_c, o_s, o_w)


ODD_NQ, ODD_NQR, ODD_KV = 0, NSA_QW, 2 * NSA_QW
ODD_HQ = ODD_KV + 6 * NSA_KVW
ODD_EQ = ODD_HQ + 3 * HGRN_W
ODD_GATE = ODD_EQ + MEM_W
ODD_NG = ODD_GATE + MIX_W
ODD_N = 4096


def _odd_weights(w_in):
    offs = np.cumsum([0, NSA_QW] + [NSA_KVW] * 6 + [3 * NSA_H] + [HGRN_W] * 3 + [MEM_W, MIX_W])
    nq = w_in[:, offs[0]:offs[1]]
    kv = w_in[:, offs[1]:offs[7]]
    ng = w_in[:, offs[7]:offs[8]]
    rest = w_in[:, offs[8]:]
    pad = jnp.zeros((w_in.shape[0], ODD_N - ODD_NG - 3 * NSA_H), w_in.dtype)
    return jnp.concatenate([nq, nq, kv, rest, ng, pad], axis=1)


def _stack_heads(a, tq, width):
    t = a.shape[0]
    a = a.reshape(t // tq, tq, NSA_KVH, NSA_REP, width).transpose(2, 0, 3, 1, 4)
    return a.reshape(NSA_KVH, t // tq, NSA_REP * tq, width)


def _unstack_heads(a, tq):
    g, nt, _, width = a.shape
    a = a.reshape(g, nt, NSA_REP, tq, width).transpose(1, 3, 0, 2, 4)
    return a.reshape(nt * tq, g * NSA_REP * width)


def _group_kv(k, v):
    t = k.shape[0]
    return jnp.concatenate([k.reshape(t, NSA_KVH, HD), v.reshape(t, NSA_KVH, HD)], axis=-1).transpose(1, 0, 2)


def _odd_prompt(x, cos, sin, g, w_aug_bf16, w_o_bf16, mem_k, mem_v, lb, cmp_w, final_g, tm, tq, kc):
    t = x.shape[0]
    assert t % NSA_BLOCK == 0
    kvo = lambda j: ODD_KV + j * NSA_KVW
    nq, nqr, kv6, qfi, eq, gate, ng = _norm_proj(
        x, g, w_aug_bf16, cos, sin, ((ODD_NQR, ODD_KV), (kvo(2), kvo(3)), (kvo(4), kvo(5))), tm,
        splits=(NSA_QW, NSA_QW, 6 * NSA_KVW, 3 * HGRN_W, MEM_W, MIX_W, LANES))
    ck, cv, sk, sv, wk, wv = (kv6[:, j * NSA_KVW:(j + 1) * NSA_KVW] for j in range(6))
    w1b, w2b, peb = cmp_w
    nb = t // NSA_BLOCK
    cmp_tok = _compress(jnp.stack([ck, cv]).reshape(2, nb, NSA_BLOCK * NSA_KVW), w1b, w2b, peb)
    ckv = _group_kv(cmp_tok[0], cmp_tok[1])
    q_st = _stack_heads(nq, tq, HD).astype(BF16)
    qr_st = _stack_heads(nqr, tq, HD).astype(BF16)
    gt_st = _stack_heads(ng[:, :3 * NSA_H], tq, 3)
    onehot = ((jnp.arange(t)[:, None] // NSA_BLOCK) % HD == jnp.arange(HD)[None, :]).astype(F32)
    ska = jnp.concatenate([sk.reshape(t, NSA_KVH, HD), jnp.broadcast_to(onehot[:, None], (t, NSA_KVH, HD))], axis=-1)
    svo = jnp.concatenate([jnp.ones((t, NSA_KVH, HD), F32), sv.reshape(t, NSA_KVH, HD)], axis=-1)
    _, sk_norm2 = _block_stats(kv6, math.gcd(t, MOBA_BLOCK), NSA_KVW, 2)
    o_nsa = _nsa_prompt(q_st, qr_st, gt_st, ckv, ska.transpose(1, 0, 2).astype(BF16),
                        svo.transpose(1, 0, 2).astype(BF16), _group_kv(wk, wv).astype(BF16),
                        _key_norm_bound(sk_norm2, NSA_KVH), tq, kc)
    o_nsa = _unstack_heads(o_nsa, tq)
    o_hg, s_hg = _hgrn_prompt(qfi, lb.reshape(1, HGRN_W))
    o_mem = _mem_prompt(eq, mem_k, mem_v, tm)
    y = _mix_out(x, o_nsa, o_hg, o_mem, gate, w_o_bf16, final_g, tm)
    rows = lambda a, b: jnp.stack([a.reshape(t, NSA_KVH, HD), b.reshape(t, NSA_KVH, HD)], axis=1)
    return y, rows(ck, cv), rows(sk, sv), rows(wk, wv), _diag_blocks(s_hg, HGRN_H)

def _heads_major(a, h):
    t = a.shape[0]
    return a.reshape(t, h, HD).transpose(1, 0, 2)


def _even_prompt(x, cos, sin, g, w_in_bf16, w_o_bf16, mem_k, mem_v, tm):
    t = x.shape[0]
    o_mq = 3 * RET_W
    o_eq = 3 * RET_W + 3 * MOBA_W
    o_gate = o_eq + MEM_W
    ret, moba, eq, gate = _norm_proj(x, g, w_in_bf16, cos, sin, ((0, 2 * RET_W), (o_mq, o_mq + 2 * MOBA_W)), tm,
                                     splits=(3 * RET_W, 3 * MOBA_W, MEM_W, MIX_W))
    mq, mk, mv = moba[:, :MOBA_W], moba[:, MOBA_W:2 * MOBA_W], moba[:, 2 * MOBA_W:]
    o_ret, s_ret = _retention_prompt(ret)
    kmean, knorm2 = _block_stats(moba, MOBA_BLOCK, MOBA_W, 1)
    q_pad = _heads_major(mq, MOBA_H)
    vh = _heads_major(mv, MOBA_H)
    nblk = t // MOBA_BLOCK
    onehot = (jnp.arange(t)[:, None] // MOBA_BLOCK == jnp.arange(HD)[None, :]).astype(F32)
    ka = jnp.concatenate([_heads_major(mk, MOBA_H), jnp.broadcast_to(onehot, (MOBA_H, t, HD))], axis=-1).astype(BF16)
    vo = jnp.concatenate([jnp.ones_like(vh), vh], axis=-1).astype(BF16)
    km = jnp.pad(_heads_major(kmean, MOBA_H), ((0, 0), (HD, LANES - HD - nblk), (0, HD)))
    o_moba = _moba_prompt(q_pad, ka, vo, km, _key_norm_bound(knorm2, MOBA_H))
    o_moba = o_moba.transpose(1, 0, 2).reshape(t, MOBA_W)
    o_mem = _mem_prompt(eq, mem_k, mem_v, tm)
    y = _mix_out(x, o_ret, o_moba, o_mem, gate, w_o_bf16, None, tm)
    rows = moba[:, MOBA_W:].reshape(t, 2, MOBA_H, HD)
    return y, rows, _diag_blocks(s_ret, RET_H)


def _pages_t(pool, width):
    n_pool, n_layer = pool.shape[0], pool.shape[1]
    return pool.transpose(0, 1, 3, 4, 5, 2).reshape(n_pool, n_layer, 2, width, PAGE)


def _mem_t(cache):
    bsz, n = cache.shape[0], cache.shape[1]
    return cache.transpose(0, 2, 3, 4, 1).reshape(bsz, 2, MEM_W, n)


def _even_sample(x, cos, sin, g, w_in_bf16, w_o_bf16, mem_cache, state, page_table, pool, layer):
    bsz = x.shape[0]
    o_mq = 3 * RET_W
    o_eq = 3 * RET_W + 3 * MOBA_W
    o_gate = o_eq + MEM_W
    proj = _norm_proj(x, g, w_in_bf16, cos, sin, ((0, 2 * RET_W), (o_mq, o_mq + 2 * MOBA_W)), bsz)
    rq, rk, rv = proj[:, :RET_W], proj[:, RET_W:2 * RET_W], proj[:, 2 * RET_W:3 * RET_W]
    mq = proj[:, o_mq:o_mq + MOBA_W]
    mk = proj[:, o_mq + MOBA_W:o_mq + 2 * MOBA_W]
    mv = proj[:, o_mq + 2 * MOBA_W:o_eq]
    gamma = np.repeat(1.0 - np.power(2.0, -5.0 - np.arange(RET_H, dtype=np.float64)), HD)
    o_ret, s_ret = _state_step(state, rq, rk, jnp.broadcast_to(jnp.asarray(gamma, F32), (bsz, RET_W)), rv, "ret")
    o_moba = _moba_sample(page_table, _pages_t(pool, MOBA_W), layer, mq, mk, mv)
    o_mem = _mem_sample(proj[:, o_eq:o_gate], _mem_t(mem_cache))
    y = _mix_out(x, o_ret, o_moba, o_mem, proj[:, o_gate:], w_o_bf16, None, bsz)
    rows = proj[:, o_mq + MOBA_W:o_eq].reshape(bsz, 1, 2, MOBA_H, HD)
    return y, rows, s_ret


def _cols(a):
    bsz = a.shape[0]
    return a.reshape(bsz, NSA_KVH, NSA_REP, HD).transpose(0, 1, 3, 2).reshape(bsz, NSA_KVW, NSA_REP)


def _uncols(a):
    bsz = a.shape[0]
    return a.reshape(bsz, NSA_KVH, HD, NSA_REP).transpose(0, 1, 3, 2).reshape(bsz, NSA_QW)


def _odd_sample(x, cos, sin, g, w_aug_bf16, w_o_bf16, mem_cache, state, lb, page_table, cmp_pool, slc_pool, wbuf,
                layer, cmp_w_t, final_g):
    bsz = x.shape[0]
    lbuf = wbuf.shape[1]
    kvo = lambda j: ODD_KV + j * NSA_KVW
    proj = _norm_proj(x, g, w_aug_bf16, cos, sin, ((ODD_NQR, ODD_KV), (kvo(2), kvo(3)), (kvo(4), kvo(5))), bsz)
    ck, cv, sk, sv, wk, wv = (proj[:, kvo(j):kvo(j + 1)] for j in range(6))
    wd, ped, w2t = cmp_w_t
    n_pool, n_layer = cmp_pool.shape[0], cmp_pool.shape[1]
    cmp_t = cmp_pool.transpose(0, 1, 3, 4, 5, 2)
    o_c, sel = _nsa_cmp_sample(page_table, cmp_t, layer, proj[:, ODD_NQ:ODD_NQ + NSA_QW].reshape(bsz, NSA_H, HD),
                               wd, ped, w2t)
    qr = proj[:, ODD_NQR:ODD_NQR + NSA_QW]
    q_cols = _cols(qr)
    o_s = _nsa_sel_sample(page_table, slc_pool.transpose(0, 1, 3, 4, 5, 2), layer, qr.reshape(bsz, NSA_H, HD),
                          sk.reshape(bsz, NSA_KVH, HD), sv.reshape(bsz, NSA_KVH, HD), sel).reshape(bsz, NSA_QW)
    wbuf_t = wbuf.transpose(0, 2, 3, 4, 1).reshape(bsz, 2, NSA_KVW, lbuf)
    o_w = _win_sample(q_cols, wbuf_t, wk, wv)
    ng = proj[:, ODD_NG:ODD_NG + 3 * NSA_H].reshape(bsz, NSA_H, 3)
    gate_logits = jnp.repeat(ng.transpose(2, 0, 1), HD, axis=-1)
    o_nsa = _gate_combine(gate_logits, o_c.reshape(bsz, NSA_QW), o_s, _uncols(o_w))
    hq, hf, hi = (proj[:, ODD_HQ + j * HGRN_W:ODD_HQ + (j + 1) * HGRN_W] for j in range(3))
    o_hg, s_hg = _state_step(state, hq, hf, jnp.broadcast_to(lb.reshape(1, HGRN_W), (bsz, HGRN_W)), hi, "hgrn")
    o_mem = _mem_sample(proj[:, ODD_EQ:ODD_GATE], _mem_t(mem_cache))
    y = _mix_out(x, o_nsa, o_hg, o_mem, proj[:, ODD_GATE:ODD_NG], w_o_bf16, final_g, bsz)
    rows = lambda a, b: jnp.stack([a.reshape(bsz, 1, NSA_KVH, HD), b.reshape(bsz, 1, NSA_KVH, HD)], axis=2)
    win = jnp.concatenate([wbuf.astype(F32), rows(wk, wv)], axis=1)[:, -lbuf:]
    return y, rows(ck, cv), rows(sk, sv), win, s_hg


def kernel(x_prompt, x_sample, mem_prompt, cache_moba_kv, state_ret, cache_nsa_cmp_kv, cache_nsa_slc_kv,
           cache_nsa_win_kv, state_hgrn, cache_mem_kv, page_table, norm_g, mem_norm_g, w_mem_kv, w_in_even,
           w_in_odd, w_out, cmp_w1, cmp_w2, cmp_pe, hgrn_lb_logits, final_g):
    bp, tp, d = x_prompt.shape
    bs, ts, _ = x_sample.shape
    depth = w_out.shape[0]
    assert bp == 1 and ts == 1 and depth == 2
    n_mem = mem_prompt.shape[1]
    past_len = page_table.shape[1] * PAGE
    assert past_len % MOBA_BLOCK == 0 and cache_moba_kv.shape[2] == PAGE
    tm, tq, kc = 256, 256, 256
    xp, xs, mem = x_prompt[0], x_sample[:, 0], mem_prompt[0]
    cos_p, sin_p = _rope_tables(jnp.arange(tp, dtype=jnp.int32))
    cos_s, sin_s = (jnp.broadcast_to(a, (bs, LANES)) for a in _rope_tables(jnp.full((1,), past_len, jnp.int32)))
    lb_prob = jax.nn.softmax(hgrn_lb_logits.astype(F32), axis=0)
    lb_all = jnp.cumsum(lb_prob, axis=0) - lb_prob[0]
    lw_p = min(NSA_WINDOW, tp)

    def mem_kv(layer):
        kv = _norm_proj(mem, mem_norm_g[layer], w_mem_kv[layer].astype(BF16), cos_p[:n_mem], sin_p[:n_mem], (), n_mem)
        return kv.reshape(n_mem, 2, MEM_W)

    w_in0 = w_in_even[0].astype(BF16)
    w_o0 = w_out[0].astype(BF16)
    mkv0 = mem_kv(0)
    hp, moba_p, ret_p = _even_prompt(xp, cos_p, sin_p, norm_g[0], w_in0, w_o0, mkv0[:, 0], mkv0[:, 1], tm)
    hs, moba_s, ret_s = _even_sample(xs, cos_s, sin_s, norm_g[0], w_in0, w_o0, cache_mem_kv[0], state_ret[0],
                                     page_table, cache_moba_kv, 0)

    w_in1 = _odd_weights(w_in_odd[0]).astype(BF16)
    w_o1 = w_out[1].astype(BF16)
    mkv1 = mem_kv(1)
    cmp_w = _compress_weights(cmp_w1[0], cmp_w2[0], cmp_pe[0])
    cmp_w_t = _compress_weights_t(cmp_w1[0], cmp_w2[0], cmp_pe[0])
    yp, cmp_p, slc_p, win_p, hg_p = _odd_prompt(hp, cos_p, sin_p, norm_g[1], w_in1, w_o1, mkv1[:, 0], mkv1[:, 1],
                                                lb_all[1], cmp_w, final_g, tm, tq, kc)
    ys, cmp_s, slc_s, win_s, hg_s = _odd_sample(hs, cos_s, sin_s, norm_g[1], w_in1, w_o1, cache_mem_kv[1],
                                                state_hgrn[0], lb_all[1], page_table, cache_nsa_cmp_kv,
                                                cache_nsa_slc_kv, cache_nsa_win_kv[0], 0, cmp_w_t, final_g)

    return (yp[None], ys[:, None], moba_p[None, None], moba_s[:, None], ret_p[None, None], ret_s[None],
            cmp_p[None, None], cmp_s[:, None], slc_p[None, None], slc_s[:, None], win_p[None, tp - lw_p:][None],
            win_s[None], hg_p[None, None], hg_s[None], jnp.stack([mkv0, mkv1]).reshape(depth, 1, n_mem, 2, MEM_H, HD))
```

```python
import functools
import math

import jax
import jax.numpy as jnp
import numpy as np
from jax import lax
from jax.experimental import pallas as pl
from jax.experimental.pallas import tpu as pltpu

F32 = jnp.float32
BF16 = jnp.bfloat16
HIGHEST = lax.Precision.HIGHEST

HD = 64
RET_H, MOBA_H, MEM_H, NSA_H, NSA_KVH, HGRN_H = 6, 6, 4, 8, 2, 4
NSA_REP = NSA_H // NSA_KVH
RET_W, MOBA_W, MEM_W = RET_H * HD, MOBA_H * HD, MEM_H * HD
NSA_QW, NSA_KVW, HGRN_W = NSA_H * HD, NSA_KVH * HD, HGRN_H * HD
MIX_W = RET_W + MOBA_W + MEM_W
PAGE = 128
RET_CHUNK, HGRN_CHUNK = 128, 64
MOBA_BLOCK, MOBA_TOPK = 256, 3
NSA_BLOCK, NSA_TOPN, NSA_WINDOW = 64, 16, 512
ROPE_THETA = 10000.0
EPS = 1e-6
NEG = -1e30
FORCE = 1e30
TINY = 1e-30
SCALE = HD ** -0.5
LANES = 128
SHIFT_SLACK = 1.02
SHIFT_LIMIT = 20.0
VMEM_LIMIT = 56 * 1024 * 1024


def _params(*sem):
    return pltpu.CompilerParams(dimension_semantics=sem, vmem_limit_bytes=VMEM_LIMIT)


def _dot(a, b, precision=None):
    return jnp.dot(a, b, preferred_element_type=F32, precision=precision)


def _dot_nt(a, b, precision=None):
    return lax.dot_general(a, b, (((1,), (1,)), ((), ())), preferred_element_type=F32, precision=precision)


def _dot_tn(a, b, precision=None):
    return lax.dot_general(a, b, (((0,), (0,)), ((), ())), preferred_element_type=F32, precision=precision)


def _block_ones(width, value=1.0):
    r = lax.broadcasted_iota(jnp.int32, (width, width), 0) // HD
    c = lax.broadcasted_iota(jnp.int32, (width, width), 1) // HD
    return jnp.where(r == c, value, 0.0).astype(F32)


def _rope_tile(a, cos, sin, first_half):
    rot = jnp.where(first_half, pltpu.roll(a, LANES - HD // 2, 1), pltpu.roll(a, HD // 2, 1))
    return a * cos + rot * sin


def _norm_proj_body(x_ref, g_ref, w_ref, cos_ref, sin_ref, *o_refs, rope_tiles, n_chunk, starts):
    x = x_ref[...]
    ms = jnp.mean(x * x, axis=-1, keepdims=True)
    y = (x * lax.rsqrt(ms + EPS) * g_ref[...]).astype(BF16)
    n = w_ref.shape[1]
    lane = lax.broadcasted_iota(jnp.int32, (x.shape[0], LANES), 1)
    first_half = (lane % HD) < HD // 2
    for c0 in range(0, n, n_chunk):
        acc = _dot(y, w_ref[:, c0:c0 + n_chunk])
        for j in range(n_chunk // LANES):
            tile = (c0 // LANES) + j
            which = max(i for i, s in enumerate(starts) if s <= tile * LANES)
            local = tile * LANES - starts[which]
            if local >= o_refs[which].shape[1]:
                continue
            a = acc[:, j * LANES:(j + 1) * LANES]
            if tile in rope_tiles:
                a = _rope_tile(a, cos_ref[...], sin_ref[...], first_half)
            o_refs[which][:, local:local + LANES] = a


def _norm_proj(x, g, w_bf16, cos, sin, rope_cols, tm, splits=None):
    m, d = x.shape
    n = w_bf16.shape[1]
    splits = (n,) if splits is None else tuple(splits)
    assert m % tm == 0 and n % LANES == 0 and all(s % LANES == 0 for s in splits) and sum(splits) <= n
    starts = tuple(int(s) for s in np.cumsum((0,) + splits[:-1]))
    tiles = n // LANES
    k = next(c for c in (4, 3, 2, 1) if tiles % c == 0)
    rope_tiles = frozenset(t for a, b in rope_cols for t in range(a // LANES, b // LANES))
    body = functools.partial(_norm_proj_body, rope_tiles=rope_tiles, n_chunk=k * LANES, starts=starts)
    outs = pl.pallas_call(
        body,
        grid=(m // tm,),
        in_specs=[
            pl.BlockSpec((tm, d), lambda i: (i, 0)),
            pl.BlockSpec((1, d), lambda i: (0, 0)),
            pl.BlockSpec((d, n), lambda i: (0, 0)),
            pl.BlockSpec((tm, LANES), lambda i: (i, 0)),
            pl.BlockSpec((tm, LANES), lambda i: (i, 0)),
        ],
        out_specs=[pl.BlockSpec((tm, s), lambda i: (i, 0)) for s in splits],
        out_shape=[jax.ShapeDtypeStruct((m, s), F32) for s in splits],
        compiler_params=_params("parallel"),
        name="norm_proj",
    )(x, g.reshape(1, d), w_bf16, cos, sin)
    return outs if len(splits) > 1 else outs[0]


def _rope_tables(pos):
    half = HD // 2
    inv = ROPE_THETA ** (-jnp.arange(half, dtype=F32) / half)
    ang = pos.astype(F32)[:, None] * inv[None, :]
    cos, sin = jnp.cos(ang), jnp.sin(ang)
    return jnp.concatenate([cos, cos, cos, cos], -1), jnp.concatenate([-sin, sin, -sin, sin], -1)


def _retention_consts(c):
    lg = np.log(1.0 - np.power(2.0, -5.0 - np.arange(RET_H, dtype=np.float64)))
    ti = np.arange(c, dtype=np.float64)
    causal = ti[:, None] >= ti[None, :]
    d_in = np.where(causal[None], np.exp(np.where(causal, ti[:, None] - ti[None, :], 0.0)[None] * lg[:, None, None]), 0.0)
    q_dec = np.repeat(np.exp((ti[:, None] + 1.0) * lg[None, :]), HD, axis=1)
    k_dec = np.repeat(np.exp((c - 1.0 - ti)[:, None] * lg[None, :]), HD, axis=1)
    c_dec = np.repeat(np.exp(c * lg), HD)
    head = np.arange(RET_W) // HD
    bd = (head[:, None] == head[None, :]).astype(np.float64)
    cmat = bd * c_dec[:, None]
    f = lambda a: jnp.asarray(a, dtype=F32)
    return f(d_in), f(q_dec), f(k_dec), f(cmat), f(bd)


def _retention_body(q_ref, k_ref, v_ref, din_ref, qdec_ref, kdec_ref, cmat_ref, bd_ref, o_ref, s_ref):
    @pl.when(pl.program_id(0) == 0)
    def _():
        s_ref[...] = jnp.zeros_like(s_ref)

    q = q_ref[...]
    k = k_ref[...] * SCALE
    v = v_ref[...]
    c, w = q.shape
    head = lax.broadcasted_iota(jnp.int32, (c, w), 1) // HD
    s = s_ref[...]
    qb, kb, vb = q.astype(BF16), k.astype(BF16), v.astype(BF16)
    o = _dot(qb, s.astype(BF16)) * qdec_ref[...]
    for h in range(RET_H):
        mh = head == h
        att = _dot_nt(jnp.where(mh, qb, jnp.zeros_like(qb)), kb) * din_ref[h]
        o = o + jnp.where(mh, _dot(att.astype(BF16), vb), 0.0)
    s_ref[...] = s * cmat_ref[...] + _dot_tn((k * kdec_ref[...]).astype(BF16), vb) * bd_ref[...]
    seg = bd_ref[...] * (1.0 / HD)
    xc = o - _dot(o, seg, HIGHEST)
    var = _dot(xc * xc, seg, HIGHEST)
    o_ref[...] = xc * lax.rsqrt(var + EPS)


def _retention_prompt(qkv):
    t, w = qkv.shape[0], RET_W
    c = math.gcd(t, RET_CHUNK)
    d_in, q_dec, k_dec, cmat, bd = _retention_consts(c)
    col = lambda j: pl.BlockSpec((c, w), lambda i: (i, j))
    const2 = lambda shape: pl.BlockSpec(shape, lambda i: (0,) * len(shape))
    return pl.pallas_call(
        _retention_body,
        grid=(t // c,),
        in_specs=[col(0), col(1), col(2), const2((RET_H, c, c)), const2((c, w)), const2((c, w)), const2((w, w)),
                  const2((w, w))],
        out_specs=[col(0), const2((w, w))],
        out_shape=[jax.ShapeDtypeStruct((t, w), F32), jax.ShapeDtypeStruct((w, w), F32)],
        compiler_params=_params("arbitrary"),
        name="retention_prompt",
    )(qkv, qkv, qkv, d_in, q_dec, k_dec, cmat, bd)


def _diag_blocks(s, h):
    s4 = s.reshape(h, HD, h, HD)
    return jnp.stack([s4[i, :, i, :] for i in range(h)], axis=0)


def _block_stats_body(k_ref, mean_ref, norm_ref):
    k = k_ref[...]
    mean_ref[0] = jnp.mean(k, axis=0, keepdims=True)
    n2 = _dot((k * k).astype(BF16), _block_ones(k.shape[1]).astype(BF16))
    norm_ref[0] = jnp.max(n2, axis=0, keepdims=True)


def _block_stats(k, blk, w, col):
    t = k.shape[0]
    assert t % blk == 0
    mean, norm = pl.pallas_call(
        _block_stats_body,
        grid=(t // blk,),
        in_specs=[pl.BlockSpec((blk, w), lambda i: (i, col))],
        out_specs=[pl.BlockSpec((1, 1, w), lambda i: (i, 0, 0))] * 2,
        out_shape=[jax.ShapeDtypeStruct((t // blk, 1, w), F32)] * 2,
        compiler_params=_params("parallel"),
        name="block_stats",
    )(k)
    return mean.reshape(t // blk, w), norm.reshape(t // blk, w)


def _key_norm_bound(norm2, heads):
    kmax = jnp.sqrt(jnp.max(norm2, axis=0)).reshape(heads, HD)[:, :1]
    return jnp.broadcast_to(kmax[:, :, None], (heads, 1, LANES))


def _topk_mask(score, k):
    n = score.shape[-1]
    idx = lax.broadcasted_iota(jnp.int32, score.shape, score.ndim - 1)
    sel = jnp.zeros(score.shape, F32)
    work = score
    for _ in range(k):
        m = jnp.max(work, axis=-1, keepdims=True)
        first = jnp.min(jnp.where(work == m, idx, n), axis=-1, keepdims=True)
        pick = idx == first
        sel = jnp.where(pick, 1.0, sel)
        work = jnp.where(pick, -jnp.inf, work)
    return sel


def _tile_loop(lo, hi, fn, unroll=8):
    def group(g, _):
        for u in range(unroll):
            fn(lo + unroll * g + u)
        return 0
    n = jnp.maximum(hi - lo, 0)
    main = n // unroll
    lax.fori_loop(0, main, group, 0)
    done = lo + main * unroll
    rem = n - main * unroll
    size = unroll // 2
    while size >= 1:
        @pl.when((rem & size) != 0)
        def _(done=done, size=size):
            for u in range(size):
                fn(done + u)
        done = done + (rem & size)
        size //= 2


def _lane_fold_max(s):
    out = s[:, 0:LANES]
    for c in range(1, s.shape[1] // LANES):
        out = jnp.maximum(out, s[:, c * LANES:(c + 1) * LANES])
    return out


def _moba_prompt_body(q_ref, ka_ref, vo_ref, kmean_ref, kmax_ref, o_ref, m_ref, acc_ref, *, topk, bq):
    qi = pl.program_id(1)
    q = jnp.concatenate([q_ref[0], jnp.zeros(q_ref.shape[1:], F32)], axis=1)
    rows = q.shape[0]
    kb = MOBA_BLOCK
    gate = _dot_nt(q, kmean_ref[0], HIGHEST)
    blk = lax.broadcasted_iota(jnp.int32, (rows, LANES), 1) - HD
    own = qi * bq + lax.broadcasted_iota(jnp.int32, (rows, LANES), 0) // kb
    past = (blk >= 0) & (blk < own)
    sel = jnp.where(past, _topk_mask(jnp.where(past, gate, NEG), topk), 0.0)
    bias = jnp.where((blk < 0) | (blk == own) | (sel > 0.5), 0.0, NEG)
    qa = (q * SCALE + bias).astype(BF16)
    n_past = qi * bq
    qpos = lax.broadcasted_iota(jnp.int32, (rows, kb), 0)
    kpos = lax.broadcasted_iota(jnp.int32, (rows, kb), 1)
    acc_ref[...] = jnp.zeros(acc_ref.shape, F32)

    def scores(j):
        start = pl.multiple_of(j * kb, kb)
        return _dot_nt(qa, ka_ref[0, pl.ds(start, kb), :])

    def own_scores(d):
        return jnp.where(kpos + d * kb <= qpos, scores(n_past + d), NEG)

    def add_values(j, p):
        start = pl.multiple_of(j * kb, kb)
        acc_ref[...] += _dot(p.astype(BF16), vo_ref[0, pl.ds(start, kb), :])

    bound = jnp.sqrt(jnp.sum(q * q, axis=-1, keepdims=True)) * kmax_ref[0][:, 0:1] * (SCALE * SHIFT_SLACK)
    small = jnp.max(bound) <= SHIFT_LIMIT

    @pl.when(small)
    def _():
        _tile_loop(0, n_past, lambda j: add_values(j, jnp.exp(scores(j) - bound)))
        for d in range(bq):
            add_values(n_past + d, jnp.exp(own_scores(d) - bound))

    @pl.when(jnp.logical_not(small))
    def _():
        m_ref[...] = jnp.full(m_ref.shape, NEG, F32)

        def fold(s):
            m_ref[...] = jnp.maximum(m_ref[...], _lane_fold_max(s))

        _tile_loop(0, n_past, lambda j: fold(scores(j)))
        for d in range(bq):
            fold(own_scores(d))
        m = jnp.max(m_ref[...], axis=-1, keepdims=True)
        _tile_loop(0, n_past, lambda j: add_values(j, jnp.exp(scores(j) - m)))
        for d in range(bq):
            add_values(n_past + d, jnp.exp(own_scores(d) - m))

    acc = acc_ref[...]
    o_ref[0] = (acc / jnp.maximum(acc[:, 0:1], TINY))[:, HD:]


def _moba_prompt(q_pad, ka_bf16, vo_bf16, kmean_rows, kmax):
    h, t, _ = q_pad.shape
    nblk = t // MOBA_BLOCK
    assert t % MOBA_BLOCK == 0 and nblk <= LANES - HD
    bq = next(c for c in (4, 2, 1) if nblk % c == 0)
    rows = bq * MOBA_BLOCK
    return pl.pallas_call(
        functools.partial(_moba_prompt_body, topk=min(MOBA_TOPK, nblk), bq=bq),
        grid=(h, nblk // bq),
        in_specs=[
            pl.BlockSpec((1, rows, HD), lambda a, i: (a, i, 0)),
            pl.BlockSpec((1, t, LANES), lambda a, i: (a, 0, 0), pipeline_mode=pl.Buffered(1)),
            pl.BlockSpec((1, t, LANES), lambda a, i: (a, 0, 0), pipeline_mode=pl.Buffered(1)),
            pl.BlockSpec((1, LANES, LANES), lambda a, i: (a, 0, 0)),
            pl.BlockSpec((1, 1, LANES), lambda a, i: (a, 0, 0)),
        ],
        out_specs=pl.BlockSpec((1, rows, HD), lambda a, i: (a, i, 0)),
        out_shape=jax.ShapeDtypeStruct((h, t, HD), F32),
        scratch_shapes=[pltpu.VMEM((rows, LANES), F32), pltpu.VMEM((rows, LANES), F32)],
        compiler_params=_params("parallel", "arbitrary"),
        name="moba_prompt",
    )(q_pad, ka_bf16, vo_bf16, kmean_rows, kmax)


def _mem_prompt_body(q_ref, k_ref, v_ref, o_ref):
    q = q_ref[...]
    kb = k_ref[...].astype(BF16)
    vb = v_ref[...].astype(BF16)
    head = lax.broadcasted_iota(jnp.int32, q.shape, 1) // HD
    o = jnp.zeros(q.shape, F32)
    for h in range(MEM_H):
        mh = head == h
        s = _dot_nt(jnp.where(mh, q, 0.0).astype(BF16), kb) * SCALE
        e = jnp.exp(s - jnp.max(s, axis=-1, keepdims=True))
        p = e / jnp.sum(e, axis=-1, keepdims=True)
        o = o + jnp.where(mh, _dot(p.astype(BF16), vb), 0.0)
    o_ref[...] = o


def _mem_prompt(q, k, v, tm):
    t, w = q.shape
    n = k.shape[0]
    return pl.pallas_call(
        _mem_prompt_body,
        grid=(t // tm,),
        in_specs=[pl.BlockSpec((tm, w), lambda i: (i, 0)), pl.BlockSpec((n, w), lambda i: (0, 0)),
                  pl.BlockSpec((n, w), lambda i: (0, 0))],
        out_specs=pl.BlockSpec((tm, w), lambda i: (i, 0)),
        out_shape=jax.ShapeDtypeStruct((t, w), F32),
        compiler_params=_params("parallel"),
        name="mem_prompt",
    )(q, k, v)


def _mix_out_body(x_ref, a_ref, b_ref, c_ref, gate_ref, wo_ref, fg_ref, o_ref, *, final_norm):
    gate = gate_ref[...]
    gate = gate * jax.nn.sigmoid(gate)
    y = x_ref[...]
    off = 0
    for ref in (a_ref, b_ref, c_ref):
        w = ref.shape[1]
        mix = (ref[...] * gate[:, off:off + w]).astype(BF16)
        y = y + _dot(mix, wo_ref[off:off + w, :])
        off += w
    if final_norm:
        ms = jnp.mean(y * y, axis=-1, keepdims=True)
        y = y * lax.rsqrt(ms + EPS) * fg_ref[...]
    o_ref[...] = y


def _mix_out(x, a, b, c, gate, wo_bf16, final_g, tm):
    m, d = x.shape
    row = lambda w: pl.BlockSpec((tm, w), lambda i: (i, 0))
    fg = jnp.ones((1, d), F32) if final_g is None else final_g.reshape(1, d).astype(F32)
    body = functools.partial(_mix_out_body, final_norm=final_g is not None)
    return pl.pallas_call(
        body,
        grid=(m // tm,),
        in_specs=[row(d), row(a.shape[1]), row(b.shape[1]), row(c.shape[1]), row(gate.shape[1]),
                  pl.BlockSpec(wo_bf16.shape, lambda i: (0, 0)), pl.BlockSpec((1, d), lambda i: (0, 0))],
        out_specs=row(d),
        out_shape=jax.ShapeDtypeStruct((m, d), F32),
        compiler_params=_params("parallel"),
        name="mix_out",
    )(x, a, b, c, gate, wo_bf16, fg)


def _compress_body(x_ref, pe_ref, w1_ref, w2_ref, o_ref):
    x = (x_ref[0] + pe_ref[0]).astype(BF16)
    hid = jax.nn.gelu(_dot(x, w1_ref[0]))
    o_ref[0] = _dot(hid.astype(BF16), w2_ref[0])


def _compress_weights(w1, w2, pe):
    eye = jnp.eye(NSA_KVH, dtype=F32)
    w1r = w1.astype(F32).reshape(2, NSA_BLOCK, HD, HD)
    w1b = jnp.einsum("klde,gh->klgdhe", w1r, eye).reshape(2, NSA_BLOCK * NSA_KVW, NSA_KVW)
    w2b = jnp.einsum("kde,gh->kgdhe", w2.astype(F32), eye).reshape(2, NSA_KVW, NSA_KVW)
    peb = jnp.broadcast_to(pe.astype(F32)[:, :, None, :], (2, NSA_BLOCK, NSA_KVH, HD)).reshape(2, 1, NSA_BLOCK * NSA_KVW)
    return w1b.astype(BF16), w2b.astype(BF16), peb


def _compress(x, w1b, w2b, peb):
    _, nb, kdim = x.shape
    return pl.pallas_call(
        _compress_body,
        grid=(2,),
        in_specs=[pl.BlockSpec((1, nb, kdim), lambda i: (i, 0, 0)), pl.BlockSpec((1, 1, kdim), lambda i: (i, 0, 0)),
                  pl.BlockSpec((1, kdim, NSA_KVW), lambda i: (i, 0, 0)),
                  pl.BlockSpec((1, NSA_KVW, NSA_KVW), lambda i: (i, 0, 0))],
        out_specs=pl.BlockSpec((1, nb, NSA_KVW), lambda i: (i, 0, 0)),
        out_shape=jax.ShapeDtypeStruct((2, nb, NSA_KVW), F32),
        compiler_params=_params("parallel"),
        name="nsa_compress",
    )(x, peb, w1b, w2b)


def _nsa_prompt_body(q_ref, qr_ref, gt_ref, ckv_ref, ska_ref, svo_ref, wkv_ref, kmax_ref, o_ref, m_ref, acc_ref,
                     *, tq, kc):
    i = pl.program_id(1)
    t0 = i * tq
    rows = q_ref.shape[2]
    nb = ckv_ref.shape[1]
    pad = lambda a: jnp.concatenate([a, jnp.zeros_like(a)], axis=1)
    q = pad(q_ref[0, 0])
    qr = pad(qr_ref[0, 0]).astype(F32)
    qrs = qr * SCALE
    qrb = qrs.astype(BF16)
    tpos = t0 + lax.broadcasted_iota(jnp.int32, (rows, 1), 0) % tq

    ckv = ckv_ref[0].astype(BF16)
    blk = lax.broadcasted_iota(jnp.int32, (nb, rows), 0)
    tpos_l = t0 + lax.broadcasted_iota(jnp.int32, (nb, rows), 1) % tq
    complete = blk * NSA_BLOCK + (NSA_BLOCK - 1) <= tpos_l
    s_c = jnp.where(complete, _dot_nt(ckv, q) * SCALE, NEG)
    e = jnp.where(complete, jnp.exp(s_c - jnp.max(s_c, axis=0, keepdims=True)), 0.0)
    pc = e / jnp.maximum(jnp.sum(e, axis=0, keepdims=True), TINY)
    o_c = _dot_tn(pc.astype(BF16), ckv)

    imp = pc[:, 0:tq]
    for r in range(1, NSA_REP):
        imp = imp + pc[:, r * tq:(r + 1) * tq]
    blk_q = lax.broadcasted_iota(jnp.int32, (nb, tq), 0)
    tpos_q = t0 + lax.broadcasted_iota(jnp.int32, (nb, tq), 1)
    cur = tpos_q // NSA_BLOCK
    forced = (blk_q == 0) | (blk_q == cur) | (blk_q == cur - 1)
    complete_q = blk_q * NSA_BLOCK + (NSA_BLOCK - 1) <= tpos_q
    score = jnp.where(forced, FORCE, jnp.where(complete_q, imp, NEG))
    sel = jnp.where(score > 0.5 * NEG, _topk_mask_axis0(score, min(NSA_TOPN, nb)), 0.0)

    bias = jnp.where(sel > 0.5, 0.0, NEG).T
    nbp = -(-nb // LANES) * LANES
    if nbp > nb:
        bias = jnp.concatenate([bias, jnp.full((tq, nbp - nb), NEG, F32)], axis=1)
    bias = jnp.concatenate([bias] * NSA_REP, axis=0)
    upper =lax.broadcasted_iota(jnp.int32, (rows, LANES), 1) >= HD
    n_span = -(-nb // HD)
    qa = []
    for sp in range(n_span):
        col = bias[:, (sp // 2) * LANES:(sp // 2 + 1) * LANES]
        if sp % 2 == 0:
            col = pltpu.roll(col, HD, 1)
        qa.append((qrs + jnp.where(upper, col, 0.0)).astype(BF16))
    cps = HD * NSA_BLOCK // kc
    n_chunks = (t0 + tq + kc - 1) // kc
    last = n_chunks - 1
    qa_last = qa[0]
    for sp in range(1, n_span):
        qa_last = jnp.where(last // cps == sp, qa[sp], qa_last)
    last_start = pl.multiple_of(last * kc, kc)
    kpos = last_start + lax.broadcasted_iota(jnp.int32, (rows, kc), 1)
    s_last = jnp.where(kpos <= tpos, _dot_nt(qa_last, ska_ref[0, pl.ds(last_start, kc), :]), NEG)

    def spans(fn):
        for sp in range(n_span):
            lo = sp * cps
            _tile_loop(jnp.minimum(lo, last), jnp.minimum(lo + cps, last), functools.partial(fn, qa[sp]))

    def scores(qsp, c):
        start = pl.multiple_of(c * kc, kc)
        return _dot_nt(qsp, ska_ref[0, pl.ds(start, kc), :])

    def add_values(c, p):
        start = pl.multiple_of(c * kc, kc)
        acc_ref[...] += _dot(p.astype(BF16), svo_ref[0, pl.ds(start, kc), :])

    acc_ref[...] = jnp.zeros(acc_ref.shape, F32)
    bound =jnp.sqrt(jnp.sum(qr * qr, axis=-1, keepdims=True)) * kmax_ref[0][:, 0:1] * (SCALE * SHIFT_SLACK)
    small = jnp.max(bound) <= SHIFT_LIMIT

    @pl.when(small)
    def _():
        add_values(last, jnp.exp(s_last - bound))
        spans(lambda qsp, c: add_values(c, jnp.exp(scores(qsp, c) - bound)))

    @pl.when(jnp.logical_not(small))
    def _():
        m_ref[...] = _lane_fold_max(s_last)

        def fold_max(qsp, c):
            m_ref[...] = jnp.maximum(m_ref[...], _lane_fold_max(scores(qsp, c)))

        spans(fold_max)
        m_s = jnp.max(m_ref[...], axis=-1, keepdims=True)
        add_values(last, jnp.exp(s_last - m_s))
        spans(lambda qsp, c: add_values(c, jnp.exp(scores(qsp, c) - m_s)))

    acc_s = acc_ref[...]
    o_s = acc_s / jnp.maximum(acc_s[:, 0:1], TINY)

    nwin = (NSA_WINDOW + tq - 1) // tq + 1
    win = []
    for d in range(nwin):
        c = i - (nwin - 1) + d
        start = pl.multiple_of(jnp.maximum(c, 0) * tq, tq)
        kv = wkv_ref[0, pl.ds(start, tq), :]
        kpos = c * tq + lax.broadcasted_iota(jnp.int32, (rows, tq), 1)
        mask = (kpos <= tpos) & (kpos > tpos - NSA_WINDOW) & (kpos >= 0)
        win.append((jnp.where(mask, _dot_nt(qrb, kv), NEG), kv))
    m_w = jnp.max(functools.reduce(jnp.maximum, [s for s, _ in win]), axis=-1, keepdims=True)
    p_w = [jnp.exp(s - m_w) for s, _ in win]
    l_w = jnp.sum(functools.reduce(jnp.add, p_w), axis=-1, keepdims=True)
    acc_w = functools.reduce(jnp.add, [_dot(p.astype(BF16), kv) for p, (_, kv) in zip(p_w, win)])
    o_w = acc_w / jnp.maximum(l_w, TINY)

    gt = jax.nn.sigmoid(gt_ref[0, 0])
    o_ref[0, 0] = (gt[:, 0:1] * o_c + gt[:, 1:2] * o_s + gt[:, 2:3] * o_w)[:, HD:]


def _nsa_prompt(q_st, qr_st, gt_st, ckv, ska, svo, wkv, kmax, tq, kc):
    g, nt, rows, _ = q_st.shape
    t = ska.shape[1]
    nb = ckv.shape[1]
    assert t % kc == 0 and kc % tq == 0 and (HD * NSA_BLOCK) % kc == 0 and t % tq == 0
    qspec = pl.BlockSpec((1, 1, rows, HD), lambda a, i: (a, i, 0, 0))
    seq = pl.BlockSpec((1, t, LANES), lambda a, i: (a, 0, 0), pipeline_mode=pl.Buffered(1))
    body = functools.partial(_nsa_prompt_body, tq=tq, kc=kc)
    return pl.pallas_call(
        body,
        grid=(g, nt),
        in_specs=[qspec, qspec, pl.BlockSpec((1, 1, rows, 3), lambda a, i: (a, i, 0, 0)),
                  pl.BlockSpec((1, nb, LANES), lambda a, i: (a, 0, 0)), seq, seq, seq,
                  pl.BlockSpec((1, 1, LANES), lambda a, i: (a, 0, 0))],
        out_specs=qspec,
        out_shape=jax.ShapeDtypeStruct((g, nt, rows, HD), F32),
        scratch_shapes=[pltpu.VMEM((rows, LANES), F32), pltpu.VMEM((rows, LANES), F32)],
        compiler_params=_params("parallel", "arbitrary"),
        name="nsa_prompt",
    )(q_st, qr_st, gt_st, ckv, ska, svo, wkv, kmax)


def _hgrn_body(q_ref, f_ref, i_ref, lb_ref, o_ref, s_ref):
    @pl.when(pl.program_id(0) == 0)
    def _():
        s_ref[...] = jnp.zeros_like(s_ref)

    q = q_ref[...]
    v = i_ref[...]
    c, w = q.shape
    lb = lb_ref[...]
    f = lb + (1.0 - lb) * jax.nn.sigmoid(f_ref[...])
    kk = 1.0 - f
    tr = lax.broadcasted_iota(jnp.int32, (c, c), 0)
    tc = lax.broadcasted_iota(jnp.int32, (c, c), 1)
    cum = _dot(jnp.where(tr >= tc, 1.0, 0.0), jnp.log(f), HIGHEST)
    bd = _block_ones(w)
    bd_b = bd.astype(BF16)
    s = s_ref[...]
    o = _dot(q * jnp.exp(cum), s, HIGHEST)
    t_idx = lax.broadcasted_iota(jnp.int32, (c, w), 0)

    sub = 8
    groups = c // sub
    o_blk = [o[i * sub:(i + 1) * sub] for i in range(groups)]
    for g in range(groups):
        r0 = g * sub
        n = c - r0
        t_g = r0 + lax.broadcasted_iota(jnp.int32, (n, w), 0)
        es = [jnp.where(t_g >= j, jnp.exp(cum[r0:] - cum[j:j + 1]) * q[r0:] * kk[j:j + 1], 0.0)
              for j in range(r0, r0 + sub)]
        e = jnp.concatenate(es, axis=0)
        a = _dot(e.astype(BF16), bd_b)
        contrib = a[0:n] * v[r0:r0 + 1]
        for u in range(1, sub):
            contrib = contrib + a[u * n:(u + 1) * n] * v[r0 + u:r0 + u + 1]
        for i in range(g, groups):
            o_blk[i] = o_blk[i] + contrib[(i - g) * sub:(i - g + 1) * sub]
    o = jnp.concatenate(o_blk, axis=0)
    last = cum[c - 1:c, :]
    row0 = t_idx == 0
    scale_mat = _dot_tn(jnp.where(row0, jnp.exp(last), 0.0), jnp.where(row0, 1.0, 0.0), HIGHEST)
    s_ref[...] = s * scale_mat + _dot_tn(kk * jnp.exp(last - cum), v, HIGHEST) * bd
    ms = _dot(o * o, bd * (1.0 / HD), HIGHEST)
    o_ref[...] = o * lax.rsqrt(ms + EPS)


def _hgrn_prompt(qfi, lb):
    t, w = qfi.shape[0], HGRN_W
    c = math.gcd(t, HGRN_CHUNK)
    col = lambda j: pl.BlockSpec((c, w), lambda n: (n, j))
    return pl.pallas_call(
        _hgrn_body,
        grid=(t // c,),
        in_specs=[col(0), col(1), col(2), pl.BlockSpec((1, w), lambda n: (0, 0))],
        out_specs=[col(0), pl.BlockSpec((w, w), lambda n: (0, 0))],
        out_shape=[jax.ShapeDtypeStruct((t, w), F32), jax.ShapeDtypeStruct((w, w), F32)],
        compiler_params=_params("arbitrary"),
        name="hgrn_prompt",
    )(qfi, qfi, qfi, lb)


def _state_step_body(s_ref, q_ref, a_ref, b_ref, v_ref, o_ref, so_ref, *, mode):
    s = s_ref[...]
    if mode == "ret":
        k = a_ref[...] * SCALE
        dec = b_ref[...]
    else:
        lb = b_ref[...]
        dec = lb + (1.0 - lb) * jax.nn.sigmoid(a_ref[...])
        k = 1.0 - dec
    kv = k * v_ref[...]
    rows = s.shape[0]
    o = (q_ref[...] * (kv + dec * s)).reshape(rows // HD, HD, HD).sum(axis=1)
    so_ref[...] = dec * s + kv
    if mode == "ret":
        xc = o - jnp.mean(o, axis=-1, keepdims=True)
        o_ref[...] = xc * lax.rsqrt(jnp.mean(xc * xc, axis=-1, keepdims=True) + EPS)
    else:
        o_ref[...] = o * lax.rsqrt(jnp.mean(o * o, axis=-1, keepdims=True) + EPS)


def _state_step(state, q, a, b, v, mode):
    bsz, h = state.shape[0], state.shape[1]
    rows = bsz * h * HD
    col = lambda x: jnp.broadcast_to(x.reshape(bsz, h, HD, 1), (bsz, h, HD, HD)).reshape(rows, HD)
    vx = jnp.broadcast_to(v.reshape(bsz, h, 1, HD), (bsz, h, HD, HD)).reshape(rows, HD)
    bb = 8 if bsz % 8 == 0 else bsz
    br = bb * h * HD
    spec = pl.BlockSpec((br, HD), lambda i: (i, 0))
    ospec = pl.BlockSpec((br // HD, HD), lambda i: (i, 0))
    o, s_new = pl.pallas_call(
        functools.partial(_state_step_body, mode=mode),
        grid=(rows // br,),
        in_specs=[spec] * 5,
        out_specs=[ospec, spec],
        out_shape=[jax.ShapeDtypeStruct((rows // HD, HD), F32), jax.ShapeDtypeStruct((rows, HD), F32)],
        compiler_params=_params("parallel"),
        name="state_step_" + mode,
    )(state.reshape(rows, HD).astype(F32), col(q), col(a), col(b), vx)
    return o.reshape(bsz, h * HD), s_new.reshape(bsz, h, HD, HD)


def _expand_heads(p, width):
    g, n = p.shape
    return jnp.broadcast_to(p[:, None, :], (g, HD, n)).reshape(width, n)


def _head_sums(x, g):
    return x.reshape(g, HD, x.shape[-1]).sum(axis=1)


def _mem_sample_body(q_ref, kt_ref, vt_ref, o_ref):
    kt = kt_ref[0, 0]
    vt = vt_ref[0, 0]
    s = _head_sums(kt * q_ref[0], MEM_H) * SCALE
    e = jnp.exp(s - jnp.max(s, axis=-1, keepdims=True))
    p = e / jnp.sum(e, axis=-1, keepdims=True)
    o_ref[0] = jnp.sum(vt * _expand_heads(p, MEM_W), axis=-1, keepdims=True)


def _mem_sample(q, kvt):
    bsz, w = q.shape
    n = kvt.shape[-1]
    out = pl.pallas_call(
        _mem_sample_body,
        grid=(bsz,),
        in_specs=[pl.BlockSpec((1, w, 1), lambda i: (i, 0, 0)),
                  pl.BlockSpec((1, 1, w, n), lambda i: (i, 0, 0, 0)),
                  pl.BlockSpec((1, 1, w, n), lambda i: (i, 1, 0, 0))],
        out_specs=pl.BlockSpec((1, w, 1), lambda i: (i, 0, 0)),
        out_shape=jax.ShapeDtypeStruct((bsz, w, 1), F32),
        compiler_params=_params("parallel"),
        name="mem_sample",
    )(q.reshape(bsz, w, 1), kvt, kvt)
    return out.reshape(bsz, w)


def _topk_mask_axis0(score, k):
    n = score.shape[0]
    idx = lax.broadcasted_iota(jnp.int32, score.shape, 0)
    sel = jnp.zeros(score.shape, F32)
    work = score
    for _ in range(k):
        m = jnp.max(work, axis=0, keepdims=True)
        first = jnp.min(jnp.where(work == m, idx, n), axis=0, keepdims=True)
        pick = idx == first
        sel = jnp.where(pick, 1.0, sel)
        work = jnp.where(pick, -jnp.inf, work)
    return sel


def _moba_sample_body(pt_ref, pool_ref, qc_ref, qbd_ref, kn_ref, vn_ref, o_ref, buf, sem, s_ref, vbuf, vsem,
                      *, layer, chunk, topk):
    b = pl.program_id(0)
    n_pages = pt_ref.shape[1]
    n = n_pages // chunk
    groups, gw = MOBA_H, MOBA_W
    per = MOBA_BLOCK // PAGE
    nblk = n_pages // per

    def copy(t, i):
        page = pt_ref[t // n, (t % n) * chunk + i]
        return pltpu.make_async_copy(pool_ref.at[page, layer, 0], buf.at[t % 2, i], sem.at[t % 2])

    def start(t):
        lax.fori_loop(0, chunk, lambda i, _: (copy(t, i).start(), 0)[1], 0)

    def wait(t):
        lax.fori_loop(0, chunk, lambda i, _: (copy(t, i).wait(), 0)[1], 0)

    @pl.when(b == 0)
    def _():
        start(0)

    def k_step(c, _):
        t = b * n + c
        slot = t % 2

        @pl.when(t + 1 < pl.num_programs(0) * n)
        def _():
            start(t + 1)

        wait(t)
        for i in range(chunk):
            sc = _dot(qbd_ref[0], buf[slot, i].astype(BF16))
            s_ref[c * chunk + i] = sc[0:groups]
        return 0

    lax.fori_loop(0, n, k_step, 0)

    sc = s_ref[...].reshape(nblk, per, groups, PAGE)
    gate = jnp.sum(jnp.sum(sc, axis=1, keepdims=True), axis=-1, keepdims=True) * (1.0 / MOBA_BLOCK)
    blk = lax.broadcasted_iota(jnp.int32, gate.shape, 0)
    sel = jnp.zeros(gate.shape, F32)
    picks = []
    for _ in range(topk):
        first = jnp.min(jnp.where(gate == jnp.max(gate, axis=0, keepdims=True), blk, nblk), axis=0, keepdims=True)
        pick = blk == first
        sel = jnp.where(pick, 1.0, sel)
        gate = jnp.where(pick, -jnp.inf, gate)
        picks.append(first)

    v_copies = []
    for h in range(groups):
        for j in range(topk):
            first_page = picks[j][0, 0, h, 0] * per
            for half in range(per):
                v_copies.append((h, j * per + half, first_page + half))
    descs = [pltpu.make_async_copy(pool_ref.at[pt_ref[b, page], layer, 1, pl.ds(h * HD, HD)], vbuf.at[h, slot_j], vsem)
             for h, slot_j, page in v_copies]
    for d in descs:
        d.start()

    mask = jnp.broadcast_to(sel, (nblk, per, groups, PAGE)).reshape(n_pages, groups, PAGE) > 0.5
    s_own = _head_sums(jnp.broadcast_to(kn_ref[0] * qc_ref[0], (gw, PAGE)), groups) * SCALE
    s = jnp.where(mask, s_ref[...] * SCALE, NEG)
    m = jnp.maximum(jnp.max(jnp.max(s, axis=0), axis=-1, keepdims=True), s_own)
    p = jnp.where(mask, jnp.exp(s - m), 0.0)
    p_own = jnp.exp(s_own - m)
    den = jnp.maximum(jnp.sum(jnp.sum(p, axis=0), axis=-1, keepdims=True) + p_own, TINY)
    s_ref[...] = p

    for d in descs:
        d.wait()
    tot = []
    for h in range(groups):
        acc = jnp.zeros((HD, PAGE), F32)
        for hh, slot_j, page in v_copies:
            if hh == h:
                acc = acc + vbuf[h, slot_j] * s_ref[page][h:h + 1]
        tot.append(jnp.sum(acc, axis=-1, keepdims=True))
    tot = jnp.concatenate(tot, axis=0)
    o_ref[0] = (tot + _expand_heads(p_own, gw)[:, 0:1] * vn_ref[0]) / _expand_heads(den, gw)[:, 0:1]


def _moba_sample(page_table, pool_t, layer, q, k_new, v_new):
    bsz, n_pages = page_table.shape
    gw = MOBA_W
    chunk = math.gcd(n_pages, 16)
    per = MOBA_BLOCK // PAGE
    topk = min(MOBA_TOPK, n_pages // per)
    nq = -(-MOBA_H // 8) * 8
    qbd = jnp.einsum("bgd,gh->bghd", q.reshape(bsz, MOBA_H, HD), jnp.eye(MOBA_H, dtype=F32)).reshape(bsz, MOBA_H, gw)
    qbd = jnp.pad(qbd, ((0, 0), (0, nq - MOBA_H), (0, 0))).astype(BF16)
    col = pl.BlockSpec((1, gw, 1), lambda i, pt: (i, 0, 0))
    grid_spec = pltpu.PrefetchScalarGridSpec(
        num_scalar_prefetch=1,
        grid=(bsz,),
        in_specs=[pl.BlockSpec(memory_space=pl.ANY), col, pl.BlockSpec((1, nq, gw), lambda i, pt: (i, 0, 0)), col, col],
        out_specs=col,
        scratch_shapes=[
            pltpu.VMEM((2, chunk, gw, PAGE), F32),
            pltpu.SemaphoreType.DMA((2,)),
            pltpu.VMEM((n_pages, MOBA_H, PAGE), F32),
            pltpu.VMEM((MOBA_H, topk * per, HD, PAGE), F32),
            pltpu.SemaphoreType.DMA(()),
        ],
    )
    out = pl.pallas_call(
        functools.partial(_moba_sample_body, layer=layer, chunk=chunk, topk=topk),
        grid_spec=grid_spec,
        out_shape=jax.ShapeDtypeStruct((bsz, gw, 1), F32),
        compiler_params=_params("arbitrary"),
        name="moba_sample",
    )(page_table, pool_t, q.reshape(bsz, gw, 1), qbd, k_new.reshape(bsz, gw, 1), v_new.reshape(bsz, gw, 1))
    return out.reshape(bsz, gw)


CMP_DB = 8


def _compress_weights_t(w1, w2, pe):
    per = PAGE // NSA_BLOCK
    eye = jnp.eye(per, dtype=F32)
    w1r = w1.astype(F32).reshape(2, NSA_BLOCK, HD, HD)
    wd = jnp.einsum("klde,gh->kdglhe", w1r, eye).reshape(2, HD // CMP_DB, CMP_DB * PAGE, per * HD)
    ped = jnp.tile(pe.astype(F32).transpose(0, 2, 1), (1, 1, per)).reshape(2, HD // CMP_DB, 1, CMP_DB * PAGE)
    w2t = jnp.einsum("kde,gh->kgdhe", w2.astype(F32), eye).reshape(2, per * HD, per * HD)
    return wd.astype(BF16), ped, w2t.astype(BF16)


def _nsa_cmp_sample_body(pt_ref, pool_ref, q_ref, wd_ref, ped_ref, w2_ref, oc_ref, sel_ref, buf, sem, tok_ref, acc_ref,
                         *, layer, nseq):
    b = pl.program_id(0)
    j = pl.program_id(1)
    bsz, n_pages = pt_ref.shape
    t = b * 2 + j
    slot = t % 2
    m_rows = nseq * NSA_KVH * n_pages

    def copy(grp, kv, sq, p, g, sl):
        row = pl.multiple_of((((sl * nseq + sq) * NSA_KVH + g) * n_pages + p) * HD, HD)
        return pltpu.make_async_copy(pool_ref.at[pt_ref[grp * nseq + sq, p], layer, kv, g], buf.at[pl.ds(row, HD)],
                                     sem.at[sl])

    def each(fn):
        def run(grp, kv, sl):
            def page(p, _):
                for sq in range(nseq):
                    for g in range(NSA_KVH):
                        fn(copy(grp, kv, sq, p, g, sl))
                return 0
            lax.fori_loop(0, n_pages, page, 0)
        return run

    start = each(lambda c: c.start())
    wait = each(lambda c: c.wait())

    @pl.when(t == 0)
    def _():
        start(0, 0, 0)

    @pl.when(t + 1 < 2 * (bsz // nseq))
    def _():
        start((t + 1) // 2, (t + 1) % 2, 1 - slot)

    wait(b, j, slot)

    base = slot * m_rows * HD
    for dd in range(HD // CMP_DB):
        parts = [buf[pl.ds(base + dd * CMP_DB + u, m_rows, stride=HD), :] for u in range(CMP_DB)]
        a = jnp.concatenate(parts, axis=1) + ped_ref[0, dd]
        part = _dot(a.astype(BF16), wd_ref[0, dd])
        if dd == 0:
            acc_ref[...] = part
        else:
            acc_ref[...] += part
    tok_ref[j] = _dot(jax.nn.gelu(acc_ref[...]).astype(BF16), w2_ref[0])

    @pl.when(j == 1)
    def _():
        pos = n_pages * PAGE
        per = PAGE // NSA_BLOCK
        n_idx = (lax.broadcasted_iota(jnp.int32, (per, n_pages), 1) * per
                 + lax.broadcasted_iota(jnp.int32, (per, n_pages), 0))
        complete = n_idx * NSA_BLOCK + (NSA_BLOCK - 1) <= pos
        cur = pos // NSA_BLOCK
        forced = (n_idx == 0) | (n_idx == cur) | (n_idx == cur - 1)
        k_past = min(NSA_TOPN, per * n_pages + 1) - 1
        for sq, g in ((a, c) for a in range(nseq) for c in range(NSA_KVH)):
            qg = q_ref[sq, g * NSA_REP:(g + 1) * NSA_REP, :].astype(BF16)
            r0 = (sq * NSA_KVH + g) * n_pages
            ck = tok_ref[0, r0:r0 + n_pages, :].astype(BF16)
            cv = tok_ref[1, r0:r0 + n_pages, :].astype(BF16)
            s = [jnp.where(complete[h:h + 1], _dot_nt(qg, ck[:, h * HD:(h + 1) * HD]) * SCALE, NEG) for h in range(per)]
            m = functools.reduce(jnp.maximum, [jnp.max(x, axis=-1, keepdims=True) for x in s])
            e = [jnp.where(complete[h:h + 1], jnp.exp(s[h] - m), 0.0) for h in range(per)]
            den = jnp.maximum(sum(jnp.sum(x, axis=-1, keepdims=True) for x in e), TINY)
            pc = [x / den for x in e]
            oc_ref[sq, g * NSA_REP:(g + 1) * NSA_REP, :] = sum(
                _dot(pc[h].astype(BF16), cv[:, h * HD:(h + 1) * HD]) for h in range(per))
            imp = jnp.concatenate([jnp.sum(x, axis=0, keepdims=True) for x in pc], axis=0)
            score = jnp.where(forced, FORCE, jnp.where(complete, imp, NEG))
            sel = jnp.zeros(score.shape, F32)
            work = score
            for _ in range(k_past):
                mx = jnp.max(jnp.max(work, axis=-1, keepdims=True), axis=0, keepdims=True)
                cand = jnp.where(work == mx, n_idx, per * n_pages)
                first = jnp.min(jnp.min(cand, axis=-1, keepdims=True), axis=0, keepdims=True)
                pick = n_idx == first
                sel = jnp.where(pick, 1.0, sel)
                work = jnp.where(pick, -jnp.inf, work)
            sel_ref[sq, g] = jnp.where(score > 0.5 * NEG, sel, 0.0)


def _nsa_cmp_sample(page_table, pool_t, layer, q, wd, ped, w2b):
    bsz, n_pages = page_table.shape
    per = PAGE // NSA_BLOCK
    nseq = 2 if bsz % 2 == 0 else 1
    m_rows = nseq * NSA_KVH * n_pages
    grid_spec = pltpu.PrefetchScalarGridSpec(
        num_scalar_prefetch=1,
        grid=(bsz // nseq, 2),
        in_specs=[
            pl.BlockSpec(memory_space=pl.ANY),
            pl.BlockSpec((nseq, NSA_H, HD), lambda i, j, pt: (i, 0, 0)),
            pl.BlockSpec((1, HD // CMP_DB, CMP_DB * PAGE, per * HD), lambda i, j, pt: (j, 0, 0, 0)),
            pl.BlockSpec((1, HD // CMP_DB, 1, CMP_DB * PAGE), lambda i, j, pt: (j, 0, 0, 0)),
            pl.BlockSpec((1, per * HD, per * HD), lambda i, j, pt: (j, 0, 0)),
        ],
        out_specs=[pl.BlockSpec((nseq, NSA_H, HD), lambda i, j, pt: (i, 0, 0)),
                   pl.BlockSpec((nseq, NSA_KVH, per, n_pages), lambda i, j, pt: (i, 0, 0, 0))],
        scratch_shapes=[
            pltpu.VMEM((2 * m_rows * HD, PAGE), F32),
            pltpu.SemaphoreType.DMA((2,)),
            pltpu.VMEM((2, m_rows, per * HD), F32),
            pltpu.VMEM((m_rows, per * HD), F32),
        ],
    )
    return pl.pallas_call(
        functools.partial(_nsa_cmp_sample_body, layer=layer, nseq=nseq),
        grid_spec=grid_spec,
        out_shape=[jax.ShapeDtypeStruct((bsz, NSA_H, HD), F32),
                   jax.ShapeDtypeStruct((bsz, NSA_KVH, per, n_pages), F32)],
        compiler_params=_params("arbitrary", "arbitrary"),
        name="nsa_cmp_sample",
    )(page_table, pool_t, q, wd, ped, w2b)


def _nsa_sel_sample_body(pt_ref, idx_ref, ok_ref, pool_ref, q_ref, kn_ref, vn_ref, o_ref, buf, sem, *, layer, ksel):
    b = pl.program_id(0)
    bsz = pt_ref.shape[0]
    slot = b % 2
    per = PAGE // NSA_BLOCK

    def copy(bb, kv, g, j, sl):
        page = pt_ref[bb, idx_ref[bb, g * ksel + j] // per]
        return pltpu.make_async_copy(pool_ref.at[page, layer, kv, g], buf.at[sl, kv, g, j], sem.at[sl])

    def each(fn):
        def run(bb, sl):
            def block(j, _):
                for kv in range(2):
                    for g in range(NSA_KVH):
                        fn(copy(bb, kv, g, j, sl))
                return 0
            lax.fori_loop(0, ksel, block, 0)
        return run

    start = each(lambda c: c.start())
    wait = each(lambda c: c.wait())

    @pl.when(b == 0)
    def _():
        start(0, 0)

    @pl.when(b + 1 < bsz)
    def _():
        start(b + 1, 1 - slot)

    wait(b, slot)
    lane_half = lax.broadcasted_iota(jnp.int32, (NSA_REP, PAGE), 1) // NSA_BLOCK
    for g in range(NSA_KVH):
        qg = q_ref[0, g * NSA_REP:(g + 1) * NSA_REP, :] * SCALE
        qb = qg.astype(BF16)
        s = []
        for j in range(ksel):
            blk = idx_ref[b, g * ksel + j]
            valid = (lane_half == blk % per) & (ok_ref[b, g * ksel + j] > 0)
            s.append(jnp.where(valid, _dot(qb, buf[slot, 0, g, j].astype(BF16)), NEG))
        s_own = jnp.sum(qg * kn_ref[0, g:g + 1, :], axis=-1, keepdims=True)
        m = functools.reduce(jnp.maximum, [jnp.max(x, axis=-1, keepdims=True) for x in s] + [s_own])
        p = [jnp.exp(x - m) for x in s]
        p_own = jnp.exp(s_own - m)
        den = functools.reduce(jnp.add, [jnp.sum(x, axis=-1, keepdims=True) for x in p]) + p_own
        o = functools.reduce(jnp.add, [_dot_nt(p[j].astype(BF16), buf[slot, 1, g, j].astype(BF16))
                                       for j in range(ksel)])
        o_ref[0, g * NSA_REP:(g + 1) * NSA_REP, :] = (o + p_own * vn_ref[0, g:g + 1, :]) / jnp.maximum(den, TINY)


def _nsa_sel_sample(page_table, pool_t, layer, q, k_new, v_new, sel):
    bsz, n_pages = page_table.shape
    per = PAGE // NSA_BLOCK
    ksel = min(NSA_TOPN, per * n_pages + 1) - 1
    flat = sel.transpose(0, 1, 3, 2).reshape(bsz, NSA_KVH, n_pages * per)
    vals, idx = lax.top_k(flat, ksel)
    idx = idx.reshape(bsz, NSA_KVH * ksel).astype(jnp.int32)
    ok = (vals > 0.5).reshape(bsz, NSA_KVH * ksel).astype(jnp.int32)
    grid_spec = pltpu.PrefetchScalarGridSpec(
        num_scalar_prefetch=3,
        grid=(bsz,),
        in_specs=[
            pl.BlockSpec(memory_space=pl.ANY),
            pl.BlockSpec((1, NSA_H, HD), lambda i, *_: (i, 0, 0)),
            pl.BlockSpec((1, NSA_KVH, HD), lambda i, *_: (i, 0, 0)),
            pl.BlockSpec((1, NSA_KVH, HD), lambda i, *_: (i, 0, 0)),
        ],
        out_specs=pl.BlockSpec((1, NSA_H, HD), lambda i, *_: (i, 0, 0)),
        scratch_shapes=[pltpu.VMEM((2, 2, NSA_KVH, ksel, HD, PAGE), F32), pltpu.SemaphoreType.DMA((2,))],
    )
    return pl.pallas_call(
        functools.partial(_nsa_sel_sample_body, layer=layer, ksel=ksel),
        grid_spec=grid_spec,
        out_shape=jax.ShapeDtypeStruct((bsz, NSA_H, HD), F32),
        compiler_params=_params("arbitrary"),
        name="nsa_sel_sample",
    )(page_table, idx, ok, pool_t, q, k_new, v_new)


def _win_sample_body(qc_ref, kt_ref, vt_ref, kn_ref, vn_ref, o_ref):
    kt = kt_ref[0, 0]
    vt = vt_ref[0, 0]
    gw, lbuf = kt.shape
    j = lax.broadcasted_iota(jnp.int32, (NSA_KVH, lbuf), 1)
    mask = j > lbuf - NSA_WINDOW
    for r in range(NSA_REP):
        qcol = qc_ref[0, :, r:r + 1]
        s = jnp.where(mask, _head_sums(kt * qcol, NSA_KVH) * SCALE, NEG)
        s_own = _head_sums(jnp.broadcast_to(kn_ref[0] * qcol, (gw, lbuf)), NSA_KVH) * SCALE
        m = jnp.maximum(jnp.max(s, axis=-1, keepdims=True), s_own)
        p = jnp.where(mask, jnp.exp(s - m), 0.0)
        p_own = jnp.exp(s_own - m)
        den = jnp.maximum(jnp.sum(p, axis=-1, keepdims=True) + p_own, TINY)
        tot = jnp.sum(vt * _expand_heads(p, gw), axis=-1, keepdims=True)
        o_ref[0, :, r:r + 1] = (tot + _expand_heads(p_own, gw)[:, 0:1] * vn_ref[0]) / _expand_heads(den, gw)[:, 0:1]


def _win_sample(q_cols, wbuf_t, k_new, v_new):
    bsz, gw, reps = q_cols.shape
    lbuf = wbuf_t.shape[-1]
    col = pl.BlockSpec((1, gw, 1), lambda i: (i, 0, 0))
    return pl.pallas_call(
        _win_sample_body,
        grid=(bsz,),
        in_specs=[pl.BlockSpec((1, gw, reps), lambda i: (i, 0, 0)),
                  pl.BlockSpec((1, 1, gw, lbuf), lambda i: (i, 0, 0, 0)),
                  pl.BlockSpec((1, 1, gw, lbuf), lambda i: (i, 1, 0, 0)), col, col],
        out_specs=pl.BlockSpec((1, gw, reps), lambda i: (i, 0, 0)),
        out_shape=jax.ShapeDtypeStruct((bsz, gw, reps), F32),
        compiler_params=_params("parallel"),
        name="nsa_win_sample",
    )(q_cols, wbuf_t, wbuf_t, k_new.reshape(bsz, gw, 1), v_new.reshape(bsz, gw, 1))


def _gate_combine_body(g_ref, c_ref, s_ref, w_ref, o_ref):
    gt = jax.nn.sigmoid(g_ref[...])
    o_ref[...] = gt[0] * c_ref[...] + gt[1] * s_ref[...] + gt[2] * w_ref[...]


def _gate_combine(gate_logits, o_c, o_s, o_w):
    return pl.pallas_call(
        _gate_combine_body,
        out_shape=jax.ShapeDtypeStruct(o_c.shape, F32),
        name="nsa_gate_combine",
    )(gate_logits, o_c, o_s, o_w)


ODD_NQ, ODD_NQR, ODD_KV = 0, NSA_QW, 2 * NSA_QW
ODD_HQ = ODD_KV + 6 * NSA_KVW
ODD_EQ = ODD_HQ + 3 * HGRN_W
ODD_GATE = ODD_EQ + MEM_W
ODD_NG = ODD_GATE + MIX_W
ODD_N = 4096


def _odd_weights(w_in):
    offs = np.cumsum([0, NSA_QW] + [NSA_KVW] * 6 + [3 * NSA_H] + [HGRN_W] * 3 + [MEM_W, MIX_W])
    nq = w_in[:, offs[0]:offs[1]]
    kv = w_in[:, offs[1]:offs[7]]
    ng = w_in[:, offs[7]:offs[8]]
    rest = w_in[:, offs[8]:]
    pad = jnp.zeros((w_in.shape[0], ODD_N - ODD_NG - 3 * NSA_H), w_in.dtype)
    return jnp.concatenate([nq, nq, kv, rest, ng, pad], axis=1)


def _stack_heads(a, tq, width):
    t = a.shape[0]
    a = a.reshape(t // tq, tq, NSA_KVH, NSA_REP, width).transpose(2, 0, 3, 1, 4)
    return a.reshape(NSA_KVH, t // tq, NSA_REP * tq, width)


def _unstack_heads(a, tq):
    g, nt, _, width = a.shape
    a = a.reshape(g, nt, NSA_REP, tq, width).transpose(1, 3, 0, 2, 4)
    return a.reshape(nt * tq, g * NSA_REP * width)


def _group_kv(k, v):
    t = k.shape[0]
    return jnp.concatenate([k.reshape(t, NSA_KVH, HD), v.reshape(t, NSA_KVH, HD)], axis=-1).transpose(1, 0, 2)


def _odd_prompt(x, cos, sin, g, w_aug_bf16, w_o_bf16, mem_k, mem_v, lb, cmp_w, final_g, tm, tq, kc):
    t = x.shape[0]
    assert t % NSA_BLOCK == 0
    kvo = lambda j: ODD_KV + j * NSA_KVW
    nq, nqr, kv6, qfi, eq, gate, ng = _norm_proj(
        x, g, w_aug_bf16, cos, sin, ((ODD_NQR, ODD_KV), (kvo(2), kvo(3)), (kvo(4), kvo(5))), tm,
        splits=(NSA_QW, NSA_QW, 6 * NSA_KVW, 3 * HGRN_W, MEM_W, MIX_W, LANES))
    ck, cv, sk, sv, wk, wv = (kv6[:, j * NSA_KVW:(j + 1) * NSA_KVW] for j in range(6))
    w1b, w2b, peb = cmp_w
    nb = t // NSA_BLOCK
    cmp_tok = _compress(jnp.stack([ck, cv]).reshape(2, nb, NSA_BLOCK * NSA_KVW), w1b, w2b, peb)
    ckv = _group_kv(cmp_tok[0], cmp_tok[1])
    q_st = _stack_heads(nq, tq, HD).astype(BF16)
    qr_st = _stack_heads(nqr, tq, HD).astype(BF16)
    gt_st = _stack_heads(ng[:, :3 * NSA_H], tq, 3)
    onehot = ((jnp.arange(t)[:, None] // NSA_BLOCK) % HD == jnp.arange(HD)[None, :]).astype(F32)
    ska = jnp.concatenate([sk.reshape(t, NSA_KVH, HD), jnp.broadcast_to(onehot[:, None], (t, NSA_KVH, HD))], axis=-1)
    svo = jnp.concatenate([jnp.ones((t, NSA_KVH, HD), F32), sv.reshape(t, NSA_KVH, HD)], axis=-1)
    _, sk_norm2 = _block_stats(kv6, math.gcd(t, MOBA_BLOCK), NSA_KVW, 2)
    o_nsa = _nsa_prompt(q_st, qr_st, gt_st, ckv, ska.transpose(1, 0, 2).astype(BF16),
                        svo.transpose(1, 0, 2).astype(BF16), _group_kv(wk, wv).astype(BF16),
                        _key_norm_bound(sk_norm2, NSA_KVH), tq, kc)
    o_nsa = _unstack_heads(o_nsa, tq)
    o_hg, s_hg = _hgrn_prompt(qfi, lb.reshape(1, HGRN_W))
    o_mem = _mem_prompt(eq, mem_k, mem_v, tm)
    y = _mix_out(x, o_nsa, o_hg, o_mem, gate, w_o_bf16, final_g, tm)
    rows = lambda a, b: jnp.stack([a.reshape(t, NSA_KVH, HD), b.reshape(t, NSA_KVH, HD)], axis=1)
    return y, rows(ck, cv), rows(sk, sv), rows(wk, wv), _diag_blocks(s_hg, HGRN_H)

def _heads_major(a, h):
    t = a.shape[0]
    return a.reshape(t, h, HD).transpose(1, 0, 2)


def _even_prompt(x, cos, sin, g, w_in_bf16, w_o_bf16, mem_k, mem_v, tm):
    t = x.shape[0]
    o_mq = 3 * RET_W
    o_eq = 3 * RET_W + 3 * MOBA_W
    o_gate = o_eq + MEM_W
    ret, moba, eq, gate = _norm_proj(x, g, w_in_bf16, cos, sin, ((0, 2 * RET_W), (o_mq, o_mq + 2 * MOBA_W)), tm,
                                     splits=(3 * RET_W, 3 * MOBA_W, MEM_W, MIX_W))
    mq, mk, mv = moba[:, :MOBA_W], moba[:, MOBA_W:2 * MOBA_W], moba[:, 2 * MOBA_W:]
    o_ret, s_ret = _retention_prompt(ret)
    kmean, knorm2 = _block_stats(moba, MOBA_BLOCK, MOBA_W, 1)
    q_pad = _heads_major(mq, MOBA_H)
    vh = _heads_major(mv, MOBA_H)
    nblk = t // MOBA_BLOCK
    onehot = (jnp.arange(t)[:, None] // MOBA_BLOCK == jnp.arange(HD)[None, :]).astype(F32)
    ka = jnp.concatenate([_heads_major(mk, MOBA_H), jnp.broadcast_to(onehot, (MOBA_H, t, HD))], axis=-1).astype(BF16)
    vo = jnp.concatenate([jnp.ones_like(vh), vh], axis=-1).astype(BF16)
    km = jnp.pad(_heads_major(kmean, MOBA_H), ((0, 0), (HD, LANES - HD - nblk), (0, HD)))
    o_moba = _moba_prompt(q_pad, ka, vo, km, _key_norm_bound(knorm2, MOBA_H))
    o_moba = o_moba.transpose(1, 0, 2).reshape(t, MOBA_W)
    o_mem = _mem_prompt(eq, mem_k, mem_v, tm)
    y = _mix_out(x, o_ret, o_moba, o_mem, gate, w_o_bf16, None, tm)
    rows = moba[:, MOBA_W:].reshape(t, 2, MOBA_H, HD)
    return y, rows, _diag_blocks(s_ret, RET_H)


def _pages_t(pool, width):
    n_pool, n_layer = pool.shape[0], pool.shape[1]
    return pool.transpose(0, 1, 3, 4, 5, 2).reshape(n_pool, n_layer, 2, width, PAGE)


def _mem_t(cache):
    bsz, n = cache.shape[0], cache.shape[1]
    return cache.transpose(0, 2, 3, 4, 1).reshape(bsz, 2, MEM_W, n)


def _even_sample(x, cos, sin, g, w_in_bf16, w_o_bf16, mem_cache, state, page_table, pool, layer):
    bsz = x.shape[0]
    o_mq = 3 * RET_W
    o_eq = 3 * RET_W + 3 * MOBA_W
    o_gate = o_eq + MEM_W
    proj = _norm_proj(x, g, w_in_bf16, cos, sin, ((0, 2 * RET_W), (o_mq, o_mq + 2 * MOBA_W)), bsz)
    rq, rk, rv = proj[:, :RET_W], proj[:, RET_W:2 * RET_W], proj[:, 2 * RET_W:3 * RET_W]
    mq = proj[:, o_mq:o_mq + MOBA_W]
    mk = proj[:, o_mq + MOBA_W:o_mq + 2 * MOBA_W]
    mv = proj[:, o_mq + 2 * MOBA_W:o_eq]
    gamma = np.repeat(1.0 - np.power(2.0, -5.0 - np.arange(RET_H, dtype=np.float64)), HD)
    o_ret, s_ret = _state_step(state, rq, rk, jnp.broadcast_to(jnp.asarray(gamma, F32), (bsz, RET_W)), rv, "ret")
    o_moba = _moba_sample(page_table, _pages_t(pool, MOBA_W), layer, mq, mk, mv)
    o_mem = _mem_sample(proj[:, o_eq:o_gate], _mem_t(mem_cache))
    y = _mix_out(x, o_ret, o_moba, o_mem, proj[:, o_gate:], w_o_bf16, None, bsz)
    rows = proj[:, o_mq + MOBA_W:o_eq].reshape(bsz, 1, 2, MOBA_H, HD)
    return y, rows, s_ret


def _cols(a):
    bsz = a.shape[0]
    return a.reshape(bsz, NSA_KVH, NSA_REP, HD).transpose(0, 1, 3, 2).reshape(bsz, NSA_KVW, NSA_REP)


def _uncols(a):
    bsz = a.shape[0]
    return a.reshape(bsz, NSA_KVH, HD, NSA_REP).transpose(0, 1, 3, 2).reshape(bsz, NSA_QW)


def _odd_sample(x, cos, sin, g, w_aug_bf16, w_o_bf16, mem_cache, state, lb, page_table, cmp_pool, slc_pool, wbuf,
                layer, cmp_w_t, final_g):
    bsz = x.shape[0]
    lbuf = wbuf.shape[1]
    kvo = lambda j: ODD_KV + j * NSA_KVW
    proj = _norm_proj(x, g, w_aug_bf16, cos, sin, ((ODD_NQR, ODD_KV), (kvo(2), kvo(3)), (kvo(4), kvo(5))), bsz)
    ck, cv, sk, sv, wk, wv = (proj[:, kvo(j):kvo(j + 1)] for j in range(6))
    wd, ped, w2t = cmp_w_t
    n_pool, n_layer = cmp_pool.shape[0], cmp_pool.shape[1]
    cmp_t = cmp_pool.transpose(0, 1, 3, 4, 5, 2)
    o_c, sel = _nsa_cmp_sample(page_table, cmp_t, layer, proj[:, ODD_NQ:ODD_NQ + NSA_QW].reshape(bsz, NSA_H, HD),
                               wd, ped, w2t)
    qr = proj[:, ODD_NQR:ODD_NQR + NSA_QW]
    q_cols = _cols(qr)
    o_s = _nsa_sel_sample(page_table, slc_pool.transpose(0, 1, 3, 4, 5, 2), layer, qr.reshape(bsz, NSA_H, HD),
                          sk.reshape(bsz, NSA_KVH, HD), sv.reshape(bsz, NSA_KVH, HD), sel).reshape(bsz, NSA_QW)
    wbuf_t = wbuf.transpose(0, 2, 3, 4, 1).reshape(bsz, 2, NSA_KVW, lbuf)
    o_w = _win_sample(q_cols, wbuf_t, wk, wv)
    ng = proj[:, ODD_NG:ODD_NG + 3 * NSA_H].reshape(bsz, NSA_H, 3)
    gate_logits = jnp.repeat(ng.transpose(2, 0, 1), HD, axis=-1)
    o_nsa = _gate_combine(gate_logits, o_c.reshape(bsz, NSA_QW), o_s, _uncols(o_w))
    hq, hf, hi = (proj[:, ODD_HQ + j * HGRN_W:ODD_HQ + (j + 1) * HGRN_W] for j in range(3))
    o_hg, s_hg = _state_step(state, hq, hf, jnp.broadcast_to(lb.reshape(1, HGRN_W), (bsz, HGRN_W)), hi, "hgrn")
    o_mem = _mem_sample(proj[:, ODD_EQ:ODD_GATE], _mem_t(mem_cache))
    y = _mix_out(x, o_nsa, o_hg, o_mem, proj[:, ODD_GATE:ODD_NG], w_o_bf16, final_g, bsz)
    rows = lambda a, b: jnp.stack([a.reshape(bsz, 1, NSA_KVH, HD), b.reshape(bsz, 1, NSA_KVH, HD)], axis=2)
    win = jnp.concatenate([wbuf.astype(F32), rows(wk, wv)], axis=1)[:, -lbuf:]
    return y, rows(ck, cv), rows(sk, sv), win, s_hg


def kernel(x_prompt, x_sample, mem_prompt, cache_moba_kv, state_ret, cache_nsa_cmp_kv, cache_nsa_slc_kv,
           cache_nsa_win_kv, state_hgrn, cache_mem_kv, page_table, norm_g, mem_norm_g, w_mem_kv, w_in_even,
           w_in_odd, w_out, cmp_w1, cmp_w2, cmp_pe, hgrn_lb_logits, final_g):
    bp, tp, d = x_prompt.shape
    bs, ts, _ = x_sample.shape
    depth = w_out.shape[0]
    assert bp == 1 and ts == 1 and depth == 2
    n_mem = mem_prompt.shape[1]
    past_len = page_table.shape[1] * PAGE
    assert past_len % MOBA_BLOCK == 0 and cache_moba_kv.shape[2] == PAGE
    tm, tq, kc = 256, 128, 256
    xp, xs, mem = x_prompt[0], x_sample[:, 0], mem_prompt[0]
    cos_p, sin_p = _rope_tables(jnp.arange(tp, dtype=jnp.int32))
    cos_s, sin_s = (jnp.broadcast_to(a, (bs, LANES)) for a in _rope_tables(jnp.full((1,), past_len, jnp.int32)))
    lb_prob = jax.nn.softmax(hgrn_lb_logits.astype(F32), axis=0)
    lb_all = jnp.cumsum(lb_prob, axis=0) - lb_prob[0]
    lw_p = min(NSA_WINDOW, tp)

    def mem_kv(layer):
        kv = _norm_proj(mem, mem_norm_g[layer], w_mem_kv[layer].astype(BF16), cos_p[:n_mem], sin_p[:n_mem], (), n_mem)
        return kv.reshape(n_mem, 2, MEM_W)

    w_in0 = w_in_even[0].astype(BF16)
    w_o0 = w_out[0].astype(BF16)
    mkv0 = mem_kv(0)
    hp, moba_p, ret_p = _even_prompt(xp, cos_p, sin_p, norm_g[0], w_in0, w_o0, mkv0[:, 0], mkv0[:, 1], tm)
    hs, moba_s, ret_s = _even_sample(xs, cos_s, sin_s, norm_g[0], w_in0, w_o0, cache_mem_kv[0], state_ret[0],
                                     page_table, cache_moba_kv, 0)

    w_in1 = _odd_weights(w_in_odd[0]).astype(BF16)
    w_o1 = w_out[1].astype(BF16)
    mkv1 = mem_kv(1)
    cmp_w = _compress_weights(cmp_w1[0], cmp_w2[0], cmp_pe[0])
    cmp_w_t = _compress_weights_t(cmp_w1[0], cmp_w2[0], cmp_pe[0])
    yp, cmp_p, slc_p, win_p, hg_p = _odd_prompt(hp, cos_p, sin_p, norm_g[1], w_in1, w_o1, mkv1[:, 0], mkv1[:, 1],
                                                lb_all[1], cmp_w, final_g, tm, tq, kc)
    ys, cmp_s, slc_s, win_s, hg_s = _odd_sample(hs, cos_s, sin_s, norm_g[1], w_in1, w_o1, cache_mem_kv[1],
                                                state_hgrn[0], lb_all[1], page_table, cache_nsa_cmp_kv,
                                                cache_nsa_slc_kv, cache_nsa_win_kv[0], 0, cmp_w_t, final_g)

    return (yp[None], ys[:, None], moba_p[None, None], moba_s[:, None], ret_p[None, None], ret_s[None],
            cmp_p[None, None], cmp_s[:, None], slc_p[None, None], slc_s[:, None], win_p[None, tp - lw_p:][None],
            win_s[None], hg_p[None, None], hg_s[None], jnp.stack([mkv0, mkv1]).reshape(depth, 1, n_mem, 2, MEM_H, HD))
```

```python
import functools
import math

import jax
import jax.numpy as jnp
import numpy as np
from jax import lax
from jax.experimental import pallas as pl
from jax.experimental.pallas import tpu as pltpu

F32 = jnp.float32
BF16 = jnp.bfloat16
HIGHEST = lax.Precision.HIGHEST

HD = 64
RET_H, MOBA_H, MEM_H, NSA_H, NSA_KVH, HGRN_H = 6, 6, 4, 8, 2, 4
NSA_REP = NSA_H // NSA_KVH
RET_W, MOBA_W, MEM_W = RET_H * HD, MOBA_H * HD, MEM_H * HD
NSA_QW, NSA_KVW, HGRN_W = NSA_H * HD, NSA_KVH * HD, HGRN_H * HD
MIX_W = RET_W + MOBA_W + MEM_W
PAGE = 128
RET_CHUNK, HGRN_CHUNK = 128, 64
MOBA_BLOCK, MOBA_TOPK = 256, 3
NSA_BLOCK, NSA_TOPN, NSA_WINDOW = 64, 16, 512
ROPE_THETA = 10000.0
EPS = 1e-6
NEG = -1e30
FORCE = 1e30
TINY = 1e-30
SCALE = HD ** -0.5
LANES = 128
SHIFT_SLACK = 1.02
SHIFT_LIMIT = 20.0
VMEM_LIMIT = 56 * 1024 * 1024


def _params(*sem):
    return pltpu.CompilerParams(dimension_semantics=sem, vmem_limit_bytes=VMEM_LIMIT)


def _dot(a, b, precision=None):
    return jnp.dot(a, b, preferred_element_type=F32, precision=precision)


def _dot_nt(a, b, precision=None):
    return lax.dot_general(a, b, (((1,), (1,)), ((), ())), preferred_element_type=F32, precision=precision)


def _dot_tn(a, b, precision=None):
    return lax.dot_general(a, b, (((0,), (0,)), ((), ())), preferred_element_type=F32, precision=precision)


def _dot_split(a, b_bf16):
    hi = a.astype(BF16)
    lo = (a - hi.astype(F32)).astype(BF16)
    return _dot(hi, b_bf16) + _dot(lo, b_bf16)


def _block_ones(width, value=1.0):
    r = lax.broadcasted_iota(jnp.int32, (width, width), 0) // HD
    c = lax.broadcasted_iota(jnp.int32, (width, width), 1) // HD
    return jnp.where(r == c, value, 0.0).astype(F32)


def _rope_tile(a, cos, sin, first_half):
    rot = jnp.where(first_half, pltpu.roll(a, LANES - HD // 2, 1), pltpu.roll(a, HD // 2, 1))
    return a * cos + rot * sin


def _norm_proj_body(x_ref, g_ref, w_ref, cos_ref, sin_ref, *o_refs, rope_tiles, n_chunk, starts):
    x = x_ref[...]
    ms = jnp.mean(x * x, axis=-1, keepdims=True)
    y = (x * lax.rsqrt(ms + EPS) * g_ref[...]).astype(BF16)
    n = w_ref.shape[1]
    lane = lax.broadcasted_iota(jnp.int32, (x.shape[0], LANES), 1)
    first_half = (lane % HD) < HD // 2
    for c0 in range(0, n, n_chunk):
        acc = _dot(y, w_ref[:, c0:c0 + n_chunk])
        for j in range(n_chunk // LANES):
            tile = (c0 // LANES) + j
            which = max(i for i, s in enumerate(starts) if s <= tile * LANES)
            local = tile * LANES - starts[which]
            if local >= o_refs[which].shape[1]:
                continue
            a = acc[:, j * LANES:(j + 1) * LANES]
            if tile in rope_tiles:
                a = _rope_tile(a, cos_ref[...], sin_ref[...], first_half)
            o_refs[which][:, local:local + LANES] = a


def _norm_proj(x, g, w_bf16, cos, sin, rope_cols, tm, splits=None):
    m, d = x.shape
    n = w_bf16.shape[1]
    splits = (n,) if splits is None else tuple(splits)
    assert m % tm == 0 and n % LANES == 0 and all(s % LANES == 0 for s in splits) and sum(splits) <= n
    starts = tuple(int(s) for s in np.cumsum((0,) + splits[:-1]))
    tiles = n // LANES
    k = next(c for c in (4, 3, 2, 1) if tiles % c == 0)
    rope_tiles = frozenset(t for a, b in rope_cols for t in range(a // LANES, b // LANES))
    body = functools.partial(_norm_proj_body, rope_tiles=rope_tiles, n_chunk=k * LANES, starts=starts)
    outs = pl.pallas_call(
        body,
        grid=(m // tm,),
        in_specs=[
            pl.BlockSpec((tm, d), lambda i: (i, 0)),
            pl.BlockSpec((1, d), lambda i: (0, 0)),
            pl.BlockSpec((d, n), lambda i: (0, 0)),
            pl.BlockSpec((tm, LANES), lambda i: (i, 0)),
            pl.BlockSpec((tm, LANES), lambda i: (i, 0)),
        ],
        out_specs=[pl.BlockSpec((tm, s), lambda i: (i, 0)) for s in splits],
        out_shape=[jax.ShapeDtypeStruct((m, s), F32) for s in splits],
        compiler_params=_params("parallel"),
        name="norm_proj",
    )(x, g.reshape(1, d), w_bf16, cos, sin)
    return outs if len(splits) > 1 else outs[0]


def _rope_tables(pos):
    half = HD // 2
    inv = ROPE_THETA ** (-jnp.arange(half, dtype=F32) / half)
    ang = pos.astype(F32)[:, None] * inv[None, :]
    cos, sin = jnp.cos(ang), jnp.sin(ang)
    return jnp.concatenate([cos, cos, cos, cos], -1), jnp.concatenate([-sin, sin, -sin, sin], -1)


def _retention_consts(c):
    lg = np.log(1.0 - np.power(2.0, -5.0 - np.arange(RET_H, dtype=np.float64)))
    ti = np.arange(c, dtype=np.float64)
    causal = ti[:, None] >= ti[None, :]
    d_in = np.where(causal[None], np.exp(np.where(causal, ti[:, None] - ti[None, :], 0.0)[None] * lg[:, None, None]), 0.0)
    q_dec = np.repeat(np.exp((ti[:, None] + 1.0) * lg[None, :]), HD, axis=1)
    k_dec = np.repeat(np.exp((c - 1.0 - ti)[:, None] * lg[None, :]), HD, axis=1)
    c_dec = np.repeat(np.exp(c * lg), HD)
    head = np.arange(RET_W) // HD
    bd = (head[:, None] == head[None, :]).astype(np.float64)
    cmat = bd * c_dec[:, None]
    f = lambda a: jnp.asarray(a, dtype=F32)
    return f(d_in), f(q_dec), f(k_dec), f(cmat), f(bd)


def _retention_body(q_ref, k_ref, v_ref, din_ref, qdec_ref, kdec_ref, cmat_ref, bd_ref, o_ref, s_ref):
    @pl.when(pl.program_id(0) == 0)
    def _():
        s_ref[...] = jnp.zeros_like(s_ref)

    q = q_ref[...]
    k = k_ref[...] * SCALE
    v = v_ref[...]
    c, w = q.shape
    head = lax.broadcasted_iota(jnp.int32, (c, w), 1) // HD
    s = s_ref[...]
    qb, kb, vb = q.astype(BF16), k.astype(BF16), v.astype(BF16)
    o = _dot(qb, s.astype(BF16)) * qdec_ref[...]
    for h in range(RET_H):
        mh = head == h
        att = _dot_nt(jnp.where(mh, qb, jnp.zeros_like(qb)), kb) * din_ref[h]
        o = o + jnp.where(mh, _dot(att.astype(BF16), vb), 0.0)
    s_ref[...] = s * cmat_ref[...] + _dot_tn((k * kdec_ref[...]).astype(BF16), vb) * bd_ref[...]
    seg = (bd_ref[...] * (1.0 / HD)).astype(BF16)
    xc = o - _dot_split(o, seg)
    var = _dot_split(xc * xc, seg)
    o_ref[...] = xc * lax.rsqrt(var + EPS)


def _retention_prompt(qkv):
    t, w = qkv.shape[0], RET_W
    c = math.gcd(t, RET_CHUNK)
    d_in, q_dec, k_dec, cmat, bd = _retention_consts(c)
    col = lambda j: pl.BlockSpec((c, w), lambda i: (i, j))
    const2 = lambda shape: pl.BlockSpec(shape, lambda i: (0,) * len(shape))
    return pl.pallas_call(
        _retention_body,
        grid=(t // c,),
        in_specs=[col(0), col(1), col(2), const2((RET_H, c, c)), const2((c, w)), const2((c, w)), const2((w, w)),
                  const2((w, w))],
        out_specs=[col(0), const2((w, w))],
        out_shape=[jax.ShapeDtypeStruct((t, w), F32), jax.ShapeDtypeStruct((w, w), F32)],
        compiler_params=_params("arbitrary"),
        name="retention_prompt",
    )(qkv, qkv, qkv, d_in, q_dec, k_dec, cmat, bd)


def _diag_blocks(s, h):
    s4 = s.reshape(h, HD, h, HD)
    return jnp.stack([s4[i, :, i, :] for i in range(h)], axis=0)


def _block_stats_body(k_ref, mean_ref, norm_ref):
    k = k_ref[...]
    mean_ref[0] = jnp.mean(k, axis=0, keepdims=True)
    n2 = _dot((k * k).astype(BF16), _block_ones(k.shape[1]).astype(BF16))
    norm_ref[0] = jnp.max(n2, axis=0, keepdims=True)


def _block_stats(k, blk, w, col):
    t = k.shape[0]
    assert t % blk == 0
    mean, norm = pl.pallas_call(
        _block_stats_body,
        grid=(t // blk,),
        in_specs=[pl.BlockSpec((blk, w), lambda i: (i, col))],
        out_specs=[pl.BlockSpec((1, 1, w), lambda i: (i, 0, 0))] * 2,
        out_shape=[jax.ShapeDtypeStruct((t // blk, 1, w), F32)] * 2,
        compiler_params=_params("parallel"),
        name="block_stats",
    )(k)
    return mean.reshape(t // blk, w), norm.reshape(t // blk, w)


def _key_norm_bound(norm2, heads):
    kmax = jnp.sqrt(jnp.max(norm2, axis=0)).reshape(heads, HD)[:, :1]
    return jnp.broadcast_to(kmax[:, :, None], (heads, 1, LANES))


def _topk_mask(score, k):
    n = score.shape[-1]
    idx = lax.broadcasted_iota(jnp.int32, score.shape, score.ndim - 1)
    sel = jnp.zeros(score.shape, F32)
    work = score
    for _ in range(k):
        m = jnp.max(work, axis=-1, keepdims=True)
        first = jnp.min(jnp.where(work == m, idx, n), axis=-1, keepdims=True)
        pick = idx == first
        sel = jnp.where(pick, 1.0, sel)
        work = jnp.where(pick, -jnp.inf, work)
    return sel


def _tile_loop(lo, hi, fn, unroll=8):
    def group(g, _):
        for u in range(unroll):
            fn(lo + unroll * g + u)
        return 0
    n = jnp.maximum(hi - lo, 0)
    main = n // unroll
    lax.fori_loop(0, main, group, 0)
    done = lo + main * unroll
    rem = n - main * unroll
    size = unroll // 2
    while size >= 1:
        @pl.when((rem & size) != 0)
        def _(done=done, size=size):
            for u in range(size):
                fn(done + u)
        done = done + (rem & size)
        size //= 2


def _lane_fold_max(s):
    out = s[:, 0:LANES]
    for c in range(1, s.shape[1] // LANES):
        out = jnp.maximum(out, s[:, c * LANES:(c + 1) * LANES])
    return out


def _moba_prompt_body(q_ref, ka_ref, vo_ref, kmean_ref, kmax_ref, o_ref, m_ref, acc_ref, *, topk, bq):
    qi = pl.program_id(1)
    q = jnp.concatenate([q_ref[0], jnp.zeros(q_ref.shape[1:], F32)], axis=1)
    rows = q.shape[0]
    kb = MOBA_BLOCK
    gate = _dot_nt(q, kmean_ref[0], HIGHEST)
    blk = lax.broadcasted_iota(jnp.int32, (rows, LANES), 1) - HD
    own = qi * bq + lax.broadcasted_iota(jnp.int32, (rows, LANES), 0) // kb
    past = (blk >= 0) & (blk < own)
    sel = jnp.where(past, _topk_mask(jnp.where(past, gate, NEG), topk), 0.0)
    bias = jnp.where((blk < 0) | (blk == own) | (sel > 0.5), 0.0, NEG)
    qa = (q * SCALE + bias).astype(BF16)
    n_past = qi * bq
    qpos = lax.broadcasted_iota(jnp.int32, (rows, kb), 0)
    kpos = lax.broadcasted_iota(jnp.int32, (rows, kb), 1)
    acc_ref[...] = jnp.zeros(acc_ref.shape, F32)

    def scores(j):
        start = pl.multiple_of(j * kb, kb)
        return _dot_nt(qa, ka_ref[0, pl.ds(start, kb), :])

    def own_scores(d):
        return jnp.where(kpos + d * kb <= qpos, scores(n_past + d), NEG)

    def add_values(j, p):
        start = pl.multiple_of(j * kb, kb)
        acc_ref[...] += _dot(p.astype(BF16), vo_ref[0, pl.ds(start, kb), :])

    bound = jnp.sqrt(jnp.sum(q * q, axis=-1, keepdims=True)) * kmax_ref[0][:, 0:1] * (SCALE * SHIFT_SLACK)
    small = jnp.max(bound) <= SHIFT_LIMIT

    @pl.when(small)
    def _():
        _tile_loop(0, n_past, lambda j: add_values(j, jnp.exp(scores(j) - bound)))
        for d in range(bq):
            add_values(n_past + d, jnp.exp(own_scores(d) - bound))

    @pl.when(jnp.logical_not(small))
    def _():
        m_ref[...] = jnp.full(m_ref.shape, NEG, F32)

        def fold(s):
            m_ref[...] = jnp.maximum(m_ref[...], _lane_fold_max(s))

        _tile_loop(0, n_past, lambda j: fold(scores(j)))
        for d in range(bq):
            fold(own_scores(d))
        m = jnp.max(m_ref[...], axis=-1, keepdims=True)
        _tile_loop(0, n_past, lambda j: add_values(j, jnp.exp(scores(j) - m)))
        for d in range(bq):
            add_values(n_past + d, jnp.exp(own_scores(d) - m))

    acc = acc_ref[...]
    o_ref[0] = (acc / jnp.maximum(acc[:, 0:1], TINY))[:, HD:]


def _moba_prompt(q_pad, ka_bf16, vo_bf16, kmean_rows, kmax):
    h, t, _ = q_pad.shape
    nblk = t // MOBA_BLOCK
    assert t % MOBA_BLOCK == 0 and nblk <= LANES - HD
    bq = next(c for c in (4, 2, 1) if nblk % c == 0)
    rows = bq * MOBA_BLOCK
    return pl.pallas_call(
        functools.partial(_moba_prompt_body, topk=min(MOBA_TOPK, nblk), bq=bq),
        grid=(h, nblk // bq),
        in_specs=[
            pl.BlockSpec((1, rows, HD), lambda a, i: (a, i, 0)),
            pl.BlockSpec((1, t, LANES), lambda a, i: (a, 0, 0), pipeline_mode=pl.Buffered(1)),
            pl.BlockSpec((1, t, LANES), lambda a, i: (a, 0, 0), pipeline_mode=pl.Buffered(1)),
            pl.BlockSpec((1, LANES, LANES), lambda a, i: (a, 0, 0)),
            pl.BlockSpec((1, 1, LANES), lambda a, i: (a, 0, 0)),
        ],
        out_specs=pl.BlockSpec((1, rows, HD), lambda a, i: (a, i, 0)),
        out_shape=jax.ShapeDtypeStruct((h, t, HD), F32),
        scratch_shapes=[pltpu.VMEM((rows, LANES), F32), pltpu.VMEM((rows, LANES), F32)],
        compiler_params=_params("parallel", "arbitrary"),
        name="moba_prompt",
    )(q_pad, ka_bf16, vo_bf16, kmean_rows, kmax)


def _mem_prompt_body(q_ref, k_ref, v_ref, o_ref):
    q = q_ref[...]
    kb = k_ref[...].astype(BF16)
    vb = v_ref[...].astype(BF16)
    head = lax.broadcasted_iota(jnp.int32, q.shape, 1) // HD
    o = jnp.zeros(q.shape, F32)
    for h in range(MEM_H):
        mh = head == h
        s = _dot_nt(jnp.where(mh, q, 0.0).astype(BF16), kb) * SCALE
        e = jnp.exp(s - jnp.max(s, axis=-1, keepdims=True))
        p = e / jnp.sum(e, axis=-1, keepdims=True)
        o = o + jnp.where(mh, _dot(p.astype(BF16), vb), 0.0)
    o_ref[...] = o


def _mem_prompt(q, k, v, tm):
    t, w = q.shape
    n = k.shape[0]
    return pl.pallas_call(
        _mem_prompt_body,
        grid=(t // tm,),
        in_specs=[pl.BlockSpec((tm, w), lambda i: (i, 0)), pl.BlockSpec((n, w), lambda i: (0, 0)),
                  pl.BlockSpec((n, w), lambda i: (0, 0))],
        out_specs=pl.BlockSpec((tm, w), lambda i: (i, 0)),
        out_shape=jax.ShapeDtypeStruct((t, w), F32),
        compiler_params=_params("parallel"),
        name="mem_prompt",
    )(q, k, v)


def _mix_out_body(x_ref, a_ref, b_ref, c_ref, gate_ref, wo_ref, fg_ref, o_ref, *, final_norm):
    gate = gate_ref[...]
    gate = gate * jax.nn.sigmoid(gate)
    y = x_ref[...]
    off = 0
    for ref in (a_ref, b_ref, c_ref):
        w = ref.shape[1]
        mix = (ref[...] * gate[:, off:off + w]).astype(BF16)
        y = y + _dot(mix, wo_ref[off:off + w, :])
        off += w
    if final_norm:
        ms = jnp.mean(y * y, axis=-1, keepdims=True)
        y = y * lax.rsqrt(ms + EPS) * fg_ref[...]
    o_ref[...] = y


def _mix_out(x, a, b, c, gate, wo_bf16, final_g, tm):
    m, d = x.shape
    row = lambda w: pl.BlockSpec((tm, w), lambda i: (i, 0))
    fg = jnp.ones((1, d), F32) if final_g is None else final_g.reshape(1, d).astype(F32)
    body = functools.partial(_mix_out_body, final_norm=final_g is not None)
    return pl.pallas_call(
        body,
        grid=(m // tm,),
        in_specs=[row(d), row(a.shape[1]), row(b.shape[1]), row(c.shape[1]), row(gate.shape[1]),
                  pl.BlockSpec(wo_bf16.shape, lambda i: (0, 0)), pl.BlockSpec((1, d), lambda i: (0, 0))],
        out_specs=row(d),
        out_shape=jax.ShapeDtypeStruct((m, d), F32),
        compiler_params=_params("parallel"),
        name="mix_out",
    )(x, a, b, c, gate, wo_bf16, fg)


def _compress_body(x_ref, pe_ref, w1_ref, w2_ref, o_ref):
    x = (x_ref[0] + pe_ref[0]).astype(BF16)
    hid = jax.nn.gelu(_dot(x, w1_ref[0]))
    o_ref[0] = _dot(hid.astype(BF16), w2_ref[0])


def _compress_weights(w1, w2, pe):
    eye = jnp.eye(NSA_KVH, dtype=F32)
    w1r = w1.astype(F32).reshape(2, NSA_BLOCK, HD, HD)
    w1b = jnp.einsum("klde,gh->klgdhe", w1r, eye).reshape(2, NSA_BLOCK * NSA_KVW, NSA_KVW)
    w2b = jnp.einsum("kde,gh->kgdhe", w2.astype(F32), eye).reshape(2, NSA_KVW, NSA_KVW)
    peb = jnp.broadcast_to(pe.astype(F32)[:, :, None, :], (2, NSA_BLOCK, NSA_KVH, HD)).reshape(2, 1, NSA_BLOCK * NSA_KVW)
    return w1b.astype(BF16), w2b.astype(BF16), peb


def _compress(x, w1b, w2b, peb):
    _, nb, kdim = x.shape
    return pl.pallas_call(
        _compress_body,
        grid=(2,),
        in_specs=[pl.BlockSpec((1, nb, kdim), lambda i: (i, 0, 0)), pl.BlockSpec((1, 1, kdim), lambda i: (i, 0, 0)),
                  pl.BlockSpec((1, kdim, NSA_KVW), lambda i: (i, 0, 0)),
                  pl.BlockSpec((1, NSA_KVW, NSA_KVW), lambda i: (i, 0, 0))],
        out_specs=pl.BlockSpec((1, nb, NSA_KVW), lambda i: (i, 0, 0)),
        out_shape=jax.ShapeDtypeStruct((2, nb, NSA_KVW), F32),
        compiler_params=_params("parallel"),
        name="nsa_compress",
    )(x, peb, w1b, w2b)


def _nsa_prompt_body(q_ref, qr_ref, gt_ref, ckv_ref, ska_ref, svo_ref, wkv_ref, kmax_ref, o_ref, m_ref, acc_ref,
                     *, tq, kc):
    i = pl.program_id(1)
    t0 = i * tq
    rows = q_ref.shape[2]
    nb = ckv_ref.shape[1]
    pad = lambda a: jnp.concatenate([a, jnp.zeros_like(a)], axis=1)
    q = pad(q_ref[0, 0])
    qr = pad(qr_ref[0, 0]).astype(F32)
    qrs = qr * SCALE
    qrb = qrs.astype(BF16)
    tpos = t0 + lax.broadcasted_iota(jnp.int32, (rows, 1), 0) % tq

    ckv = ckv_ref[0].astype(BF16)
    blk = lax.broadcasted_iota(jnp.int32, (nb, rows), 0)
    tpos_l = t0 + lax.broadcasted_iota(jnp.int32, (nb, rows), 1) % tq
    complete = blk * NSA_BLOCK + (NSA_BLOCK - 1) <= tpos_l
    s_c = jnp.where(complete, _dot_nt(ckv, q) * SCALE, NEG)
    e = jnp.where(complete, jnp.exp(s_c - jnp.max(s_c, axis=0, keepdims=True)), 0.0)
    pc = e / jnp.maximum(jnp.sum(e, axis=0, keepdims=True), TINY)
    o_c = _dot_tn(pc.astype(BF16), ckv)

    imp = pc[:, 0:tq]
    for r in range(1, NSA_REP):
        imp = imp + pc[:, r * tq:(r + 1) * tq]
    blk_q = lax.broadcasted_iota(jnp.int32, (nb, tq), 0)
    tpos_q = t0 + lax.broadcasted_iota(jnp.int32, (nb, tq), 1)
    cur = tpos_q // NSA_BLOCK
    forced = (blk_q == 0) | (blk_q == cur) | (blk_q == cur - 1)
    complete_q = blk_q * NSA_BLOCK + (NSA_BLOCK - 1) <= tpos_q
    score = jnp.where(forced, FORCE, jnp.where(complete_q, imp, NEG))
    sel = jnp.where(score > 0.5 * NEG, _topk_mask_axis0(score, min(NSA_TOPN, nb)), 0.0)

    bias = jnp.where(sel > 0.5, 0.0, NEG).T
    nbp = -(-nb // LANES) * LANES
    if nbp > nb:
        bias = jnp.concatenate([bias, jnp.full((tq, nbp - nb), NEG, F32)], axis=1)
    bias = jnp.concatenate([bias] * NSA_REP, axis=0)
    upper =lax.broadcasted_iota(jnp.int32, (rows, LANES), 1) >= HD
    n_span = -(-nb // HD)
    qa = []
    for sp in range(n_span):
        col = bias[:, (sp // 2) * LANES:(sp // 2 + 1) * LANES]
        if sp % 2 == 0:
            col = pltpu.roll(col, HD, 1)
        qa.append((qrs + jnp.where(upper, col, 0.0)).astype(BF16))
    cps = HD * NSA_BLOCK // kc
    n_chunks = (t0 + tq + kc - 1) // kc
    last = n_chunks - 1
    qa_last = qa[0]
    for sp in range(1, n_span):
        qa_last = jnp.where(last // cps == sp, qa[sp], qa_last)
    last_start = pl.multiple_of(last * kc, kc)
    kpos = last_start + lax.broadcasted_iota(jnp.int32, (rows, kc), 1)
    s_last = jnp.where(kpos <= tpos, _dot_nt(qa_last, ska_ref[0, pl.ds(last_start, kc), :]), NEG)

    def spans(fn):
        for sp in range(n_span):
            lo = sp * cps
            _tile_loop(jnp.minimum(lo, last), jnp.minimum(lo + cps, last), functools.partial(fn, qa[sp]))

    def scores(qsp, c):
        start = pl.multiple_of(c * kc, kc)
        return _dot_nt(qsp, ska_ref[0, pl.ds(start, kc), :])

    def add_values(c, p):
        start = pl.multiple_of(c * kc, kc)
        acc_ref[...] += _dot(p.astype(BF16), svo_ref[0, pl.ds(start, kc), :])

    acc_ref[...] = jnp.zeros(acc_ref.shape, F32)
    bound =jnp.sqrt(jnp.sum(qr * qr, axis=-1, keepdims=True)) * kmax_ref[0][:, 0:1] * (SCALE * SHIFT_SLACK)
    small = jnp.max(bound) <= SHIFT_LIMIT

    @pl.when(small)
    def _():
        add_values(last, jnp.exp(s_last - bound))
        spans(lambda qsp, c: add_values(c, jnp.exp(scores(qsp, c) - bound)))

    @pl.when(jnp.logical_not(small))
    def _():
        m_ref[...] = _lane_fold_max(s_last)

        def fold_max(qsp, c):
            m_ref[...] = jnp.maximum(m_ref[...], _lane_fold_max(scores(qsp, c)))

        spans(fold_max)
        m_s = jnp.max(m_ref[...], axis=-1, keepdims=True)
        add_values(last, jnp.exp(s_last - m_s))
        spans(lambda qsp, c: add_values(c, jnp.exp(scores(qsp, c) - m_s)))

    acc_s = acc_ref[...]
    o_s = acc_s / jnp.maximum(acc_s[:, 0:1], TINY)

    nwin = (NSA_WINDOW + tq - 1) // tq + 1
    win = []
    for d in range(nwin):
        c = i - (nwin - 1) + d
        start = pl.multiple_of(jnp.maximum(c, 0) * tq, tq)
        kv = wkv_ref[0, pl.ds(start, tq), :]
        kpos = c * tq + lax.broadcasted_iota(jnp.int32, (rows, tq), 1)
        mask = (kpos <= tpos) & (kpos > tpos - NSA_WINDOW) & (kpos >= 0)
        win.append((jnp.where(mask, _dot_nt(qrb, kv), NEG), kv))
    m_w = jnp.max(functools.reduce(jnp.maximum, [s for s, _ in win]), axis=-1, keepdims=True)
    p_w = [jnp.exp(s - m_w) for s, _ in win]
    l_w = jnp.sum(functools.reduce(jnp.add, p_w), axis=-1, keepdims=True)
    acc_w = functools.reduce(jnp.add, [_dot(p.astype(BF16), kv) for p, (_, kv) in zip(p_w, win)])
    o_w = acc_w / jnp.maximum(l_w, TINY)

    gt = jax.nn.sigmoid(gt_ref[0, 0])
    o_ref[0, 0] = (gt[:, 0:1] * o_c + gt[:, 1:2] * o_s + gt[:, 2:3] * o_w)[:, HD:]


def _nsa_prompt(q_st, qr_st, gt_st, ckv, ska, svo, wkv, kmax, tq, kc):
    g, nt, rows, _ = q_st.shape
    t = ska.shape[1]
    nb = ckv.shape[1]
    assert t % kc == 0 and kc % tq == 0 and (HD * NSA_BLOCK) % kc == 0 and t % tq == 0
    qspec = pl.BlockSpec((1, 1, rows, HD), lambda a, i: (a, i, 0, 0))
    seq = pl.BlockSpec((1, t, LANES), lambda a, i: (a, 0, 0), pipeline_mode=pl.Buffered(1))
    body = functools.partial(_nsa_prompt_body, tq=tq, kc=kc)
    return pl.pallas_call(
        body,
        grid=(g, nt),
        in_specs=[qspec, qspec, pl.BlockSpec((1, 1, rows, 3), lambda a, i: (a, i, 0, 0)),
                  pl.BlockSpec((1, nb, LANES), lambda a, i: (a, 0, 0)), seq, seq, seq,
                  pl.BlockSpec((1, 1, LANES), lambda a, i: (a, 0, 0))],
        out_specs=qspec,
        out_shape=jax.ShapeDtypeStruct((g, nt, rows, HD), F32),
        scratch_shapes=[pltpu.VMEM((rows, LANES), F32), pltpu.VMEM((rows, LANES), F32)],
        compiler_params=_params("parallel", "arbitrary"),
        name="nsa_prompt",
    )(q_st, qr_st, gt_st, ckv, ska, svo, wkv, kmax)


def _hgrn_body(q_ref, f_ref, i_ref, lb_ref, o_ref, s_ref):
    @pl.when(pl.program_id(0) == 0)
    def _():
        s_ref[...] = jnp.zeros_like(s_ref)

    q = q_ref[...]
    v = i_ref[...]
    c, w = q.shape
    lb = lb_ref[...]
    f = lb + (1.0 - lb) * jax.nn.sigmoid(f_ref[...])
    kk = 1.0 - f
    tr = lax.broadcasted_iota(jnp.int32, (c, c), 0)
    tc = lax.broadcasted_iota(jnp.int32, (c, c), 1)
    cum = _dot(jnp.where(tr >= tc, 1.0, 0.0), jnp.log(f), HIGHEST)
    bd = _block_ones(w)
    bd_b = bd.astype(BF16)
    s = s_ref[...]
    o = _dot(q * jnp.exp(cum), s, HIGHEST)
    t_idx = lax.broadcasted_iota(jnp.int32, (c, w), 0)

    sub = 8
    groups = c // sub
    o_blk = [o[i * sub:(i + 1) * sub] for i in range(groups)]
    for g in range(groups):
        r0 = g * sub
        n = c - r0
        t_g = r0 + lax.broadcasted_iota(jnp.int32, (n, w), 0)
        es = [jnp.where(t_g >= j, jnp.exp(cum[r0:] - cum[j:j + 1]) * q[r0:] * kk[j:j + 1], 0.0)
              for j in range(r0, r0 + sub)]
        e = jnp.concatenate(es, axis=0)
        a = _dot(e.astype(BF16), bd_b)
        contrib = a[0:n] * v[r0:r0 + 1]
        for u in range(1, sub):
            contrib = contrib + a[u * n:(u + 1) * n] * v[r0 + u:r0 + u + 1]
        for i in range(g, groups):
            o_blk[i] = o_blk[i] + contrib[(i - g) * sub:(i - g + 1) * sub]
    o = jnp.concatenate(o_blk, axis=0)
    last = cum[c - 1:c, :]
    row0 = t_idx == 0
    scale_mat = _dot_tn(jnp.where(row0, jnp.exp(last), 0.0), jnp.where(row0, 1.0, 0.0), HIGHEST)
    s_ref[...] = s * scale_mat + _dot_tn(kk * jnp.exp(last - cum), v, HIGHEST) * bd
    ms = _dot_split(o * o, (bd * (1.0 / HD)).astype(BF16))
    o_ref[...] = o * lax.rsqrt(ms + EPS)


def _hgrn_prompt(qfi, lb):
    t, w = qfi.shape[0], HGRN_W
    c = math.gcd(t, HGRN_CHUNK)
    col = lambda j: pl.BlockSpec((c, w), lambda n: (n, j))
    return pl.pallas_call(
        _hgrn_body,
        grid=(t // c,),
        in_specs=[col(0), col(1), col(2), pl.BlockSpec((1, w), lambda n: (0, 0))],
        out_specs=[col(0), pl.BlockSpec((w, w), lambda n: (0, 0))],
        out_shape=[jax.ShapeDtypeStruct((t, w), F32), jax.ShapeDtypeStruct((w, w), F32)],
        compiler_params=_params("arbitrary"),
        name="hgrn_prompt",
    )(qfi, qfi, qfi, lb)


def _state_step_body(s_ref, q_ref, a_ref, b_ref, v_ref, o_ref, so_ref, *, mode):
    s = s_ref[...]
    if mode == "ret":
        k = a_ref[...] * SCALE
        dec = b_ref[...]
    else:
        lb = b_ref[...]
        dec = lb + (1.0 - lb) * jax.nn.sigmoid(a_ref[...])
        k = 1.0 - dec
    kv = k * v_ref[...]
    rows = s.shape[0]
    o = (q_ref[...] * (kv + dec * s)).reshape(rows // HD, HD, HD).sum(axis=1)
    so_ref[...] = dec * s + kv
    if mode == "ret":
        xc = o - jnp.mean(o, axis=-1, keepdims=True)
        o_ref[...] = xc * lax.rsqrt(jnp.mean(xc * xc, axis=-1, keepdims=True) + EPS)
    else:
        o_ref[...] = o * lax.rsqrt(jnp.mean(o * o, axis=-1, keepdims=True) + EPS)


def _state_step(state, q, a, b, v, mode):
    bsz, h = state.shape[0], state.shape[1]
    rows = bsz * h * HD
    col = lambda x: jnp.broadcast_to(x.reshape(bsz, h, HD, 1), (bsz, h, HD, HD)).reshape(rows, HD)
    vx = jnp.broadcast_to(v.reshape(bsz, h, 1, HD), (bsz, h, HD, HD)).reshape(rows, HD)
    bb = 8 if bsz % 8 == 0 else bsz
    br = bb * h * HD
    spec = pl.BlockSpec((br, HD), lambda i: (i, 0))
    ospec = pl.BlockSpec((br // HD, HD), lambda i: (i, 0))
    o, s_new = pl.pallas_call(
        functools.partial(_state_step_body, mode=mode),
        grid=(rows // br,),
        in_specs=[spec] * 5,
        out_specs=[ospec, spec],
        out_shape=[jax.ShapeDtypeStruct((rows // HD, HD), F32), jax.ShapeDtypeStruct((rows, HD), F32)],
        compiler_params=_params("parallel"),
        name="state_step_" + mode,
    )(state.reshape(rows, HD).astype(F32), col(q), col(a), col(b), vx)
    return o.reshape(bsz, h * HD), s_new.reshape(bsz, h, HD, HD)


def _expand_heads(p, width):
    g, n = p.shape
    return jnp.broadcast_to(p[:, None, :], (g, HD, n)).reshape(width, n)


def _head_sums(x, g):
    return x.reshape(g, HD, x.shape[-1]).sum(axis=1)


def _mem_sample_body(q_ref, kt_ref, vt_ref, o_ref):
    kt = kt_ref[0, 0]
    vt = vt_ref[0, 0]
    s = _head_sums(kt * q_ref[0], MEM_H) * SCALE
    e = jnp.exp(s - jnp.max(s, axis=-1, keepdims=True))
    p = e / jnp.sum(e, axis=-1, keepdims=True)
    o_ref[0] = jnp.sum(vt * _expand_heads(p, MEM_W), axis=-1, keepdims=True)


def _mem_sample(q, kvt):
    bsz, w = q.shape
    n = kvt.shape[-1]
    out = pl.pallas_call(
        _mem_sample_body,
        grid=(bsz,),
        in_specs=[pl.BlockSpec((1, w, 1), lambda i: (i, 0, 0)),
                  pl.BlockSpec((1, 1, w, n), lambda i: (i, 0, 0, 0)),
                  pl.BlockSpec((1, 1, w, n), lambda i: (i, 1, 0, 0))],
        out_specs=pl.BlockSpec((1, w, 1), lambda i: (i, 0, 0)),
        out_shape=jax.ShapeDtypeStruct((bsz, w, 1), F32),
        compiler_params=_params("parallel"),
        name="mem_sample",
    )(q.reshape(bsz, w, 1), kvt, kvt)
    return out.reshape(bsz, w)


def _topk_mask_axis0(score, k):
    n = score.shape[0]
    idx = lax.broadcasted_iota(jnp.int32, score.shape, 0)
    sel = jnp.zeros(score.shape, F32)
    work = score
    for _ in range(k):
        m = jnp.max(work, axis=0, keepdims=True)
        first = jnp.min(jnp.where(work == m, idx, n), axis=0, keepdims=True)
        pick = idx == first
        sel = jnp.where(pick, 1.0, sel)
        work = jnp.where(pick, -jnp.inf, work)
    return sel


def _moba_sample_body(pt_ref, pool_ref, qc_ref, qbd_ref, kn_ref, vn_ref, o_ref, buf, sem, s_ref, vbuf, vsem,
                      *, layer, chunk, topk):
    b = pl.program_id(0)
    n_pages = pt_ref.shape[1]
    n = n_pages // chunk
    groups, gw = MOBA_H, MOBA_W
    per = MOBA_BLOCK // PAGE
    nblk = n_pages // per

    def copy(t, i):
        page = pt_ref[t // n, (t % n) * chunk + i]
        return pltpu.make_async_copy(pool_ref.at[page, layer, 0], buf.at[t % 2, i], sem.at[t % 2])

    def start(t):
        lax.fori_loop(0, chunk, lambda i, _: (copy(t, i).start(), 0)[1], 0)

    def wait(t):
        lax.fori_loop(0, chunk, lambda i, _: (copy(t, i).wait(), 0)[1], 0)

    @pl.when(b == 0)
    def _():
        start(0)

    def k_step(c, _):
        t = b * n + c
        slot = t % 2

        @pl.when(t + 1 < pl.num_programs(0) * n)
        def _():
            start(t + 1)

        wait(t)
        for i in range(chunk):
            sc = _dot(qbd_ref[0], buf[slot, i].astype(BF16))
            s_ref[c * chunk + i] = sc[0:groups]
        return 0

    lax.fori_loop(0, n, k_step, 0)

    sc = s_ref[...].reshape(nblk, per, groups, PAGE)
    gate = jnp.sum(jnp.sum(sc, axis=1, keepdims=True), axis=-1, keepdims=True) * (1.0 / MOBA_BLOCK)
    blk = lax.broadcasted_iota(jnp.int32, gate.shape, 0)
    sel = jnp.zeros(gate.shape, F32)
    picks = []
    for _ in range(topk):
        first = jnp.min(jnp.where(gate == jnp.max(gate, axis=0, keepdims=True), blk, nblk), axis=0, keepdims=True)
        pick = blk == first
        sel = jnp.where(pick, 1.0, sel)
        gate = jnp.where(pick, -jnp.inf, gate)
        picks.append(first)

    v_copies = []
    for h in range(groups):
        for j in range(topk):
            first_page = picks[j][0, 0, h, 0] * per
            for half in range(per):
                v_copies.append((h, j * per + half, first_page + half))
    descs = [pltpu.make_async_copy(pool_ref.at[pt_ref[b, page], layer, 1, pl.ds(h * HD, HD)], vbuf.at[h, slot_j], vsem)
             for h, slot_j, page in v_copies]
    for d in descs:
        d.start()

    mask = jnp.broadcast_to(sel, (nblk, per, groups, PAGE)).reshape(n_pages, groups, PAGE) > 0.5
    s_own = _head_sums(jnp.broadcast_to(kn_ref[0] * qc_ref[0], (gw, PAGE)), groups) * SCALE
    s = jnp.where(mask, s_ref[...] * SCALE, NEG)
    m = jnp.maximum(jnp.max(jnp.max(s, axis=0), axis=-1, keepdims=True), s_own)
    p = jnp.where(mask, jnp.exp(s - m), 0.0)
    p_own = jnp.exp(s_own - m)
    den = jnp.maximum(jnp.sum(jnp.sum(p, axis=0), axis=-1, keepdims=True) + p_own, TINY)
    s_ref[...] = p

    for d in descs:
        d.wait()
    tot = []
    for h in range(groups):
        acc = jnp.zeros((HD, PAGE), F32)
        for hh, slot_j, page in v_copies:
            if hh == h:
                acc = acc + vbuf[h, slot_j] * s_ref[page][h:h + 1]
        tot.append(jnp.sum(acc, axis=-1, keepdims=True))
    tot = jnp.concatenate(tot, axis=0)
    o_ref[0] = (tot + _expand_heads(p_own, gw)[:, 0:1] * vn_ref[0]) / _expand_heads(den, gw)[:, 0:1]


def _moba_sample(page_table, pool_t, layer, q, k_new, v_new):
    bsz, n_pages = page_table.shape
    gw = MOBA_W
    chunk = math.gcd(n_pages, 16)
    per = MOBA_BLOCK // PAGE
    topk = min(MOBA_TOPK, n_pages // per)
    nq = -(-MOBA_H // 8) * 8
    qbd = jnp.einsum("bgd,gh->bghd", q.reshape(bsz, MOBA_H, HD), jnp.eye(MOBA_H, dtype=F32)).reshape(bsz, MOBA_H, gw)
    qbd = jnp.pad(qbd, ((0, 0), (0, nq - MOBA_H), (0, 0))).astype(BF16)
    col = pl.BlockSpec((1, gw, 1), lambda i, pt: (i, 0, 0))
    grid_spec = pltpu.PrefetchScalarGridSpec(
        num_scalar_prefetch=1,
        grid=(bsz,),
        in_specs=[pl.BlockSpec(memory_space=pl.ANY), col, pl.BlockSpec((1, nq, gw), lambda i, pt: (i, 0, 0)), col, col],
        out_specs=col,
        scratch_shapes=[
            pltpu.VMEM((2, chunk, gw, PAGE), F32),
            pltpu.SemaphoreType.DMA((2,)),
            pltpu.VMEM((n_pages, MOBA_H, PAGE), F32),
            pltpu.VMEM((MOBA_H, topk * per, HD, PAGE), F32),
            pltpu.SemaphoreType.DMA(()),
        ],
    )
    out = pl.pallas_call(
        functools.partial(_moba_sample_body, layer=layer, chunk=chunk, topk=topk),
        grid_spec=grid_spec,
        out_shape=jax.ShapeDtypeStruct((bsz, gw, 1), F32),
        compiler_params=_params("arbitrary"),
        name="moba_sample",
    )(page_table, pool_t, q.reshape(bsz, gw, 1), qbd, k_new.reshape(bsz, gw, 1), v_new.reshape(bsz, gw, 1))
    return out.reshape(bsz, gw)


CMP_DB = 8


def _compress_weights_t(w1, w2, pe):
    per = PAGE // NSA_BLOCK
    eye = jnp.eye(per, dtype=F32)
    w1r = w1.astype(F32).reshape(2, NSA_BLOCK, HD, HD)
    wd = jnp.einsum("klde,gh->kdglhe", w1r, eye).reshape(2, HD // CMP_DB, CMP_DB * PAGE, per * HD)
    ped = jnp.tile(pe.astype(F32).transpose(0, 2, 1), (1, 1, per)).reshape(2, HD // CMP_DB, 1, CMP_DB * PAGE)
    w2t = jnp.einsum("kde,gh->kgdhe", w2.astype(F32), eye).reshape(2, per * HD, per * HD)
    return wd.astype(BF16), ped, w2t.astype(BF16)


def _nsa_cmp_sample_body(pt_ref, pool_ref, q_ref, wd_ref, ped_ref, w2_ref, oc_ref, sel_ref, buf, sem, tok_ref, acc_ref,
                         *, layer, nseq):
    b = pl.program_id(0)
    j = pl.program_id(1)
    bsz, n_pages = pt_ref.shape
    t = b * 2 + j
    slot = t % 2
    m_rows = nseq * NSA_KVH * n_pages

    def copy(grp, kv, sq, p, g, sl):
        row = pl.multiple_of((((sl * nseq + sq) * NSA_KVH + g) * n_pages + p) * HD, HD)
        return pltpu.make_async_copy(pool_ref.at[pt_ref[grp * nseq + sq, p], layer, kv, g], buf.at[pl.ds(row, HD)],
                                     sem.at[sl])

    def each(fn):
        def run(grp, kv, sl):
            def page(p, _):
                for sq in range(nseq):
                    for g in range(NSA_KVH):
                        fn(copy(grp, kv, sq, p, g, sl))
                return 0
            lax.fori_loop(0, n_pages, page, 0)
        return run

    start = each(lambda c: c.start())
    wait = each(lambda c: c.wait())

    @pl.when(t == 0)
    def _():
        start(0, 0, 0)

    @pl.when(t + 1 < 2 * (bsz // nseq))
    def _():
        start((t + 1) // 2, (t + 1) % 2, 1 - slot)

    wait(b, j, slot)

    base = slot * m_rows * HD
    for dd in range(HD // CMP_DB):
        parts = [buf[pl.ds(base + dd * CMP_DB + u, m_rows, stride=HD), :] for u in range(CMP_DB)]
        a = jnp.concatenate(parts, axis=1) + ped_ref[0, dd]
        part = _dot(a.astype(BF16), wd_ref[0, dd])
        if dd == 0:
            acc_ref[...] = part
        else:
            acc_ref[...] += part
    tok_ref[j] = _dot(jax.nn.gelu(acc_ref[...]).astype(BF16), w2_ref[0])

    @pl.when(j == 1)
    def _():
        pos = n_pages * PAGE
        per = PAGE // NSA_BLOCK
        n_idx = (lax.broadcasted_iota(jnp.int32, (per, n_pages), 1) * per
                 + lax.broadcasted_iota(jnp.int32, (per, n_pages), 0))
        complete = n_idx * NSA_BLOCK + (NSA_BLOCK - 1) <= pos
        cur = pos // NSA_BLOCK
        forced = (n_idx == 0) | (n_idx == cur) | (n_idx == cur - 1)
        k_past = min(NSA_TOPN, per * n_pages + 1) - 1
        for sq, g in ((a, c) for a in range(nseq) for c in range(NSA_KVH)):
            qg = q_ref[sq, g * NSA_REP:(g + 1) * NSA_REP, :].astype(BF16)
            r0 = (sq * NSA_KVH + g) * n_pages
            ck = tok_ref[0, r0:r0 + n_pages, :].astype(BF16)
            cv = tok_ref[1, r0:r0 + n_pages, :].astype(BF16)
            s = [jnp.where(complete[h:h + 1], _dot_nt(qg, ck[:, h * HD:(h + 1) * HD]) * SCALE, NEG) for h in range(per)]
            m = functools.reduce(jnp.maximum, [jnp.max(x, axis=-1, keepdims=True) for x in s])
            e = [jnp.where(complete[h:h + 1], jnp.exp(s[h] - m), 0.0) for h in range(per)]
            den = jnp.maximum(sum(jnp.sum(x, axis=-1, keepdims=True) for x in e), TINY)
            pc = [x / den for x in e]
            oc_ref[sq, g * NSA_REP:(g + 1) * NSA_REP, :] = sum(
                _dot(pc[h].astype(BF16), cv[:, h * HD:(h + 1) * HD]) for h in range(per))
            imp = jnp.concatenate([jnp.sum(x, axis=0, keepdims=True) for x in pc], axis=0)
            score = jnp.where(forced, FORCE, jnp.where(complete, imp, NEG))
            sel = jnp.zeros(score.shape, F32)
            work = score
            for _ in range(k_past):
                mx = jnp.max(jnp.max(work, axis=-1, keepdims=True), axis=0, keepdims=True)
                cand = jnp.where(work == mx, n_idx, per * n_pages)
                first = jnp.min(jnp.min(cand, axis=-1, keepdims=True), axis=0, keepdims=True)
                pick = n_idx == first
                sel = jnp.where(pick, 1.0, sel)
                work = jnp.where(pick, -jnp.inf, work)
            sel_ref[sq, g] = jnp.where(score > 0.5 * NEG, sel, 0.0)


def _nsa_cmp_sample(page_table, pool_t, layer, q, wd, ped, w2b):
    bsz, n_pages = page_table.shape
    per = PAGE // NSA_BLOCK
    nseq = 2 if bsz % 2 == 0 else 1
    m_rows = nseq * NSA_KVH * n_pages
    grid_spec = pltpu.PrefetchScalarGridSpec(
        num_scalar_prefetch=1,
        grid=(bsz // nseq, 2),
        in_specs=[
            pl.BlockSpec(memory_space=pl.ANY),
            pl.BlockSpec((nseq, NSA_H, HD), lambda i, j, pt: (i, 0, 0)),
            pl.BlockSpec((1, HD // CMP_DB, CMP_DB * PAGE, per * HD), lambda i, j, pt: (j, 0, 0, 0)),
            pl.BlockSpec((1, HD // CMP_DB, 1, CMP_DB * PAGE), lambda i, j, pt: (j, 0, 0, 0)),
            pl.BlockSpec((1, per * HD, per * HD), lambda i, j, pt: (j, 0, 0)),
        ],
        out_specs=[pl.BlockSpec((nseq, NSA_H, HD), lambda i, j, pt: (i, 0, 0)),
                   pl.BlockSpec((nseq, NSA_KVH, per, n_pages), lambda i, j, pt: (i, 0, 0, 0))],
        scratch_shapes=[
            pltpu.VMEM((2 * m_rows * HD, PAGE), F32),
            pltpu.SemaphoreType.DMA((2,)),
            pltpu.VMEM((2, m_rows, per * HD), F32),
            pltpu.VMEM((m_rows, per * HD), F32),
        ],
    )
    return pl.pallas_call(
        functools.partial(_nsa_cmp_sample_body, layer=layer, nseq=nseq),
        grid_spec=grid_spec,
        out_shape=[jax.ShapeDtypeStruct((bsz, NSA_H, HD), F32),
                   jax.ShapeDtypeStruct((bsz, NSA_KVH, per, n_pages), F32)],
        compiler_params=_params("arbitrary", "arbitrary"),
        name="nsa_cmp_sample",
    )(page_table, pool_t, q, wd, ped, w2b)


def _nsa_sel_sample_body(pt_ref, idx_ref, ok_ref, pool_ref, q_ref, kn_ref, vn_ref, o_ref, buf, sem, *, layer, ksel):
    b = pl.program_id(0)
    bsz = pt_ref.shape[0]
    slot = b % 2
    per = PAGE // NSA_BLOCK

    def copy(bb, kv, g, j, sl):
        page = pt_ref[bb, idx_ref[bb, g * ksel + j] // per]
        return pltpu.make_async_copy(pool_ref.at[page, layer, kv, g], buf.at[sl, kv, g, j], sem.at[sl])

    def each(fn):
        def run(bb, sl):
            def block(j, _):
                for kv in range(2):
                    for g in range(NSA_KVH):
                        fn(copy(bb, kv, g, j, sl))
                return 0
            lax.fori_loop(0, ksel, block, 0)
        return run

    start = each(lambda c: c.start())
    wait = each(lambda c: c.wait())

    @pl.when(b == 0)
    def _():
        start(0, 0)

    @pl.when(b + 1 < bsz)
    def _():
        start(b + 1, 1 - slot)

    wait(b, slot)
    lane_half = lax.broadcasted_iota(jnp.int32, (NSA_REP, PAGE), 1) // NSA_BLOCK
    for g in range(NSA_KVH):
        qg = q_ref[0, g * NSA_REP:(g + 1) * NSA_REP, :] * SCALE
        qb = qg.astype(BF16)
        s = []
        for j in range(ksel):
            blk = idx_ref[b, g * ksel + j]
            valid = (lane_half == blk % per) & (ok_ref[b, g * ksel + j] > 0)
            s.append(jnp.where(valid, _dot(qb, buf[slot, 0, g, j].astype(BF16)), NEG))
        s_own = jnp.sum(qg * kn_ref[0, g:g + 1, :], axis=-1, keepdims=True)
        m = functools.reduce(jnp.maximum, [jnp.max(x, axis=-1, keepdims=True) for x in s] + [s_own])
        p = [jnp.exp(x - m) for x in s]
        p_own = jnp.exp(s_own - m)
        den = functools.reduce(jnp.add, [jnp.sum(x, axis=-1, keepdims=True) for x in p]) + p_own
        o = functools.reduce(jnp.add, [_dot_nt(p[j].astype(BF16), buf[slot, 1, g, j].astype(BF16))
                                       for j in range(ksel)])
        o_ref[0, g * NSA_REP:(g + 1) * NSA_REP, :] = (o + p_own * vn_ref[0, g:g + 1, :]) / jnp.maximum(den, TINY)


def _nsa_sel_sample(page_table, pool_t, layer, q, k_new, v_new, sel):
    bsz, n_pages = page_table.shape
    per = PAGE // NSA_BLOCK
    ksel = min(NSA_TOPN, per * n_pages + 1) - 1
    flat = sel.transpose(0, 1, 3, 2).reshape(bsz, NSA_KVH, n_pages * per)
    vals, idx = lax.top_k(flat, ksel)
    idx = idx.reshape(bsz, NSA_KVH * ksel).astype(jnp.int32)
    ok = (vals > 0.5).reshape(bsz, NSA_KVH * ksel).astype(jnp.int32)
    grid_spec = pltpu.PrefetchScalarGridSpec(
        num_scalar_prefetch=3,
        grid=(bsz,),
        in_specs=[
            pl.BlockSpec(memory_space=pl.ANY),
            pl.BlockSpec((1, NSA_H, HD), lambda i, *_: (i, 0, 0)),
            pl.BlockSpec((1, NSA_KVH, HD), lambda i, *_: (i, 0, 0)),
            pl.BlockSpec((1, NSA_KVH, HD), lambda i, *_: (i, 0, 0)),
        ],
        out_specs=pl.BlockSpec((1, NSA_H, HD), lambda i, *_: (i, 0, 0)),
        scratch_shapes=[pltpu.VMEM((2, 2, NSA_KVH, ksel, HD, PAGE), F32), pltpu.SemaphoreType.DMA((2,))],
    )
    return pl.pallas_call(
        functools.partial(_nsa_sel_sample_body, layer=layer, ksel=ksel),
        grid_spec=grid_spec,
        out_shape=jax.ShapeDtypeStruct((bsz, NSA_H, HD), F32),
        compiler_params=_params("arbitrary"),
        name="nsa_sel_sample",
    )(page_table, idx, ok, pool_t, q, k_new, v_new)


def _win_sample_body(qc_ref, kt_ref, vt_ref, kn_ref, vn_ref, o_ref):
    kt = kt_ref[0, 0]
    vt = vt_ref[0, 0]
    gw, lbuf = kt.shape
    j = lax.broadcasted_iota(jnp.int32, (NSA_KVH, lbuf), 1)
    mask = j > lbuf - NSA_WINDOW
    for r in range(NSA_REP):
        qcol = qc_ref[0, :, r:r + 1]
        s = jnp.where(mask, _head_sums(kt * qcol, NSA_KVH) * SCALE, NEG)
        s_own = _head_sums(jnp.broadcast_to(kn_ref[0] * qcol, (gw, lbuf)), NSA_KVH) * SCALE
        m = jnp.maximum(jnp.max(s, axis=-1, keepdims=True), s_own)
        p = jnp.where(mask, jnp.exp(s - m), 0.0)
        p_own = jnp.exp(s_own - m)
        den = jnp.maximum(jnp.sum(p, axis=-1, keepdims=True) + p_own, TINY)
        tot = jnp.sum(vt * _expand_heads(p, gw), axis=-1, keepdims=True)
        o_ref[0, :, r:r + 1] = (tot + _expand_heads(p_own, gw)[:, 0:1] * vn_ref[0]) / _expand_heads(den, gw)[:, 0:1]


def _win_sample(q_cols, wbuf_t, k_new, v_new):
    bsz, gw, reps = q_cols.shape
    lbuf = wbuf_t.shape[-1]
    col = pl.BlockSpec((1, gw, 1), lambda i: (i, 0, 0))
    return pl.pallas_call(
        _win_sample_body,
        grid=(bsz,),
        in_specs=[pl.BlockSpec((1, gw, reps), lambda i: (i, 0, 0)),
                  pl.BlockSpec((1, 1, gw, lbuf), lambda i: (i, 0, 0, 0)),
                  pl.BlockSpec((1, 1, gw, lbuf), lambda i: (i, 1, 0, 0)), col, col],
        out_specs=pl.BlockSpec((1, gw, reps), lambda i: (i, 0, 0)),
        out_shape=jax.ShapeDtypeStruct((bsz, gw, reps), F32),
        compiler_params=_params("parallel"),
        name="nsa_win_sample",
    )(q_cols, wbuf_t, wbuf_t, k_new.reshape(bsz, gw, 1), v_new.reshape(bsz, gw, 1))


def _gate_combine_body(g_ref, c_ref, s_ref, w_ref, o_ref):
    gt = jax.nn.sigmoid(g_ref[...])
    o_ref[...] = gt[0] * c_ref[...] + gt[1] * s_ref[...] + gt[2] * w_ref[...]


def _gate_combine(gate_logits, o_c, o_s, o_w):
    return pl.pallas_call(
        _gate_combine_body,
        out_shape=jax.ShapeDtypeStruct(o_c.shape, F32),
        name="nsa_gate_combine",
    )(gate_logits, o_c, o_s, o_w)


ODD_NQ, ODD_NQR, ODD_KV = 0, NSA_QW, 2 * NSA_QW
ODD_HQ = ODD_KV + 6 * NSA_KVW
ODD_EQ = ODD_HQ + 3 * HGRN_W
ODD_GATE = ODD_EQ + MEM_W
ODD_NG = ODD_GATE + MIX_W
ODD_N = 4096


def _odd_weights(w_in):
    offs = np.cumsum([0, NSA_QW] + [NSA_KVW] * 6 + [3 * NSA_H] + [HGRN_W] * 3 + [MEM_W, MIX_W])
    nq = w_in[:, offs[0]:offs[1]]
    kv = w_in[:, offs[1]:offs[7]]
    ng = w_in[:, offs[7]:offs[8]]
    rest = w_in[:, offs[8]:]
    pad = jnp.zeros((w_in.shape[0], ODD_N - ODD_NG - 3 * NSA_H), w_in.dtype)
    return jnp.concatenate([nq, nq, kv, rest, ng, pad], axis=1)


def _stack_heads(a, tq, width):
    t = a.shape[0]
    a = a.reshape(t // tq, tq, NSA_KVH, NSA_REP, width).transpose(2, 0, 3, 1, 4)
    return a.reshape(NSA_KVH, t // tq, NSA_REP * tq, width)


def _unstack_heads(a, tq):
    g, nt, _, width = a.shape
    a = a.reshape(g, nt, NSA_REP, tq, width).transpose(1, 3, 0, 2, 4)
    return a.reshape(nt * tq, g * NSA_REP * width)


def _group_kv(k, v):
    t = k.shape[0]
    return jnp.concatenate([k.reshape(t, NSA_KVH, HD), v.reshape(t, NSA_KVH, HD)], axis=-1).transpose(1, 0, 2)


def _odd_prompt(x, cos, sin, g, w_aug_bf16, w_o_bf16, mem_k, mem_v, lb, cmp_w, final_g, tm, tq, kc):
    t = x.shape[0]
    assert t % NSA_BLOCK == 0
    kvo = lambda j: ODD_KV + j * NSA_KVW
    nq, nqr, kv6, qfi, eq, gate, ng = _norm_proj(
        x, g, w_aug_bf16, cos, sin, ((ODD_NQR, ODD_KV), (kvo(2), kvo(3)), (kvo(4), kvo(5))), tm,
        splits=(NSA_QW, NSA_QW, 6 * NSA_KVW, 3 * HGRN_W, MEM_W, MIX_W, LANES))
    ck, cv, sk, sv, wk, wv = (kv6[:, j * NSA_KVW:(j + 1) * NSA_KVW] for j in range(6))
    w1b, w2b, peb = cmp_w
    nb = t // NSA_BLOCK
    cmp_tok = _compress(jnp.stack([ck, cv]).reshape(2, nb, NSA_BLOCK * NSA_KVW), w1b, w2b, peb)
    ckv = _group_kv(cmp_tok[0], cmp_tok[1])
    q_st = _stack_heads(nq, tq, HD).astype(BF16)
    qr_st = _stack_heads(nqr, tq, HD).astype(BF16)
    gt_st = _stack_heads(ng[:, :3 * NSA_H], tq, 3)
    onehot = ((jnp.arange(t)[:, None] // NSA_BLOCK) % HD == jnp.arange(HD)[None, :]).astype(F32)
    ska = jnp.concatenate([sk.reshape(t, NSA_KVH, HD), jnp.broadcast_to(onehot[:, None], (t, NSA_KVH, HD))], axis=-1)
    svo = jnp.concatenate([jnp.ones((t, NSA_KVH, HD), F32), sv.reshape(t, NSA_KVH, HD)], axis=-1)
    _, sk_norm2 = _block_stats(kv6, math.gcd(t, MOBA_BLOCK), NSA_KVW, 2)
    o_nsa = _nsa_prompt(q_st, qr_st, gt_st, ckv, ska.transpose(1, 0, 2).astype(BF16),
                        svo.transpose(1, 0, 2).astype(BF16), _group_kv(wk, wv).astype(BF16),
                        _key_norm_bound(sk_norm2, NSA_KVH), tq, kc)
    o_nsa = _unstack_heads(o_nsa, tq)
    o_hg, s_hg = _hgrn_prompt(qfi, lb.reshape(1, HGRN_W))
    o_mem = _mem_prompt(eq, mem_k, mem_v, tm)
    y = _mix_out(x, o_nsa, o_hg, o_mem, gate, w_o_bf16, final_g, tm)
    rows = lambda a, b: jnp.stack([a.reshape(t, NSA_KVH, HD), b.reshape(t, NSA_KVH, HD)], axis=1)
    return y, rows(ck, cv), rows(sk, sv), rows(wk, wv), _diag_blocks(s_hg, HGRN_H)

def _heads_major(a, h):
    t = a.shape[0]
    return a.reshape(t, h, HD).transpose(1, 0, 2)


def _even_prompt(x, cos, sin, g, w_in_bf16, w_o_bf16, mem_k, mem_v, tm):
    t = x.shape[0]
    o_mq = 3 * RET_W
    o_eq = 3 * RET_W + 3 * MOBA_W
    o_gate = o_eq + MEM_W
    ret, moba, eq, gate = _norm_proj(x, g, w_in_bf16, cos, sin, ((0, 2 * RET_W), (o_mq, o_mq + 2 * MOBA_W)), tm,
                                     splits=(3 * RET_W, 3 * MOBA_W, MEM_W, MIX_W))
    mq, mk, mv = moba[:, :MOBA_W], moba[:, MOBA_W:2 * MOBA_W], moba[:, 2 * MOBA_W:]
    o_ret, s_ret = _retention_prompt(ret)
    kmean, knorm2 = _block_stats(moba, MOBA_BLOCK, MOBA_W, 1)
    q_pad = _heads_major(mq, MOBA_H)
    vh = _heads_major(mv, MOBA_H)
    nblk = t // MOBA_BLOCK
    onehot = (jnp.arange(t)[:, None] // MOBA_BLOCK == jnp.arange(HD)[None, :]).astype(F32)
    ka = jnp.concatenate([_heads_major(mk, MOBA_H), jnp.broadcast_to(onehot, (MOBA_H, t, HD))], axis=-1).astype(BF16)
    vo = jnp.concatenate([jnp.ones_like(vh), vh], axis=-1).astype(BF16)
    km = jnp.pad(_heads_major(kmean, MOBA_H), ((0, 0), (HD, LANES - HD - nblk), (0, HD)))
    o_moba = _moba_prompt(q_pad, ka, vo, km, _key_norm_bound(knorm2, MOBA_H))
    o_moba = o_moba.transpose(1, 0, 2).reshape(t, MOBA_W)
    o_mem = _mem_prompt(eq, mem_k, mem_v, tm)
    y = _mix_out(x, o_ret, o_moba, o_mem, gate, w_o_bf16, None, tm)
    rows = moba[:, MOBA_W:].reshape(t, 2, MOBA_H, HD)
    return y, rows, _diag_blocks(s_ret, RET_H)


def _pages_t(pool, width):
    n_pool, n_layer = pool.shape[0], pool.shape[1]
    return pool.transpose(0, 1, 3, 4, 5, 2).reshape(n_pool, n_layer, 2, width, PAGE)


def _mem_t(cache):
    bsz, n = cache.shape[0], cache.shape[1]
    return cache.transpose(0, 2, 3, 4, 1).reshape(bsz, 2, MEM_W, n)


def _even_sample(x, cos, sin, g, w_in_bf16, w_o_bf16, mem_cache, state, page_table, pool, layer):
    bsz = x.shape[0]
    o_mq = 3 * RET_W
    o_eq = 3 * RET_W + 3 * MOBA_W
    o_gate = o_eq + MEM_W
    proj = _norm_proj(x, g, w_in_bf16, cos, sin, ((0, 2 * RET_W), (o_mq, o_mq + 2 * MOBA_W)), bsz)
    rq, rk, rv = proj[:, :RET_W], proj[:, RET_W:2 * RET_W], proj[:, 2 * RET_W:3 * RET_W]
    mq = proj[:, o_mq:o_mq + MOBA_W]
    mk = proj[:, o_mq + MOBA_W:o_mq + 2 * MOBA_W]
    mv = proj[:, o_mq + 2 * MOBA_W:o_eq]
    gamma = np.repeat(1.0 - np.power(2.0, -5.0 - np.arange(RET_H, dtype=np.float64)), HD)
    o_ret, s_ret = _state_step(state, rq, rk, jnp.broadcast_to(jnp.asarray(gamma, F32), (bsz, RET_W)), rv, "ret")
    o_moba = _moba_sample(page_table, _pages_t(pool, MOBA_W), layer, mq, mk, mv)
    o_mem = _mem_sample(proj[:, o_eq:o_gate], _mem_t(mem_cache))
    y = _mix_out(x, o_ret, o_moba, o_mem, proj[:, o_gate:], w_o_bf16, None, bsz)
    rows = proj[:, o_mq + MOBA_W:o_eq].reshape(bsz, 1, 2, MOBA_H, HD)
    return y, rows, s_ret


def _cols(a):
    bsz = a.shape[0]
    return a.reshape(bsz, NSA_KVH, NSA_REP, HD).transpose(0, 1, 3, 2).reshape(bsz, NSA_KVW, NSA_REP)


def _uncols(a):
    bsz = a.shape[0]
    return a.reshape(bsz, NSA_KVH, HD, NSA_REP).transpose(0, 1, 3, 2).reshape(bsz, NSA_QW)


def _odd_sample(x, cos, sin, g, w_aug_bf16, w_o_bf16, mem_cache, state, lb, page_table, cmp_pool, slc_pool, wbuf,
                layer, cmp_w_t, final_g):
    bsz = x.shape[0]
    lbuf = wbuf.shape[1]
    kvo = lambda j: ODD_KV + j * NSA_KVW
    proj = _norm_proj(x, g, w_aug_bf16, cos, sin, ((ODD_NQR, ODD_KV), (kvo(2), kvo(3)), (kvo(4), kvo(5))), bsz)
    ck, cv, sk, sv, wk, wv = (proj[:, kvo(j):kvo(j + 1)] for j in range(6))
    wd, ped, w2t = cmp_w_t
    n_pool, n_layer = cmp_pool.shape[0], cmp_pool.shape[1]
    cmp_t = cmp_pool.transpose(0, 1, 3, 4, 5, 2)
    o_c, sel = _nsa_cmp_sample(page_table, cmp_t, layer, proj[:, ODD_NQ:ODD_NQ + NSA_QW].reshape(bsz, NSA_H, HD),
                               wd, ped, w2t)
    qr = proj[:, ODD_NQR:ODD_NQR + NSA_QW]
    q_cols = _cols(qr)
    o_s = _nsa_sel_sample(page_table, slc_pool.transpose(0, 1, 3, 4, 5, 2), layer, qr.reshape(bsz, NSA_H, HD),
                          sk.reshape(bsz, NSA_KVH, HD), sv.reshape(bsz, NSA_KVH, HD), sel).reshape(bsz, NSA_QW)
    wbuf_t = wbuf.transpose(0, 2, 3, 4, 1).reshape(bsz, 2, NSA_KVW, lbuf)
    o_w = _win_sample(q_cols, wbuf_t, wk, wv)
    ng = proj[:, ODD_NG:ODD_NG + 3 * NSA_H].reshape(bsz, NSA_H, 3)
    gate_logits = jnp.repeat(ng.transpose(2, 0, 1), HD, axis=-1)
    o_nsa = _gate_combine(gate_logits, o_c.reshape(bsz, NSA_QW), o_s, _uncols(o_w))
    hq, hf, hi = (proj[:, ODD_HQ + j * HGRN_W:ODD_HQ + (j + 1) * HGRN_W] for j in range(3))
    o_hg, s_hg = _state_step(state, hq, hf, jnp.broadcast_to(lb.reshape(1, HGRN_W), (bsz, HGRN_W)), hi, "hgrn")
    o_mem = _mem_sample(proj[:, ODD_EQ:ODD_GATE], _mem_t(mem_cache))
    y = _mix_out(x, o_nsa, o_hg, o_mem, proj[:, ODD_GATE:ODD_NG], w_o_bf16, final_g, bsz)
    rows = lambda a, b: jnp.stack([a.reshape(bsz, 1, NSA_KVH, HD), b.reshape(bsz, 1, NSA_KVH, HD)], axis=2)
    win = jnp.concatenate([wbuf.astype(F32), rows(wk, wv)], axis=1)[:, -lbuf:]
    return y, rows(ck, cv), rows(sk, sv), win, s_hg


def kernel(x_prompt, x_sample, mem_prompt, cache_moba_kv, state_ret, cache_nsa_cmp_kv, cache_nsa_slc_kv,
           cache_nsa_win_kv, state_hgrn, cache_mem_kv, page_table, norm_g, mem_norm_g, w_mem_kv, w_in_even,
           w_in_odd, w_out, cmp_w1, cmp_w2, cmp_pe, hgrn_lb_logits, final_g):
    bp, tp, d = x_prompt.shape
    bs, ts, _ = x_sample.shape
    depth = w_out.shape[0]
    assert bp == 1 and ts == 1 and depth == 2
    n_mem = mem_prompt.shape[1]
    past_len = page_table.shape[1] * PAGE
    assert past_len % MOBA_BLOCK == 0 and cache_moba_kv.shape[2] == PAGE
    tm, tq, kc = 256, 128, 256
    xp, xs, mem = x_prompt[0], x_sample[:, 0], mem_prompt[0]
    cos_p, sin_p = _rope_tables(jnp.arange(tp, dtype=jnp.int32))
    cos_s, sin_s = (jnp.broadcast_to(a, (bs, LANES)) for a in _rope_tables(jnp.full((1,), past_len, jnp.int32)))
    lb_prob = jax.nn.softmax(hgrn_lb_logits.astype(F32), axis=0)
    lb_all = jnp.cumsum(lb_prob, axis=0) - lb_prob[0]
    lw_p = min(NSA_WINDOW, tp)

    def mem_kv(layer):
        kv = _norm_proj(mem, mem_norm_g[layer], w_mem_kv[layer].astype(BF16), cos_p[:n_mem], sin_p[:n_mem], (), n_mem)
        return kv.reshape(n_mem, 2, MEM_W)

    w_in0 = w_in_even[0].astype(BF16)
    w_o0 = w_out[0].astype(BF16)
    mkv0 = mem_kv(0)
    hp, moba_p, ret_p = _even_prompt(xp, cos_p, sin_p, norm_g[0], w_in0, w_o0, mkv0[:, 0], mkv0[:, 1], tm)
    hs, moba_s, ret_s = _even_sample(xs, cos_s, sin_s, norm_g[0], w_in0, w_o0, cache_mem_kv[0], state_ret[0],
                                     page_table, cache_moba_kv, 0)

    w_in1 = _odd_weights(w_in_odd[0]).astype(BF16)
    w_o1 = w_out[1].astype(BF16)
    mkv1 = mem_kv(1)
    cmp_w = _compress_weights(cmp_w1[0], cmp_w2[0], cmp_pe[0])
    cmp_w_t = _compress_weights_t(cmp_w1[0], cmp_w2[0], cmp_pe[0])
    yp, cmp_p, slc_p, win_p, hg_p = _odd_prompt(hp, cos_p, sin_p, norm_g[1], w_in1, w_o1, mkv1[:, 0], mkv1[:, 1],
                                                lb_all[1], cmp_w, final_g, tm, tq, kc)
    ys, cmp_s, slc_s, win_s, hg_s = _odd_sample(hs, cos_s, sin_s, norm_g[1], w_in1, w_o1, cache_mem_kv[1],
                                                state_hgrn[0], lb_all[1], page_table, cache_nsa_cmp_kv,
                                                cache_nsa_slc_kv, cache_nsa_win_kv[0], 0, cmp_w_t, final_g)

    return (yp[None], ys[:, None], moba_p[None, None], moba_s[:, None], ret_p[None, None], ret_s[None],
            cmp_p[None, None], cmp_s[:, None], slc_p[None, None], slc_s[:, None], win_p[None, tp - lw_p:][None],
            win_s[None], hg_p[None, None], hg_s[None], jnp.stack([mkv0, mkv1]).reshape(depth, 1, n_mem, 2, MEM_H, HD))
```
